```python
import jax, jax.numpy as jnp
from jax import lax
import numpy as np

D_MODEL = 2048
BATCH = 8
SEQ = 2048
DEPTH = 4

N_EVEN = (DEPTH + 1) // 2
N_ODD = DEPTH // 2

A_KEY = 128
A_VAL = 128
A_HEADS = D_MODEL // A_VAL
A_KDIM = A_HEADS * A_KEY
A_WIDTH = A_HEADS * A_VAL
A_CHUNK = 64
A_F_MIN = 1e-6
B_HEAD_DIM = 64
B_WIDTH = D_MODEL
B_HEADS = B_WIDTH // B_HEAD_DIM
B_GROUPS = 4
B_STATE = 128
B_CONV = 4
B_CHUNK = 128
B_CONV_CH = B_WIDTH + 2 * B_GROUPS * B_STATE
EVEN_IN_SIZES = (A_KDIM, A_KDIM, A_WIDTH, A_WIDTH, B_WIDTH, B_CONV_CH, B_HEADS)
EVEN_IN = sum(EVEN_IN_SIZES)
EVEN_MIX = A_WIDTH + B_WIDTH
C_WIDTH = D_MODEL
C_KERNEL = 31
FFN_HIDDEN = -(-8 * D_MODEL // (3 * 256)) * 256
RMS_EPS = 1e-6
LN_EPS = 1e-5

kernel_name = 'hgrn2_mamba2_conformer_hybrid'


def rms_norm(x, g, eps=RMS_EPS):
    x32 = x.astype(jnp.float32)
    y = x32 * lax.rsqrt(jnp.mean(x32 * x32, axis=-1, keepdims=True) + eps)
    return (y * g.astype(jnp.float32)).astype(x.dtype)


def group_rms_norm(x, g, n_groups, eps=RMS_EPS):
    shp = x.shape
    x32 = x.astype(jnp.float32).reshape(shp[:-1] + (n_groups, shp[-1] // n_groups))
    y = x32 * lax.rsqrt(jnp.mean(x32 * x32, axis=-1, keepdims=True) + eps)
    return y.reshape(shp) * g.astype(jnp.float32)


def layer_norm(x, g, b, eps=LN_EPS):
    x32 = x.astype(jnp.float32)
    mu = jnp.mean(x32, axis=-1, keepdims=True)
    xc = x32 - mu
    y = xc * lax.rsqrt(jnp.mean(xc * xc, axis=-1, keepdims=True) + eps)
    return (y * g.astype(jnp.float32) + b.astype(jnp.float32)).astype(x.dtype)


def causal_depthwise_conv(x, w):
    k, c = w.shape
    return lax.conv_general_dilated(
        x, w[:, None, :].astype(x.dtype), window_strides=(1,), padding=[(k - 1, 0)],
        dimension_numbers=('NWC', 'WIO', 'NWC'), feature_group_count=c)


def masked_exp(mask, t):
    return jnp.where(mask, jnp.exp(jnp.where(mask, t, 0.0)), 0.0)


def hgrn2_chunkwise(q, k, v, log_f):
    bsz, s, h, dk = q.shape
    dv = v.shape[-1]
    nc = s // A_CHUNK

    def to_chunks(t):
        return t.reshape(bsz, nc, A_CHUNK, h, t.shape[-1]).transpose(1, 0, 3, 2, 4)

    causal = jnp.tril(jnp.ones((A_CHUNK, A_CHUNK), dtype=bool))[:, :, None]

    def step(state, inp):
        qc, kc, vc, gc = inp
        b = jnp.cumsum(gc, axis=2)
        o_inter = jnp.einsum('bhtk,bhkv->bhtv', qc * jnp.exp(b), state)
        diff = b[:, :, :, None, :] - b[:, :, None, :, :]
        decay = masked_exp(causal, diff)
        scores = jnp.einsum('bhtk,bhsk,bhtsk->bhts', qc, kc, decay)
        o_intra = jnp.einsum('bhts,bhsv->bhtv', scores, vc)
        b_last = b[:, :, -1, :]
        state = state * jnp.exp(b_last)[..., None] + jnp.einsum(
            'bhsk,bhsv->bhkv', kc * jnp.exp(b_last[:, :, None, :] - b), vc)
        return state, o_inter + o_intra

    init = jnp.zeros((bsz, h, dk, dv), jnp.float32)
    _, o = lax.scan(step, init, (to_chunks(q), to_chunks(k), to_chunks(v), to_chunks(log_f)))
    return o.transpose(1, 0, 3, 2, 4).reshape(bsz, s, h * dv)


def ssd_chunked(x, dt, a, bm, cm):
    bsz, s, nh, p = x.shape
    g, n = bm.shape[-2:]
    r = nh // g
    nc = s // B_CHUNK
    L = B_CHUNK
    xdt = (x * dt[..., None]).reshape(bsz, nc, L, g, r, p)
    cs = jnp.cumsum((dt * a).reshape(bsz, nc, L, g, r), axis=2)
    bc = bm.reshape(bsz, nc, L, g, n)
    cc = cm.reshape(bsz, nc, L, g, n)
    causal = jnp.tril(jnp.ones((L, L), dtype=bool))[:, :, None, None]
    seg = cs[:, :, :, None] - cs[:, :, None, :]
    decay = masked_exp(causal, seg)
    cb = jnp.einsum('bctgn,bcsgn->bctsg', cc, bc)
    y_diag = jnp.einsum('bctsg,bctsgr,bcsgrp->bctgrp', cb, decay, xdt)
    decay_to_end = jnp.exp(cs[:, :, -1:] - cs)
    states = jnp.einsum('bcsgn,bcsgr,bcsgrp->bcgrpn', bc, decay_to_end, xdt)
    chunk_decay = jnp.exp(cs[:, :, -1])

    def pass_state(hs, inp):
        st, dec = inp
        return hs * dec[..., None, None] + st, hs

    init = jnp.zeros((bsz, g, r, p, n), jnp.float32)
    _, h_prev = lax.scan(pass_state, init,
                         (states.transpose(1, 0, 2, 3, 4, 5), chunk_decay.transpose(1, 0, 2, 3)))
    h_prev = h_prev.transpose(1, 0, 2, 3, 4, 5)
    y_off = jnp.einsum('bctgn,bcgrpn,bctgr->bctgrp', cc, h_prev, jnp.exp(cs))
    return (y_diag + y_off).reshape(bsz, s, nh, p)


def hgrn2_mamba2_mixer(u, w_in, lower_bound, a_norm, conv_w, conv_b, dt_bias, a_log,
                       d_skip, b_norm, w_out):
    f32 = jnp.float32
    bsz, s, _ = u.shape
    split_points = np.cumsum(EVEN_IN_SIZES)[:-1].tolist()
    q, f_pre, v, gate, z, xbc, dt_raw = jnp.split(u @ w_in, split_points, axis=-1)

    lb = lower_bound.astype(f32)
    sig = jax.nn.sigmoid(f_pre.astype(f32))
    f = lb + (1.0 - lb) * sig
    k = (1.0 - lb) * (1.0 - sig)
    log_f = jnp.log(jnp.maximum(f, A_F_MIN))
    heads = lambda t: t.reshape(bsz, s, A_HEADS, -1)
    o_a = hgrn2_chunkwise(heads(jax.nn.silu(q.astype(f32))), heads(k),
                          heads(v.astype(f32)), heads(log_f))
    o_a = group_rms_norm(o_a, a_norm, A_HEADS) * jax.nn.silu(gate.astype(f32))

    xbc = jax.nn.silu(causal_depthwise_conv(xbc, conv_w) + conv_b.astype(xbc.dtype))
    xs, bm, cm = jnp.split(xbc, [B_WIDTH, B_WIDTH + B_GROUPS * B_STATE], axis=-1)
    dt = jax.nn.softplus(dt_raw.astype(f32) + dt_bias.astype(f32))
    a = -jnp.exp(a_log.astype(f32))
    xh = xs.astype(f32).reshape(bsz, s, B_HEADS, B_HEAD_DIM)
    y = ssd_chunked(xh, dt, a,
                    bm.astype(f32).reshape(bsz, s, B_GROUPS, B_STATE),
                    cm.astype(f32).reshape(bsz, s, B_GROUPS, B_STATE))
    y = (y + d_skip.astype(f32)[:, None] * xh).reshape(bsz, s, B_WIDTH)
    o_b = group_rms_norm(y * jax.nn.silu(z.astype(f32)), b_norm, B_GROUPS)

    mixed = jnp.concatenate([o_a, o_b], axis=-1).astype(u.dtype)
    return mixed @ w_out


def conformer_conv_module(u, w1, b1, dw_w, dw_b, ln_g, ln_b, w2, b2):
    a, g = jnp.split(u @ w1 + b1, 2, axis=-1)
    c = a * jax.nn.sigmoid(g)
    c = causal_depthwise_conv(c, dw_w) + dw_b.astype(c.dtype)
    c = jax.nn.silu(layer_norm(c, ln_g, ln_b))
    return c.astype(u.dtype) @ w2 + b2


def swiglu(u, w_gate, w_up, w_down):
    return (jax.nn.silu(u @ w_gate) * (u @ w_up)) @ w_down


def _fwd_setup_inputs(seed: int = 0) -> dict:
    key = jax.random.key(seed)
    ks = jax.random.split(key, 32)
    f32 = jnp.float32
    nrm = lambda k, shape, scale: jax.random.normal(k, shape, f32) * scale
    gain = lambda k, shape: 1.0 + 0.05 * jax.random.normal(k, shape, f32)
    dt0 = jnp.exp(jax.random.uniform(ks[10], (N_EVEN, B_HEADS), f32,
                                     minval=float(np.log(1e-3)), maxval=float(np.log(1e-1))))
    return {
        'x': jax.random.normal(ks[0], (BATCH, SEQ, D_MODEL), f32),
        'mix_pre_g': gain(ks[1], (DEPTH, D_MODEL)),
        'mix_post_g': gain(ks[2], (DEPTH, D_MODEL)),
        'ffn_pre_g': gain(ks[3], (DEPTH, D_MODEL)),
        'ffn_post_g': gain(ks[4], (DEPTH, D_MODEL)),
        'hgrn_lb_logits': nrm(ks[5], (N_EVEN, A_KDIM), 0.5),
        'even_w_in': nrm(ks[6], (N_EVEN, D_MODEL, EVEN_IN), D_MODEL ** -0.5),
        'hgrn_norm_g': gain(ks[7], (N_EVEN, A_WIDTH)),
        'ssd_conv_w': nrm(ks[8], (N_EVEN, B_CONV, B_CONV_CH), B_CONV ** -0.5),
        'ssd_conv_b': nrm(ks[9], (N_EVEN, B_CONV_CH), 0.02),
        'ssd_dt_bias': dt0 + jnp.log(-jnp.expm1(-dt0)),
        'ssd_a_log': jnp.log(jax.random.uniform(ks[11], (N_EVEN, B_HEADS), f32, minval=1.0, maxval=16.0)),
        'ssd_d': 1.0 + 0.1 * jax.random.normal(ks[12], (N_EVEN, B_HEADS), f32),
        'ssd_norm_g': gain(ks[13], (N_EVEN, B_WIDTH)),
        'even_w_out': nrm(ks[14], (N_EVEN, EVEN_MIX, D_MODEL), EVEN_MIX ** -0.5),
        'conf_w1': nrm(ks[15], (N_ODD, D_MODEL, 2 * C_WIDTH), D_MODEL ** -0.5),
        'conf_b1': nrm(ks[16], (N_ODD, 2 * C_WIDTH), 0.02),
        'conf_dw_w': nrm(ks[17], (N_ODD, C_KERNEL, C_WIDTH), C_KERNEL ** -0.5),
        'conf_dw_b': nrm(ks[18], (N_ODD, C_WIDTH), 0.02),
        'conf_ln_g': gain(ks[19], (N_ODD, C_WIDTH)),
        'conf_ln_b': nrm(ks[20], (N_ODD, C_WIDTH), 0.02),
        'conf_w2': nrm(ks[21], (N_ODD, C_WIDTH, D_MODEL), C_WIDTH ** -0.5),
        'conf_b2': nrm(ks[22], (N_ODD, D_MODEL), 0.02),
        'ffn_w_gate': nrm(ks[23], (DEPTH, D_MODEL, FFN_HIDDEN), D_MODEL ** -0.5),
        'ffn_w_up': nrm(ks[24], (DEPTH, D_MODEL, FFN_HIDDEN), D_MODEL ** -0.5),
        'ffn_w_down': nrm(ks[25], (DEPTH, FFN_HIDDEN, D_MODEL), FFN_HIDDEN ** -0.5),
    }


def _fwd_reference(x, mix_pre_g, mix_post_g, ffn_pre_g, ffn_post_g, hgrn_lb_logits, even_w_in,
              hgrn_norm_g, ssd_conv_w, ssd_conv_b, ssd_dt_bias, ssd_a_log, ssd_d, ssd_norm_g,
              even_w_out, conf_w1, conf_b1, conf_dw_w, conf_dw_b, conf_ln_g, conf_ln_b,
              conf_w2, conf_b2, ffn_w_gate, ffn_w_up, ffn_w_down):
    lb_p = jax.nn.softmax(hgrn_lb_logits.astype(jnp.float32), axis=0)
    lower_bounds = jnp.cumsum(lb_p, axis=0) - lb_p[0]
    h = x
    for layer in range(DEPTH):
        i = layer // 2
        u = rms_norm(h, mix_pre_g[layer])
        if layer % 2 == 0:
            m = hgrn2_mamba2_mixer(u, even_w_in[i], lower_bounds[i], hgrn_norm_g[i],
                                   ssd_conv_w[i], ssd_conv_b[i], ssd_dt_bias[i], ssd_a_log[i],
                                   ssd_d[i], ssd_norm_g[i], even_w_out[i])
        else:
            m = conformer_conv_module(u, conf_w1[i], conf_b1[i], conf_dw_w[i], conf_dw_b[i],
                                      conf_ln_g[i], conf_ln_b[i], conf_w2[i], conf_b2[i])
        h = h + rms_norm(m, mix_post_g[layer])
        u = rms_norm(h, ffn_pre_g[layer])
        h = h + rms_norm(swiglu(u, ffn_w_gate[layer], ffn_w_up[layer], ffn_w_down[layer]),
                         ffn_post_g[layer])
    return h


import jax as _jax
import jax.numpy as _jnp

TWIN_FORMAT = 'train_step'
FWD_PARAMS = ['x', 'mix_pre_g', 'mix_post_g', 'ffn_pre_g', 'ffn_post_g', 'hgrn_lb_logits', 'even_w_in', 'hgrn_norm_g', 'ssd_conv_w', 'ssd_conv_b', 'ssd_dt_bias', 'ssd_a_log', 'ssd_d', 'ssd_norm_g', 'even_w_out', 'conf_w1', 'conf_b1', 'conf_dw_w', 'conf_dw_b', 'conf_ln_g', 'conf_ln_b', 'conf_w2', 'conf_b2', 'ffn_w_gate', 'ffn_w_up', 'ffn_w_down']
TWIN_WEIGHTS = ['mix_pre_g', 'mix_post_g', 'ffn_pre_g', 'ffn_post_g', 'hgrn_lb_logits', 'even_w_in', 'hgrn_norm_g', 'ssd_conv_w', 'ssd_conv_b', 'ssd_dt_bias', 'ssd_a_log', 'ssd_d', 'ssd_norm_g', 'even_w_out', 'conf_w1', 'conf_b1', 'conf_dw_w', 'conf_dw_b', 'conf_ln_g', 'conf_ln_b', 'conf_w2', 'conf_b2', 'ffn_w_gate', 'ffn_w_up', 'ffn_w_down']
TWIN_DIFF_INPUT = 'x'
TWIN_INPUTS = ['x', 'mix_pre_g', 'mix_post_g', 'ffn_pre_g', 'ffn_post_g', 'hgrn_lb_logits', 'even_w_in', 'hgrn_norm_g', 'ssd_conv_w', 'ssd_conv_b', 'ssd_dt_bias', 'ssd_a_log', 'ssd_d', 'ssd_norm_g', 'even_w_out', 'conf_w1', 'conf_b1', 'conf_dw_w', 'conf_dw_b', 'conf_ln_g', 'conf_ln_b', 'conf_w2', 'conf_b2', 'ffn_w_gate', 'ffn_w_up', 'ffn_w_down', 'loss_target', 'm_mix_pre_g', 'm_mix_post_g', 'm_ffn_pre_g', 'm_ffn_post_g', 'm_hgrn_lb_logits', 'm_even_w_in', 'm_hgrn_norm_g', 'm_ssd_conv_w', 'm_ssd_conv_b', 'm_ssd_dt_bias', 'm_ssd_a_log', 'm_ssd_d', 'm_ssd_norm_g', 'm_even_w_out', 'm_conf_w1', 'm_conf_b1', 'm_conf_dw_w', 'm_conf_dw_b', 'm_conf_ln_g', 'm_conf_ln_b', 'm_conf_w2', 'm_conf_b2', 'm_ffn_w_gate', 'm_ffn_w_up', 'm_ffn_w_down', 'v_mix_pre_g', 'v_mix_post_g', 'v_ffn_pre_g', 'v_ffn_post_g', 'v_hgrn_lb_logits', 'v_even_w_in', 'v_hgrn_norm_g', 'v_ssd_conv_w', 'v_ssd_conv_b', 'v_ssd_dt_bias', 'v_ssd_a_log', 'v_ssd_d', 'v_ssd_norm_g', 'v_even_w_out', 'v_conf_w1', 'v_conf_b1', 'v_conf_dw_w', 'v_conf_dw_b', 'v_conf_ln_g', 'v_conf_ln_b', 'v_conf_w2', 'v_conf_b2', 'v_ffn_w_gate', 'v_ffn_w_up', 'v_ffn_w_down']
TWIN_OUTPUTS = ['loss', 'grad_x', 'grad_mix_pre_g', 'grad_mix_post_g', 'grad_ffn_pre_g', 'grad_ffn_post_g', 'grad_hgrn_lb_logits', 'grad_even_w_in', 'grad_hgrn_norm_g', 'grad_ssd_conv_w', 'grad_ssd_conv_b', 'grad_ssd_dt_bias', 'grad_ssd_a_log', 'grad_ssd_d', 'grad_ssd_norm_g', 'grad_even_w_out', 'grad_conf_w1', 'grad_conf_b1', 'grad_conf_dw_w', 'grad_conf_dw_b', 'grad_conf_ln_g', 'grad_conf_ln_b', 'grad_conf_w2', 'grad_conf_b2', 'grad_ffn_w_gate', 'grad_ffn_w_up', 'grad_ffn_w_down', 'delta_mix_pre_g', 'delta_mix_post_g', 'delta_ffn_pre_g', 'delta_ffn_post_g', 'delta_hgrn_lb_logits', 'delta_even_w_in', 'delta_hgrn_norm_g', 'delta_ssd_conv_w', 'delta_ssd_conv_b', 'delta_ssd_dt_bias', 'delta_ssd_a_log', 'delta_ssd_d', 'delta_ssd_norm_g', 'delta_even_w_out', 'delta_conf_w1', 'delta_conf_b1', 'delta_conf_dw_w', 'delta_conf_dw_b', 'delta_conf_ln_g', 'delta_conf_ln_b', 'delta_conf_w2', 'delta_conf_b2', 'delta_ffn_w_gate', 'delta_ffn_w_up', 'delta_ffn_w_down', 'new_m_mix_pre_g', 'new_m_mix_post_g', 'new_m_ffn_pre_g', 'new_m_ffn_post_g', 'new_m_hgrn_lb_logits', 'new_m_even_w_in', 'new_m_hgrn_norm_g', 'new_m_ssd_conv_w', 'new_m_ssd_conv_b', 'new_m_ssd_dt_bias', 'new_m_ssd_a_log', 'new_m_ssd_d', 'new_m_ssd_norm_g', 'new_m_even_w_out', 'new_m_conf_w1', 'new_m_conf_b1', 'new_m_conf_dw_w', 'new_m_conf_dw_b', 'new_m_conf_ln_g', 'new_m_conf_ln_b', 'new_m_conf_w2', 'new_m_conf_b2', 'new_m_ffn_w_gate', 'new_m_ffn_w_up', 'new_m_ffn_w_down', 'new_v_mix_pre_g', 'new_v_mix_post_g', 'new_v_ffn_pre_g', 'new_v_ffn_post_g', 'new_v_hgrn_lb_logits', 'new_v_even_w_in', 'new_v_hgrn_norm_g', 'new_v_ssd_conv_w', 'new_v_ssd_conv_b', 'new_v_ssd_dt_bias', 'new_v_ssd_a_log', 'new_v_ssd_d', 'new_v_ssd_norm_g', 'new_v_even_w_out', 'new_v_conf_w1', 'new_v_conf_b1', 'new_v_conf_dw_w', 'new_v_conf_dw_b', 'new_v_conf_ln_g', 'new_v_conf_ln_b', 'new_v_conf_w2', 'new_v_conf_b2', 'new_v_ffn_w_gate', 'new_v_ffn_w_up', 'new_v_ffn_w_down']
TWIN_LEAF_KINDS = {'loss': 'loss', 'grad_x': 'grad_x', 'grad_mix_pre_g': 'grad_w', 'grad_mix_post_g': 'grad_w', 'grad_ffn_pre_g': 'grad_w', 'grad_ffn_post_g': 'grad_w', 'grad_hgrn_lb_logits': 'grad_w', 'grad_even_w_in': 'grad_w', 'grad_hgrn_norm_g': 'grad_w', 'grad_ssd_conv_w': 'grad_w', 'grad_ssd_conv_b': 'grad_w', 'grad_ssd_dt_bias': 'grad_w', 'grad_ssd_a_log': 'grad_w', 'grad_ssd_d': 'grad_w', 'grad_ssd_norm_g': 'grad_w', 'grad_even_w_out': 'grad_w', 'grad_conf_w1': 'grad_w', 'grad_conf_b1': 'grad_w', 'grad_conf_dw_w': 'grad_w', 'grad_conf_dw_b': 'grad_w', 'grad_conf_ln_g': 'grad_w', 'grad_conf_ln_b': 'grad_w', 'grad_conf_w2': 'grad_w', 'grad_conf_b2': 'grad_w', 'grad_ffn_w_gate': 'grad_w', 'grad_ffn_w_up': 'grad_w', 'grad_ffn_w_down': 'grad_w', 'delta_mix_pre_g': 'delta_w', 'delta_mix_post_g': 'delta_w', 'delta_ffn_pre_g': 'delta_w', 'delta_ffn_post_g': 'delta_w', 'delta_hgrn_lb_logits': 'delta_w', 'delta_even_w_in': 'delta_w', 'delta_hgrn_norm_g': 'delta_w', 'delta_ssd_conv_w': 'delta_w', 'delta_ssd_conv_b': 'delta_w', 'delta_ssd_dt_bias': 'delta_w', 'delta_ssd_a_log': 'delta_w', 'delta_ssd_d': 'delta_w', 'delta_ssd_norm_g': 'delta_w', 'delta_even_w_out': 'delta_w', 'delta_conf_w1': 'delta_w', 'delta_conf_b1': 'delta_w', 'delta_conf_dw_w': 'delta_w', 'delta_conf_dw_b': 'delta_w', 'delta_conf_ln_g': 'delta_w', 'delta_conf_ln_b': 'delta_w', 'delta_conf_w2': 'delta_w', 'delta_conf_b2': 'delta_w', 'delta_ffn_w_gate': 'delta_w', 'delta_ffn_w_up': 'delta_w', 'delta_ffn_w_down': 'delta_w', 'new_m_mix_pre_g': 'new_m', 'new_m_mix_post_g': 'new_m', 'new_m_ffn_pre_g': 'new_m', 'new_m_ffn_post_g': 'new_m', 'new_m_hgrn_lb_logits': 'new_m', 'new_m_even_w_in': 'new_m', 'new_m_hgrn_norm_g': 'new_m', 'new_m_ssd_conv_w': 'new_m', 'new_m_ssd_conv_b': 'new_m', 'new_m_ssd_dt_bias': 'new_m', 'new_m_ssd_a_log': 'new_m', 'new_m_ssd_d': 'new_m', 'new_m_ssd_norm_g': 'new_m', 'new_m_even_w_out': 'new_m', 'new_m_conf_w1': 'new_m', 'new_m_conf_b1': 'new_m', 'new_m_conf_dw_w': 'new_m', 'new_m_conf_dw_b': 'new_m', 'new_m_conf_ln_g': 'new_m', 'new_m_conf_ln_b': 'new_m', 'new_m_conf_w2': 'new_m', 'new_m_conf_b2': 'new_m', 'new_m_ffn_w_gate': 'new_m', 'new_m_ffn_w_up': 'new_m', 'new_m_ffn_w_down': 'new_m', 'new_v_mix_pre_g': 'new_v', 'new_v_mix_post_g': 'new_v', 'new_v_ffn_pre_g': 'new_v', 'new_v_ffn_post_g': 'new_v', 'new_v_hgrn_lb_logits': 'new_v', 'new_v_even_w_in': 'new_v', 'new_v_hgrn_norm_g': 'new_v', 'new_v_ssd_conv_w': 'new_v', 'new_v_ssd_conv_b': 'new_v', 'new_v_ssd_dt_bias': 'new_v', 'new_v_ssd_a_log': 'new_v', 'new_v_ssd_d': 'new_v', 'new_v_ssd_norm_g': 'new_v', 'new_v_even_w_out': 'new_v', 'new_v_conf_w1': 'new_v', 'new_v_conf_b1': 'new_v', 'new_v_conf_dw_w': 'new_v', 'new_v_conf_dw_b': 'new_v', 'new_v_conf_ln_g': 'new_v', 'new_v_conf_ln_b': 'new_v', 'new_v_conf_w2': 'new_v', 'new_v_conf_b2': 'new_v', 'new_v_ffn_w_gate': 'new_v', 'new_v_ffn_w_up': 'new_v', 'new_v_ffn_w_down': 'new_v'}


def _forward(args):
    return _fwd_reference(*[args[k] for k in FWD_PARAMS])


def _output_shape():
    out = _jax.eval_shape(lambda: _forward(_fwd_setup_inputs(0)))
    return out.shape, out.dtype

N_MICROBATCH = 1
ADAM_LR = 0.001
ADAM_B1 = 0.9
ADAM_B2 = 0.999
ADAM_EPS = 1e-08
ADAM_WD = 0.01
ADAM_STEP = 10
PER_EXAMPLE_BATCH_AXIS = {'x': 0, 'loss_target': 0}
SHARED_INPUTS = []
_WEIGHT_DTYPES = {'mix_pre_g': _jnp.float32, 'mix_post_g': _jnp.float32, 'ffn_pre_g': _jnp.float32, 'ffn_post_g': _jnp.float32, 'hgrn_lb_logits': _jnp.float32, 'even_w_in': _jnp.float32, 'hgrn_norm_g': _jnp.float32, 'ssd_conv_w': _jnp.float32, 'ssd_conv_b': _jnp.float32, 'ssd_dt_bias': _jnp.float32, 'ssd_a_log': _jnp.float32, 'ssd_d': _jnp.float32, 'ssd_norm_g': _jnp.float32, 'even_w_out': _jnp.float32, 'conf_w1': _jnp.float32, 'conf_b1': _jnp.float32, 'conf_dw_w': _jnp.float32, 'conf_dw_b': _jnp.float32, 'conf_ln_g': _jnp.float32, 'conf_ln_b': _jnp.float32, 'conf_w2': _jnp.float32, 'conf_b2': _jnp.float32, 'ffn_w_gate': _jnp.float32, 'ffn_w_up': _jnp.float32, 'ffn_w_down': _jnp.float32}
MOMENT_SCALE = {'mix_pre_g': 6.770986e-01, 'mix_post_g': 8.196810e+00, 'ffn_pre_g': 5.934169e-01, 'ffn_post_g': 7.954281e+00, 'hgrn_lb_logits': 1.316479e-02, 'even_w_in': 3.278055e-01, 'hgrn_norm_g': 3.537034e-01, 'ssd_conv_w': 5.867834e-01, 'ssd_conv_b': 1.634199e+00, 'ssd_dt_bias': 9.821342e-01, 'ssd_a_log': 2.292835e+00, 'ssd_d': 3.679202e+00, 'ssd_norm_g': 9.185652e-01, 'even_w_out': 9.414681e-01, 'conf_w1': 3.524670e-01, 'conf_b1': 2.542959e+00, 'conf_dw_w': 5.933165e-01, 'conf_dw_b': 6.552658e+00, 'conf_ln_g': 2.522595e+00, 'conf_ln_b': 3.712339e+00, 'conf_w2': 1.498176e+00, 'conf_b2': 7.366738e+00, 'ffn_w_gate': 2.189513e-01, 'ffn_w_up': 2.681916e-01, 'ffn_w_down': 4.453075e-01}


def _to_microbatches(a, axis):
    t = _jnp.moveaxis(a, axis, 0)
    t = t.reshape((N_MICROBATCH, t.shape[0] // N_MICROBATCH) + t.shape[1:])
    return _jnp.moveaxis(t, 1, axis + 1)


def setup_inputs(seed: int = 0) -> dict:
    inp = _fwd_setup_inputs(seed)
    key = _jax.random.fold_in(_jax.random.key(seed), 7919)
    shape, _ = _output_shape()
    out = dict(inp)
    out["loss_target"] = _jax.random.normal(_jax.random.fold_in(key, 0), shape, _jnp.float32)
    for i, name in enumerate(TWIN_WEIGHTS):
        w = inp[name].astype(_jnp.float32)
        if MOMENT_SCALE is None:
            s = _jnp.sqrt(_jnp.mean(_jnp.square(w)) + 1e-30)
        else:
            s = MOMENT_SCALE[name]
        km, kv = _jax.random.split(_jax.random.fold_in(key, i + 1))
        out[name] = w
        out["m_" + name] = s * _jax.random.normal(km, w.shape, _jnp.float32)
        out["v_" + name] = (s * s) * _jax.random.uniform(kv, w.shape, _jnp.float32, 0.5, 1.5)
    if N_MICROBATCH > 1:
        for name, axis in PER_EXAMPLE_BATCH_AXIS.items():
            out[name] = _to_microbatches(out[name], axis)
    return {'x': out['x'], 'mix_pre_g': out['mix_pre_g'], 'mix_post_g': out['mix_post_g'], 'ffn_pre_g': out['ffn_pre_g'], 'ffn_post_g': out['ffn_post_g'], 'hgrn_lb_logits': out['hgrn_lb_logits'], 'even_w_in': out['even_w_in'], 'hgrn_norm_g': out['hgrn_norm_g'], 'ssd_conv_w': out['ssd_conv_w'], 'ssd_conv_b': out['ssd_conv_b'], 'ssd_dt_bias': out['ssd_dt_bias'], 'ssd_a_log': out['ssd_a_log'], 'ssd_d': out['ssd_d'], 'ssd_norm_g': out['ssd_norm_g'], 'even_w_out': out['even_w_out'], 'conf_w1': out['conf_w1'], 'conf_b1': out['conf_b1'], 'conf_dw_w': out['conf_dw_w'], 'conf_dw_b': out['conf_dw_b'], 'conf_ln_g': out['conf_ln_g'], 'conf_ln_b': out['conf_ln_b'], 'conf_w2': out['conf_w2'], 'conf_b2': out['conf_b2'], 'ffn_w_gate': out['ffn_w_gate'], 'ffn_w_up': out['ffn_w_up'], 'ffn_w_down': out['ffn_w_down'], 'loss_target': out['loss_target'], 'm_mix_pre_g': out['m_mix_pre_g'], 'm_mix_post_g': out['m_mix_post_g'], 'm_ffn_pre_g': out['m_ffn_pre_g'], 'm_ffn_post_g': out['m_ffn_post_g'], 'm_hgrn_lb_logits': out['m_hgrn_lb_logits'], 'm_even_w_in': out['m_even_w_in'], 'm_hgrn_norm_g': out['m_hgrn_norm_g'], 'm_ssd_conv_w': out['m_ssd_conv_w'], 'm_ssd_conv_b': out['m_ssd_conv_b'], 'm_ssd_dt_bias': out['m_ssd_dt_bias'], 'm_ssd_a_log': out['m_ssd_a_log'], 'm_ssd_d': out['m_ssd_d'], 'm_ssd_norm_g': out['m_ssd_norm_g'], 'm_even_w_out': out['m_even_w_out'], 'm_conf_w1': out['m_conf_w1'], 'm_conf_b1': out['m_conf_b1'], 'm_conf_dw_w': out['m_conf_dw_w'], 'm_conf_dw_b': out['m_conf_dw_b'], 'm_conf_ln_g': out['m_conf_ln_g'], 'm_conf_ln_b': out['m_conf_ln_b'], 'm_conf_w2': out['m_conf_w2'], 'm_conf_b2': out['m_conf_b2'], 'm_ffn_w_gate': out['m_ffn_w_gate'], 'm_ffn_w_up': out['m_ffn_w_up'], 'm_ffn_w_down': out['m_ffn_w_down'], 'v_mix_pre_g': out['v_mix_pre_g'], 'v_mix_post_g': out['v_mix_post_g'], 'v_ffn_pre_g': out['v_ffn_pre_g'], 'v_ffn_post_g': out['v_ffn_post_g'], 'v_hgrn_lb_logits': out['v_hgrn_lb_logits'], 'v_even_w_in': out['v_even_w_in'], 'v_hgrn_norm_g': out['v_hgrn_norm_g'], 'v_ssd_conv_w': out['v_ssd_conv_w'], 'v_ssd_conv_b': out['v_ssd_conv_b'], 'v_ssd_dt_bias': out['v_ssd_dt_bias'], 'v_ssd_a_log': out['v_ssd_a_log'], 'v_ssd_d': out['v_ssd_d'], 'v_ssd_norm_g': out['v_ssd_norm_g'], 'v_even_w_out': out['v_even_w_out'], 'v_conf_w1': out['v_conf_w1'], 'v_conf_b1': out['v_conf_b1'], 'v_conf_dw_w': out['v_conf_dw_w'], 'v_conf_dw_b': out['v_conf_dw_b'], 'v_conf_ln_g': out['v_conf_ln_g'], 'v_conf_ln_b': out['v_conf_ln_b'], 'v_conf_w2': out['v_conf_w2'], 'v_conf_b2': out['v_conf_b2'], 'v_ffn_w_gate': out['v_ffn_w_gate'], 'v_ffn_w_up': out['v_ffn_w_up'], 'v_ffn_w_down': out['v_ffn_w_down']}


def _loss(weights, diff, rest, loss_target):
    with _jax.named_scope("forward"):
        args = {**rest, TWIN_DIFF_INPUT: diff, **{k: w.astype(_WEIGHT_DTYPES[k]) for k, w in weights.items()}}
        y = _forward(args)
    with _jax.named_scope("loss_head"):
        err = _jnp.square(y.astype(_jnp.float32) - loss_target)
        return 0.5 * _jnp.sum(_jnp.mean(err, axis=-1)) if err.ndim else 0.5 * err


def _adamw(w, g, m, v):
    m = ADAM_B1 * m + (1.0 - ADAM_B1) * g
    v = ADAM_B2 * v + (1.0 - ADAM_B2) * _jnp.square(g)
    m_hat = m / (1.0 - ADAM_B1 ** ADAM_STEP)
    v_hat = v / (1.0 - ADAM_B2 ** ADAM_STEP)
    delta = -ADAM_LR * (m_hat / (_jnp.sqrt(v_hat) + ADAM_EPS) + ADAM_WD * w)
    return delta, m, v


def reference(x, mix_pre_g, mix_post_g, ffn_pre_g, ffn_post_g, hgrn_lb_logits, even_w_in, hgrn_norm_g, ssd_conv_w, ssd_conv_b, ssd_dt_bias, ssd_a_log, ssd_d, ssd_norm_g, even_w_out, conf_w1, conf_b1, conf_dw_w, conf_dw_b, conf_ln_g, conf_ln_b, conf_w2, conf_b2, ffn_w_gate, ffn_w_up, ffn_w_down, loss_target, m_mix_pre_g, m_mix_post_g, m_ffn_pre_g, m_ffn_post_g, m_hgrn_lb_logits, m_even_w_in, m_hgrn_norm_g, m_ssd_conv_w, m_ssd_conv_b, m_ssd_dt_bias, m_ssd_a_log, m_ssd_d, m_ssd_norm_g, m_even_w_out, m_conf_w1, m_conf_b1, m_conf_dw_w, m_conf_dw_b, m_conf_ln_g, m_conf_ln_b, m_conf_w2, m_conf_b2, m_ffn_w_gate, m_ffn_w_up, m_ffn_w_down, v_mix_pre_g, v_mix_post_g, v_ffn_pre_g, v_ffn_post_g, v_hgrn_lb_logits, v_even_w_in, v_hgrn_norm_g, v_ssd_conv_w, v_ssd_conv_b, v_ssd_dt_bias, v_ssd_a_log, v_ssd_d, v_ssd_norm_g, v_even_w_out, v_conf_w1, v_conf_b1, v_conf_dw_w, v_conf_dw_b, v_conf_ln_g, v_conf_ln_b, v_conf_w2, v_conf_b2, v_ffn_w_gate, v_ffn_w_up, v_ffn_w_down):
    given = dict(x=x, mix_pre_g=mix_pre_g, mix_post_g=mix_post_g, ffn_pre_g=ffn_pre_g, ffn_post_g=ffn_post_g, hgrn_lb_logits=hgrn_lb_logits, even_w_in=even_w_in, hgrn_norm_g=hgrn_norm_g, ssd_conv_w=ssd_conv_w, ssd_conv_b=ssd_conv_b, ssd_dt_bias=ssd_dt_bias, ssd_a_log=ssd_a_log, ssd_d=ssd_d, ssd_norm_g=ssd_norm_g, even_w_out=even_w_out, conf_w1=conf_w1, conf_b1=conf_b1, conf_dw_w=conf_dw_w, conf_dw_b=conf_dw_b, conf_ln_g=conf_ln_g, conf_ln_b=conf_ln_b, conf_w2=conf_w2, conf_b2=conf_b2, ffn_w_gate=ffn_w_gate, ffn_w_up=ffn_w_up, ffn_w_down=ffn_w_down, loss_target=loss_target, m_mix_pre_g=m_mix_pre_g, m_mix_post_g=m_mix_post_g, m_ffn_pre_g=m_ffn_pre_g, m_ffn_post_g=m_ffn_post_g, m_hgrn_lb_logits=m_hgrn_lb_logits, m_even_w_in=m_even_w_in, m_hgrn_norm_g=m_hgrn_norm_g, m_ssd_conv_w=m_ssd_conv_w, m_ssd_conv_b=m_ssd_conv_b, m_ssd_dt_bias=m_ssd_dt_bias, m_ssd_a_log=m_ssd_a_log, m_ssd_d=m_ssd_d, m_ssd_norm_g=m_ssd_norm_g, m_even_w_out=m_even_w_out, m_conf_w1=m_conf_w1, m_conf_b1=m_conf_b1, m_conf_dw_w=m_conf_dw_w, m_conf_dw_b=m_conf_dw_b, m_conf_ln_g=m_conf_ln_g, m_conf_ln_b=m_conf_ln_b, m_conf_w2=m_conf_w2, m_conf_b2=m_conf_b2, m_ffn_w_gate=m_ffn_w_gate, m_ffn_w_up=m_ffn_w_up, m_ffn_w_down=m_ffn_w_down, v_mix_pre_g=v_mix_pre_g, v_mix_post_g=v_mix_post_g, v_ffn_pre_g=v_ffn_pre_g, v_ffn_post_g=v_ffn_post_g, v_hgrn_lb_logits=v_hgrn_lb_logits, v_even_w_in=v_even_w_in, v_hgrn_norm_g=v_hgrn_norm_g, v_ssd_conv_w=v_ssd_conv_w, v_ssd_conv_b=v_ssd_conv_b, v_ssd_dt_bias=v_ssd_dt_bias, v_ssd_a_log=v_ssd_a_log, v_ssd_d=v_ssd_d, v_ssd_norm_g=v_ssd_norm_g, v_even_w_out=v_even_w_out, v_conf_w1=v_conf_w1, v_conf_b1=v_conf_b1, v_conf_dw_w=v_conf_dw_w, v_conf_dw_b=v_conf_dw_b, v_conf_ln_g=v_conf_ln_g, v_conf_ln_b=v_conf_ln_b, v_conf_w2=v_conf_w2, v_conf_b2=v_conf_b2, v_ffn_w_gate=v_ffn_w_gate, v_ffn_w_up=v_ffn_w_up, v_ffn_w_down=v_ffn_w_down)
    weights = {n: given[n] for n in TWIN_WEIGHTS}
    shared = {n: given[n] for n in SHARED_INPUTS}
    per_example = {n: given[n] for n in ['x']}
    grad_fn = _jax.value_and_grad(_loss, argnums=(0, 1))

    def one_microbatch(ex, loss_target):
        ex = dict(ex)
        diff = ex.pop(TWIN_DIFF_INPUT)
        return grad_fn(weights, diff, {**shared, **ex}, loss_target)

    if N_MICROBATCH == 1:
        loss, (grad_w, grad_x) = one_microbatch(per_example, given["loss_target"])
    else:
        def body(carry, xs):
            loss_sum, grad_sum = carry
            l_k, (gw_k, gx_k) = one_microbatch(xs[0], xs[1])
            with _jax.named_scope("update"):
                return (loss_sum + l_k, _jax.tree.map(_jnp.add, grad_sum, gw_k)), gx_k

        init = (_jnp.zeros((), _jnp.float32), _jax.tree.map(_jnp.zeros_like, weights))
        (loss, grad_w), grad_x = _jax.lax.scan(body, init, (per_example, given["loss_target"]))
    with _jax.named_scope("update"):
        delta_w, new_m, new_v = {}, {}, {}
        for n in TWIN_WEIGHTS:
            delta_w[n], new_m[n], new_v[n] = _adamw(weights[n], grad_w[n], given["m_" + n], given["v_" + n])
    return (loss, grad_x, *[grad_w[n] for n in TWIN_WEIGHTS], *[delta_w[n] for n in TWIN_WEIGHTS],
            *[new_m[n] for n in TWIN_WEIGHTS], *[new_v[n] for n in TWIN_WEIGHTS])
```

```python
import functools
import math

import jax
import jax.numpy as jnp
from jax import lax
from jax.experimental import pallas as pl
from jax.experimental.pallas import tpu as pltpu

f32 = jnp.float32
bf16 = jnp.bfloat16
MESH = pl.DeviceIdType.MESH
HI = lax.Precision.HIGHEST

A_HEAD = 128
A_CHUNK = 64
A_F_MIN = 1e-6
B_HEAD = 64
B_GROUPS = 4
B_STATE = 128
B_CONV = 4
B_CHUNK = 128
C_KERNEL = 31
RMS_EPS = 1e-6
LN_EPS = 1e-5
ADAM_LR = 0.001
ADAM_B1 = 0.9
ADAM_B2 = 0.999
ADAM_EPS = 1e-08
ADAM_WD = 0.01
ADAM_STEP = 10

N_CHIPS = 4
N_DEV = 8
V7X_VMEM_LIMIT = 56 * 1024 * 1024
LANES = 128
CONV_PAD = 32
CONV_ROWS = 128
CONV_CH = 256

NN = (((1,), (0,)), ((), ()))
NT = (((1,), (1,)), ((), ()))
TN = (((0,), (0,)), ((), ()))


def _tile(n, cap, unit=LANES):
    best = None
    for t in range(unit, min(n, cap) + 1, unit):
        if n % t == 0:
            best = t
    return n if best is None else best


def _cp(*sem):
    return pltpu.CompilerParams(dimension_semantics=sem, vmem_limit_bytes=V7X_VMEM_LIMIT)


def _sigmoid(x):
    return jax.nn.sigmoid(x)


def _silu(x):
    return x * jax.nn.sigmoid(x)


def _rms(x, g):
    return x * lax.rsqrt(jnp.mean(x * x, axis=-1, keepdims=True) + RMS_EPS) * g


def _pair(a):
    return a if isinstance(a, tuple) else (a, 0)


def _stage(name, fn, rows, params, outs, par_outs=(), *, tm, tn=None):
    rows = [_pair(r) for r in rows]
    params = [_pair(p) for p in params]
    S = rows[0][0].shape[0]
    n_in, n_o = len(rows) + len(params), len(outs)
    if tn is None:
        grid = (S // tm,)
        in_specs = [pl.BlockSpec((tm, a.shape[1]), lambda i: (i, 0)) for a, _ in rows]
        in_specs += [pl.BlockSpec(a.shape, lambda i: (0, 0)) for a, _ in params]
        out_specs = [pl.BlockSpec((tm, w), lambda i: (i, 0)) for w, _ in outs]
        out_specs += [pl.BlockSpec((k, w), lambda i: (0, 0)) for k, w in par_outs]
        row_axis = 0
        sem = ("arbitrary",) if par_outs else ("parallel",)
    else:
        grid = (outs[0][0] // tn, S // tm)
        in_specs = [pl.BlockSpec((tm, tn), lambda j, i, o=o: (i, j + o)) for _, o in rows]
        in_specs += [pl.BlockSpec((a.shape[0], tn), lambda j, i, o=o: (0, j + o)) for a, o in params]
        out_specs = [pl.BlockSpec((tm, tn), lambda j, i: (i, j)) for _ in outs]
        out_specs += [pl.BlockSpec((k, tn), lambda j, i: (0, j)) for k, _ in par_outs]
        row_axis = 1
        sem = ("parallel", "arbitrary") if par_outs else ("parallel", "parallel")
    out_shape = [jax.ShapeDtypeStruct((S, w), d) for w, d in outs]
    out_shape += [jax.ShapeDtypeStruct((k, w), f32) for k, w in par_outs]

    def body(*refs):
        res = fn(*[r[...] for r in refs[:n_in]])
        for r, v in zip(refs[n_in:n_in + n_o], res[:n_o]):
            r[...] = v.astype(r.dtype)
        if par_outs:
            acc_refs = refs[n_in + n_o:]

            @pl.when(pl.program_id(row_axis) == 0)
            def _():
                for r in acc_refs:
                    r[...] = jnp.zeros_like(r)

            for r, v in zip(acc_refs, res[n_o:]):
                r[...] += v

    return pl.pallas_call(
        body, name=name, grid=grid, in_specs=in_specs, out_specs=out_specs, out_shape=out_shape,
        compiler_params=_cp(*sem))(*[a for a, _ in rows], *[a for a, _ in params])


def _stage_fwd(name, fn, rows, params, outs, *, tm, tn=None):
    n_r = len(rows)

    def ffn(*t):
        return fn(*[v.astype(f32) for v in t[:n_r]], *t[n_r:])

    return _stage(name, ffn, rows, params, outs, tm=tm, tn=tn)


def _stage_bwd(name, fn, rows, params, cts, drow, *, tm, tn=None):
    rows = [_pair(r) for r in rows]
    params = [_pair(p) for p in params]
    n_r, n_p = len(rows), len(params)
    flat_ct = [_pair(c) for group in cts for c in group]
    counts = [len(group) for group in cts]
    need = [i for i, d in enumerate(drow) if d is not None]

    def bfn(*t):
        r = [v.astype(f32) for v in t[:n_r]]
        c = t[n_r:n_r + len(flat_ct)]
        p = list(t[n_r + len(flat_ct):])
        res, vjp = jax.vjp(fn, *r, *p)
        ct, pos = [], 0
        for o, k in zip(res, counts):
            s = c[pos].astype(f32)
            for e in range(1, k):
                s = s + c[pos + e].astype(f32)
            pos += k
            ct.append(s.astype(o.dtype))
        g = vjp(tuple(ct))
        return tuple(g[i] for i in need) + tuple(g[n_r:])

    if tn is None:
        outs = [(rows[i][0].shape[1], drow[i]) for i in need]
        par_outs = [p.shape for p, _ in params]
    else:
        w_all = flat_ct[0][0].shape[1]
        outs = [(w_all, drow[i]) for i in need]
        par_outs = [(p.shape[0], w_all) for p, _ in params]
    res = _stage(name, bfn, rows + flat_ct, params, outs, par_outs, tm=tm, tn=tn)
    return res[:len(need)], res[len(need):]


def _mm(name, dims, a, b, grid, a_spec, b_spec, o_spec, out_shape, acc_shape, prev=None):
    nk = grid[2]

    def body(*refs):
        a_ref, b_ref = refs[0], refs[1]
        o_ref, acc = refs[-2], refs[-1]
        k = pl.program_id(2)

        @pl.when(k == 0)
        def _():
            acc[...] = jnp.zeros_like(acc)

        acc[...] += lax.dot_general(a_ref[...].astype(bf16), b_ref[...].astype(bf16), dims,
                                    preferred_element_type=f32)

        @pl.when(k == nk - 1)
        def _():
            o_ref[...] = acc[...].astype(o_ref.dtype)

    in_specs = [a_spec, b_spec]
    args = [a, b]
    alias = {}
    if prev is not None:
        in_specs.append(pl.BlockSpec(memory_space=pl.ANY))
        args.append(prev)
        alias = {2: 0}
    return pl.pallas_call(
        body, name=name, grid=grid, in_specs=in_specs, out_specs=o_spec, out_shape=out_shape,
        scratch_shapes=[pltpu.VMEM(acc_shape, f32)], input_output_aliases=alias,
        compiler_params=_cp("parallel", "parallel", "arbitrary"))(*args)


def _mm_tiles(S):
    return _tile(S, 512)


def mm_nn_col(name, a, W, li, out_dtype=f32):
    P, _, K, C = W.shape
    S = a.shape[0]
    tm, tn, tk = _mm_tiles(S), _tile(C, 1536), _tile(K, 2048)
    nc = C // tn
    return _mm(name, NN, a, W, (S // tm, P * nc, K // tk),
               pl.BlockSpec((tm, tk), lambda i, j, k: (i, k)),
               pl.BlockSpec((None, None, tk, tn), lambda i, j, k: (j // nc, li, k, j % nc)),
               pl.BlockSpec((tm, tn), lambda i, j, k: (i, j)),
               jax.ShapeDtypeStruct((S, P * C), out_dtype), (tm, tn))


def mm_nn_row(name, a, W, li, out_dtype=f32):
    P, _, R, N = W.shape
    S = a.shape[0]
    tm, tn, tk = _mm_tiles(S), _tile(N, 1024), _tile(R, 2048)
    nr = R // tk
    return _mm(name, NN, a, W, (S // tm, N // tn, P * nr),
               pl.BlockSpec((tm, tk), lambda i, j, k: (i, k)),
               pl.BlockSpec((None, None, tk, tn), lambda i, j, k: (k // nr, li, k % nr, j)),
               pl.BlockSpec((tm, tn), lambda i, j, k: (i, j)),
               jax.ShapeDtypeStruct((S, N), out_dtype), (tm, tn))


def mm_nt_col(name, dy, W, li, out_dtype=f32):
    P, _, K, C = W.shape
    S = dy.shape[0]
    tm, tn, tk = _mm_tiles(S), _tile(K, 1024), _tile(C, 2048)
    nc = C // tk
    return _mm(name, NT, dy, W, (S // tm, K // tn, P * nc),
               pl.BlockSpec((tm, tk), lambda i, j, k: (i, k)),
               pl.BlockSpec((None, None, tn, tk), lambda i, j, k: (k // nc, li, j, k % nc)),
               pl.BlockSpec((tm, tn), lambda i, j, k: (i, j)),
               jax.ShapeDtypeStruct((S, K), out_dtype), (tm, tn))


def mm_nt_row(name, dy, W, li, out_dtype=f32):
    P, _, R, N = W.shape
    S = dy.shape[0]
    tm, tn, tk = _mm_tiles(S), _tile(R, 1536), _tile(N, 2048)
    nr = R // tn
    return _mm(name, NT, dy, W, (S // tm, P * nr, N // tk),
               pl.BlockSpec((tm, tk), lambda i, j, k: (i, k)),
               pl.BlockSpec((None, None, tn, tk), lambda i, j, k: (j // nr, li, j % nr, k)),
               pl.BlockSpec((tm, tn), lambda i, j, k: (i, j)),
               jax.ShapeDtypeStruct((S, P * R), out_dtype), (tm, tn))


def mm_tn_col(name, a, dy, G, li):
    P, _, K, C = G.shape
    S = a.shape[0]
    tm, tn, tk = _tile(K, 512), _tile(C, 1536), _tile(S, 1024)
    nc = C // tn
    return _mm(name, TN, a, dy, (K // tm, P * nc, S // tk),
               pl.BlockSpec((tk, tm), lambda i, j, k: (k, i)),
               pl.BlockSpec((tk, tn), lambda i, j, k: (k, j)),
               pl.BlockSpec((None, None, tm, tn), lambda i, j, k: (j // nc, li, i, j % nc)),
               jax.ShapeDtypeStruct(G.shape, G.dtype), (tm, tn), prev=G)


def mm_tn_row(name, a, dy, G, li):
    P, _, R, N = G.shape
    S = a.shape[0]
    tm, tn, tk = _tile(R, 1536), _tile(N, 1024), _tile(S, 1024)
    nr = R // tm
    return _mm(name, TN, a, dy, (P * nr, N // tn, S // tk),
               pl.BlockSpec((tk, tm), lambda i, j, k: (k, i)),
               pl.BlockSpec((tk, tn), lambda i, j, k: (k, j)),
               pl.BlockSpec((None, None, tm, tn), lambda i, j, k: (i // nr, li, i % nr, j)),
               jax.ShapeDtypeStruct(G.shape, G.dtype), (tm, tn), prev=G)


def _hgrn_chunk(st, q, fp, v, gt, lb, an):
    C = q.shape[0]
    sig = _sigmoid(fp)
    f = lb + (1.0 - lb) * sig
    kk = (1.0 - lb) * (1.0 - sig)
    g = jnp.log(jnp.maximum(f, A_F_MIN))
    qs = _silu(q)
    row = lax.broadcasted_iota(jnp.int32, (C, C), 0)
    col = lax.broadcasted_iota(jnp.int32, (C, C), 1)
    tri = (col <= row).astype(f32)
    b = jnp.dot(tri, g, precision=HI, preferred_element_type=f32)
    o_inter = lax.dot_general((qs * jnp.exp(b)).astype(bf16), st.astype(bf16), NT,
                              preferred_element_type=f32)
    t3 = lax.broadcasted_iota(jnp.int32, (C, C, A_HEAD), 0)
    s3 = lax.broadcasted_iota(jnp.int32, (C, C, A_HEAD), 1)
    m3 = s3 <= t3
    diff = b[:, None, :] - b[None, :, :]
    dec = jnp.where(m3, jnp.exp(jnp.where(m3, diff, 0.0)), 0.0)
    scores = jnp.sum(qs[:, None, :] * kk[None, :, :] * dec, axis=-1)
    o_intra = jnp.dot(scores.astype(bf16), v.astype(bf16), preferred_element_type=f32)
    bl = b[C - 1:C, :]
    kd = kk * jnp.exp(bl - b)
    st_new = st * jnp.exp(bl) + lax.dot_general(v.astype(bf16), kd.astype(bf16), TN,
                                                preferred_element_type=f32)
    o = o_inter + o_intra
    y = o * lax.rsqrt(jnp.mean(o * o, axis=-1, keepdims=True) + RMS_EPS) * an * _silu(gt)
    return st_new, y


def _hgrn_in_specs(HA, cidx):
    C = A_CHUNK
    specs = [pl.BlockSpec((C, A_HEAD), lambda h, c, s=s: (cidx(c), s * HA + h)) for s in range(4)]
    specs += [pl.BlockSpec((1, A_HEAD), lambda h, c: (0, h))] * 2
    return specs


def hgrn_fwd(name, ymain, lb, an, D):
    S = ymain.shape[0]
    HA, nc = D // A_HEAD, S // A_CHUNK

    def body(q, fp, v, gt, lb_ref, an_ref, o_ref, sv_ref, st):
        @pl.when(pl.program_id(1) == 0)
        def _():
            st[...] = jnp.zeros_like(st)

        sv_ref[...] = st[...]
        st_new, y = _hgrn_chunk(st[...], q[...], fp[...], v[...], gt[...], lb_ref[...], an_ref[...])
        st[...] = st_new
        o_ref[...] = y.astype(o_ref.dtype)

    return pl.pallas_call(
        body, name=name, grid=(HA, nc),
        in_specs=_hgrn_in_specs(HA, lambda c: c),
        out_specs=[pl.BlockSpec((A_CHUNK, A_HEAD), lambda h, c: (c, h)),
                   pl.BlockSpec((None, None, A_HEAD, A_HEAD), lambda h, c: (h, c, 0, 0))],
        out_shape=[jax.ShapeDtypeStruct((S, D), bf16),
                   jax.ShapeDtypeStruct((HA, nc, A_HEAD, A_HEAD), f32)],
        scratch_shapes=[pltpu.VMEM((A_HEAD, A_HEAD), f32)],
        compiler_params=_cp("parallel", "arbitrary"))(ymain, ymain, ymain, ymain, lb, an)


def hgrn_bwd(name, ymain, lb, an, saved, dmixed, D):
    S = ymain.shape[0]
    HA, nc = D // A_HEAD, S // A_CHUNK
    rev = lambda c: nc - 1 - c

    def body(q, fp, v, gt, lb_ref, an_ref, sv_ref, do_ref, dq, df, dv, dg, dlb, dan, dst):
        @pl.when(pl.program_id(1) == 0)
        def _():
            dst[...] = jnp.zeros_like(dst)
            dlb[...] = jnp.zeros_like(dlb)
            dan[...] = jnp.zeros_like(dan)

        _, vjp = jax.vjp(_hgrn_chunk, sv_ref[...], q[...], fp[...], v[...], gt[...], lb_ref[...], an_ref[...])
        g = vjp((dst[...], do_ref[...].astype(f32)))
        dst[...] = g[0]
        for r, x in zip((dq, df, dv, dg), g[1:5]):
            r[...] = x.astype(r.dtype)
        dlb[...] += g[5]
        dan[...] += g[6]

    blk = pl.BlockSpec((A_CHUNK, A_HEAD), lambda h, c: (rev(c), h))
    vec = pl.BlockSpec((1, A_HEAD), lambda h, c: (0, h))
    return pl.pallas_call(
        body, name=name, grid=(HA, nc),
        in_specs=_hgrn_in_specs(HA, rev) + [
            pl.BlockSpec((None, None, A_HEAD, A_HEAD), lambda h, c: (h, rev(c), 0, 0)), blk],
        out_specs=[blk] * 4 + [vec] * 2,
        out_shape=[jax.ShapeDtypeStruct((S, D), bf16)] * 4 + [jax.ShapeDtypeStruct((1, D), f32)] * 2,
        scratch_shapes=[pltpu.VMEM((A_HEAD, A_HEAD), f32)],
        compiler_params=_cp("parallel", "arbitrary"))(ymain, ymain, ymain, ymain, lb, an, saved, dmixed)


def _ssd_chunk(hp, xs, bm, cm, z, dtr, dtb, alog, dsk, bn, g, R):
    L, GW = xs.shape
    HB = dtr.shape[1]
    R8 = max(R, 8)
    dt = jax.nn.softplus(dtr + dtb)
    a = -jnp.exp(alog)
    row = lax.broadcasted_iota(jnp.int32, (L, L), 0)
    col = lax.broadcasted_iota(jnp.int32, (L, L), 1)
    causal = col <= row
    cs = jnp.dot(causal.astype(f32), dt * a, precision=HI, preferred_element_type=f32)
    eh = lax.broadcasted_iota(jnp.int32, (HB, GW), 0)
    ec = lax.broadcasted_iota(jnp.int32, (HB, GW), 1)
    spread = (eh == g * R + ec // B_HEAD).astype(f32)
    sh = lax.broadcasted_iota(jnp.int32, (R8, HB), 1)
    sr = lax.broadcasted_iota(jnp.int32, (R8, HB), 0)
    pick_t = jnp.logical_and(sh == g * R + sr, sr < R).astype(f32)
    dtf = jnp.dot(dt, spread, precision=HI, preferred_element_type=f32)
    csf = jnp.dot(cs, spread, precision=HI, preferred_element_type=f32)
    dsf = jnp.dot(jnp.broadcast_to(dsk, (8, HB)), spread, precision=HI, preferred_element_type=f32)[0:1, :]
    cs_col = lax.dot_general(cs, pick_t, NT, precision=HI, preferred_element_type=f32)
    cs_row = lax.dot_general(pick_t, cs, NT, precision=HI, preferred_element_type=f32)
    xdt = xs * dtf
    cb = lax.dot_general(cm.astype(bf16), bm.astype(bf16), NT, preferred_element_type=f32)
    lane_head = lax.broadcasted_iota(jnp.int32, (1, GW), 1) // B_HEAD
    y = jnp.zeros((L, GW), f32)
    for r in range(R):
        seg = cs_col[:, r:r + 1] - cs_row[r:r + 1, :]
        dec = jnp.where(causal, jnp.exp(jnp.where(causal, seg, 0.0)), 0.0)
        xm = jnp.where(lane_head == r, xdt, 0.0)
        y = y + jnp.dot((cb * dec).astype(bf16), xm.astype(bf16), preferred_element_type=f32)
    csl = csf[L - 1:L, :]
    dte = jnp.exp(csl - csf)
    states = lax.dot_general(bm.astype(bf16), (xdt * dte).astype(bf16), TN, preferred_element_type=f32)
    y_off = jnp.dot(cm.astype(bf16), hp.astype(bf16), preferred_element_type=f32) * jnp.exp(csf)
    hn = hp * jnp.exp(csl) + states
    gated = (y + y_off + dsf * xs) * _silu(z)
    out = gated * lax.rsqrt(jnp.mean(gated * gated, axis=-1, keepdims=True) + RMS_EPS) * bn
    return hn, out


def _ssd_in_specs(D, HB, cidx):
    L, GW, N = B_CHUNK, D // B_GROUPS, B_STATE
    zoff, boff = 4 * D // GW, D // N
    return [
        pl.BlockSpec((L, GW), lambda c, g: (cidx(c), g)),
        pl.BlockSpec((L, N), lambda c, g: (cidx(c), boff + g)),
        pl.BlockSpec((L, N), lambda c, g: (cidx(c), boff + B_GROUPS + g)),
        pl.BlockSpec((L, GW), lambda c, g: (cidx(c), zoff + g)),
        pl.BlockSpec((L, HB), lambda c, g: (cidx(c), 0)),
        pl.BlockSpec((1, HB), lambda c, g: (0, 0)),
        pl.BlockSpec((1, HB), lambda c, g: (0, 0)),
        pl.BlockSpec((1, HB), lambda c, g: (0, 0)),
        pl.BlockSpec((1, GW), lambda c, g: (0, g)),
    ]


def ssd_fwd(name, xact, ymain, dtr, dtb, alog, dsk, bn, D):
    S, HB = dtr.shape
    nc, GW, R = S // B_CHUNK, D // B_GROUPS, HB // B_GROUPS

    def body(xs, bm, cm, z, dt_ref, dtb_ref, al_ref, ds_ref, bn_ref, o_ref, sv_ref, hs):
        g = pl.program_id(1)

        @pl.when(pl.program_id(0) == 0)
        def _():
            hs[g] = jnp.zeros((B_STATE, GW), f32)

        hp = hs[g]
        sv_ref[...] = hp
        hn, out = _ssd_chunk(hp, xs[...], bm[...], cm[...], z[...], dt_ref[...], dtb_ref[...], al_ref[...],
                             ds_ref[...], bn_ref[...], g, R)
        hs[g] = hn
        o_ref[...] = out.astype(o_ref.dtype)

    return pl.pallas_call(
        body, name=name, grid=(nc, B_GROUPS),
        in_specs=_ssd_in_specs(D, HB, lambda c: c),
        out_specs=[pl.BlockSpec((B_CHUNK, GW), lambda c, g: (c, g)),
                   pl.BlockSpec((None, None, B_STATE, GW), lambda c, g: (c, g, 0, 0))],
        out_shape=[jax.ShapeDtypeStruct((S, D), bf16),
                   jax.ShapeDtypeStruct((nc, B_GROUPS, B_STATE, GW), f32)],
        scratch_shapes=[pltpu.VMEM((B_GROUPS, B_STATE, GW), f32)],
        compiler_params=_cp("arbitrary", "arbitrary"))(xact, xact, xact, ymain, dtr, dtb, alog, dsk, bn)


def ssd_bwd(name, xact, ymain, dtr, dtb, alog, dsk, bn, saved, dmixed, D):
    S, HB = dtr.shape
    nc, GW, R = S // B_CHUNK, D // B_GROUPS, HB // B_GROUPS
    rev = lambda c: nc - 1 - c
    ooff = D // GW

    def body(xs, bm, cm, z, dt_ref, dtb_ref, al_ref, ds_ref, bn_ref, sv_ref, do_ref,
             dxs, dbm, dcm, dz, ddt, ddtb, dal, dds, dbn, dhs):
        c, g = pl.program_id(0), pl.program_id(1)

        @pl.when(c == 0)
        def _():
            dhs[g] = jnp.zeros((B_STATE, GW), f32)
            dbn[g] = jnp.zeros((1, GW), f32)

        @pl.when(jnp.logical_and(c == 0, g == 0))
        def _():
            ddtb[...] = jnp.zeros_like(ddtb)
            dal[...] = jnp.zeros_like(dal)
            dds[...] = jnp.zeros_like(dds)

        @pl.when(g == 0)
        def _():
            ddt[...] = jnp.zeros_like(ddt)

        fn = functools.partial(_ssd_chunk, g=g, R=R)
        _, vjp = jax.vjp(fn, sv_ref[...], xs[...], bm[...], cm[...], z[...], dt_ref[...], dtb_ref[...],
                         al_ref[...], ds_ref[...], bn_ref[...])
        gr = vjp((dhs[g], do_ref[...].astype(f32)))
        dhs[g] = gr[0]
        dxs[...] = gr[1]
        dbm[...] = gr[2]
        dcm[...] = gr[3]
        dz[...] = gr[4].astype(dz.dtype)
        ddt[...] += gr[5]
        ddtb[...] += gr[6]
        dal[...] += gr[7]
        dds[...] += gr[8]
        dbn[g] += gr[9]

    hb_vec = pl.BlockSpec((1, HB), lambda c, g: (0, 0))
    return pl.pallas_call(
        body, name=name, grid=(nc, B_GROUPS),
        in_specs=_ssd_in_specs(D, HB, rev) + [
            pl.BlockSpec((None, None, B_STATE, GW), lambda c, g: (rev(c), g, 0, 0)),
            pl.BlockSpec((B_CHUNK, GW), lambda c, g: (rev(c), ooff + g))],
        out_specs=[pl.BlockSpec((B_CHUNK, GW), lambda c, g: (rev(c), g)),
                   pl.BlockSpec((B_CHUNK, B_STATE), lambda c, g: (rev(c), g)),
                   pl.BlockSpec((B_CHUNK, B_STATE), lambda c, g: (rev(c), g)),
                   pl.BlockSpec((B_CHUNK, GW), lambda c, g: (rev(c), g)),
                   pl.BlockSpec((B_CHUNK, HB), lambda c, g: (rev(c), 0)),
                   hb_vec, hb_vec, hb_vec,
                   pl.BlockSpec((B_GROUPS, 1, GW), lambda c, g: (0, 0, 0))],
        out_shape=[jax.ShapeDtypeStruct((S, D), f32),
                   jax.ShapeDtypeStruct((S, B_GROUPS * B_STATE), f32),
                   jax.ShapeDtypeStruct((S, B_GROUPS * B_STATE), f32),
                   jax.ShapeDtypeStruct((S, D), bf16),
                   jax.ShapeDtypeStruct((S, HB), f32),
                   jax.ShapeDtypeStruct((1, HB), f32), jax.ShapeDtypeStruct((1, HB), f32),
                   jax.ShapeDtypeStruct((1, HB), f32),
                   jax.ShapeDtypeStruct((B_GROUPS, 1, GW), f32)],
        scratch_shapes=[pltpu.VMEM((B_GROUPS, B_STATE, GW), f32)],
        compiler_params=_cp("arbitrary", "arbitrary"))(
            xact, xact, xact, ymain, dtr, dtb, alog, dsk, bn, saved, dmixed)


def _conv_taps(xp, w_ref, b_ref, r0, K):
    acc = jnp.broadcast_to(b_ref[...], (CONV_ROWS, b_ref.shape[1]))
    for k in range(K):
        acc = acc + w_ref[k:k + 1, :] * xp[r0 + CONV_PAD - (K - 1) + k:r0 + CONV_PAD - (K - 1) + k + CONV_ROWS, :]
    return acc


def conv_fwd(name, x, xoff, w, b, act, out_dtype):
    S = x.shape[0]
    K, CW = w.shape
    tc = CONV_CH

    def body(x_ref, w_ref, b_ref, o_ref, xp):
        xp[0:CONV_PAD, :] = jnp.zeros((CONV_PAD, tc), f32)
        xp[CONV_PAD:CONV_PAD + S, :] = x_ref[...].astype(f32)
        for r0 in range(0, S, CONV_ROWS):
            acc = _conv_taps(xp, w_ref, b_ref, r0, K)
            if act:
                acc = _silu(acc)
            o_ref[r0:r0 + CONV_ROWS, :] = acc.astype(o_ref.dtype)

    return pl.pallas_call(
        body, name=name, grid=(CW // tc,),
        in_specs=[pl.BlockSpec((S, tc), lambda j: (0, j + xoff)),
                  pl.BlockSpec((K, tc), lambda j: (0, j)),
                  pl.BlockSpec((1, tc), lambda j: (0, j))],
        out_specs=pl.BlockSpec((S, tc), lambda j: (0, j)),
        out_shape=jax.ShapeDtypeStruct((S, CW), out_dtype),
        scratch_shapes=[pltpu.VMEM((S + CONV_PAD, tc), f32)],
        compiler_params=_cp("parallel"))(x, w, b)


def conv_bwd(name, x, xoff, w, b, dout, act, dx_dtype):
    S = x.shape[0]
    K, CW = w.shape
    tc = CONV_CH

    def body(x_ref, w_ref, b_ref, d_ref, dx_ref, dw_ref, db_ref, xp, dp):
        xp[0:CONV_PAD, :] = jnp.zeros((CONV_PAD, tc), f32)
        xp[CONV_PAD:CONV_PAD + S, :] = x_ref[...].astype(f32)
        dp[S:S + CONV_PAD, :] = jnp.zeros((CONV_PAD, tc), f32)
        db = jnp.zeros((1, tc), f32)
        for r0 in range(0, S, CONV_ROWS):
            d = d_ref[r0:r0 + CONV_ROWS, :].astype(f32)
            if act:
                pre = _conv_taps(xp, w_ref, b_ref, r0, K)
                s = _sigmoid(pre)
                d = d * (s + pre * s * (1.0 - s))
            dp[r0:r0 + CONV_ROWS, :] = d
            db = db + jnp.sum(d, axis=0, keepdims=True)
        db_ref[...] = db
        for r0 in range(0, S, CONV_ROWS):
            acc = jnp.zeros((CONV_ROWS, tc), f32)
            for k in range(K):
                acc = acc + w_ref[k:k + 1, :] * dp[r0 + (K - 1 - k):r0 + (K - 1 - k) + CONV_ROWS, :]
            dx_ref[r0:r0 + CONV_ROWS, :] = acc.astype(dx_ref.dtype)
        for k in range(K):
            acc = jnp.zeros((1, tc), f32)
            for r0 in range(0, S, CONV_ROWS):
                lo = r0 + CONV_PAD - (K - 1) + k
                acc = acc + jnp.sum(dp[r0:r0 + CONV_ROWS, :] * xp[lo:lo + CONV_ROWS, :], axis=0, keepdims=True)
            dw_ref[k:k + 1, :] = acc

    return pl.pallas_call(
        body, name=name, grid=(CW // tc,),
        in_specs=[pl.BlockSpec((S, tc), lambda j: (0, j + xoff)),
                  pl.BlockSpec((K, tc), lambda j: (0, j)),
                  pl.BlockSpec((1, tc), lambda j: (0, j)),
                  pl.BlockSpec((S, tc), lambda j: (0, j))],
        out_specs=[pl.BlockSpec((S, tc), lambda j: (0, j)),
                   pl.BlockSpec((K, tc), lambda j: (0, j)),
                   pl.BlockSpec((1, tc), lambda j: (0, j))],
        out_shape=[jax.ShapeDtypeStruct((S, CW), dx_dtype),
                   jax.ShapeDtypeStruct((K, CW), f32),
                   jax.ShapeDtypeStruct((1, CW), f32)],
        scratch_shapes=[pltpu.VMEM((S + CONV_PAD, tc), f32), pltpu.VMEM((S + CONV_PAD, tc), f32)],
        compiler_params=_cp("parallel"))(x, w, b, dout)


def loss_head(name, y, target, tm):
    S, D = y.shape

    def body(y_ref, t_ref, dy_ref, l_ref):
        @pl.when(pl.program_id(0) == 0)
        def _():
            l_ref[...] = jnp.zeros_like(l_ref)

        err = y_ref[...] - t_ref[...]
        dy_ref[...] = err * (1.0 / D)
        l_ref[...] += jnp.sum(err * err) * (0.5 / D)

    dy, l = pl.pallas_call(
        body, name=name, grid=(S // tm,),
        in_specs=[pl.BlockSpec((tm, D), lambda i: (i, 0))] * 2,
        out_specs=[pl.BlockSpec((tm, D), lambda i: (i, 0)), pl.BlockSpec((8, LANES), lambda i: (0, 0))],
        out_shape=[jax.ShapeDtypeStruct((S, D), f32), jax.ShapeDtypeStruct((8, LANES), f32)],
        compiler_params=_cp("arbitrary"))(y, target)
    return dy, l[0, 0]


def _flat2d_tiles(rows, cols, itemsize, target_bytes):
    tc = _tile(cols, 1024) if cols % LANES == 0 else cols
    cap = max(8, target_bytes // (tc * itemsize))
    tr = _tile(rows, cap, 16) if rows % 16 == 0 else rows
    return tr, tc


def adamw(name, w, g, m, v):
    shape = w.shape
    cols = shape[-1]
    rows = math.prod(shape[:-1])
    tr, tc = _flat2d_tiles(rows, cols, 4, 1 << 20)
    c1 = 1.0 - ADAM_B1 ** ADAM_STEP
    c2 = 1.0 - ADAM_B2 ** ADAM_STEP

    def body(w_ref, g_ref, m_ref, v_ref, d_ref, nm_ref, nv_ref):
        gg = g_ref[...]
        nm = ADAM_B1 * m_ref[...] + (1.0 - ADAM_B1) * gg
        nv = ADAM_B2 * v_ref[...] + (1.0 - ADAM_B2) * (gg * gg)
        d_ref[...] = -ADAM_LR * ((nm / c1) / (jnp.sqrt(nv / c2) + ADAM_EPS) + ADAM_WD * w_ref[...])
        nm_ref[...] = nm
        nv_ref[...] = nv

    spec = pl.BlockSpec((tr, tc), lambda i, j: (i, j))
    outs = pl.pallas_call(
        body, name=name, grid=(rows // tr, cols // tc), in_specs=[spec] * 4, out_specs=[spec] * 3,
        out_shape=[jax.ShapeDtypeStruct((rows, cols), f32)] * 3,
        compiler_params=_cp("parallel", "parallel"))(*[a.reshape(rows, cols) for a in (w, g, m, v)])
    return [o.reshape(shape) for o in outs]


def _core_index():
    return lax.axis_index("c").astype(jnp.int32).reshape(1)


def rs_add(name, G, buf):
    P, L, R, C = G.shape
    Lh = L // 2
    tr = _tile(R, max(16, (2 << 20) // (C * 2)), 16)

    def body(c_ref, g_ref, b_ref, o_ref):
        o_ref[...] = (g_ref[...].astype(f32) + b_ref[...].astype(f32)).astype(o_ref.dtype)

    blk = (None, None, tr, C)
    return pl.pallas_call(
        body, name=name,
        grid_spec=pltpu.PrefetchScalarGridSpec(
            num_scalar_prefetch=1, grid=(P, Lh, R // tr),
            in_specs=[pl.BlockSpec(blk, lambda p, l, i, c: (p, c[0] * Lh + l, i, 0)),
                      pl.BlockSpec(blk, lambda p, l, i, c: (p, l, i, 0))],
            out_specs=pl.BlockSpec(blk, lambda p, l, i, c: (p, l, i, 0))),
        out_shape=jax.ShapeDtypeStruct((P, Lh, R, C), bf16),
        compiler_params=_cp("parallel", "parallel", "parallel"))(_core_index(), G, buf)


def rs_sum4(name, buf):
    P, Lh, R, C = buf.shape
    tr = _tile(R, max(16, (2 << 20) // (C * 2)), 16)

    def body(b0, b1, b2, b3, o_ref):
        o_ref[...] = ((b0[...].astype(f32) + b1[...].astype(f32)) + b2[...].astype(f32)) + b3[...].astype(f32)

    blk = (None, None, tr, C)
    return pl.pallas_call(
        body, name=name, grid=(Lh, R // tr),
        in_specs=[pl.BlockSpec(blk, lambda l, i, p=p: (p, l, i, 0)) for p in range(P)],
        out_specs=pl.BlockSpec((None, tr, C), lambda l, i: (l, i, 0)),
        out_shape=jax.ShapeDtypeStruct((Lh, R, C), f32),
        compiler_params=_cp("parallel", "parallel"))(buf, buf, buf, buf)


_ANY = pl.BlockSpec(memory_space=pl.ANY)
_CHIP_FLIPS = ((1, 0), (0, 1), (1, 1))


def _place():
    return lax.axis_index("x"), lax.axis_index("y"), lax.axis_index("c")


def _flip(v, f):
    return 1 - v if f else v


def _remote(src, dst, ssem, rsem, dev):
    return pltpu.make_async_remote_copy(src_ref=src, dst_ref=dst, send_sem=ssem, recv_sem=rsem,
                                        device_id=dev, device_id_type=MESH)


def all_gather_weights(name, ws):
    T = len(ws)

    def body(*refs):
        w, o = refs[:T], refs[T:2 * T]
        send, recv, send2, recv2, lsem = refs[2 * T:]
        x, y, c = _place()
        q = 2 * x + y
        sib = (x, y, 1 - c)
        local = [pltpu.make_async_copy(w[t], o[t].at[q], lsem.at[t]) for t in range(T)]
        for cp in local:
            cp.start()
        sends = []
        for t in range(T):
            Lh = w[t].shape[0] // 2
            mine = pl.ds(c * Lh, Lh)
            for k, (fx, fy) in enumerate(_CHIP_FLIPS):
                cp = _remote(w[t].at[mine], o[t].at[q, mine], send.at[t, k], recv.at[t, k],
                             (_flip(x, fx), _flip(y, fy), c))
                cp.start()
                sends.append(cp)
        for t in range(T):
            Lh = w[t].shape[0] // 2
            mine = pl.ds(c * Lh, Lh)
            for k, (fx, fy) in enumerate(_CHIP_FLIPS):
                qk = 2 * _flip(x, fx) + _flip(y, fy)
                slab = o[t].at[qk, mine]
                _remote(slab, slab, send.at[t, k], recv.at[t, k], sib).wait_recv()
                cp = _remote(slab, slab, send2.at[t, k], recv2.at[t, k], sib)
                cp.start()
                sends.append(cp)
        for t in range(T):
            Lh = w[t].shape[0] // 2
            other = pl.ds((1 - c) * Lh, Lh)
            for k, (fx, fy) in enumerate(_CHIP_FLIPS):
                qk = 2 * _flip(x, fx) + _flip(y, fy)
                slab = o[t].at[qk, other]
                _remote(slab, slab, send2.at[t, k], recv2.at[t, k], sib).wait_recv()
        for cp in sends:
            cp.wait_send()
        for cp in local:
            cp.wait()

    return pl.pallas_call(
        body, name=name, in_specs=[_ANY] * T, out_specs=[_ANY] * T,
        out_shape=[jax.ShapeDtypeStruct((N_CHIPS,) + w.shape, w.dtype) for w in ws],
        scratch_shapes=[pltpu.SemaphoreType.DMA((T, 3)), pltpu.SemaphoreType.DMA((T, 3)),
                        pltpu.SemaphoreType.DMA((T, 3)), pltpu.SemaphoreType.DMA((T, 3)),
                        pltpu.SemaphoreType.DMA((T,))],
        )(*ws)


def rs_sibling(name, Gs):
    T = len(Gs)

    def body(*refs):
        g, o = refs[:T], refs[T:2 * T]
        send, recv = refs[2 * T:]
        x, y, c = _place()
        cps = []
        for t in range(T):
            Lh = g[t].shape[1] // 2
            cp = _remote(g[t].at[:, pl.ds((1 - c) * Lh, Lh)], o[t], send.at[t], recv.at[t], (x, y, 1 - c))
            cp.start()
            cps.append(cp)
        for cp in cps:
            cp.wait()

    return pl.pallas_call(
        body, name=name, in_specs=[_ANY] * T, out_specs=[_ANY] * T,
        out_shape=[jax.ShapeDtypeStruct((G.shape[0], G.shape[1] // 2) + G.shape[2:], G.dtype) for G in Gs],
        scratch_shapes=[pltpu.SemaphoreType.DMA((T,)), pltpu.SemaphoreType.DMA((T,))],
        )(*Gs)


def rs_chips(name, Ss):
    T = len(Ss)

    def body(*refs):
        s, o = refs[:T], refs[T:2 * T]
        send, recv, lsem = refs[2 * T:]
        x, y, c = _place()
        q = 2 * x + y
        local = [pltpu.make_async_copy(s[t].at[q], o[t].at[q], lsem.at[t]) for t in range(T)]
        for cp in local:
            cp.start()
        cps = []
        for t in range(T):
            for k, (fx, fy) in enumerate(_CHIP_FLIPS):
                px, py = _flip(x, fx), _flip(y, fy)
                cp = _remote(s[t].at[2 * px + py], o[t].at[q], send.at[t, k], recv.at[t, k], (px, py, c))
                cp.start()
                cps.append(cp)
        for t in range(T):
            for k, (fx, fy) in enumerate(_CHIP_FLIPS):
                slab = o[t].at[2 * _flip(x, fx) + _flip(y, fy)]
                _remote(slab, slab, send.at[t, k], recv.at[t, k], (x, y, c)).wait_recv()
        for cp in cps:
            cp.wait_send()
        for cp in local:
            cp.wait()

    return pl.pallas_call(
        body, name=name, in_specs=[_ANY] * T, out_specs=[_ANY] * T,
        out_shape=[jax.ShapeDtypeStruct(S.shape, S.dtype) for S in Ss],
        scratch_shapes=[pltpu.SemaphoreType.DMA((T, 3)), pltpu.SemaphoreType.DMA((T, 3)),
                        pltpu.SemaphoreType.DMA((T,))],
        )(*Ss)


def rs_share(name, tots):
    T = len(tots)

    def body(*refs):
        s, o = refs[:T], refs[T:2 * T]
        send, recv, lsem = refs[2 * T:]
        x, y, c = _place()
        cps, local = [], []
        for t in range(T):
            Lh = s[t].shape[0]
            mine = o[t].at[pl.ds(c * Lh, Lh)]
            lc = pltpu.make_async_copy(s[t], mine, lsem.at[t])
            lc.start()
            local.append(lc)
            cp = _remote(s[t], mine, send.at[t], recv.at[t], (x, y, 1 - c))
            cp.start()
            cps.append(cp)
        for t in range(T):
            Lh = s[t].shape[0]
            other = o[t].at[pl.ds((1 - c) * Lh, Lh)]
            _remote(other, other, send.at[t], recv.at[t], (x, y, 1 - c)).wait_recv()
        for cp in cps:
            cp.wait_send()
        for cp in local:
            cp.wait()

    return pl.pallas_call(
        body, name=name, in_specs=[_ANY] * T, out_specs=[_ANY] * T,
        out_shape=[jax.ShapeDtypeStruct((2 * s.shape[0],) + s.shape[1:], s.dtype) for s in tots],
        scratch_shapes=[pltpu.SemaphoreType.DMA((T,)), pltpu.SemaphoreType.DMA((T,)),
                        pltpu.SemaphoreType.DMA((T,))],
        )(*tots)


def all_reduce_small(name, vec):
    rows = vec.shape[0]
    flips = [(fx, fy, fc) for fx in (0, 1) for fy in (0, 1) for fc in (0, 1)][1:]

    def body(v_ref, o_ref, buf, send, recv):
        x, y, c = _place()
        me = 4 * x + 2 * y + c
        buf[me] = v_ref[...]
        cps = []
        for k, (fx, fy, fc) in enumerate(flips):
            cp = _remote(buf.at[me], buf.at[me], send.at[k], recv.at[k],
                         (_flip(x, fx), _flip(y, fy), _flip(c, fc)))
            cp.start()
            cps.append(cp)
        for k, (fx, fy, fc) in enumerate(flips):
            slab = buf.at[4 * _flip(x, fx) + 2 * _flip(y, fy) + _flip(c, fc)]
            _remote(slab, slab, send.at[k], recv.at[k], (x, y, c)).wait_recv()
        for cp in cps:
            cp.wait_send()
        acc = buf[0]
        for d in range(1, N_DEV):
            acc = acc + buf[d]
        o_ref[...] = acc

    return pl.pallas_call(
        body, name=name,
        in_specs=[pl.BlockSpec(memory_space=pltpu.VMEM)], out_specs=pl.BlockSpec(memory_space=pltpu.VMEM),
        out_shape=jax.ShapeDtypeStruct((rows, LANES), f32),
        scratch_shapes=[pltpu.VMEM((N_DEV, rows, LANES), f32),
                        pltpu.SemaphoreType.DMA((N_DEV - 1,)), pltpu.SemaphoreType.DMA((N_DEV - 1,))],
        compiler_params=pltpu.CompilerParams(vmem_limit_bytes=V7X_VMEM_LIMIT))(vec)


def _pack(arrays):
    flat = jnp.concatenate([a.reshape(-1) for a in arrays])
    n = flat.shape[0]
    rows = -(-n // (8 * LANES)) * 8
    return jnp.pad(flat, (0, rows * LANES - n)).reshape(rows, LANES)


def _unpack(vec, shapes):
    flat = vec.reshape(-1)
    out, pos = [], 0
    for s in shapes:
        n = math.prod(s)
        out.append(flat[pos:pos + n].reshape(s))
        pos += n
    return out


def _f_first(x, g):
    return x, _rms(x, g)


def _f_mid(h, m, gp, gn):
    h1 = h + _rms(m, gp)
    return h1, _rms(h1, gn)


def _f_mid_bias(h, m, b, gp, gn):
    h1 = h + _rms(m + b, gp)
    return h1, _rms(h1, gn)


def _f_last(h, m, gp):
    return (h + _rms(m, gp),)


def _f_swiglu(gate, up):
    return (_silu(gate) * up,)


def _f_glu(a, g, ba, bg):
    return ((a + ba) * _sigmoid(g + bg),)


def _f_ln_silu(x, g, b):
    mu = jnp.mean(x, axis=-1, keepdims=True)
    xc = x - mu
    y = xc * lax.rsqrt(jnp.mean(xc * xc, axis=-1, keepdims=True) + LN_EPS) * g + b
    return (_silu(y),)


def _f_lower_bounds(logits):
    n = logits.shape[0]
    e = jnp.exp(logits - jnp.max(logits, axis=0, keepdims=True))
    p = e / jnp.sum(e, axis=0, keepdims=True)
    layer = lax.broadcasted_iota(jnp.int32, logits.shape, 0)
    out = -jnp.broadcast_to(p[0:1, :], logits.shape)
    for j in range(n):
        out = out + jnp.where(layer >= j, p[j:j + 1, :], 0.0)
    return (out,)


WEIGHT_NAMES = ['mix_pre_g', 'mix_post_g', 'ffn_pre_g', 'ffn_post_g', 'hgrn_lb_logits', 'even_w_in',
                'hgrn_norm_g', 'ssd_conv_w', 'ssd_conv_b', 'ssd_dt_bias', 'ssd_a_log', 'ssd_d', 'ssd_norm_g',
                'even_w_out', 'conf_w1', 'conf_b1', 'conf_dw_w', 'conf_dw_b', 'conf_ln_g', 'conf_ln_b',
                'conf_w2', 'conf_b2', 'ffn_w_gate', 'ffn_w_up', 'ffn_w_down']
BIG = ['even_w_in', 'even_w_out', 'conf_w1', 'conf_w2', 'ffn_w_gate', 'ffn_w_up', 'ffn_w_down']
SMALL_SHARDED = {'ssd_conv_w': 2, 'conf_b1': 1, 'conf_dw_w': 2, 'conf_dw_b': 1, 'conf_ln_g': 1,
                 'conf_ln_b': 1, 'conf_b2': 1}


def _train_step(x, target, w, m, v):
    S, D = x.shape[1], x.shape[2]
    x2, t2 = x[0], target[0]
    NL = w['mix_pre_g'].shape[0]
    HB = w['ssd_dt_bias'].shape[1]
    GN = B_GROUPS * B_STATE
    xw, yw, cw = _place()
    chip = 2 * xw + yw
    tm = _tile(S, 128, 8)
    row1 = lambda a, i: a[i:i + 1]

    sharded = list(SMALL_SHARDED)
    placed = []
    for n in sharded:
        ax, a = SMALL_SHARDED[n], w[n]
        full = jnp.zeros(a.shape[:ax] + (a.shape[ax] * N_CHIPS,) + a.shape[ax + 1:], f32)
        start = [0] * a.ndim
        start[ax] = chip * a.shape[ax]
        placed.append(lax.dynamic_update_slice(full, jnp.where(cw == 0, a, 0.0), start))
    whole = dict(zip(sharded, _unpack(all_reduce_small("gather_small", _pack(placed)), [p.shape for p in placed])))
    small = {n: whole.get(n, w[n]) for n in WEIGHT_NAMES if n not in BIG}

    gathered = all_gather_weights("gather_weights", [w[n].astype(bf16) for n in BIG])
    W = dict(zip(BIG, gathered))
    win = jnp.concatenate([W['even_w_in'][j] for j in range(N_CHIPS)], axis=-1)
    WM = 6 * D + 2 * GN
    w_main, w_dt = win[None, :, :, :WM], win[None, :, :, WM:]

    n_even = small['hgrn_lb_logits'].shape[0]
    (lbs,) = _stage_fwd("lower_bounds", _f_lower_bounds, [small['hgrn_lb_logits']], [], [(D, f32)], tm=n_even)
    saved = []
    h = x2
    (u,) = _stage_fwd("pre_norm", lambda a, g: (_rms(a, g),), [h], [row1(small['mix_pre_g'], 0)],
                      [(D, bf16)], tm=tm)
    for layer in range(NL):
        li = layer // 2
        r = {'h': h, 'u': u}
        if layer % 2 == 0:
            r['ymain'] = mm_nn_col("in_proj", u, w_main, li)
            r['dtr'] = mm_nn_col("in_proj_dt", u, w_dt, li)
            r['xact'] = conv_fwd("ssd_conv", r['ymain'], 5 * D // CONV_CH, small['ssd_conv_w'][li],
                                 row1(small['ssd_conv_b'], li), True, f32)
            o_a, r['hg_st'] = hgrn_fwd("hgrn", r['ymain'], row1(lbs, li), row1(small['hgrn_norm_g'], li), D)
            o_b, r['ssd_st'] = ssd_fwd("ssd", r['xact'], r['ymain'], r['dtr'], row1(small['ssd_dt_bias'], li),
                                       row1(small['ssd_a_log'], li), row1(small['ssd_d'], li),
                                       row1(small['ssd_norm_g'], li), D)
            r['mixed'] = jnp.concatenate([o_a, o_b], axis=1)
            r['m'] = mm_nn_row("out_proj", r['mixed'], W['even_w_out'], li)
            mid_fn, mid_par = _f_mid, []
        else:
            r['c1'] = mm_nn_col("conf_in", u, W['conf_w1'], li)
            b1 = row1(small['conf_b1'], li)
            tn = _tile(D, 512)
            (r['glu'],) = _stage_fwd("conf_glu", _f_glu, [r['c1'], (r['c1'], D // tn)], [b1, (b1, D // tn)],
                                     [(D, f32)], tm=tm, tn=tn)
            r['cc'] = conv_fwd("conf_conv", r['glu'], 0, small['conf_dw_w'][li], row1(small['conf_dw_b'], li),
                               False, f32)
            (r['c2'],) = _stage_fwd("conf_ln", _f_ln_silu, [r['cc']],
                                    [row1(small['conf_ln_g'], li), row1(small['conf_ln_b'], li)],
                                    [(D, bf16)], tm=tm)
            r['m'] = mm_nn_row("conf_out", r['c2'], W['conf_w2'], li)
            mid_fn, mid_par = _f_mid_bias, [row1(small['conf_b2'], li)]
        r['mid_fn'] = mid_fn
        r['mid_par'] = mid_par + [row1(small['mix_post_g'], layer), row1(small['ffn_pre_g'], layer)]
        r['h1'], r['u2'] = _stage_fwd("mid_norm", mid_fn, [h, r['m']], r['mid_par'], [(D, f32), (D, bf16)], tm=tm)
        r['gate'] = mm_nn_col("ffn_gate", r['u2'], W['ffn_w_gate'], layer)
        r['up'] = mm_nn_col("ffn_up", r['u2'], W['ffn_w_up'], layer)
        F = r['gate'].shape[1]
        tnf = _tile(F, 1024)
        (r['act'],) = _stage_fwd("swiglu", _f_swiglu, [r['gate'], r['up']], [], [(F, bf16)], tm=tm, tn=tnf)
        r['dn'] = mm_nn_row("ffn_down", r['act'], W['ffn_w_down'], layer)
        if layer + 1 < NL:
            r['end_fn'] = _f_mid
            r['end_par'] = [row1(small['ffn_post_g'], layer), row1(small['mix_pre_g'], layer + 1)]
            h, u = _stage_fwd("end_norm", _f_mid, [r['h1'], r['dn']], r['end_par'], [(D, f32), (D, bf16)], tm=tm)
        else:
            r['end_fn'] = _f_last
            r['end_par'] = [row1(small['ffn_post_g'], layer)]
            (h,) = _stage_fwd("last_norm", _f_last, [r['h1'], r['dn']], r['end_par'], [(D, f32)], tm=tm)
        saved.append(r)

    dy, loss_local = loss_head("loss_head", h, t2, tm)
    loss = lax.psum(loss_local, ("x", "y", "c"))

    gs = {n: jnp.zeros(small[n].shape, f32) for n in small}

    def put(n, i, val):
        gs[n] = gs[n].at[i].add(val.reshape(gs[n].shape[1:]))

    G = {n: lax.empty((N_CHIPS,) + w[n].shape, bf16) for n in BIG if n != 'even_w_in'}
    L2 = w['even_w_in'].shape[0]
    g_main = lax.empty((1, L2, D, WM), bf16)
    g_dt = lax.empty((1, L2, D, HB), bf16)

    dh = dy
    du_parts = None
    for layer in reversed(range(NL)):
        li = layer // 2
        r = saved[layer]
        cts = [[dh]] if du_parts is None else [[dh], du_parts]
        (dh1, d_dn), pg = _stage_bwd("end_norm_bwd", r['end_fn'], [r['h1'], r['dn']], r['end_par'], cts,
                                     [f32, bf16], tm=tm)
        put('ffn_post_g', layer, pg[0])
        if du_parts is not None:
            put('mix_pre_g', layer + 1, pg[1])
        d_act = mm_nt_row("ffn_down_dx", d_dn, W['ffn_w_down'], layer)
        G['ffn_w_down'] = mm_tn_row("ffn_down_dw", r['act'], d_dn, G['ffn_w_down'], layer)
        F = r['gate'].shape[1]
        (d_gate, d_up), _ = _stage_bwd("swiglu_bwd", _f_swiglu, [r['gate'], r['up']], [], [[d_act]],
                                       [bf16, bf16], tm=tm, tn=_tile(F, 1024))
        du_a = mm_nt_col("ffn_gate_dx", d_gate, W['ffn_w_gate'], layer)
        du_b = mm_nt_col("ffn_up_dx", d_up, W['ffn_w_up'], layer)
        G['ffn_w_gate'] = mm_tn_col("ffn_gate_dw", r['u2'], d_gate, G['ffn_w_gate'], layer)
        G['ffn_w_up'] = mm_tn_col("ffn_up_dw", r['u2'], d_up, G['ffn_w_up'], layer)
        (dh, dm), pg = _stage_bwd("mid_norm_bwd", r['mid_fn'], [r['h'], r['m']], r['mid_par'],
                                  [[dh1], [du_a, du_b]], [f32, bf16], tm=tm)
        put('mix_post_g', layer, pg[-2])
        put('ffn_pre_g', layer, pg[-1])
        if layer % 2 == 0:
            d_mixed = mm_nt_row("out_proj_dx", dm, W['even_w_out'], li)
            G['even_w_out'] = mm_tn_row("out_proj_dw", r['mixed'], dm, G['even_w_out'], li)
            dxs, dbm, dcm, dz, ddt, ddtb, dal, dds, dbn = ssd_bwd(
                "ssd_bwd", r['xact'], r['ymain'], r['dtr'], row1(small['ssd_dt_bias'], li),
                row1(small['ssd_a_log'], li), row1(small['ssd_d'], li), row1(small['ssd_norm_g'], li),
                r['ssd_st'], d_mixed, D)
            put('ssd_dt_bias', li, ddtb)
            put('ssd_a_log', li, dal)
            put('ssd_d', li, dds)
            put('ssd_norm_g', li, dbn)
            d_xact = jnp.concatenate([dxs, dbm, dcm], axis=1)
            d_xbc, dcw, dcb = conv_bwd("ssd_conv_bwd", r['ymain'], 5 * D // CONV_CH, small['ssd_conv_w'][li],
                                       row1(small['ssd_conv_b'], li), d_xact, True, bf16)
            put('ssd_conv_w', li, dcw)
            put('ssd_conv_b', li, dcb)
            dq, df, dv, dg, dlb, dan = hgrn_bwd("hgrn_bwd", r['ymain'], row1(lbs, li), row1(small['hgrn_norm_g'], li),
                                               r['hg_st'], d_mixed, D)
            put('hgrn_norm_g', li, dan)
            r['dlb'] = dlb
            d_main = jnp.concatenate([dq, df, dv, dg, dz, d_xbc], axis=1)
            du_parts = [mm_nt_col("in_proj_dx", d_main, w_main, li), mm_nt_col("in_proj_dt_dx", ddt, w_dt, li)]
            g_main = mm_tn_col("in_proj_dw", r['u'], d_main, g_main, li)
            g_dt = mm_tn_col("in_proj_dt_dw", r['u'], ddt, g_dt, li)
        else:
            put('conf_b2', li, pg[0])
            d_c2 = mm_nt_row("conf_out_dx", dm, W['conf_w2'], li)
            G['conf_w2'] = mm_tn_row("conf_out_dw", r['c2'], dm, G['conf_w2'], li)
            (d_cc,), pl_ = _stage_bwd("conf_ln_bwd", _f_ln_silu, [r['cc']],
                                      [row1(small['conf_ln_g'], li), row1(small['conf_ln_b'], li)],
                                      [[d_c2]], [f32], tm=tm)
            put('conf_ln_g', li, pl_[0])
            put('conf_ln_b', li, pl_[1])
            d_glu, ddw, ddb = conv_bwd("conf_conv_bwd", r['glu'], 0, small['conf_dw_w'][li],
                                       row1(small['conf_dw_b'], li), d_cc, False, f32)
            put('conf_dw_w', li, ddw)
            put('conf_dw_b', li, ddb)
            b1 = row1(small['conf_b1'], li)
            tn = _tile(D, 512)
            (da, dg_), pb = _stage_bwd("conf_glu_bwd", _f_glu, [r['c1'], (r['c1'], D // tn)], [b1, (b1, D // tn)],
                                       [[d_glu]], [bf16, bf16], tm=tm, tn=tn)
            put('conf_b1', li, jnp.concatenate([pb[0], pb[1]], axis=1))
            d_c1 = jnp.concatenate([da, dg_], axis=1)
            du_parts = [mm_nt_col("conf_in_dx", d_c1, W['conf_w1'], li)]
            G['conf_w1'] = mm_tn_col("conf_in_dw", r['u'], d_c1, G['conf_w1'], li)
    (grad_x2,), pg = _stage_bwd("pre_norm_bwd", _f_first, [x2], [row1(small['mix_pre_g'], 0)],
                                [[dh], du_parts], [f32], tm=tm)
    put('mix_pre_g', 0, pg[0])
    dlbs = jnp.concatenate([saved[2 * i]['dlb'] for i in range(n_even)], axis=0)
    (dlogits,), _ = _stage_bwd("lower_bounds_bwd", _f_lower_bounds, [small['hgrn_lb_logits']], [],
                               [[dlbs]], [f32], tm=n_even)
    gs['hgrn_lb_logits'] = dlogits

    names_s = [n for n in WEIGHT_NAMES if n not in BIG]
    summed = _unpack(all_reduce_small("reduce_small", _pack([gs[n] for n in names_s])),
                     [gs[n].shape for n in names_s])
    grads = {}
    for n, a in zip(names_s, summed):
        if n in SMALL_SHARDED:
            ax = SMALL_SHARDED[n]
            size = w[n].shape[ax]
            start = [0] * a.ndim
            start[ax] = chip * size
            a = lax.dynamic_slice(a, start, a.shape[:ax] + (size,) + a.shape[ax + 1:])
        grads[n] = a

    g_in = jnp.concatenate([g_main[0], g_dt[0]], axis=-1)
    CI = w['even_w_in'].shape[2]
    G['even_w_in'] = jnp.stack([g_in[:, :, j * CI:(j + 1) * CI] for j in range(N_CHIPS)])
    parts = [G[n] for n in BIG]
    from_sib = rs_sibling("reduce_sibling", parts)
    pair = [rs_add("reduce_add", a, b) for a, b in zip(parts, from_sib)]
    from_chips = rs_chips("reduce_chips", pair)
    tot = [rs_sum4("reduce_sum", b) for b in from_chips]
    for n, a in zip(BIG, rs_share("reduce_share", tot)):
        grads[n] = a

    delta, new_m, new_v = {}, {}, {}
    for n in WEIGHT_NAMES:
        delta[n], new_m[n], new_v[n] = adamw("adamw", w[n], grads[n], m[n], v[n])
    return (loss, grad_x2[None], *[grads[n] for n in WEIGHT_NAMES], *[delta[n] for n in WEIGHT_NAMES],
            *[new_m[n] for n in WEIGHT_NAMES], *[new_v[n] for n in WEIGHT_NAMES])


def kernel(x, mix_pre_g, mix_post_g, ffn_pre_g, ffn_post_g, hgrn_lb_logits, even_w_in, hgrn_norm_g, ssd_conv_w, ssd_conv_b, ssd_dt_bias, ssd_a_log, ssd_d, ssd_norm_g, even_w_out, conf_w1, conf_b1, conf_dw_w, conf_dw_b, conf_ln_g, conf_ln_b, conf_w2, conf_b2, ffn_w_gate, ffn_w_up, ffn_w_down, loss_target, m_mix_pre_g, m_mix_post_g, m_ffn_pre_g, m_ffn_post_g, m_hgrn_lb_logits, m_even_w_in, m_hgrn_norm_g, m_ssd_conv_w, m_ssd_conv_b, m_ssd_dt_bias, m_ssd_a_log, m_ssd_d, m_ssd_norm_g, m_even_w_out, m_conf_w1, m_conf_b1, m_conf_dw_w, m_conf_dw_b, m_conf_ln_g, m_conf_ln_b, m_conf_w2, m_conf_b2, m_ffn_w_gate, m_ffn_w_up, m_ffn_w_down, v_mix_pre_g, v_mix_post_g, v_ffn_pre_g, v_ffn_post_g, v_hgrn_lb_logits, v_even_w_in, v_hgrn_norm_g, v_ssd_conv_w, v_ssd_conv_b, v_ssd_dt_bias, v_ssd_a_log, v_ssd_d, v_ssd_norm_g, v_even_w_out, v_conf_w1, v_conf_b1, v_conf_dw_w, v_conf_dw_b, v_conf_ln_g, v_conf_ln_b, v_conf_w2, v_conf_b2, v_ffn_w_gate, v_ffn_w_up, v_ffn_w_down):
    args = locals()
    w = {n: args[n] for n in WEIGHT_NAMES}
    m = {n: args["m_" + n] for n in WEIGHT_NAMES}
    v = {n: args["v_" + n] for n in WEIGHT_NAMES}
    return _train_step(x, loss_target, w, m, v)
```

```python
import functools
import math

import jax
import jax.numpy as jnp
from jax import lax
from jax.experimental import pallas as pl
from jax.experimental.pallas import tpu as pltpu

f32 = jnp.float32
bf16 = jnp.bfloat16
MESH = pl.DeviceIdType.MESH
HI = lax.Precision.HIGHEST

A_HEAD = 128
A_CHUNK = 64
A_SUB = 16
A_F_MIN = 1e-6
B_HEAD = 64
B_GROUPS = 4
B_STATE = 128
B_CONV = 4
B_CHUNK = 128
C_KERNEL = 31
RMS_EPS = 1e-6
LN_EPS = 1e-5
ADAM_LR = 0.001
ADAM_B1 = 0.9
ADAM_B2 = 0.999
ADAM_EPS = 1e-08
ADAM_WD = 0.01
ADAM_STEP = 10

N_CHIPS = 4
N_DEV = 8
V7X_VMEM_LIMIT = 56 * 1024 * 1024
LANES = 128
CONV_PAD = 32
CONV_ROWS = 128
CONV_CH = 256

NN = (((1,), (0,)), ((), ()))
NT = (((1,), (1,)), ((), ()))
TN = (((0,), (0,)), ((), ()))


def _tile(n, cap, unit=LANES):
    best = None
    for t in range(unit, min(n, cap) + 1, unit):
        if n % t == 0:
            best = t
    return n if best is None else best


def _cp(*sem):
    return pltpu.CompilerParams(dimension_semantics=sem, vmem_limit_bytes=V7X_VMEM_LIMIT)


def _sigmoid(x):
    return jax.nn.sigmoid(x)


def _silu(x):
    return x * jax.nn.sigmoid(x)


def _rms(x, g):
    return x * lax.rsqrt(jnp.mean(x * x, axis=-1, keepdims=True) + RMS_EPS) * g


def _pair(a):
    return a if isinstance(a, tuple) else (a, 0)


def _stage(name, fn, rows, params, outs, par_outs=(), *, tm, tn=None):
    rows = [_pair(r) for r in rows]
    params = [_pair(p) for p in params]
    S = rows[0][0].shape[0]
    n_in, n_o = len(rows) + len(params), len(outs)
    if tn is None:
        grid = (S // tm,)
        in_specs = [pl.BlockSpec((tm, a.shape[1]), lambda i: (i, 0)) for a, _ in rows]
        in_specs += [pl.BlockSpec(a.shape, lambda i: (0, 0)) for a, _ in params]
        out_specs = [pl.BlockSpec((tm, w), lambda i: (i, 0)) for w, _ in outs]
        out_specs += [pl.BlockSpec((k, w), lambda i: (0, 0)) for k, w in par_outs]
        row_axis = 0
        sem = ("arbitrary",) if par_outs else ("parallel",)
    else:
        grid = (outs[0][0] // tn, S // tm)
        in_specs = [pl.BlockSpec((tm, tn), lambda j, i, o=o: (i, j + o)) for _, o in rows]
        in_specs += [pl.BlockSpec((a.shape[0], tn), lambda j, i, o=o: (0, j + o)) for a, o in params]
        out_specs = [pl.BlockSpec((tm, tn), lambda j, i: (i, j)) for _ in outs]
        out_specs += [pl.BlockSpec((k, tn), lambda j, i: (0, j)) for k, _ in par_outs]
        row_axis = 1
        sem = ("parallel", "arbitrary") if par_outs else ("parallel", "parallel")
    out_shape = [jax.ShapeDtypeStruct((S, w), d) for w, d in outs]
    out_shape += [jax.ShapeDtypeStruct((k, w), f32) for k, w in par_outs]

    def body(*refs):
        res = fn(*[r[...] for r in refs[:n_in]])
        for r, v in zip(refs[n_in:n_in + n_o], res[:n_o]):
            r[...] = v.astype(r.dtype)
        if par_outs:
            acc_refs = refs[n_in + n_o:]

            @pl.when(pl.program_id(row_axis) == 0)
            def _():
                for r in acc_refs:
                    r[...] = jnp.zeros_like(r)

            for r, v in zip(acc_refs, res[n_o:]):
                r[...] += v

    return pl.pallas_call(
        body, name=name, grid=grid, in_specs=in_specs, out_specs=out_specs, out_shape=out_shape,
        compiler_params=_cp(*sem))(*[a for a, _ in rows], *[a for a, _ in params])


def _stage_fwd(name, fn, rows, params, outs, *, tm, tn=None):
    n_r = len(rows)

    def ffn(*t):
        return fn(*[v.astype(f32) for v in t[:n_r]], *t[n_r:])

    return _stage(name, ffn, rows, params, outs, tm=tm, tn=tn)


def _stage_bwd(name, fn, rows, params, cts, drow, *, tm, tn=None):
    rows = [_pair(r) for r in rows]
    params = [_pair(p) for p in params]
    n_r, n_p = len(rows), len(params)
    flat_ct = [_pair(c) for group in cts for c in group]
    counts = [len(group) for group in cts]
    need = [i for i, d in enumerate(drow) if d is not None]

    def bfn(*t):
        r = [v.astype(f32) for v in t[:n_r]]
        c = t[n_r:n_r + len(flat_ct)]
        p = list(t[n_r + len(flat_ct):])
        res, vjp = jax.vjp(fn, *r, *p)
        ct, pos = [], 0
        for o, k in zip(res, counts):
            s = c[pos].astype(f32)
            for e in range(1, k):
                s = s + c[pos + e].astype(f32)
            pos += k
            ct.append(s.astype(o.dtype))
        g = vjp(tuple(ct))
        return tuple(g[i] for i in need) + tuple(g[n_r:])

    if tn is None:
        outs = [(rows[i][0].shape[1], drow[i]) for i in need]
        par_outs = [p.shape for p, _ in params]
    else:
        w_all = flat_ct[0][0].shape[1]
        outs = [(w_all, drow[i]) for i in need]
        par_outs = [(p.shape[0], w_all) for p, _ in params]
    res = _stage(name, bfn, rows + flat_ct, params, outs, par_outs, tm=tm, tn=tn)
    return res[:len(need)], res[len(need):]


def _mm(name, dims, a, b, grid, a_spec, b_spec, o_spec, out_shape, acc_shape, prev=None):
    nk = grid[2]

    def body(*refs):
        a_ref, b_ref = refs[0], refs[1]
        o_ref, acc = refs[-2], refs[-1]
        k = pl.program_id(2)

        @pl.when(k == 0)
        def _():
            acc[...] = jnp.zeros_like(acc)

        acc[...] += lax.dot_general(a_ref[...].astype(bf16), b_ref[...].astype(bf16), dims,
                                    preferred_element_type=f32)

        @pl.when(k == nk - 1)
        def _():
            o_ref[...] = acc[...].astype(o_ref.dtype)

    in_specs = [a_spec, b_spec]
    args = [a, b]
    alias = {}
    if prev is not None:
        in_specs.append(pl.BlockSpec(memory_space=pl.ANY))
        args.append(prev)
        alias = {2: 0}
    return pl.pallas_call(
        body, name=name, grid=grid, in_specs=in_specs, out_specs=o_spec, out_shape=out_shape,
        scratch_shapes=[pltpu.VMEM(acc_shape, f32)], input_output_aliases=alias,
        compiler_params=_cp("parallel", "parallel", "arbitrary"))(*args)


def _mm_tiles(S):
    return _tile(S, 512)


def mm_nn_col(name, a, W, li, out_dtype=f32):
    P, _, K, C = W.shape
    S = a.shape[0]
    tm, tn, tk = _mm_tiles(S), _tile(C, 1536), _tile(K, 2048)
    nc = C // tn
    return _mm(name, NN, a, W, (S // tm, P * nc, K // tk),
               pl.BlockSpec((tm, tk), lambda i, j, k: (i, k)),
               pl.BlockSpec((None, None, tk, tn), lambda i, j, k: (j // nc, li, k, j % nc)),
               pl.BlockSpec((tm, tn), lambda i, j, k: (i, j)),
               jax.ShapeDtypeStruct((S, P * C), out_dtype), (tm, tn))


def mm_nn_row(name, a, W, li, out_dtype=f32):
    P, _, R, N = W.shape
    S = a.shape[0]
    tm, tn, tk = _mm_tiles(S), _tile(N, 1024), _tile(R, 2048)
    nr = R // tk
    return _mm(name, NN, a, W, (S // tm, N // tn, P * nr),
               pl.BlockSpec((tm, tk), lambda i, j, k: (i, k)),
               pl.BlockSpec((None, None, tk, tn), lambda i, j, k: (k // nr, li, k % nr, j)),
               pl.BlockSpec((tm, tn), lambda i, j, k: (i, j)),
               jax.ShapeDtypeStruct((S, N), out_dtype), (tm, tn))


def mm_nt_col(name, dy, W, li, out_dtype=f32):
    P, _, K, C = W.shape
    S = dy.shape[0]
    tm, tn, tk = _mm_tiles(S), _tile(K, 1024), _tile(C, 2048)
    nc = C // tk
    return _mm(name, NT, dy, W, (S // tm, K // tn, P * nc),
               pl.BlockSpec((tm, tk), lambda i, j, k: (i, k)),
               pl.BlockSpec((None, None, tn, tk), lambda i, j, k: (k // nc, li, j, k % nc)),
               pl.BlockSpec((tm, tn), lambda i, j, k: (i, j)),
               jax.ShapeDtypeStruct((S, K), out_dtype), (tm, tn))


def mm_nt_row(name, dy, W, li, out_dtype=f32):
    P, _, R, N = W.shape
    S = dy.shape[0]
    tm, tn, tk = _mm_tiles(S), _tile(R, 1536), _tile(N, 2048)
    nr = R // tn
    return _mm(name, NT, dy, W, (S // tm, P * nr, N // tk),
               pl.BlockSpec((tm, tk), lambda i, j, k: (i, k)),
               pl.BlockSpec((None, None, tn, tk), lambda i, j, k: (j // nr, li, j % nr, k)),
               pl.BlockSpec((tm, tn), lambda i, j, k: (i, j)),
               jax.ShapeDtypeStruct((S, P * R), out_dtype), (tm, tn))


def mm_tn_col(name, a, dy, G, li):
    P, _, K, C = G.shape
    S = a.shape[0]
    tm, tn, tk = _tile(K, 512), _tile(C, 1536), _tile(S, 1024)
    nc = C // tn
    return _mm(name, TN, a, dy, (K // tm, P * nc, S // tk),
               pl.BlockSpec((tk, tm), lambda i, j, k: (k, i)),
               pl.BlockSpec((tk, tn), lambda i, j, k: (k, j)),
               pl.BlockSpec((None, None, tm, tn), lambda i, j, k: (j // nc, li, i, j % nc)),
               jax.ShapeDtypeStruct(G.shape, G.dtype), (tm, tn), prev=G)


def mm_tn_row(name, a, dy, G, li):
    P, _, R, N = G.shape
    S = a.shape[0]
    tm, tn, tk = _tile(R, 1536), _tile(N, 1024), _tile(S, 1024)
    nr = R // tm
    return _mm(name, TN, a, dy, (P * nr, N // tn, S // tk),
               pl.BlockSpec((tk, tm), lambda i, j, k: (k, i)),
               pl.BlockSpec((tk, tn), lambda i, j, k: (k, j)),
               pl.BlockSpec((None, None, tm, tn), lambda i, j, k: (i // nr, li, i % nr, j)),
               jax.ShapeDtypeStruct(G.shape, G.dtype), (tm, tn), prev=G)


def _hgrn_chunk(st, q, fp, v, gt, lb, an):
    C = q.shape[0]
    sig = _sigmoid(fp)
    f = lb + (1.0 - lb) * sig
    kk = (1.0 - lb) * (1.0 - sig)
    g = jnp.log(jnp.maximum(f, A_F_MIN))
    qs = _silu(q)
    row = lax.broadcasted_iota(jnp.int32, (C, C), 0)
    col = lax.broadcasted_iota(jnp.int32, (C, C), 1)
    tri = (col <= row).astype(f32)
    b = jnp.dot(tri, g, precision=HI, preferred_element_type=f32)
    o_inter = lax.dot_general((qs * jnp.exp(b)).astype(bf16), st.astype(bf16), NT,
                              preferred_element_type=f32)
    T = A_SUB
    t3 = lax.broadcasted_iota(jnp.int32, (T, T, A_HEAD), 0)
    s3 = lax.broadcasted_iota(jnp.int32, (T, T, A_HEAD), 1)
    m3 = s3 <= t3
    vb = v.astype(bf16)
    blocks = []
    for i in range(C // T):
        lo = i * T
        b_i, q_i, k_i = b[lo:lo + T], qs[lo:lo + T], kk[lo:lo + T]
        diff = b_i[:, None, :] - b_i[None, :, :]
        dec = jnp.where(m3, jnp.exp(jnp.where(m3, diff, 0.0)), 0.0)
        s_ii = jnp.sum(q_i[:, None, :] * k_i[None, :, :] * dec, axis=-1)
        o_i = jnp.dot(s_ii.astype(bf16), vb[lo:lo + T], preferred_element_type=f32)
        if i > 0:
            start = b[lo - 1:lo, :]
            q_t = q_i * jnp.exp(b_i - start)
            k_t = kk[0:lo] * jnp.exp(start - b[0:lo])
            s_off = lax.dot_general(q_t.astype(bf16), k_t.astype(bf16), NT, preferred_element_type=f32)
            o_i = o_i + jnp.dot(s_off.astype(bf16), vb[0:lo], preferred_element_type=f32)
        blocks.append(o_i)
    o_intra = jnp.concatenate(blocks, axis=0)
    bl = b[C - 1:C, :]
    kd = kk * jnp.exp(bl - b)
    st_new = st * jnp.exp(bl) + lax.dot_general(v.astype(bf16), kd.astype(bf16), TN,
                                                preferred_element_type=f32)
    o = o_inter + o_intra
    y = o * lax.rsqrt(jnp.mean(o * o, axis=-1, keepdims=True) + RMS_EPS) * an * _silu(gt)
    return st_new, y


def _hgrn_in_specs(HA, cidx):
    C = A_CHUNK
    specs = [pl.BlockSpec((C, A_HEAD), lambda h, c, s=s: (cidx(c), s * HA + h)) for s in range(4)]
    specs += [pl.BlockSpec((1, A_HEAD), lambda h, c: (0, h))] * 2
    return specs


def hgrn_fwd(name, ymain, lb, an, D):
    S = ymain.shape[0]
    HA, nc = D // A_HEAD, S // A_CHUNK

    def body(q, fp, v, gt, lb_ref, an_ref, o_ref, sv_ref, st):
        @pl.when(pl.program_id(1) == 0)
        def _():
            st[...] = jnp.zeros_like(st)

        sv_ref[...] = st[...]
        st_new, y = _hgrn_chunk(st[...], q[...], fp[...], v[...], gt[...], lb_ref[...], an_ref[...])
        st[...] = st_new
        o_ref[...] = y.astype(o_ref.dtype)

    return pl.pallas_call(
        body, name=name, grid=(HA, nc),
        in_specs=_hgrn_in_specs(HA, lambda c: c),
        out_specs=[pl.BlockSpec((A_CHUNK, A_HEAD), lambda h, c: (c, h)),
                   pl.BlockSpec((None, None, A_HEAD, A_HEAD), lambda h, c: (h, c, 0, 0))],
        out_shape=[jax.ShapeDtypeStruct((S, D), bf16),
                   jax.ShapeDtypeStruct((HA, nc, A_HEAD, A_HEAD), f32)],
        scratch_shapes=[pltpu.VMEM((A_HEAD, A_HEAD), f32)],
        compiler_params=_cp("parallel", "arbitrary"))(ymain, ymain, ymain, ymain, lb, an)


def hgrn_bwd(name, ymain, lb, an, saved, dmixed, D):
    S = ymain.shape[0]
    HA, nc = D // A_HEAD, S // A_CHUNK
    rev = lambda c: nc - 1 - c

    def body(q, fp, v, gt, lb_ref, an_ref, sv_ref, do_ref, dq, df, dv, dg, dlb, dan, dst):
        @pl.when(pl.program_id(1) == 0)
        def _():
            dst[...] = jnp.zeros_like(dst)
            dlb[...] = jnp.zeros_like(dlb)
            dan[...] = jnp.zeros_like(dan)

        _, vjp = jax.vjp(_hgrn_chunk, sv_ref[...], q[...], fp[...], v[...], gt[...], lb_ref[...], an_ref[...])
        g = vjp((dst[...], do_ref[...].astype(f32)))
        dst[...] = g[0]
        for r, x in zip((dq, df, dv, dg), g[1:5]):
            r[...] = x.astype(r.dtype)
        dlb[...] += g[5]
        dan[...] += g[6]

    blk = pl.BlockSpec((A_CHUNK, A_HEAD), lambda h, c: (rev(c), h))
    vec = pl.BlockSpec((1, A_HEAD), lambda h, c: (0, h))
    return pl.pallas_call(
        body, name=name, grid=(HA, nc),
        in_specs=_hgrn_in_specs(HA, rev) + [
            pl.BlockSpec((None, None, A_HEAD, A_HEAD), lambda h, c: (h, rev(c), 0, 0)), blk],
        out_specs=[blk] * 4 + [vec] * 2,
        out_shape=[jax.ShapeDtypeStruct((S, D), bf16)] * 4 + [jax.ShapeDtypeStruct((1, D), f32)] * 2,
        scratch_shapes=[pltpu.VMEM((A_HEAD, A_HEAD), f32)],
        compiler_params=_cp("parallel", "arbitrary"))(ymain, ymain, ymain, ymain, lb, an, saved, dmixed)


def _ssd_chunk(hp, xs, bm, cm, z, dtr, dtb, alog, dsk, bn, g, R):
    L, GW = xs.shape
    HB = dtr.shape[1]
    R8 = max(R, 8)
    dt = jax.nn.softplus(dtr + dtb)
    a = -jnp.exp(alog)
    row = lax.broadcasted_iota(jnp.int32, (L, L), 0)
    col = lax.broadcasted_iota(jnp.int32, (L, L), 1)
    causal = col <= row
    cs = jnp.dot(causal.astype(f32), dt * a, precision=HI, preferred_element_type=f32)
    eh = lax.broadcasted_iota(jnp.int32, (HB, GW), 0)
    ec = lax.broadcasted_iota(jnp.int32, (HB, GW), 1)
    spread = (eh == g * R + ec // B_HEAD).astype(f32)
    sh = lax.broadcasted_iota(jnp.int32, (R8, HB), 1)
    sr = lax.broadcasted_iota(jnp.int32, (R8, HB), 0)
    pick_t = jnp.logical_and(sh == g * R + sr, sr < R).astype(f32)
    dtf = jnp.dot(dt, spread, precision=HI, preferred_element_type=f32)
    csf = jnp.dot(cs, spread, precision=HI, preferred_element_type=f32)
    dsf = jnp.dot(jnp.broadcast_to(dsk, (8, HB)), spread, precision=HI, preferred_element_type=f32)[0:1, :]
    cs_col = lax.dot_general(cs, pick_t, NT, precision=HI, preferred_element_type=f32)
    cs_row = lax.dot_general(pick_t, cs, NT, precision=HI, preferred_element_type=f32)
    xdt = xs * dtf
    cb = lax.dot_general(cm.astype(bf16), bm.astype(bf16), NT, preferred_element_type=f32)
    lane_head = lax.broadcasted_iota(jnp.int32, (1, GW), 1) // B_HEAD
    y = jnp.zeros((L, GW), f32)
    for r in range(R):
        seg = cs_col[:, r:r + 1] - cs_row[r:r + 1, :]
        dec = jnp.where(causal, jnp.exp(jnp.where(causal, seg, 0.0)), 0.0)
        xm = jnp.where(lane_head == r, xdt, 0.0)
        y = y + jnp.dot((cb * dec).astype(bf16), xm.astype(bf16), preferred_element_type=f32)
    csl = csf[L - 1:L, :]
    dte = jnp.exp(csl - csf)
    states = lax.dot_general(bm.astype(bf16), (xdt * dte).astype(bf16), TN, preferred_element_type=f32)
    y_off = jnp.dot(cm.astype(bf16), hp.astype(bf16), preferred_element_type=f32) * jnp.exp(csf)
    hn = hp * jnp.exp(csl) + states
    gated = (y + y_off + dsf * xs) * _silu(z)
    out = gated * lax.rsqrt(jnp.mean(gated * gated, axis=-1, keepdims=True) + RMS_EPS) * bn
    return hn, out


def _ssd_in_specs(D, HB, cidx):
    L, GW, N = B_CHUNK, D // B_GROUPS, B_STATE
    zoff, boff = 4 * D // GW, D // N
    return [
        pl.BlockSpec((L, GW), lambda c, g: (cidx(c), g)),
        pl.BlockSpec((L, N), lambda c, g: (cidx(c), boff + g)),
        pl.BlockSpec((L, N), lambda c, g: (cidx(c), boff + B_GROUPS + g)),
        pl.BlockSpec((L, GW), lambda c, g: (cidx(c), zoff + g)),
        pl.BlockSpec((L, HB), lambda c, g: (cidx(c), 0)),
        pl.BlockSpec((1, HB), lambda c, g: (0, 0)),
        pl.BlockSpec((1, HB), lambda c, g: (0, 0)),
        pl.BlockSpec((1, HB), lambda c, g: (0, 0)),
        pl.BlockSpec((1, GW), lambda c, g: (0, g)),
    ]


def ssd_fwd(name, xact, ymain, dtr, dtb, alog, dsk, bn, D):
    S, HB = dtr.shape
    nc, GW, R = S // B_CHUNK, D // B_GROUPS, HB // B_GROUPS

    def body(xs, bm, cm, z, dt_ref, dtb_ref, al_ref, ds_ref, bn_ref, o_ref, sv_ref, hs):
        g = pl.program_id(1)

        @pl.when(pl.program_id(0) == 0)
        def _():
            hs[g] = jnp.zeros((B_STATE, GW), f32)

        hp = hs[g]
        sv_ref[...] = hp
        hn, out = _ssd_chunk(hp, xs[...], bm[...], cm[...], z[...], dt_ref[...], dtb_ref[...], al_ref[...],
                             ds_ref[...], bn_ref[...], g, R)
        hs[g] = hn
        o_ref[...] = out.astype(o_ref.dtype)

    return pl.pallas_call(
        body, name=name, grid=(nc, B_GROUPS),
        in_specs=_ssd_in_specs(D, HB, lambda c: c),
        out_specs=[pl.BlockSpec((B_CHUNK, GW), lambda c, g: (c, g)),
                   pl.BlockSpec((None, None, B_STATE, GW), lambda c, g: (c, g, 0, 0))],
        out_shape=[jax.ShapeDtypeStruct((S, D), bf16),
                   jax.ShapeDtypeStruct((nc, B_GROUPS, B_STATE, GW), f32)],
        scratch_shapes=[pltpu.VMEM((B_GROUPS, B_STATE, GW), f32)],
        compiler_params=_cp("arbitrary", "arbitrary"))(xact, xact, xact, ymain, dtr, dtb, alog, dsk, bn)


def ssd_bwd(name, xact, ymain, dtr, dtb, alog, dsk, bn, saved, dmixed, D):
    S, HB = dtr.shape
    nc, GW, R = S // B_CHUNK, D // B_GROUPS, HB // B_GROUPS
    rev = lambda c: nc - 1 - c
    ooff = D // GW

    def body(xs, bm, cm, z, dt_ref, dtb_ref, al_ref, ds_ref, bn_ref, sv_ref, do_ref,
             dxs, dbm, dcm, dz, ddt, ddtb, dal, dds, dbn, dhs):
        c, g = pl.program_id(0), pl.program_id(1)

        @pl.when(c == 0)
        def _():
            dhs[g] = jnp.zeros((B_STATE, GW), f32)
            dbn[g] = jnp.zeros((1, GW), f32)

        @pl.when(jnp.logical_and(c == 0, g == 0))
        def _():
            ddtb[...] = jnp.zeros_like(ddtb)
            dal[...] = jnp.zeros_like(dal)
            dds[...] = jnp.zeros_like(dds)

        @pl.when(g == 0)
        def _():
            ddt[...] = jnp.zeros_like(ddt)

        fn = functools.partial(_ssd_chunk, g=g, R=R)
        _, vjp = jax.vjp(fn, sv_ref[...], xs[...], bm[...], cm[...], z[...], dt_ref[...], dtb_ref[...],
                         al_ref[...], ds_ref[...], bn_ref[...])
        gr = vjp((dhs[g], do_ref[...].astype(f32)))
        dhs[g] = gr[0]
        dxs[...] = gr[1]
        dbm[...] = gr[2]
        dcm[...] = gr[3]
        dz[...] = gr[4].astype(dz.dtype)
        ddt[...] += gr[5]
        ddtb[...] += gr[6]
        dal[...] += gr[7]
        dds[...] += gr[8]
        dbn[g] += gr[9]

    hb_vec = pl.BlockSpec((1, HB), lambda c, g: (0, 0))
    return pl.pallas_call(
        body, name=name, grid=(nc, B_GROUPS),
        in_specs=_ssd_in_specs(D, HB, rev) + [
            pl.BlockSpec((None, None, B_STATE, GW), lambda c, g: (rev(c), g, 0, 0)),
            pl.BlockSpec((B_CHUNK, GW), lambda c, g: (rev(c), ooff + g))],
        out_specs=[pl.BlockSpec((B_CHUNK, GW), lambda c, g: (rev(c), g)),
                   pl.BlockSpec((B_CHUNK, B_STATE), lambda c, g: (rev(c), g)),
                   pl.BlockSpec((B_CHUNK, B_STATE), lambda c, g: (rev(c), g)),
                   pl.BlockSpec((B_CHUNK, GW), lambda c, g: (rev(c), g)),
                   pl.BlockSpec((B_CHUNK, HB), lambda c, g: (rev(c), 0)),
                   hb_vec, hb_vec, hb_vec,
                   pl.BlockSpec((B_GROUPS, 1, GW), lambda c, g: (0, 0, 0))],
        out_shape=[jax.ShapeDtypeStruct((S, D), f32),
                   jax.ShapeDtypeStruct((S, B_GROUPS * B_STATE), f32),
                   jax.ShapeDtypeStruct((S, B_GROUPS * B_STATE), f32),
                   jax.ShapeDtypeStruct((S, D), bf16),
                   jax.ShapeDtypeStruct((S, HB), f32),
                   jax.ShapeDtypeStruct((1, HB), f32), jax.ShapeDtypeStruct((1, HB), f32),
                   jax.ShapeDtypeStruct((1, HB), f32),
                   jax.ShapeDtypeStruct((B_GROUPS, 1, GW), f32)],
        scratch_shapes=[pltpu.VMEM((B_GROUPS, B_STATE, GW), f32)],
        compiler_params=_cp("arbitrary", "arbitrary"))(
            xact, xact, xact, ymain, dtr, dtb, alog, dsk, bn, saved, dmixed)


def _conv_taps(xp, w_ref, b_ref, r0, K):
    acc = jnp.broadcast_to(b_ref[...], (CONV_ROWS, b_ref.shape[1]))
    for k in range(K):
        acc = acc + w_ref[k:k + 1, :] * xp[r0 + CONV_PAD - (K - 1) + k:r0 + CONV_PAD - (K - 1) + k + CONV_ROWS, :]
    return acc


def conv_fwd(name, x, xoff, w, b, act, out_dtype):
    S = x.shape[0]
    K, CW = w.shape
    tc = CONV_CH

    def body(x_ref, w_ref, b_ref, o_ref, xp):
        xp[0:CONV_PAD, :] = jnp.zeros((CONV_PAD, tc), f32)
        xp[CONV_PAD:CONV_PAD + S, :] = x_ref[...].astype(f32)
        for r0 in range(0, S, CONV_ROWS):
            acc = _conv_taps(xp, w_ref, b_ref, r0, K)
            if act:
                acc = _silu(acc)
            o_ref[r0:r0 + CONV_ROWS, :] = acc.astype(o_ref.dtype)

    return pl.pallas_call(
        body, name=name, grid=(CW // tc,),
        in_specs=[pl.BlockSpec((S, tc), lambda j: (0, j + xoff)),
                  pl.BlockSpec((K, tc), lambda j: (0, j)),
                  pl.BlockSpec((1, tc), lambda j: (0, j))],
        out_specs=pl.BlockSpec((S, tc), lambda j: (0, j)),
        out_shape=jax.ShapeDtypeStruct((S, CW), out_dtype),
        scratch_shapes=[pltpu.VMEM((S + CONV_PAD, tc), f32)],
        compiler_params=_cp("parallel"))(x, w, b)


def conv_bwd(name, x, xoff, w, b, dout, act, dx_dtype):
    S = x.shape[0]
    K, CW = w.shape
    tc = CONV_CH

    def body(x_ref, w_ref, b_ref, d_ref, dx_ref, dw_ref, db_ref, xp, dp):
        xp[0:CONV_PAD, :] = jnp.zeros((CONV_PAD, tc), f32)
        xp[CONV_PAD:CONV_PAD + S, :] = x_ref[...].astype(f32)
        dp[S:S + CONV_PAD, :] = jnp.zeros((CONV_PAD, tc), f32)
        db = jnp.zeros((1, tc), f32)
        for r0 in range(0, S, CONV_ROWS):
            d = d_ref[r0:r0 + CONV_ROWS, :].astype(f32)
            if act:
                pre = _conv_taps(xp, w_ref, b_ref, r0, K)
                s = _sigmoid(pre)
                d = d * (s + pre * s * (1.0 - s))
            dp[r0:r0 + CONV_ROWS, :] = d
            db = db + jnp.sum(d, axis=0, keepdims=True)
        db_ref[...] = db
        for r0 in range(0, S, CONV_ROWS):
            acc = jnp.zeros((CONV_ROWS, tc), f32)
            for k in range(K):
                acc = acc + w_ref[k:k + 1, :] * dp[r0 + (K - 1 - k):r0 + (K - 1 - k) + CONV_ROWS, :]
            dx_ref[r0:r0 + CONV_ROWS, :] = acc.astype(dx_ref.dtype)
        for k in range(K):
            acc = jnp.zeros((1, tc), f32)
            for r0 in range(0, S, CONV_ROWS):
                lo = r0 + CONV_PAD - (K - 1) + k
                acc = acc + jnp.sum(dp[r0:r0 + CONV_ROWS, :] * xp[lo:lo + CONV_ROWS, :], axis=0, keepdims=True)
            dw_ref[k:k + 1, :] = acc

    return pl.pallas_call(
        body, name=name, grid=(CW // tc,),
        in_specs=[pl.BlockSpec((S, tc), lambda j: (0, j + xoff)),
                  pl.BlockSpec((K, tc), lambda j: (0, j)),
                  pl.BlockSpec((1, tc), lambda j: (0, j)),
                  pl.BlockSpec((S, tc), lambda j: (0, j))],
        out_specs=[pl.BlockSpec((S, tc), lambda j: (0, j)),
                   pl.BlockSpec((K, tc), lambda j: (0, j)),
                   pl.BlockSpec((1, tc), lambda j: (0, j))],
        out_shape=[jax.ShapeDtypeStruct((S, CW), dx_dtype),
                   jax.ShapeDtypeStruct((K, CW), f32),
                   jax.ShapeDtypeStruct((1, CW), f32)],
        scratch_shapes=[pltpu.VMEM((S + CONV_PAD, tc), f32), pltpu.VMEM((S + CONV_PAD, tc), f32)],
        compiler_params=_cp("parallel"))(x, w, b, dout)


def loss_head(name, y, target, tm):
    S, D = y.shape

    def body(y_ref, t_ref, dy_ref, l_ref):
        @pl.when(pl.program_id(0) == 0)
        def _():
            l_ref[...] = jnp.zeros_like(l_ref)

        err = y_ref[...] - t_ref[...]
        dy_ref[...] = err * (1.0 / D)
        l_ref[...] += jnp.sum(err * err) * (0.5 / D)

    dy, l = pl.pallas_call(
        body, name=name, grid=(S // tm,),
        in_specs=[pl.BlockSpec((tm, D), lambda i: (i, 0))] * 2,
        out_specs=[pl.BlockSpec((tm, D), lambda i: (i, 0)), pl.BlockSpec((8, LANES), lambda i: (0, 0))],
        out_shape=[jax.ShapeDtypeStruct((S, D), f32), jax.ShapeDtypeStruct((8, LANES), f32)],
        compiler_params=_cp("arbitrary"))(y, target)
    return dy, l[0, 0]


def _flat2d_tiles(rows, cols, itemsize, target_bytes):
    tc = _tile(cols, 1024) if cols % LANES == 0 else cols
    cap = max(8, target_bytes // (tc * itemsize))
    tr = _tile(rows, cap, 16) if rows % 16 == 0 else rows
    return tr, tc


def adamw(name, w, g, m, v):
    shape = w.shape
    cols = shape[-1]
    rows = math.prod(shape[:-1])
    tr, tc = _flat2d_tiles(rows, cols, 4, 1 << 20)
    c1 = 1.0 - ADAM_B1 ** ADAM_STEP
    c2 = 1.0 - ADAM_B2 ** ADAM_STEP

    def body(w_ref, g_ref, m_ref, v_ref, d_ref, nm_ref, nv_ref):
        gg = g_ref[...]
        nm = ADAM_B1 * m_ref[...] + (1.0 - ADAM_B1) * gg
        nv = ADAM_B2 * v_ref[...] + (1.0 - ADAM_B2) * (gg * gg)
        d_ref[...] = -ADAM_LR * ((nm / c1) / (jnp.sqrt(nv / c2) + ADAM_EPS) + ADAM_WD * w_ref[...])
        nm_ref[...] = nm
        nv_ref[...] = nv

    spec = pl.BlockSpec((tr, tc), lambda i, j: (i, j))
    outs = pl.pallas_call(
        body, name=name, grid=(rows // tr, cols // tc), in_specs=[spec] * 4, out_specs=[spec] * 3,
        out_shape=[jax.ShapeDtypeStruct((rows, cols), f32)] * 3,
        compiler_params=_cp("parallel", "parallel"))(*[a.reshape(rows, cols) for a in (w, g, m, v)])
    return [o.reshape(shape) for o in outs]


def _core_index():
    return lax.axis_index("c").astype(jnp.int32).reshape(1)


def rs_add(name, G, buf):
    P, L, R, C = G.shape
    Lh = L // 2
    tr = _tile(R, max(16, (2 << 20) // (C * 2)), 16)

    def body(c_ref, g_ref, b_ref, o_ref):
        o_ref[...] = (g_ref[...].astype(f32) + b_ref[...].astype(f32)).astype(o_ref.dtype)

    blk = (None, None, tr, C)
    return pl.pallas_call(
        body, name=name,
        grid_spec=pltpu.PrefetchScalarGridSpec(
            num_scalar_prefetch=1, grid=(P, Lh, R // tr),
            in_specs=[pl.BlockSpec(blk, lambda p, l, i, c: (p, c[0] * Lh + l, i, 0)),
                      pl.BlockSpec(blk, lambda p, l, i, c: (p, l, i, 0))],
            out_specs=pl.BlockSpec(blk, lambda p, l, i, c: (p, l, i, 0))),
        out_shape=jax.ShapeDtypeStruct((P, Lh, R, C), bf16),
        compiler_params=_cp("parallel", "parallel", "parallel"))(_core_index(), G, buf)


def _chip_indices():
    x, y, c = lax.axis_index("x"), lax.axis_index("y"), lax.axis_index("c")
    ids = [2 * x + y] + [2 * _flip(x, fx) + _flip(y, fy) for fx, fy in _CHIP_FLIPS] + [c]
    return [i.astype(jnp.int32).reshape(1) for i in ids]


def rs_sum4(name, pair, buf):
    P, Lh, R, C = buf.shape
    tr = _tile(R, max(16, (2 << 20) // (C * 2)), 16)

    def body(i0, i1, i2, i3, ic, b0, b1, b2, b3, o_ref):
        o_ref[...] = ((b0[...].astype(f32) + b1[...].astype(f32)) + b2[...].astype(f32)) + b3[...].astype(f32)

    blk = (None, None, tr, C)
    return pl.pallas_call(
        body, name=name,
        grid_spec=pltpu.PrefetchScalarGridSpec(
            num_scalar_prefetch=5, grid=(Lh, R // tr),
            in_specs=[pl.BlockSpec(blk, lambda l, i, *ids, k=k: (ids[k][0], l, i, 0)) for k in range(P)],
            out_specs=pl.BlockSpec((None, tr, C), lambda l, i, *ids: (ids[4][0] * Lh + l, i, 0))),
        out_shape=jax.ShapeDtypeStruct((2 * Lh, R, C), f32),
        compiler_params=_cp("parallel", "parallel"))(*_chip_indices(), pair, buf, buf, buf)


def place_own(name, gathered, w):
    L, R, C = w.shape
    tr = _tile(R, max(16, (2 << 20) // (C * 2)), 16)

    def body(q, w_ref, g_ref, o_ref):
        o_ref[...] = w_ref[...]

    return pl.pallas_call(
        body, name=name,
        grid_spec=pltpu.PrefetchScalarGridSpec(
            num_scalar_prefetch=1, grid=(L, R // tr),
            in_specs=[pl.BlockSpec((None, tr, C), lambda l, i, q: (l, i, 0)), _ANY],
            out_specs=pl.BlockSpec((None, None, tr, C), lambda l, i, q: (q[0], l, i, 0))),
        out_shape=jax.ShapeDtypeStruct(gathered.shape, gathered.dtype),
        input_output_aliases={2: 0},
        compiler_params=_cp("parallel", "parallel"))(_chip_indices()[0], w, gathered)


_ANY = pl.BlockSpec(memory_space=pl.ANY)
_CHIP_FLIPS = ((1, 0), (0, 1), (1, 1))


def _place():
    return lax.axis_index("x"), lax.axis_index("y"), lax.axis_index("c")


def _flip(v, f):
    return 1 - v if f else v


def _remote(src, dst, ssem, rsem, dev):
    return pltpu.make_async_remote_copy(src_ref=src, dst_ref=dst, send_sem=ssem, recv_sem=rsem,
                                        device_id=dev, device_id_type=MESH)


def all_gather_weights(name, ws):
    T = len(ws)

    def body(*refs):
        w, o = refs[:T], refs[T:2 * T]
        send, recv, send2, recv2 = refs[2 * T:]
        x, y, c = _place()
        q = 2 * x + y
        sib = (x, y, 1 - c)
        sends = []
        for t in range(T):
            Lh = w[t].shape[0] // 2
            mine = pl.ds(c * Lh, Lh)
            for k, (fx, fy) in enumerate(_CHIP_FLIPS):
                cp = _remote(w[t].at[mine], o[t].at[q, mine], send.at[t, k], recv.at[t, k],
                             (_flip(x, fx), _flip(y, fy), c))
                cp.start()
                sends.append(cp)
        for t in range(T):
            Lh = w[t].shape[0] // 2
            mine = pl.ds(c * Lh, Lh)
            for k, (fx, fy) in enumerate(_CHIP_FLIPS):
                qk = 2 * _flip(x, fx) + _flip(y, fy)
                slab = o[t].at[qk, mine]
                _remote(slab, slab, send.at[t, k], recv.at[t, k], sib).wait_recv()
                cp = _remote(slab, slab, send2.at[t, k], recv2.at[t, k], sib)
                cp.start()
                sends.append(cp)
        for t in range(T):
            Lh = w[t].shape[0] // 2
            other = pl.ds((1 - c) * Lh, Lh)
            for k, (fx, fy) in enumerate(_CHIP_FLIPS):
                qk = 2 * _flip(x, fx) + _flip(y, fy)
                slab = o[t].at[qk, other]
                _remote(slab, slab, send2.at[t, k], recv2.at[t, k], sib).wait_recv()
        for cp in sends:
            cp.wait_send()

    return pl.pallas_call(
        body, name=name, in_specs=[_ANY] * T, out_specs=[_ANY] * T,
        out_shape=[jax.ShapeDtypeStruct((N_CHIPS,) + w.shape, w.dtype) for w in ws],
        scratch_shapes=[pltpu.SemaphoreType.DMA((T, 3)), pltpu.SemaphoreType.DMA((T, 3)),
                        pltpu.SemaphoreType.DMA((T, 3)), pltpu.SemaphoreType.DMA((T, 3))],
        )(*ws)


def rs_sibling(name, Gs):
    T = len(Gs)

    def body(*refs):
        g, o = refs[:T], refs[T:2 * T]
        send, recv = refs[2 * T:]
        x, y, c = _place()
        cps = []
        for t in range(T):
            Lh = g[t].shape[1] // 2
            cp = _remote(g[t].at[:, pl.ds((1 - c) * Lh, Lh)], o[t], send.at[t], recv.at[t], (x, y, 1 - c))
            cp.start()
            cps.append(cp)
        for cp in cps:
            cp.wait()

    return pl.pallas_call(
        body, name=name, in_specs=[_ANY] * T, out_specs=[_ANY] * T,
        out_shape=[jax.ShapeDtypeStruct((G.shape[0], G.shape[1] // 2) + G.shape[2:], G.dtype) for G in Gs],
        scratch_shapes=[pltpu.SemaphoreType.DMA((T,)), pltpu.SemaphoreType.DMA((T,))],
        )(*Gs)


def rs_chips(name, Ss):
    T = len(Ss)

    def body(*refs):
        s, o = refs[:T], refs[T:2 * T]
        send, recv = refs[2 * T:]
        x, y, c = _place()
        q = 2 * x + y
        cps = []
        for t in range(T):
            for k, (fx, fy) in enumerate(_CHIP_FLIPS):
                px, py = _flip(x, fx), _flip(y, fy)
                cp = _remote(s[t].at[2 * px + py], o[t].at[q], send.at[t, k], recv.at[t, k], (px, py, c))
                cp.start()
                cps.append(cp)
        for t in range(T):
            for k, (fx, fy) in enumerate(_CHIP_FLIPS):
                slab = o[t].at[2 * _flip(x, fx) + _flip(y, fy)]
                _remote(slab, slab, send.at[t, k], recv.at[t, k], (x, y, c)).wait_recv()
        for cp in cps:
            cp.wait_send()

    return pl.pallas_call(
        body, name=name, in_specs=[_ANY] * T, out_specs=[_ANY] * T,
        out_shape=[jax.ShapeDtypeStruct(S.shape, S.dtype) for S in Ss],
        scratch_shapes=[pltpu.SemaphoreType.DMA((T, 3)), pltpu.SemaphoreType.DMA((T, 3))],
        )(*Ss)


def rs_share(name, tots):
    T = len(tots)

    def body(*refs):
        s, o = refs[:T], refs[T:2 * T]
        send, recv = refs[2 * T:]
        x, y, c = _place()
        cps = []
        for t in range(T):
            Lh = s[t].shape[0] // 2
            mine = o[t].at[pl.ds(c * Lh, Lh)]
            cp = _remote(mine, mine, send.at[t], recv.at[t], (x, y, 1 - c))
            cp.start()
            cps.append(cp)
        for t in range(T):
            Lh = s[t].shape[0] // 2
            other = o[t].at[pl.ds((1 - c) * Lh, Lh)]
            _remote(other, other, send.at[t], recv.at[t], (x, y, 1 - c)).wait_recv()
        for cp in cps:
            cp.wait_send()

    return pl.pallas_call(
        body, name=name, in_specs=[_ANY] * T, out_specs=[_ANY] * T,
        out_shape=[jax.ShapeDtypeStruct(s.shape, s.dtype) for s in tots],
        input_output_aliases={t: t for t in range(T)},
        scratch_shapes=[pltpu.SemaphoreType.DMA((T,)), pltpu.SemaphoreType.DMA((T,))],
        )(*tots)


def all_reduce_small(name, vec):
    rows = vec.shape[0]
    flips = [(fx, fy, fc) for fx in (0, 1) for fy in (0, 1) for fc in (0, 1)][1:]

    def body(v_ref, o_ref, buf, send, recv):
        x, y, c = _place()
        me = 4 * x + 2 * y + c
        buf[me] = v_ref[...]
        cps = []
        for k, (fx, fy, fc) in enumerate(flips):
            cp = _remote(buf.at[me], buf.at[me], send.at[k], recv.at[k],
                         (_flip(x, fx), _flip(y, fy), _flip(c, fc)))
            cp.start()
            cps.append(cp)
        for k, (fx, fy, fc) in enumerate(flips):
            slab = buf.at[4 * _flip(x, fx) + 2 * _flip(y, fy) + _flip(c, fc)]
            _remote(slab, slab, send.at[k], recv.at[k], (x, y, c)).wait_recv()
        for cp in cps:
            cp.wait_send()
        acc = buf[0]
        for d in range(1, N_DEV):
            acc = acc + buf[d]
        o_ref[...] = acc

    return pl.pallas_call(
        body, name=name,
        in_specs=[pl.BlockSpec(memory_space=pltpu.VMEM)], out_specs=pl.BlockSpec(memory_space=pltpu.VMEM),
        out_shape=jax.ShapeDtypeStruct((rows, LANES), f32),
        scratch_shapes=[pltpu.VMEM((N_DEV, rows, LANES), f32),
                        pltpu.SemaphoreType.DMA((N_DEV - 1,)), pltpu.SemaphoreType.DMA((N_DEV - 1,))],
        compiler_params=pltpu.CompilerParams(vmem_limit_bytes=V7X_VMEM_LIMIT))(vec)


def _pack(arrays):
    flat = jnp.concatenate([a.reshape(-1) for a in arrays])
    n = flat.shape[0]
    rows = -(-n // (8 * LANES)) * 8
    return jnp.pad(flat, (0, rows * LANES - n)).reshape(rows, LANES)


def _unpack(vec, shapes):
    flat = vec.reshape(-1)
    out, pos = [], 0
    for s in shapes:
        n = math.prod(s)
        out.append(flat[pos:pos + n].reshape(s))
        pos += n
    return out


def _f_first(x, g):
    return x, _rms(x, g)


def _f_mid(h, m, gp, gn):
    h1 = h + _rms(m, gp)
    return h1, _rms(h1, gn)


def _f_mid_bias(h, m, b, gp, gn):
    h1 = h + _rms(m + b, gp)
    return h1, _rms(h1, gn)


def _f_last(h, m, gp):
    return (h + _rms(m, gp),)


def _f_swiglu(gate, up):
    return (_silu(gate) * up,)


def _f_glu(a, g, ba, bg):
    return ((a + ba) * _sigmoid(g + bg),)


def _f_ln_silu(x, g, b):
    mu = jnp.mean(x, axis=-1, keepdims=True)
    xc = x - mu
    y = xc * lax.rsqrt(jnp.mean(xc * xc, axis=-1, keepdims=True) + LN_EPS) * g + b
    return (_silu(y),)


def _f_lower_bounds(logits):
    n = logits.shape[0]
    e = jnp.exp(logits - jnp.max(logits, axis=0, keepdims=True))
    p = e / jnp.sum(e, axis=0, keepdims=True)
    layer = lax.broadcasted_iota(jnp.int32, logits.shape, 0)
    out = -jnp.broadcast_to(p[0:1, :], logits.shape)
    for j in range(n):
        out = out + jnp.where(layer >= j, p[j:j + 1, :], 0.0)
    return (out,)


WEIGHT_NAMES = ['mix_pre_g', 'mix_post_g', 'ffn_pre_g', 'ffn_post_g', 'hgrn_lb_logits', 'even_w_in',
                'hgrn_norm_g', 'ssd_conv_w', 'ssd_conv_b', 'ssd_dt_bias', 'ssd_a_log', 'ssd_d', 'ssd_norm_g',
                'even_w_out', 'conf_w1', 'conf_b1', 'conf_dw_w', 'conf_dw_b', 'conf_ln_g', 'conf_ln_b',
                'conf_w2', 'conf_b2', 'ffn_w_gate', 'ffn_w_up', 'ffn_w_down']
BIG = ['even_w_in', 'even_w_out', 'conf_w1', 'conf_w2', 'ffn_w_gate', 'ffn_w_up', 'ffn_w_down']
SMALL_SHARDED = {'ssd_conv_w': 2, 'conf_b1': 1, 'conf_dw_w': 2, 'conf_dw_b': 1, 'conf_ln_g': 1,
                 'conf_ln_b': 1, 'conf_b2': 1}


def _train_step(x, target, w, m, v):
    S, D = x.shape[1], x.shape[2]
    x2, t2 = x[0], target[0]
    NL = w['mix_pre_g'].shape[0]
    HB = w['ssd_dt_bias'].shape[1]
    GN = B_GROUPS * B_STATE
    xw, yw, cw = _place()
    chip = 2 * xw + yw
    tm = _tile(S, 128, 8)
    row1 = lambda a, i: a[i:i + 1]

    sharded = list(SMALL_SHARDED)
    placed = []
    for n in sharded:
        ax, a = SMALL_SHARDED[n], w[n]
        full = jnp.zeros(a.shape[:ax] + (a.shape[ax] * N_CHIPS,) + a.shape[ax + 1:], f32)
        start = [0] * a.ndim
        start[ax] = chip * a.shape[ax]
        placed.append(lax.dynamic_update_slice(full, jnp.where(cw == 0, a, 0.0), start))
    whole = dict(zip(sharded, _unpack(all_reduce_small("gather_small", _pack(placed)), [p.shape for p in placed])))
    small = {n: whole.get(n, w[n]) for n in WEIGHT_NAMES if n not in BIG}

    own = [w[n].astype(bf16) for n in BIG]
    gathered = all_gather_weights("gather_weights", own)
    W = {n: place_own("place_own", g, o) for n, g, o in zip(BIG, gathered, own)}
    win = jnp.concatenate([W['even_w_in'][j] for j in range(N_CHIPS)], axis=-1)
    WM = 6 * D + 2 * GN
    w_main, w_dt = win[None, :, :, :WM], win[None, :, :, WM:]

    n_even = small['hgrn_lb_logits'].shape[0]
    (lbs,) = _stage_fwd("lower_bounds", _f_lower_bounds, [small['hgrn_lb_logits']], [], [(D, f32)], tm=n_even)
    saved = []
    h = x2
    (u,) = _stage_fwd("pre_norm", lambda a, g: (_rms(a, g),), [h], [row1(small['mix_pre_g'], 0)],
                      [(D, bf16)], tm=tm)
    for layer in range(NL):
        li = layer // 2
        r = {'h': h, 'u': u}
        if layer % 2 == 0:
            r['ymain'] = mm_nn_col("in_proj", u, w_main, li)
            r['dtr'] = mm_nn_col("in_proj_dt", u, w_dt, li)
            r['xact'] = conv_fwd("ssd_conv", r['ymain'], 5 * D // CONV_CH, small['ssd_conv_w'][li],
                                 row1(small['ssd_conv_b'], li), True, f32)
            o_a, r['hg_st'] = hgrn_fwd("hgrn", r['ymain'], row1(lbs, li), row1(small['hgrn_norm_g'], li), D)
            o_b, r['ssd_st'] = ssd_fwd("ssd", r['xact'], r['ymain'], r['dtr'], row1(small['ssd_dt_bias'], li),
                                       row1(small['ssd_a_log'], li), row1(small['ssd_d'], li),
                                       row1(small['ssd_norm_g'], li), D)
            r['mixed'] = jnp.concatenate([o_a, o_b], axis=1)
            r['m'] = mm_nn_row("out_proj", r['mixed'], W['even_w_out'], li)
            mid_fn, mid_par = _f_mid, []
        else:
            r['c1'] = mm_nn_col("conf_in", u, W['conf_w1'], li)
            b1 = row1(small['conf_b1'], li)
            tn = _tile(D, 512)
            (r['glu'],) = _stage_fwd("conf_glu", _f_glu, [r['c1'], (r['c1'], D // tn)], [b1, (b1, D // tn)],
                                     [(D, f32)], tm=tm, tn=tn)
            r['cc'] = conv_fwd("conf_conv", r['glu'], 0, small['conf_dw_w'][li], row1(small['conf_dw_b'], li),
                               False, f32)
            (r['c2'],) = _stage_fwd("conf_ln", _f_ln_silu, [r['cc']],
                                    [row1(small['conf_ln_g'], li), row1(small['conf_ln_b'], li)],
                                    [(D, bf16)], tm=tm)
            r['m'] = mm_nn_row("conf_out", r['c2'], W['conf_w2'], li)
            mid_fn, mid_par = _f_mid_bias, [row1(small['conf_b2'], li)]
        r['mid_fn'] = mid_fn
        r['mid_par'] = mid_par + [row1(small['mix_post_g'], layer), row1(small['ffn_pre_g'], layer)]
        r['h1'], r['u2'] = _stage_fwd("mid_norm", mid_fn, [h, r['m']], r['mid_par'], [(D, f32), (D, bf16)], tm=tm)
        r['gate'] = mm_nn_col("ffn_gate", r['u2'], W['ffn_w_gate'], layer, bf16)
        r['up'] = mm_nn_col("ffn_up", r['u2'], W['ffn_w_up'], layer, bf16)
        F = r['gate'].shape[1]
        tnf = _tile(F, 1024)
        (r['act'],) = _stage_fwd("swiglu", _f_swiglu, [r['gate'], r['up']], [], [(F, bf16)], tm=tm, tn=tnf)
        r['dn'] = mm_nn_row("ffn_down", r['act'], W['ffn_w_down'], layer)
        if layer + 1 < NL:
            r['end_fn'] = _f_mid
            r['end_par'] = [row1(small['ffn_post_g'], layer), row1(small['mix_pre_g'], layer + 1)]
            h, u = _stage_fwd("end_norm", _f_mid, [r['h1'], r['dn']], r['end_par'], [(D, f32), (D, bf16)], tm=tm)
        else:
            r['end_fn'] = _f_last
            r['end_par'] = [row1(small['ffn_post_g'], layer)]
            (h,) = _stage_fwd("last_norm", _f_last, [r['h1'], r['dn']], r['end_par'], [(D, f32)], tm=tm)
        saved.append(r)

    dy, loss_local = loss_head("loss_head", h, t2, tm)
    loss = lax.psum(loss_local, ("x", "y", "c"))

    gs = {n: jnp.zeros(small[n].shape, f32) for n in small}

    def put(n, i, val):
        gs[n] = gs[n].at[i].add(val.reshape(gs[n].shape[1:]))

    G = {n: lax.empty((N_CHIPS,) + w[n].shape, bf16) for n in BIG if n != 'even_w_in'}
    L2 = w['even_w_in'].shape[0]
    g_main = lax.empty((1, L2, D, WM), bf16)
    g_dt = lax.empty((1, L2, D, HB), bf16)

    dh = dy
    du_parts = None
    for layer in reversed(range(NL)):
        li = layer // 2
        r = saved[layer]
        cts = [[dh]] if du_parts is None else [[dh], du_parts]
        (dh1, d_dn), pg = _stage_bwd("end_norm_bwd", r['end_fn'], [r['h1'], r['dn']], r['end_par'], cts,
                                     [f32, bf16], tm=tm)
        put('ffn_post_g', layer, pg[0])
        if du_parts is not None:
            put('mix_pre_g', layer + 1, pg[1])
        d_act = mm_nt_row("ffn_down_dx", d_dn, W['ffn_w_down'], layer, bf16)
        G['ffn_w_down'] = mm_tn_row("ffn_down_dw", r['act'], d_dn, G['ffn_w_down'], layer)
        F = r['gate'].shape[1]
        (d_gate, d_up), _ = _stage_bwd("swiglu_bwd", _f_swiglu, [r['gate'], r['up']], [], [[d_act]],
                                       [bf16, bf16], tm=tm, tn=_tile(F, 1024))
        du_a = mm_nt_col("ffn_gate_dx", d_gate, W['ffn_w_gate'], layer)
        du_b = mm_nt_col("ffn_up_dx", d_up, W['ffn_w_up'], layer)
        G['ffn_w_gate'] = mm_tn_col("ffn_gate_dw", r['u2'], d_gate, G['ffn_w_gate'], layer)
        G['ffn_w_up'] = mm_tn_col("ffn_up_dw", r['u2'], d_up, G['ffn_w_up'], layer)
        (dh, dm), pg = _stage_bwd("mid_norm_bwd", r['mid_fn'], [r['h'], r['m']], r['mid_par'],
                                  [[dh1], [du_a, du_b]], [f32, bf16], tm=tm)
        put('mix_post_g', layer, pg[-2])
        put('ffn_pre_g', layer, pg[-1])
        if layer % 2 == 0:
            d_mixed = mm_nt_row("out_proj_dx", dm, W['even_w_out'], li)
            G['even_w_out'] = mm_tn_row("out_proj_dw", r['mixed'], dm, G['even_w_out'], li)
            dxs, dbm, dcm, dz, ddt, ddtb, dal, dds, dbn = ssd_bwd(
                "ssd_bwd", r['xact'], r['ymain'], r['dtr'], row1(small['ssd_dt_bias'], li),
                row1(small['ssd_a_log'], li), row1(small['ssd_d'], li), row1(small['ssd_norm_g'], li),
                r['ssd_st'], d_mixed, D)
            put('ssd_dt_bias', li, ddtb)
            put('ssd_a_log', li, dal)
            put('ssd_d', li, dds)
            put('ssd_norm_g', li, dbn)
            d_xact = jnp.concatenate([dxs, dbm, dcm], axis=1)
            d_xbc, dcw, dcb = conv_bwd("ssd_conv_bwd", r['ymain'], 5 * D // CONV_CH, small['ssd_conv_w'][li],
                                       row1(small['ssd_conv_b'], li), d_xact, True, bf16)
            put('ssd_conv_w', li, dcw)
            put('ssd_conv_b', li, dcb)
            dq, df, dv, dg, dlb, dan = hgrn_bwd("hgrn_bwd", r['ymain'], row1(lbs, li), row1(small['hgrn_norm_g'], li),
                                               r['hg_st'], d_mixed, D)
            put('hgrn_norm_g', li, dan)
            r['dlb'] = dlb
            d_main = jnp.concatenate([dq, df, dv, dg, dz, d_xbc], axis=1)
            du_parts = [mm_nt_col("in_proj_dx", d_main, w_main, li), mm_nt_col("in_proj_dt_dx", ddt, w_dt, li)]
            g_main = mm_tn_col("in_proj_dw", r['u'], d_main, g_main, li)
            g_dt = mm_tn_col("in_proj_dt_dw", r['u'], ddt, g_dt, li)
        else:
            put('conf_b2', li, pg[0])
            d_c2 = mm_nt_row("conf_out_dx", dm, W['conf_w2'], li)
            G['conf_w2'] = mm_tn_row("conf_out_dw", r['c2'], dm, G['conf_w2'], li)
            (d_cc,), pl_ = _stage_bwd("conf_ln_bwd", _f_ln_silu, [r['cc']],
                                      [row1(small['conf_ln_g'], li), row1(small['conf_ln_b'], li)],
                                      [[d_c2]], [f32], tm=tm)
            put('conf_ln_g', li, pl_[0])
            put('conf_ln_b', li, pl_[1])
            d_glu, ddw, ddb = conv_bwd("conf_conv_bwd", r['glu'], 0, small['conf_dw_w'][li],
                                       row1(small['conf_dw_b'], li), d_cc, False, f32)
            put('conf_dw_w', li, ddw)
            put('conf_dw_b', li, ddb)
            b1 = row1(small['conf_b1'], li)
            tn = _tile(D, 512)
            (da, dg_), pb = _stage_bwd("conf_glu_bwd", _f_glu, [r['c1'], (r['c1'], D // tn)], [b1, (b1, D // tn)],
                                       [[d_glu]], [bf16, bf16], tm=tm, tn=tn)
            put('conf_b1', li, jnp.concatenate([pb[0], pb[1]], axis=1))
            d_c1 = jnp.concatenate([da, dg_], axis=1)
            du_parts = [mm_nt_col("conf_in_dx", d_c1, W['conf_w1'], li)]
            G['conf_w1'] = mm_tn_col("conf_in_dw", r['u'], d_c1, G['conf_w1'], li)
    (grad_x2,), pg = _stage_bwd("pre_norm_bwd", _f_first, [x2], [row1(small['mix_pre_g'], 0)],
                                [[dh], du_parts], [f32], tm=tm)
    put('mix_pre_g', 0, pg[0])
    dlbs = jnp.concatenate([saved[2 * i]['dlb'] for i in range(n_even)], axis=0)
    (dlogits,), _ = _stage_bwd("lower_bounds_bwd", _f_lower_bounds, [small['hgrn_lb_logits']], [],
                               [[dlbs]], [f32], tm=n_even)
    gs['hgrn_lb_logits'] = dlogits

    names_s = [n for n in WEIGHT_NAMES if n not in BIG]
    summed = _unpack(all_reduce_small("reduce_small", _pack([gs[n] for n in names_s])),
                     [gs[n].shape for n in names_s])
    grads = {}
    for n, a in zip(names_s, summed):
        if n in SMALL_SHARDED:
            ax = SMALL_SHARDED[n]
            size = w[n].shape[ax]
            start = [0] * a.ndim
            start[ax] = chip * size
            a = lax.dynamic_slice(a, start, a.shape[:ax] + (size,) + a.shape[ax + 1:])
        grads[n] = a

    g_in = jnp.concatenate([g_main[0], g_dt[0]], axis=-1)
    CI = w['even_w_in'].shape[2]
    G['even_w_in'] = jnp.stack([g_in[:, :, j * CI:(j + 1) * CI] for j in range(N_CHIPS)])
    parts = [G[n] for n in BIG]
    from_sib = rs_sibling("reduce_sibling", parts)
    pair = [rs_add("reduce_add", a, b) for a, b in zip(parts, from_sib)]
    from_chips = rs_chips("reduce_chips", pair)
    tot = [rs_sum4("reduce_sum", a, b) for a, b in zip(pair, from_chips)]
    for n, a in zip(BIG, rs_share("reduce_share", tot)):
        grads[n] = a

    delta, new_m, new_v = {}, {}, {}
    for n in WEIGHT_NAMES:
        delta[n], new_m[n], new_v[n] = adamw("adamw", w[n], grads[n], m[n], v[n])
    return (loss, grad_x2[None], *[grads[n] for n in WEIGHT_NAMES], *[delta[n] for n in WEIGHT_NAMES],
            *[new_m[n] for n in WEIGHT_NAMES], *[new_v[n] for n in WEIGHT_NAMES])


def kernel(x, mix_pre_g, mix_post_g, ffn_pre_g, ffn_post_g, hgrn_lb_logits, even_w_in, hgrn_norm_g, ssd_conv_w, ssd_conv_b, ssd_dt_bias, ssd_a_log, ssd_d, ssd_norm_g, even_w_out, conf_w1, conf_b1, conf_dw_w, conf_dw_b, conf_ln_g, conf_ln_b, conf_w2, conf_b2, ffn_w_gate, ffn_w_up, ffn_w_down, loss_target, m_mix_pre_g, m_mix_post_g, m_ffn_pre_g, m_ffn_post_g, m_hgrn_lb_logits, m_even_w_in, m_hgrn_norm_g, m_ssd_conv_w, m_ssd_conv_b, m_ssd_dt_bias, m_ssd_a_log, m_ssd_d, m_ssd_norm_g, m_even_w_out, m_conf_w1, m_conf_b1, m_conf_dw_w, m_conf_dw_b, m_conf_ln_g, m_conf_ln_b, m_conf_w2, m_conf_b2, m_ffn_w_gate, m_ffn_w_up, m_ffn_w_down, v_mix_pre_g, v_mix_post_g, v_ffn_pre_g, v_ffn_post_g, v_hgrn_lb_logits, v_even_w_in, v_hgrn_norm_g, v_ssd_conv_w, v_ssd_conv_b, v_ssd_dt_bias, v_ssd_a_log, v_ssd_d, v_ssd_norm_g, v_even_w_out, v_conf_w1, v_conf_b1, v_conf_dw_w, v_conf_dw_b, v_conf_ln_g, v_conf_ln_b, v_conf_w2, v_conf_b2, v_ffn_w_gate, v_ffn_w_up, v_ffn_w_down):
    args = locals()
    w = {n: args[n] for n in WEIGHT_NAMES}
    m = {n: args["m_" + n] for n in WEIGHT_NAMES}
    v = {n: args["v_" + n] for n in WEIGHT_NAMES}
    return _train_step(x, loss_target, w, m, v)
```

```python
import functools
import math

import jax
import jax.numpy as jnp
from jax import lax
from jax.experimental import pallas as pl
from jax.experimental.pallas import tpu as pltpu

f32 = jnp.float32
bf16 = jnp.bfloat16
MESH = pl.DeviceIdType.MESH
HI = lax.Precision.HIGHEST

A_HEAD = 128
A_CHUNK = 64
A_SUB = 16
A_HEADS_PER_STEP = 4
A_F_MIN = 1e-6
B_HEAD = 64
B_GROUPS = 4
B_STATE = 128
B_CONV = 4
B_CHUNK = 128
C_KERNEL = 31
RMS_EPS = 1e-6
LN_EPS = 1e-5
ADAM_LR = 0.001
ADAM_B1 = 0.9
ADAM_B2 = 0.999
ADAM_EPS = 1e-08
ADAM_WD = 0.01
ADAM_STEP = 10

N_CHIPS = 4
N_DEV = 8
V7X_VMEM_LIMIT = 56 * 1024 * 1024
LANES = 128
CONV_PAD = 32
CONV_ROWS = 128
CONV_CH = 256

NN = (((1,), (0,)), ((), ()))
NT = (((1,), (1,)), ((), ()))
TN = (((0,), (0,)), ((), ()))


def _tile(n, cap, unit=LANES):
    best = None
    for t in range(unit, min(n, cap) + 1, unit):
        if n % t == 0:
            best = t
    return n if best is None else best


def _cp(*sem):
    return pltpu.CompilerParams(dimension_semantics=sem, vmem_limit_bytes=V7X_VMEM_LIMIT)


def _sigmoid(x):
    return jax.nn.sigmoid(x)


def _silu(x):
    return x * jax.nn.sigmoid(x)


def _rms(x, g):
    return x * lax.rsqrt(jnp.mean(x * x, axis=-1, keepdims=True) + RMS_EPS) * g


def _pair(a):
    return a if isinstance(a, tuple) else (a, 0)


def _stage(name, fn, rows, params, outs, par_outs=(), *, tm, tn=None):
    rows = [_pair(r) for r in rows]
    params = [_pair(p) for p in params]
    S = rows[0][0].shape[0]
    n_in, n_o = len(rows) + len(params), len(outs)
    if tn is None:
        grid = (S // tm,)
        in_specs = [pl.BlockSpec((tm, a.shape[1]), lambda i: (i, 0)) for a, _ in rows]
        in_specs += [pl.BlockSpec(a.shape, lambda i: (0, 0)) for a, _ in params]
        out_specs = [pl.BlockSpec((tm, w), lambda i: (i, 0)) for w, _ in outs]
        out_specs += [pl.BlockSpec((k, w), lambda i: (0, 0)) for k, w in par_outs]
        row_axis = 0
        sem = ("arbitrary",) if par_outs else ("parallel",)
    else:
        grid = (outs[0][0] // tn, S // tm)
        in_specs = [pl.BlockSpec((tm, tn), lambda j, i, o=o: (i, j + o)) for _, o in rows]
        in_specs += [pl.BlockSpec((a.shape[0], tn), lambda j, i, o=o: (0, j + o)) for a, o in params]
        out_specs = [pl.BlockSpec((tm, tn), lambda j, i: (i, j)) for _ in outs]
        out_specs += [pl.BlockSpec((k, tn), lambda j, i: (0, j)) for k, _ in par_outs]
        row_axis = 1
        sem = ("parallel", "arbitrary") if par_outs else ("parallel", "parallel")
    out_shape = [jax.ShapeDtypeStruct((S, w), d) for w, d in outs]
    out_shape += [jax.ShapeDtypeStruct((k, w), f32) for k, w in par_outs]

    def body(*refs):
        res = fn(*[r[...] for r in refs[:n_in]])
        for r, v in zip(refs[n_in:n_in + n_o], res[:n_o]):
            r[...] = v.astype(r.dtype)
        if par_outs:
            acc_refs = refs[n_in + n_o:]

            @pl.when(pl.program_id(row_axis) == 0)
            def _():
                for r in acc_refs:
                    r[...] = jnp.zeros_like(r)

            for r, v in zip(acc_refs, res[n_o:]):
                r[...] += v

    return pl.pallas_call(
        body, name=name, grid=grid, in_specs=in_specs, out_specs=out_specs, out_shape=out_shape,
        compiler_params=_cp(*sem))(*[a for a, _ in rows], *[a for a, _ in params])


def _stage_fwd(name, fn, rows, params, outs, *, tm, tn=None):
    n_r = len(rows)

    def ffn(*t):
        return fn(*[v.astype(f32) for v in t[:n_r]], *t[n_r:])

    return _stage(name, ffn, rows, params, outs, tm=tm, tn=tn)


def _stage_bwd(name, fn, rows, params, cts, drow, *, tm, tn=None):
    rows = [_pair(r) for r in rows]
    params = [_pair(p) for p in params]
    n_r, n_p = len(rows), len(params)
    flat_ct = [_pair(c) for group in cts for c in group]
    counts = [len(group) for group in cts]
    need = [i for i, d in enumerate(drow) if d is not None]

    def bfn(*t):
        r = [v.astype(f32) for v in t[:n_r]]
        c = t[n_r:n_r + len(flat_ct)]
        p = list(t[n_r + len(flat_ct):])
        res, vjp = jax.vjp(fn, *r, *p)
        ct, pos = [], 0
        for o, k in zip(res, counts):
            s = c[pos].astype(f32)
            for e in range(1, k):
                s = s + c[pos + e].astype(f32)
            pos += k
            ct.append(s.astype(o.dtype))
        g = vjp(tuple(ct))
        return tuple(g[i] for i in need) + tuple(g[n_r:])

    if tn is None:
        outs = [(rows[i][0].shape[1], drow[i]) for i in need]
        par_outs = [p.shape for p, _ in params]
    else:
        w_all = flat_ct[0][0].shape[1]
        outs = [(w_all, drow[i]) for i in need]
        par_outs = [(p.shape[0], w_all) for p, _ in params]
    res = _stage(name, bfn, rows + flat_ct, params, outs, par_outs, tm=tm, tn=tn)
    return res[:len(need)], res[len(need):]


def _mm(name, dims, a, b, grid, a_spec, b_spec, o_spec, out_shape, acc_shape):
    nk = grid[2]

    def body(a_ref, b_ref, o_ref, acc):
        k = pl.program_id(2)

        @pl.when(k == 0)
        def _():
            acc[...] = jnp.zeros_like(acc)

        acc[...] += lax.dot_general(a_ref[...].astype(bf16), b_ref[...].astype(bf16), dims,
                                    preferred_element_type=f32)

        @pl.when(k == nk - 1)
        def _():
            o_ref[...] = acc[...].astype(o_ref.dtype)

    return pl.pallas_call(
        body, name=name, grid=grid, in_specs=[a_spec, b_spec], out_specs=o_spec, out_shape=out_shape,
        scratch_shapes=[pltpu.VMEM(acc_shape, f32)],
        compiler_params=_cp("parallel", "parallel", "arbitrary"))(a, b)


def _mm_tiles(S):
    return _tile(S, 512)


def mm_nn_col(name, a, W, li, out_dtype=f32):
    P, _, K, C = W.shape
    S = a.shape[0]
    tm, tn, tk = _mm_tiles(S), _tile(C, 1536), _tile(K, 2048)
    nc = C // tn
    return _mm(name, NN, a, W, (S // tm, P * nc, K // tk),
               pl.BlockSpec((tm, tk), lambda i, j, k: (i, k)),
               pl.BlockSpec((None, None, tk, tn), lambda i, j, k: (j // nc, li, k, j % nc)),
               pl.BlockSpec((tm, tn), lambda i, j, k: (i, j)),
               jax.ShapeDtypeStruct((S, P * C), out_dtype), (tm, tn))


def mm_nn_row(name, a, W, li, out_dtype=f32):
    P, _, R, N = W.shape
    S = a.shape[0]
    tm, tn, tk = _mm_tiles(S), _tile(N, 1024), _tile(R, 2048)
    nr = R // tk
    return _mm(name, NN, a, W, (S // tm, N // tn, P * nr),
               pl.BlockSpec((tm, tk), lambda i, j, k: (i, k)),
               pl.BlockSpec((None, None, tk, tn), lambda i, j, k: (k // nr, li, k % nr, j)),
               pl.BlockSpec((tm, tn), lambda i, j, k: (i, j)),
               jax.ShapeDtypeStruct((S, N), out_dtype), (tm, tn))


def mm_nt_col(name, dy, W, li, out_dtype=f32):
    P, _, K, C = W.shape
    S = dy.shape[0]
    tm, tn, tk = _mm_tiles(S), _tile(K, 1024), _tile(C, 2048)
    nc = C // tk
    return _mm(name, NT, dy, W, (S // tm, K // tn, P * nc),
               pl.BlockSpec((tm, tk), lambda i, j, k: (i, k)),
               pl.BlockSpec((None, None, tn, tk), lambda i, j, k: (k // nc, li, j, k % nc)),
               pl.BlockSpec((tm, tn), lambda i, j, k: (i, j)),
               jax.ShapeDtypeStruct((S, K), out_dtype), (tm, tn))


def mm_nt_row(name, dy, W, li, out_dtype=f32):
    P, _, R, N = W.shape
    S = dy.shape[0]
    tm, tn, tk = _mm_tiles(S), _tile(R, 1536), _tile(N, 2048)
    nr = R // tn
    return _mm(name, NT, dy, W, (S // tm, P * nr, N // tk),
               pl.BlockSpec((tm, tk), lambda i, j, k: (i, k)),
               pl.BlockSpec((None, None, tn, tk), lambda i, j, k: (j // nr, li, j % nr, k)),
               pl.BlockSpec((tm, tn), lambda i, j, k: (i, j)),
               jax.ShapeDtypeStruct((S, P * R), out_dtype), (tm, tn))


def mm_tn_col(name, a, dy, P):
    S, K = a.shape
    C = dy.shape[1] // P
    tm, tn, tk = _tile(K, 512), _tile(C, 1536), _tile(S, 1024)
    nc = C // tn
    return _mm(name, TN, a, dy, (K // tm, P * nc, S // tk),
               pl.BlockSpec((tk, tm), lambda i, j, k: (k, i)),
               pl.BlockSpec((tk, tn), lambda i, j, k: (k, j)),
               pl.BlockSpec((None, tm, tn), lambda i, j, k: (j // nc, i, j % nc)),
               jax.ShapeDtypeStruct((P, K, C), bf16), (tm, tn))


def mm_tn_row(name, a, dy, P):
    S, N = dy.shape
    R = a.shape[1] // P
    tm, tn, tk = _tile(R, 1536), _tile(N, 1024), _tile(S, 1024)
    nr = R // tm
    return _mm(name, TN, a, dy, (P * nr, N // tn, S // tk),
               pl.BlockSpec((tk, tm), lambda i, j, k: (k, i)),
               pl.BlockSpec((tk, tn), lambda i, j, k: (k, j)),
               pl.BlockSpec((None, tm, tn), lambda i, j, k: (i // nr, i % nr, j)),
               jax.ShapeDtypeStruct((P, R, N), bf16), (tm, tn))


def _hgrn_chunk(st, q, fp, v, gt, lb, an):
    C = q.shape[0]
    sig = _sigmoid(fp)
    f = lb + (1.0 - lb) * sig
    kk = (1.0 - lb) * (1.0 - sig)
    g = jnp.log(jnp.maximum(f, A_F_MIN))
    qs = _silu(q)
    row = lax.broadcasted_iota(jnp.int32, (C, C), 0)
    col = lax.broadcasted_iota(jnp.int32, (C, C), 1)
    tri = (col <= row).astype(f32)
    b = jnp.dot(tri, g, precision=HI, preferred_element_type=f32)
    o_inter = lax.dot_general((qs * jnp.exp(b)).astype(bf16), st.astype(bf16), NT,
                              preferred_element_type=f32)
    T = A_SUB
    t3 = lax.broadcasted_iota(jnp.int32, (T, T, A_HEAD), 0)
    s3 = lax.broadcasted_iota(jnp.int32, (T, T, A_HEAD), 1)
    m3 = s3 <= t3
    vb = v.astype(bf16)
    blocks = []
    for i in range(C // T):
        lo = i * T
        b_i, q_i, k_i = b[lo:lo + T], qs[lo:lo + T], kk[lo:lo + T]
        diff = b_i[:, None, :] - b_i[None, :, :]
        dec = jnp.where(m3, jnp.exp(jnp.where(m3, diff, 0.0)), 0.0)
        s_ii = jnp.sum(q_i[:, None, :] * k_i[None, :, :] * dec, axis=-1)
        o_i = jnp.dot(s_ii.astype(bf16), vb[lo:lo + T], preferred_element_type=f32)
        if i > 0:
            start = b[lo - 1:lo, :]
            q_t = q_i * jnp.exp(b_i - start)
            k_t = kk[0:lo] * jnp.exp(start - b[0:lo])
            s_off = lax.dot_general(q_t.astype(bf16), k_t.astype(bf16), NT, preferred_element_type=f32)
            o_i = o_i + jnp.dot(s_off.astype(bf16), vb[0:lo], preferred_element_type=f32)
        blocks.append(o_i)
    o_intra = jnp.concatenate(blocks, axis=0)
    bl = b[C - 1:C, :]
    kd = kk * jnp.exp(bl - b)
    st_new = st * jnp.exp(bl) + lax.dot_general(v.astype(bf16), kd.astype(bf16), TN,
                                                preferred_element_type=f32)
    o = o_inter + o_intra
    y = o * lax.rsqrt(jnp.mean(o * o, axis=-1, keepdims=True) + RMS_EPS) * an * _silu(gt)
    return st_new, y


def _hgrn_heads_per_step(HA):
    return A_HEADS_PER_STEP if HA % A_HEADS_PER_STEP == 0 else 1


def _hgrn_in_specs(HA, HP, cidx):
    W = HP * A_HEAD
    specs = [pl.BlockSpec((A_CHUNK, W), lambda h, c, s=s: (cidx(c), s * (HA // HP) + h)) for s in range(4)]
    specs += [pl.BlockSpec((1, W), lambda h, c: (0, h))] * 2
    return specs


def _head(ref, j):
    return ref[:, j * A_HEAD:(j + 1) * A_HEAD]


def hgrn_fwd(name, ymain, lb, an, D):
    S = ymain.shape[0]
    HA, nc = D // A_HEAD, S // A_CHUNK
    HP = _hgrn_heads_per_step(HA)
    W = HP * A_HEAD

    def body(q, fp, v, gt, lb_ref, an_ref, o_ref, sv_ref, st):
        @pl.when(pl.program_id(1) == 0)
        def _():
            st[...] = jnp.zeros_like(st)

        sv_ref[...] = st[...]
        for j in range(HP):
            st_new, y = _hgrn_chunk(st[j], _head(q, j), _head(fp, j), _head(v, j), _head(gt, j),
                                    _head(lb_ref, j), _head(an_ref, j))
            st[j] = st_new
            o_ref[:, j * A_HEAD:(j + 1) * A_HEAD] = y.astype(o_ref.dtype)

    return pl.pallas_call(
        body, name=name, grid=(HA // HP, nc),
        in_specs=_hgrn_in_specs(HA, HP, lambda c: c),
        out_specs=[pl.BlockSpec((A_CHUNK, W), lambda h, c: (c, h)),
                   pl.BlockSpec((HP, None, A_HEAD, A_HEAD), lambda h, c: (h, c, 0, 0))],
        out_shape=[jax.ShapeDtypeStruct((S, D), bf16),
                   jax.ShapeDtypeStruct((HA, nc, A_HEAD, A_HEAD), f32)],
        scratch_shapes=[pltpu.VMEM((HP, A_HEAD, A_HEAD), f32)],
        compiler_params=_cp("parallel", "arbitrary"))(ymain, ymain, ymain, ymain, lb, an)


def hgrn_bwd(name, ymain, lb, an, saved, dmixed, D):
    S = ymain.shape[0]
    HA, nc = D // A_HEAD, S // A_CHUNK
    HP = _hgrn_heads_per_step(HA)
    W = HP * A_HEAD
    rev = lambda c: nc - 1 - c

    def body(q, fp, v, gt, lb_ref, an_ref, sv_ref, do_ref, dq, df, dv, dg, dlb, dan, dst):
        @pl.when(pl.program_id(1) == 0)
        def _():
            dst[...] = jnp.zeros_like(dst)
            dlb[...] = jnp.zeros_like(dlb)
            dan[...] = jnp.zeros_like(dan)

        for j in range(HP):
            cols = slice(j * A_HEAD, (j + 1) * A_HEAD)
            _, vjp = jax.vjp(_hgrn_chunk, sv_ref[j], _head(q, j), _head(fp, j), _head(v, j), _head(gt, j),
                             _head(lb_ref, j), _head(an_ref, j))
            g = vjp((dst[j], _head(do_ref, j).astype(f32)))
            dst[j] = g[0]
            for r, x in zip((dq, df, dv, dg), g[1:5]):
                r[:, cols] = x.astype(r.dtype)
            dlb[:, cols] += g[5]
            dan[:, cols] += g[6]

    blk = pl.BlockSpec((A_CHUNK, W), lambda h, c: (rev(c), h))
    vec = pl.BlockSpec((1, W), lambda h, c: (0, h))
    return pl.pallas_call(
        body, name=name, grid=(HA // HP, nc),
        in_specs=_hgrn_in_specs(HA, HP, rev) + [
            pl.BlockSpec((HP, None, A_HEAD, A_HEAD), lambda h, c: (h, rev(c), 0, 0)), blk],
        out_specs=[blk] * 4 + [vec] * 2,
        out_shape=[jax.ShapeDtypeStruct((S, D), bf16)] * 4 + [jax.ShapeDtypeStruct((1, D), f32)] * 2,
        scratch_shapes=[pltpu.VMEM((HP, A_HEAD, A_HEAD), f32)],
        compiler_params=_cp("parallel", "arbitrary"))(ymain, ymain, ymain, ymain, lb, an, saved, dmixed)


def _ssd_chunk(hp, xs, bm, cm, z, dtr, dtb, alog, dsk, bn, g, R):
    L, GW = xs.shape
    HB = dtr.shape[1]
    R8 = max(R, 8)
    dt = jax.nn.softplus(dtr + dtb)
    a = -jnp.exp(alog)
    row = lax.broadcasted_iota(jnp.int32, (L, L), 0)
    col = lax.broadcasted_iota(jnp.int32, (L, L), 1)
    causal = col <= row
    cs = jnp.dot(causal.astype(f32), dt * a, precision=HI, preferred_element_type=f32)
    eh = lax.broadcasted_iota(jnp.int32, (HB, GW), 0)
    ec = lax.broadcasted_iota(jnp.int32, (HB, GW), 1)
    spread = (eh == g * R + ec // B_HEAD).astype(f32)
    sh = lax.broadcasted_iota(jnp.int32, (R8, HB), 1)
    sr = lax.broadcasted_iota(jnp.int32, (R8, HB), 0)
    pick_t = jnp.logical_and(sh == g * R + sr, sr < R).astype(f32)
    dtf = jnp.dot(dt, spread, precision=HI, preferred_element_type=f32)
    csf = jnp.dot(cs, spread, precision=HI, preferred_element_type=f32)
    dsf = jnp.dot(jnp.broadcast_to(dsk, (8, HB)), spread, precision=HI, preferred_element_type=f32)[0:1, :]
    cs_col = lax.dot_general(cs, pick_t, NT, precision=HI, preferred_element_type=f32)
    cs_row = lax.dot_general(pick_t, cs, NT, precision=HI, preferred_element_type=f32)
    xdt = xs * dtf
    cb = lax.dot_general(cm.astype(bf16), bm.astype(bf16), NT, preferred_element_type=f32)
    lane_head = lax.broadcasted_iota(jnp.int32, (1, GW), 1) // B_HEAD
    y = jnp.zeros((L, GW), f32)
    for r in range(R):
        seg = cs_col[:, r:r + 1] - cs_row[r:r + 1, :]
        dec = jnp.where(causal, jnp.exp(jnp.where(causal, seg, 0.0)), 0.0)
        xm = jnp.where(lane_head == r, xdt, 0.0)
        y = y + jnp.dot((cb * dec).astype(bf16), xm.astype(bf16), preferred_element_type=f32)
    csl = csf[L - 1:L, :]
    dte = jnp.exp(csl - csf)
    states = lax.dot_general(bm.astype(bf16), (xdt * dte).astype(bf16), TN, preferred_element_type=f32)
    y_off = jnp.dot(cm.astype(bf16), hp.astype(bf16), preferred_element_type=f32) * jnp.exp(csf)
    hn = hp * jnp.exp(csl) + states
    gated = (y + y_off + dsf * xs) * _silu(z)
    out = gated * lax.rsqrt(jnp.mean(gated * gated, axis=-1, keepdims=True) + RMS_EPS) * bn
    return hn, out


def _ssd_in_specs(D, HB, cidx):
    L, GW, N = B_CHUNK, D // B_GROUPS, B_STATE
    zoff, boff = 4 * D // GW, D // N
    return [
        pl.BlockSpec((L, GW), lambda c, g: (cidx(c), g)),
        pl.BlockSpec((L, N), lambda c, g: (cidx(c), boff + g)),
        pl.BlockSpec((L, N), lambda c, g: (cidx(c), boff + B_GROUPS + g)),
        pl.BlockSpec((L, GW), lambda c, g: (cidx(c), zoff + g)),
        pl.BlockSpec((L, HB), lambda c, g: (cidx(c), 0)),
        pl.BlockSpec((1, HB), lambda c, g: (0, 0)),
        pl.BlockSpec((1, HB), lambda c, g: (0, 0)),
        pl.BlockSpec((1, HB), lambda c, g: (0, 0)),
        pl.BlockSpec((1, GW), lambda c, g: (0, g)),
    ]


def ssd_fwd(name, xact, ymain, dtr, dtb, alog, dsk, bn, D):
    S, HB = dtr.shape
    nc, GW, R = S // B_CHUNK, D // B_GROUPS, HB // B_GROUPS

    def body(xs, bm, cm, z, dt_ref, dtb_ref, al_ref, ds_ref, bn_ref, o_ref, sv_ref, hs):
        g = pl.program_id(1)

        @pl.when(pl.program_id(0) == 0)
        def _():
            hs[g] = jnp.zeros((B_STATE, GW), f32)

        hp = hs[g]
        sv_ref[...] = hp
        hn, out = _ssd_chunk(hp, xs[...], bm[...], cm[...], z[...], dt_ref[...], dtb_ref[...], al_ref[...],
                             ds_ref[...], bn_ref[...], g, R)
        hs[g] = hn
        o_ref[...] = out.astype(o_ref.dtype)

    return pl.pallas_call(
        body, name=name, grid=(nc, B_GROUPS),
        in_specs=_ssd_in_specs(D, HB, lambda c: c),
        out_specs=[pl.BlockSpec((B_CHUNK, GW), lambda c, g: (c, g)),
                   pl.BlockSpec((None, None, B_STATE, GW), lambda c, g: (c, g, 0, 0))],
        out_shape=[jax.ShapeDtypeStruct((S, D), bf16),
                   jax.ShapeDtypeStruct((nc, B_GROUPS, B_STATE, GW), f32)],
        scratch_shapes=[pltpu.VMEM((B_GROUPS, B_STATE, GW), f32)],
        compiler_params=_cp("arbitrary", "arbitrary"))(xact, xact, xact, ymain, dtr, dtb, alog, dsk, bn)


def ssd_bwd(name, xact, ymain, dtr, dtb, alog, dsk, bn, saved, dmixed, D):
    S, HB = dtr.shape
    nc, GW, R = S // B_CHUNK, D // B_GROUPS, HB // B_GROUPS
    rev = lambda c: nc - 1 - c
    ooff = D // GW

    def body(xs, bm, cm, z, dt_ref, dtb_ref, al_ref, ds_ref, bn_ref, sv_ref, do_ref,
             dxs, dbm, dcm, dz, ddt, ddtb, dal, dds, dbn, dhs):
        c, g = pl.program_id(0), pl.program_id(1)

        @pl.when(c == 0)
        def _():
            dhs[g] = jnp.zeros((B_STATE, GW), f32)
            dbn[g] = jnp.zeros((1, GW), f32)

        @pl.when(jnp.logical_and(c == 0, g == 0))
        def _():
            ddtb[...] = jnp.zeros_like(ddtb)
            dal[...] = jnp.zeros_like(dal)
            dds[...] = jnp.zeros_like(dds)

        @pl.when(g == 0)
        def _():
            ddt[...] = jnp.zeros_like(ddt)

        fn = functools.partial(_ssd_chunk, g=g, R=R)
        _, vjp = jax.vjp(fn, sv_ref[...], xs[...], bm[...], cm[...], z[...], dt_ref[...], dtb_ref[...],
                         al_ref[...], ds_ref[...], bn_ref[...])
        gr = vjp((dhs[g], do_ref[...].astype(f32)))
        dhs[g] = gr[0]
        dxs[...] = gr[1]
        dbm[...] = gr[2]
        dcm[...] = gr[3]
        dz[...] = gr[4].astype(dz.dtype)
        ddt[...] += gr[5]
        ddtb[...] += gr[6]
        dal[...] += gr[7]
        dds[...] += gr[8]
        dbn[g] += gr[9]

    hb_vec = pl.BlockSpec((1, HB), lambda c, g: (0, 0))
    return pl.pallas_call(
        body, name=name, grid=(nc, B_GROUPS),
        in_specs=_ssd_in_specs(D, HB, rev) + [
            pl.BlockSpec((None, None, B_STATE, GW), lambda c, g: (rev(c), g, 0, 0)),
            pl.BlockSpec((B_CHUNK, GW), lambda c, g: (rev(c), ooff + g))],
        out_specs=[pl.BlockSpec((B_CHUNK, GW), lambda c, g: (rev(c), g)),
                   pl.BlockSpec((B_CHUNK, B_STATE), lambda c, g: (rev(c), g)),
                   pl.BlockSpec((B_CHUNK, B_STATE), lambda c, g: (rev(c), g)),
                   pl.BlockSpec((B_CHUNK, GW), lambda c, g: (rev(c), g)),
                   pl.BlockSpec((B_CHUNK, HB), lambda c, g: (rev(c), 0)),
                   hb_vec, hb_vec, hb_vec,
                   pl.BlockSpec((B_GROUPS, 1, GW), lambda c, g: (0, 0, 0))],
        out_shape=[jax.ShapeDtypeStruct((S, D), f32),
                   jax.ShapeDtypeStruct((S, B_GROUPS * B_STATE), f32),
                   jax.ShapeDtypeStruct((S, B_GROUPS * B_STATE), f32),
                   jax.ShapeDtypeStruct((S, D), bf16),
                   jax.ShapeDtypeStruct((S, HB), f32),
                   jax.ShapeDtypeStruct((1, HB), f32), jax.ShapeDtypeStruct((1, HB), f32),
                   jax.ShapeDtypeStruct((1, HB), f32),
                   jax.ShapeDtypeStruct((B_GROUPS, 1, GW), f32)],
        scratch_shapes=[pltpu.VMEM((B_GROUPS, B_STATE, GW), f32)],
        compiler_params=_cp("arbitrary", "arbitrary"))(
            xact, xact, xact, ymain, dtr, dtb, alog, dsk, bn, saved, dmixed)


def _conv_taps(xp, w_ref, b_ref, r0, K):
    acc = jnp.broadcast_to(b_ref[...], (CONV_ROWS, b_ref.shape[1]))
    for k in range(K):
        acc = acc + w_ref[k:k + 1, :] * xp[r0 + CONV_PAD - (K - 1) + k:r0 + CONV_PAD - (K - 1) + k + CONV_ROWS, :]
    return acc


def conv_fwd(name, x, xoff, w, b, act, out_dtype):
    S = x.shape[0]
    K, CW = w.shape
    tc = CONV_CH

    def body(x_ref, w_ref, b_ref, o_ref, xp):
        xp[0:CONV_PAD, :] = jnp.zeros((CONV_PAD, tc), f32)
        xp[CONV_PAD:CONV_PAD + S, :] = x_ref[...].astype(f32)
        for r0 in range(0, S, CONV_ROWS):
            acc = _conv_taps(xp, w_ref, b_ref, r0, K)
            if act:
                acc = _silu(acc)
            o_ref[r0:r0 + CONV_ROWS, :] = acc.astype(o_ref.dtype)

    return pl.pallas_call(
        body, name=name, grid=(CW // tc,),
        in_specs=[pl.BlockSpec((S, tc), lambda j: (0, j + xoff)),
                  pl.BlockSpec((K, tc), lambda j: (0, j)),
                  pl.BlockSpec((1, tc), lambda j: (0, j))],
        out_specs=pl.BlockSpec((S, tc), lambda j: (0, j)),
        out_shape=jax.ShapeDtypeStruct((S, CW), out_dtype),
        scratch_shapes=[pltpu.VMEM((S + CONV_PAD, tc), f32)],
        compiler_params=_cp("parallel"))(x, w, b)


def conv_bwd(name, x, xoff, w, b, dout, act, dx_dtype):
    S = x.shape[0]
    K, CW = w.shape
    tc = CONV_CH

    def body(x_ref, w_ref, b_ref, d_ref, dx_ref, dw_ref, db_ref, xp, dp):
        xp[0:CONV_PAD, :] = jnp.zeros((CONV_PAD, tc), f32)
        xp[CONV_PAD:CONV_PAD + S, :] = x_ref[...].astype(f32)
        dp[S:S + CONV_PAD, :] = jnp.zeros((CONV_PAD, tc), f32)
        db = jnp.zeros((1, tc), f32)
        for r0 in range(0, S, CONV_ROWS):
            d = d_ref[r0:r0 + CONV_ROWS, :].astype(f32)
            if act:
                pre = _conv_taps(xp, w_ref, b_ref, r0, K)
                s = _sigmoid(pre)
                d = d * (s + pre * s * (1.0 - s))
            dp[r0:r0 + CONV_ROWS, :] = d
            db = db + jnp.sum(d, axis=0, keepdims=True)
        db_ref[...] = db
        for r0 in range(0, S, CONV_ROWS):
            acc = jnp.zeros((CONV_ROWS, tc), f32)
            for k in range(K):
                acc = acc + w_ref[k:k + 1, :] * dp[r0 + (K - 1 - k):r0 + (K - 1 - k) + CONV_ROWS, :]
            dx_ref[r0:r0 + CONV_ROWS, :] = acc.astype(dx_ref.dtype)
        for k in range(K):
            acc = jnp.zeros((1, tc), f32)
            for r0 in range(0, S, CONV_ROWS):
                lo = r0 + CONV_PAD - (K - 1) + k
                acc = acc + jnp.sum(dp[r0:r0 + CONV_ROWS, :] * xp[lo:lo + CONV_ROWS, :], axis=0, keepdims=True)
            dw_ref[k:k + 1, :] = acc

    return pl.pallas_call(
        body, name=name, grid=(CW // tc,),
        in_specs=[pl.BlockSpec((S, tc), lambda j: (0, j + xoff)),
                  pl.BlockSpec((K, tc), lambda j: (0, j)),
                  pl.BlockSpec((1, tc), lambda j: (0, j)),
                  pl.BlockSpec((S, tc), lambda j: (0, j))],
        out_specs=[pl.BlockSpec((S, tc), lambda j: (0, j)),
                   pl.BlockSpec((K, tc), lambda j: (0, j)),
                   pl.BlockSpec((1, tc), lambda j: (0, j))],
        out_shape=[jax.ShapeDtypeStruct((S, CW), dx_dtype),
                   jax.ShapeDtypeStruct((K, CW), f32),
                   jax.ShapeDtypeStruct((1, CW), f32)],
        scratch_shapes=[pltpu.VMEM((S + CONV_PAD, tc), f32), pltpu.VMEM((S + CONV_PAD, tc), f32)],
        compiler_params=_cp("parallel"))(x, w, b, dout)


def loss_head(name, y, target, tm):
    S, D = y.shape

    def body(y_ref, t_ref, dy_ref, l_ref):
        @pl.when(pl.program_id(0) == 0)
        def _():
            l_ref[...] = jnp.zeros_like(l_ref)

        err = y_ref[...] - t_ref[...]
        dy_ref[...] = err * (1.0 / D)
        l_ref[...] += jnp.sum(err * err) * (0.5 / D)

    dy, l = pl.pallas_call(
        body, name=name, grid=(S // tm,),
        in_specs=[pl.BlockSpec((tm, D), lambda i: (i, 0))] * 2,
        out_specs=[pl.BlockSpec((tm, D), lambda i: (i, 0)), pl.BlockSpec((8, LANES), lambda i: (0, 0))],
        out_shape=[jax.ShapeDtypeStruct((S, D), f32), jax.ShapeDtypeStruct((8, LANES), f32)],
        compiler_params=_cp("arbitrary"))(y, target)
    return dy, l[0, 0]


def _flat2d_tiles(rows, cols, itemsize, target_bytes):
    tc = _tile(cols, 1024) if cols % LANES == 0 else cols
    cap = max(8, target_bytes // (tc * itemsize))
    tr = _tile(rows, cap, 16) if rows % 16 == 0 else rows
    return tr, tc


def adamw(name, w, g, m, v):
    shape = w.shape
    cols = shape[-1]
    rows = math.prod(shape[:-1])
    tr, tc = _flat2d_tiles(rows, cols, 4, 1 << 20)
    c1 = 1.0 - ADAM_B1 ** ADAM_STEP
    c2 = 1.0 - ADAM_B2 ** ADAM_STEP

    def body(w_ref, g_ref, m_ref, v_ref, d_ref, nm_ref, nv_ref):
        gg = g_ref[...]
        nm = ADAM_B1 * m_ref[...] + (1.0 - ADAM_B1) * gg
        nv = ADAM_B2 * v_ref[...] + (1.0 - ADAM_B2) * (gg * gg)
        d_ref[...] = -ADAM_LR * ((nm / c1) / (jnp.sqrt(nv / c2) + ADAM_EPS) + ADAM_WD * w_ref[...])
        nm_ref[...] = nm
        nv_ref[...] = nv

    spec = pl.BlockSpec((tr, tc), lambda i, j: (i, j))
    outs = pl.pallas_call(
        body, name=name, grid=(rows // tr, cols // tc), in_specs=[spec] * 4, out_specs=[spec] * 3,
        out_shape=[jax.ShapeDtypeStruct((rows, cols), f32)] * 3,
        compiler_params=_cp("parallel", "parallel"))(*[a.reshape(rows, cols) for a in (w, g, m, v)])
    return [o.reshape(shape) for o in outs]


def _core_index():
    return lax.axis_index("c").astype(jnp.int32).reshape(1)


def _half_rows_tile(Rh, C):
    return _tile(Rh, max(16, (2 << 20) // (C * 2)), 16)


def rs_add(name, G, buf):
    P, _, Rh, C = G.shape
    tr = _half_rows_tile(Rh, C)

    def body(c_ref, g_ref, b_ref, o_ref):
        o_ref[...] = (g_ref[...].astype(f32) + b_ref[...].astype(f32)).astype(o_ref.dtype)

    return pl.pallas_call(
        body, name=name,
        grid_spec=pltpu.PrefetchScalarGridSpec(
            num_scalar_prefetch=1, grid=(P, Rh // tr),
            in_specs=[pl.BlockSpec((None, None, tr, C), lambda p, i, c: (p, c[0], i, 0)),
                      pl.BlockSpec((None, tr, C), lambda p, i, c: (p, i, 0))],
            out_specs=pl.BlockSpec((None, tr, C), lambda p, i, c: (p, i, 0))),
        out_shape=jax.ShapeDtypeStruct((P, Rh, C), bf16),
        compiler_params=_cp("parallel", "parallel"))(_core_index(), G, buf)


def _chip_indices():
    x, y, c = lax.axis_index("x"), lax.axis_index("y"), lax.axis_index("c")
    ids = [2 * x + y] + [2 * _flip(x, fx) + _flip(y, fy) for fx, fy in _CHIP_FLIPS] + [c]
    return [i.astype(jnp.int32).reshape(1) for i in ids]


def rs_sum4(name, pair, buf, final, layer):
    P, Rh, C = buf.shape
    tr = _half_rows_tile(Rh, C)

    def body(i0, i1, i2, i3, ic, b0, b1, b2, b3, f_ref, o_ref):
        o_ref[...] = ((b0[...].astype(f32) + b1[...].astype(f32)) + b2[...].astype(f32)) + b3[...].astype(f32)

    blk = (None, tr, C)
    return pl.pallas_call(
        body, name=name,
        grid_spec=pltpu.PrefetchScalarGridSpec(
            num_scalar_prefetch=5, grid=(Rh // tr,),
            in_specs=[pl.BlockSpec(blk, lambda i, *ids, k=k: (ids[k][0], i, 0)) for k in range(P)] + [_ANY],
            out_specs=pl.BlockSpec((None, None, tr, C), lambda i, *ids: (layer, ids[4][0], i, 0))),
        out_shape=jax.ShapeDtypeStruct(final.shape, final.dtype),
        input_output_aliases={9: 0},
        compiler_params=_cp("parallel"))(*_chip_indices(), pair, buf, buf, buf, final)


def place_own(name, w, layer):
    _, R, C = w.shape
    tr = _tile(R, max(16, (2 << 20) // (C * 2)), 16)

    def body(q, w_ref, own_ref, land_ref):
        wb = w_ref[...].astype(bf16)
        own_ref[...] = wb
        land_ref[...] = wb

    return pl.pallas_call(
        body, name=name,
        grid_spec=pltpu.PrefetchScalarGridSpec(
            num_scalar_prefetch=1, grid=(R // tr,),
            in_specs=[pl.BlockSpec((None, tr, C), lambda i, q: (layer, i, 0))],
            out_specs=[pl.BlockSpec((tr, C), lambda i, q: (i, 0)),
                       pl.BlockSpec((None, tr, C), lambda i, q: (q[0], i, 0))]),
        out_shape=[jax.ShapeDtypeStruct((R, C), bf16), jax.ShapeDtypeStruct((N_CHIPS, R, C), bf16)],
        compiler_params=_cp("parallel"))(_chip_indices()[0], w)


_ANY = pl.BlockSpec(memory_space=pl.ANY)
_CHIP_FLIPS = ((1, 0), (0, 1), (1, 1))


def _place():
    return lax.axis_index("x"), lax.axis_index("y"), lax.axis_index("c")


def _flip(v, f):
    return 1 - v if f else v


def _remote(src, dst, ssem, rsem, dev):
    return pltpu.make_async_remote_copy(src_ref=src, dst_ref=dst, send_sem=ssem, recv_sem=rsem,
                                        device_id=dev, device_id_type=MESH)


_HBM = pl.BlockSpec(memory_space=pltpu.HBM)
_SEM = pl.BlockSpec(memory_space=pltpu.SEMAPHORE)
_DATAFLOW = pltpu.SideEffectType.DATAFLOW_SIDE_EFFECTING
_TOKEN = jax.ShapeDtypeStruct((8, LANES), f32)


def _in_hbm(a):
    return pltpu.with_memory_space_constraint(a, pltpu.HBM)


def _hbm_like(a):
    return pltpu.HBM(a.shape, a.dtype)


def gather_ici_start(name, groups):
    sizes = [len(g) for g in groups]
    owns = [o for g in groups for o, _ in g]
    lands = [l for g in groups for _, l in g]
    n, ng = len(owns), len(groups)

    def body(*refs):
        own, land = refs[:n], refs[n:2 * n]
        sems = refs[2 * n:2 * n + 2 * ng]
        token = refs[-1]
        x, y, c = _place()
        q = 2 * x + y
        t = 0
        for gi, size in enumerate(sizes):
            for j in range(size):
                for k, (fx, fy) in enumerate(_CHIP_FLIPS):
                    _remote(own[t].at[c], land[t].at[q, c], sems[2 * gi].at[3 * j + k], sems[2 * gi + 1].at[3 * j + k],
                            (_flip(x, fx), _flip(y, fy), c)).start()
                t += 1
        token[...] = jnp.zeros_like(token)

    sem_shapes = [pltpu.SemaphoreType.DMA((3 * size,)) for size in sizes for _ in range(2)]
    res = pl.pallas_call(
        body, name=name,
        in_specs=[_HBM] * (2 * n),
        out_specs=[_SEM] * (2 * ng) + [_HBM] * (2 * n) + [pl.BlockSpec(memory_space=pltpu.VMEM)],
        out_shape=sem_shapes + [_hbm_like(a) for a in owns + lands] + [_TOKEN],
        input_output_aliases={i: 2 * ng + i for i in range(2 * n)},
        compiler_params=pltpu.CompilerParams(has_side_effects=_DATAFLOW),
    )(*[_in_hbm(a) for a in owns + lands])
    own_thru, land_thru = res[2 * ng:2 * ng + n], res[2 * ng + n:2 * ng + 2 * n]
    handles, t = [], 0
    for gi, size in enumerate(sizes):
        handles.append((res[2 * gi], res[2 * gi + 1], list(own_thru[t:t + size]), list(land_thru[t:t + size])))
        t += size
    return handles


def gather_ici_wait(name, handle, after):
    send, recv, owns, lands = handle
    n = len(owns)

    def body(*refs):
        own, land = refs[:n], refs[n:2 * n]
        send_ref, recv_ref = refs[2 * n], refs[2 * n + 1]
        x, y, c = _place()
        for j in range(n):
            for k, (fx, fy) in enumerate(_CHIP_FLIPS):
                px, py = _flip(x, fx), _flip(y, fy)
                cp = _remote(own[j].at[c], land[j].at[2 * px + py, c], send_ref.at[3 * j + k], recv_ref.at[3 * j + k],
                             (px, py, c))
                cp.wait_send()
                cp.wait_recv()

    res = pl.pallas_call(
        body, name=name,
        in_specs=[_HBM] * (2 * n) + [_SEM, _SEM, _ANY],
        out_specs=[_HBM] * (2 * n),
        out_shape=[_hbm_like(a) for a in owns + lands],
        input_output_aliases={i: i for i in range(2 * n)},
        compiler_params=pltpu.CompilerParams(has_side_effects=_DATAFLOW),
    )(*owns, *lands, send, recv, after)
    return list(res[n:])


def gather_forward(name, lands):
    T = len(lands)

    def body(*refs):
        o = refs[T:2 * T]
        send, recv = refs[2 * T:]
        x, y, c = _place()
        sib = (x, y, 1 - c)
        cps = []
        for t in range(T):
            for k, (fx, fy) in enumerate(_CHIP_FLIPS):
                slab = o[t].at[2 * _flip(x, fx) + _flip(y, fy), c]
                cp = _remote(slab, slab, send.at[t, k], recv.at[t, k], sib)
                cp.start()
                cps.append(cp)
        for t in range(T):
            for k, (fx, fy) in enumerate(_CHIP_FLIPS):
                slab = o[t].at[2 * _flip(x, fx) + _flip(y, fy), 1 - c]
                _remote(slab, slab, send.at[t, k], recv.at[t, k], sib).wait_recv()
        for cp in cps:
            cp.wait_send()

    return pl.pallas_call(
        body, name=name, in_specs=[_ANY] * T, out_specs=[_ANY] * T,
        out_shape=[jax.ShapeDtypeStruct(a.shape, a.dtype) for a in lands],
        input_output_aliases={t: t for t in range(T)},
        scratch_shapes=[pltpu.SemaphoreType.DMA((T, 3)), pltpu.SemaphoreType.DMA((T, 3))],
        )(*lands)


def rs_sibling(name, Gs):
    T = len(Gs)

    def body(*refs):
        g, o = refs[:T], refs[T:2 * T]
        send, recv = refs[2 * T:]
        x, y, c = _place()
        cps = []
        for t in range(T):
            cp = _remote(g[t].at[:, 1 - c], o[t], send.at[t], recv.at[t], (x, y, 1 - c))
            cp.start()
            cps.append(cp)
        for cp in cps:
            cp.wait()

    return pl.pallas_call(
        body, name=name, in_specs=[_ANY] * T, out_specs=[_ANY] * T,
        out_shape=[jax.ShapeDtypeStruct((G.shape[0],) + G.shape[2:], G.dtype) for G in Gs],
        scratch_shapes=[pltpu.SemaphoreType.DMA((T,)), pltpu.SemaphoreType.DMA((T,))],
        )(*Gs)


def reduce_ici_start(name, Ss):
    T = len(Ss)
    lands = [lax.empty(S.shape, S.dtype) for S in Ss]

    def body(*refs):
        s, land = refs[:T], refs[T:2 * T]
        send, recv = refs[2 * T], refs[2 * T + 1]
        token = refs[-1]
        x, y, c = _place()
        q = 2 * x + y
        for t in range(T):
            for k, (fx, fy) in enumerate(_CHIP_FLIPS):
                px, py = _flip(x, fx), _flip(y, fy)
                _remote(s[t].at[2 * px + py], land[t].at[q], send.at[3 * t + k], recv.at[3 * t + k],
                        (px, py, c)).start()
        token[...] = jnp.zeros_like(token)

    res = pl.pallas_call(
        body, name=name,
        in_specs=[_HBM] * (2 * T),
        out_specs=[_SEM, _SEM] + [_HBM] * (2 * T) + [pl.BlockSpec(memory_space=pltpu.VMEM)],
        out_shape=[pltpu.SemaphoreType.DMA((3 * T,)), pltpu.SemaphoreType.DMA((3 * T,))]
        + [_hbm_like(a) for a in Ss + lands] + [_TOKEN],
        input_output_aliases={i: 2 + i for i in range(2 * T)},
        compiler_params=pltpu.CompilerParams(has_side_effects=_DATAFLOW),
    )(*[_in_hbm(a) for a in Ss + lands])
    return res[0], res[1], list(res[2:2 + T]), list(res[2 + T:2 + 2 * T]), res[-1]


def reduce_ici_wait(name, handle, after):
    send, recv, Ss, lands, _ = handle
    T = len(Ss)

    def body(*refs):
        s, land = refs[:T], refs[T:2 * T]
        send_ref, recv_ref = refs[2 * T], refs[2 * T + 1]
        x, y, c = _place()
        for t in range(T):
            for k, (fx, fy) in enumerate(_CHIP_FLIPS):
                px, py = _flip(x, fx), _flip(y, fy)
                cp = _remote(s[t].at[2 * px + py], land[t].at[2 * px + py], send_ref.at[3 * t + k], recv_ref.at[3 * t + k],
                             (px, py, c))
                cp.wait_send()
                cp.wait_recv()

    res = pl.pallas_call(
        body, name=name,
        in_specs=[_HBM] * (2 * T) + [_SEM, _SEM, _ANY],
        out_specs=[_HBM] * (2 * T),
        out_shape=[_hbm_like(a) for a in Ss + lands],
        input_output_aliases={i: i for i in range(2 * T)},
        compiler_params=pltpu.CompilerParams(has_side_effects=_DATAFLOW),
    )(*Ss, *lands, send, recv, after)
    return list(res[:T]), list(res[T:])


def rs_share(name, tots, layers):
    T = len(tots)

    def body(*refs):
        s, o = refs[:T], refs[T:2 * T]
        send, recv = refs[2 * T:]
        x, y, c = _place()
        cps = []
        for t in range(T):
            mine = o[t].at[layers[t], c]
            cp = _remote(mine, mine, send.at[t], recv.at[t], (x, y, 1 - c))
            cp.start()
            cps.append(cp)
        for t in range(T):
            other = o[t].at[layers[t], 1 - c]
            _remote(other, other, send.at[t], recv.at[t], (x, y, 1 - c)).wait_recv()
        for cp in cps:
            cp.wait_send()

    return pl.pallas_call(
        body, name=name, in_specs=[_ANY] * T, out_specs=[_ANY] * T,
        out_shape=[jax.ShapeDtypeStruct(s.shape, s.dtype) for s in tots],
        input_output_aliases={t: t for t in range(T)},
        scratch_shapes=[pltpu.SemaphoreType.DMA((T,)), pltpu.SemaphoreType.DMA((T,))],
        )(*tots)


def all_reduce_small(name, vec):
    rows = vec.shape[0]
    flips = [(fx, fy, fc) for fx in (0, 1) for fy in (0, 1) for fc in (0, 1)][1:]

    def body(v_ref, o_ref, buf, send, recv):
        x, y, c = _place()
        me = 4 * x + 2 * y + c
        buf[me] = v_ref[...]
        cps = []
        for k, (fx, fy, fc) in enumerate(flips):
            cp = _remote(buf.at[me], buf.at[me], send.at[k], recv.at[k],
                         (_flip(x, fx), _flip(y, fy), _flip(c, fc)))
            cp.start()
            cps.append(cp)
        for k, (fx, fy, fc) in enumerate(flips):
            slab = buf.at[4 * _flip(x, fx) + 2 * _flip(y, fy) + _flip(c, fc)]
            _remote(slab, slab, send.at[k], recv.at[k], (x, y, c)).wait_recv()
        for cp in cps:
            cp.wait_send()
        acc = buf[0]
        for d in range(1, N_DEV):
            acc = acc + buf[d]
        o_ref[...] = acc

    return pl.pallas_call(
        body, name=name,
        in_specs=[pl.BlockSpec(memory_space=pltpu.VMEM)], out_specs=pl.BlockSpec(memory_space=pltpu.VMEM),
        out_shape=jax.ShapeDtypeStruct((rows, LANES), f32),
        scratch_shapes=[pltpu.VMEM((N_DEV, rows, LANES), f32),
                        pltpu.SemaphoreType.DMA((N_DEV - 1,)), pltpu.SemaphoreType.DMA((N_DEV - 1,))],
        compiler_params=pltpu.CompilerParams(vmem_limit_bytes=V7X_VMEM_LIMIT))(vec)


def _pack(arrays):
    flat = jnp.concatenate([a.reshape(-1) for a in arrays])
    n = flat.shape[0]
    rows = -(-n // (8 * LANES)) * 8
    return jnp.pad(flat, (0, rows * LANES - n)).reshape(rows, LANES)


def _unpack(vec, shapes):
    flat = vec.reshape(-1)
    out, pos = [], 0
    for s in shapes:
        n = math.prod(s)
        out.append(flat[pos:pos + n].reshape(s))
        pos += n
    return out


def _f_first(x, g):
    return x, _rms(x, g)


def _f_mid(h, m, gp, gn):
    h1 = h + _rms(m, gp)
    return h1, _rms(h1, gn)


def _f_mid_bias(h, m, b, gp, gn):
    h1 = h + _rms(m + b, gp)
    return h1, _rms(h1, gn)


def _f_last(h, m, gp):
    return (h + _rms(m, gp),)


def _f_swiglu(gate, up):
    return (_silu(gate) * up,)


def _f_glu(a, g, ba, bg):
    return ((a + ba) * _sigmoid(g + bg),)


def _f_ln_silu(x, g, b):
    mu = jnp.mean(x, axis=-1, keepdims=True)
    xc = x - mu
    y = xc * lax.rsqrt(jnp.mean(xc * xc, axis=-1, keepdims=True) + LN_EPS) * g + b
    return (_silu(y),)


def _f_lower_bounds(logits):
    n = logits.shape[0]
    e = jnp.exp(logits - jnp.max(logits, axis=0, keepdims=True))
    p = e / jnp.sum(e, axis=0, keepdims=True)
    layer = lax.broadcasted_iota(jnp.int32, logits.shape, 0)
    out = -jnp.broadcast_to(p[0:1, :], logits.shape)
    for j in range(n):
        out = out + jnp.where(layer >= j, p[j:j + 1, :], 0.0)
    return (out,)


WEIGHT_NAMES = ['mix_pre_g', 'mix_post_g', 'ffn_pre_g', 'ffn_post_g', 'hgrn_lb_logits', 'even_w_in',
                'hgrn_norm_g', 'ssd_conv_w', 'ssd_conv_b', 'ssd_dt_bias', 'ssd_a_log', 'ssd_d', 'ssd_norm_g',
                'even_w_out', 'conf_w1', 'conf_b1', 'conf_dw_w', 'conf_dw_b', 'conf_ln_g', 'conf_ln_b',
                'conf_w2', 'conf_b2', 'ffn_w_gate', 'ffn_w_up', 'ffn_w_down']
BIG = ['even_w_in', 'even_w_out', 'conf_w1', 'conf_w2', 'ffn_w_gate', 'ffn_w_up', 'ffn_w_down']
SMALL_SHARDED = {'ssd_conv_w': 2, 'conf_b1': 1, 'conf_dw_w': 2, 'conf_dw_b': 1, 'conf_ln_g': 1,
                 'conf_ln_b': 1, 'conf_b2': 1}


def _train_step(x, target, w, m, v):
    S, D = x.shape[1], x.shape[2]
    x2, t2 = x[0], target[0]
    NL = w['mix_pre_g'].shape[0]
    HB = w['ssd_dt_bias'].shape[1]
    GN = B_GROUPS * B_STATE
    xw, yw, cw = _place()
    chip = 2 * xw + yw
    tm = _tile(S, 128, 8)
    row1 = lambda a, i: a[i:i + 1]

    sharded = list(SMALL_SHARDED)
    placed = []
    for n in sharded:
        ax, a = SMALL_SHARDED[n], w[n]
        full = jnp.zeros(a.shape[:ax] + (a.shape[ax] * N_CHIPS,) + a.shape[ax + 1:], f32)
        start = [0] * a.ndim
        start[ax] = chip * a.shape[ax]
        placed.append(lax.dynamic_update_slice(full, jnp.where(cw == 0, a, 0.0), start))
    whole = dict(zip(sharded, _unpack(all_reduce_small("gather_small", _pack(placed)), [p.shape for p in placed])))
    small = {n: whole.get(n, w[n]) for n in WEIGHT_NAMES if n not in BIG}

    own, land = {}, {}
    for n in BIG:
        for l in range(w[n].shape[0]):
            own[n, l], land[n, l] = place_own("place_own", w[n], l)

    def mixer_keys(layer):
        names = ('even_w_in', 'even_w_out') if layer % 2 == 0 else ('conf_w1', 'conf_w2')
        return [(n, layer // 2) for n in names]

    def ffn_keys(layer):
        return [(n, layer) for n in ('ffn_w_gate', 'ffn_w_up', 'ffn_w_down')]

    groups = [keys(layer) for layer in range(NL) for keys in (mixer_keys, ffn_keys)]
    halves = lambda a: a.reshape(a.shape[:-2] + (2, a.shape[-2] // 2, a.shape[-1]))
    handles = gather_ici_start("gather_start", [[(halves(own[k]), halves(land[k])) for k in g] for g in groups])
    W = {}

    def fetch(gi, after):
        arrived = gather_ici_wait(f"gather_wait_{gi}", handles[gi], after)
        for k, a in zip(groups[gi], gather_forward("gather_forward", arrived)):
            W[k] = a.reshape((N_CHIPS, 1) + own[k].shape)

    WM = 6 * D + 2 * GN
    w_main, w_dt = {}, {}

    n_even = small['hgrn_lb_logits'].shape[0]
    (lbs,) = _stage_fwd("lower_bounds", _f_lower_bounds, [small['hgrn_lb_logits']], [], [(D, f32)], tm=n_even)
    saved = []
    h = x2
    (u,) = _stage_fwd("pre_norm", lambda a, g: (_rms(a, g),), [h], [row1(small['mix_pre_g'], 0)],
                      [(D, bf16)], tm=tm)
    for layer in range(NL):
        li = layer // 2
        r = {'h': h, 'u': u}
        fetch(2 * layer, h)
        if layer % 2 == 0:
            win = jnp.concatenate([W['even_w_in', li][j, 0] for j in range(N_CHIPS)], axis=-1)
            w_main[li], w_dt[li] = win[None, None, :, :WM], win[None, None, :, WM:]
            r['ymain'] = mm_nn_col("in_proj", u, w_main[li], 0)
            r['dtr'] = mm_nn_col("in_proj_dt", u, w_dt[li], 0)
            r['xact'] = conv_fwd("ssd_conv", r['ymain'], 5 * D // CONV_CH, small['ssd_conv_w'][li],
                                 row1(small['ssd_conv_b'], li), True, f32)
            o_a, r['hg_st'] = hgrn_fwd("hgrn", r['ymain'], row1(lbs, li), row1(small['hgrn_norm_g'], li), D)
            o_b, r['ssd_st'] = ssd_fwd("ssd", r['xact'], r['ymain'], r['dtr'], row1(small['ssd_dt_bias'], li),
                                       row1(small['ssd_a_log'], li), row1(small['ssd_d'], li),
                                       row1(small['ssd_norm_g'], li), D)
            r['mixed'] = jnp.concatenate([o_a, o_b], axis=1)
            r['m'] = mm_nn_row("out_proj", r['mixed'], W['even_w_out', li], 0)
            mid_fn, mid_par = _f_mid, []
        else:
            r['c1'] = mm_nn_col("conf_in", u, W['conf_w1', li], 0)
            b1 = row1(small['conf_b1'], li)
            tn = _tile(D, 512)
            (r['glu'],) = _stage_fwd("conf_glu", _f_glu, [r['c1'], (r['c1'], D // tn)], [b1, (b1, D // tn)],
                                     [(D, f32)], tm=tm, tn=tn)
            r['cc'] = conv_fwd("conf_conv", r['glu'], 0, small['conf_dw_w'][li], row1(small['conf_dw_b'], li),
                               False, f32)
            (r['c2'],) = _stage_fwd("conf_ln", _f_ln_silu, [r['cc']],
                                    [row1(small['conf_ln_g'], li), row1(small['conf_ln_b'], li)],
                                    [(D, bf16)], tm=tm)
            r['m'] = mm_nn_row("conf_out", r['c2'], W['conf_w2', li], 0)
            mid_fn, mid_par = _f_mid_bias, [row1(small['conf_b2'], li)]
        r['mid_fn'] = mid_fn
        r['mid_par'] = mid_par + [row1(small['mix_post_g'], layer), row1(small['ffn_pre_g'], layer)]
        r['h1'], r['u2'] = _stage_fwd("mid_norm", mid_fn, [h, r['m']], r['mid_par'], [(D, f32), (D, bf16)], tm=tm)
        fetch(2 * layer + 1, r['u2'])
        r['gate'] = mm_nn_col("ffn_gate", r['u2'], W['ffn_w_gate', layer], 0, bf16)
        r['up'] = mm_nn_col("ffn_up", r['u2'], W['ffn_w_up', layer], 0, bf16)
        F = r['gate'].shape[1]
        tnf = _tile(F, 1024)
        (r['act'],) = _stage_fwd("swiglu", _f_swiglu, [r['gate'], r['up']], [], [(F, bf16)], tm=tm, tn=tnf)
        r['dn'] = mm_nn_row("ffn_down", r['act'], W['ffn_w_down', layer], 0)
        if layer + 1 < NL:
            r['end_fn'] = _f_mid
            r['end_par'] = [row1(small['ffn_post_g'], layer), row1(small['mix_pre_g'], layer + 1)]
            h, u = _stage_fwd("end_norm", _f_mid, [r['h1'], r['dn']], r['end_par'], [(D, f32), (D, bf16)], tm=tm)
        else:
            r['end_fn'] = _f_last
            r['end_par'] = [row1(small['ffn_post_g'], layer)]
            (h,) = _stage_fwd("last_norm", _f_last, [r['h1'], r['dn']], r['end_par'], [(D, f32)], tm=tm)
        saved.append(r)

    dy, loss_local = loss_head("loss_head", h, t2, tm)
    loss = lax.psum(loss_local, ("x", "y", "c"))

    gs = {n: jnp.zeros(small[n].shape, f32) for n in small}

    def put(n, i, val):
        gs[n] = gs[n].at[i].add(val.reshape(gs[n].shape[1:]))

    final = {n: lax.empty((w[n].shape[0], 2, w[n].shape[1] // 2, w[n].shape[2]), f32) for n in BIG}
    pending = []

    def reduce_start(gi, parts):
        parts = [halves(p) for p in parts]
        from_sib = rs_sibling("reduce_sibling", parts)
        sums = [rs_add("reduce_add", a, b) for a, b in zip(parts, from_sib)]
        handle = reduce_ici_start(f"reduce_start_{gi}", sums)
        pending.append((gi, handle))
        return handle[-1][0, 0]

    def reduce_finish(after):
        gi, handle = pending.pop(0)
        keys = groups[gi]
        sums, lands = reduce_ici_wait(f"reduce_wait_{gi}", handle, after)
        for (n, l), s_, b_ in zip(keys, sums, lands):
            final[n] = rs_sum4("reduce_sum", s_, b_, final[n], l)
        names = [n for n, _ in keys]
        shared = rs_share("reduce_share", [final[n] for n in names], [l for _, l in keys])
        final.update(zip(names, shared))

    dh = dy
    du_parts = None
    started = None
    for layer in reversed(range(NL)):
        li = layer // 2
        r = saved[layer]
        cts = [[dh]] if du_parts is None else [[dh], du_parts]
        par = r['end_par'] if started is None else [r['end_par'][0] + started] + r['end_par'][1:]
        (dh1, d_dn), pg = _stage_bwd("end_norm_bwd", r['end_fn'], [r['h1'], r['dn']], par, cts,
                                     [f32, bf16], tm=tm)
        put('ffn_post_g', layer, pg[0])
        if du_parts is not None:
            put('mix_pre_g', layer + 1, pg[1])
        d_act = mm_nt_row("ffn_down_dx", d_dn, W['ffn_w_down', layer], 0, bf16)
        g_down = mm_tn_row("ffn_down_dw", r['act'], d_dn, N_CHIPS)
        F = r['gate'].shape[1]
        (d_gate, d_up), _ = _stage_bwd("swiglu_bwd", _f_swiglu, [r['gate'], r['up']], [], [[d_act]],
                                       [bf16, bf16], tm=tm, tn=_tile(F, 1024))
        du_a = mm_nt_col("ffn_gate_dx", d_gate, W['ffn_w_gate', layer], 0)
        du_b = mm_nt_col("ffn_up_dx", d_up, W['ffn_w_up', layer], 0)
        g_gate = mm_tn_col("ffn_gate_dw", r['u2'], d_gate, N_CHIPS)
        g_up = mm_tn_col("ffn_up_dw", r['u2'], d_up, N_CHIPS)
        started = reduce_start(2 * layer + 1, [g_gate, g_up, g_down])
        if len(pending) > 1:
            reduce_finish(du_b)
        par = r['mid_par'][:-1] + [r['mid_par'][-1] + started]
        (dh, dm), pg = _stage_bwd("mid_norm_bwd", r['mid_fn'], [r['h'], r['m']], par,
                                  [[dh1], [du_a, du_b]], [f32, bf16], tm=tm)
        put('mix_post_g', layer, pg[-2])
        put('ffn_pre_g', layer, pg[-1])
        if layer % 2 == 0:
            d_mixed = mm_nt_row("out_proj_dx", dm, W['even_w_out', li], 0)
            g_out = mm_tn_row("out_proj_dw", r['mixed'], dm, N_CHIPS)
            dxs, dbm, dcm, dz, ddt, ddtb, dal, dds, dbn = ssd_bwd(
                "ssd_bwd", r['xact'], r['ymain'], r['dtr'], row1(small['ssd_dt_bias'], li),
                row1(small['ssd_a_log'], li), row1(small['ssd_d'], li), row1(small['ssd_norm_g'], li),
                r['ssd_st'], d_mixed, D)
            put('ssd_dt_bias', li, ddtb)
            put('ssd_a_log', li, dal)
            put('ssd_d', li, dds)
            put('ssd_norm_g', li, dbn)
            d_xact = jnp.concatenate([dxs, dbm, dcm], axis=1)
            d_xbc, dcw, dcb = conv_bwd("ssd_conv_bwd", r['ymain'], 5 * D // CONV_CH, small['ssd_conv_w'][li],
                                       row1(small['ssd_conv_b'], li), d_xact, True, bf16)
            put('ssd_conv_w', li, dcw)
            put('ssd_conv_b', li, dcb)
            dq, df, dv, dg, dlb, dan = hgrn_bwd("hgrn_bwd", r['ymain'], row1(lbs, li), row1(small['hgrn_norm_g'], li),
                                               r['hg_st'], d_mixed, D)
            put('hgrn_norm_g', li, dan)
            r['dlb'] = dlb
            d_main = jnp.concatenate([dq, df, dv, dg, dz, d_xbc], axis=1)
            du_parts = [mm_nt_col("in_proj_dx", d_main, w_main[li], 0),
                        mm_nt_col("in_proj_dt_dx", ddt, w_dt[li], 0)]
            g_in = jnp.concatenate([mm_tn_col("in_proj_dw", r['u'], d_main, 1)[0],
                                    mm_tn_col("in_proj_dt_dw", r['u'], ddt, 1)[0]], axis=-1)
            CI = w['even_w_in'].shape[2]
            g_in = jnp.stack([g_in[:, j * CI:(j + 1) * CI] for j in range(N_CHIPS)])
            started = reduce_start(2 * layer, [g_in, g_out])
        else:
            put('conf_b2', li, pg[0])
            d_c2 = mm_nt_row("conf_out_dx", dm, W['conf_w2', li], 0)
            g_w2 = mm_tn_row("conf_out_dw", r['c2'], dm, N_CHIPS)
            (d_cc,), pl_ = _stage_bwd("conf_ln_bwd", _f_ln_silu, [r['cc']],
                                      [row1(small['conf_ln_g'], li), row1(small['conf_ln_b'], li)],
                                      [[d_c2]], [f32], tm=tm)
            put('conf_ln_g', li, pl_[0])
            put('conf_ln_b', li, pl_[1])
            d_glu, ddw, ddb = conv_bwd("conf_conv_bwd", r['glu'], 0, small['conf_dw_w'][li],
                                       row1(small['conf_dw_b'], li), d_cc, False, f32)
            put('conf_dw_w', li, ddw)
            put('conf_dw_b', li, ddb)
            b1 = row1(small['conf_b1'], li)
            tn = _tile(D, 512)
            (da, dg_), pb = _stage_bwd("conf_glu_bwd", _f_glu, [r['c1'], (r['c1'], D // tn)], [b1, (b1, D // tn)],
                                       [[d_glu]], [bf16, bf16], tm=tm, tn=tn)
            put('conf_b1', li, jnp.concatenate([pb[0], pb[1]], axis=1))
            d_c1 = jnp.concatenate([da, dg_], axis=1)
            du_parts = [mm_nt_col("conf_in_dx", d_c1, W['conf_w1', li], 0)]
            g_w1 = mm_tn_col("conf_in_dw", r['u'], d_c1, N_CHIPS)
            started = reduce_start(2 * layer, [g_w1, g_w2])
        reduce_finish(du_parts[0])
    (grad_x2,), pg = _stage_bwd("pre_norm_bwd", _f_first, [x2], [row1(small['mix_pre_g'], 0) + started],
                                [[dh], du_parts], [f32], tm=tm)
    put('mix_pre_g', 0, pg[0])
    reduce_finish(grad_x2)
    dlbs = jnp.concatenate([saved[2 * i]['dlb'] for i in range(n_even)], axis=0)
    (dlogits,), _ = _stage_bwd("lower_bounds_bwd", _f_lower_bounds, [small['hgrn_lb_logits']], [],
                               [[dlbs]], [f32], tm=n_even)
    gs['hgrn_lb_logits'] = dlogits

    names_s = [n for n in WEIGHT_NAMES if n not in BIG]
    summed = _unpack(all_reduce_small("reduce_small", _pack([gs[n] for n in names_s])),
                     [gs[n].shape for n in names_s])
    grads = {}
    for n, a in zip(names_s, summed):
        if n in SMALL_SHARDED:
            ax = SMALL_SHARDED[n]
            size = w[n].shape[ax]
            start = [0] * a.ndim
            start[ax] = chip * size
            a = lax.dynamic_slice(a, start, a.shape[:ax] + (size,) + a.shape[ax + 1:])
        grads[n] = a

    for n in BIG:
        grads[n] = final[n].reshape(w[n].shape)

    delta, new_m, new_v = {}, {}, {}
    for n in WEIGHT_NAMES:
        delta[n], new_m[n], new_v[n] = adamw("adamw", w[n], grads[n], m[n], v[n])
    return (loss, grad_x2[None], *[grads[n] for n in WEIGHT_NAMES], *[delta[n] for n in WEIGHT_NAMES],
            *[new_m[n] for n in WEIGHT_NAMES], *[new_v[n] for n in WEIGHT_NAMES])


def kernel(x, mix_pre_g, mix_post_g, ffn_pre_g, ffn_post_g, hgrn_lb_logits, even_w_in, hgrn_norm_g, ssd_conv_w, ssd_conv_b, ssd_dt_bias, ssd_a_log, ssd_d, ssd_norm_g, even_w_out, conf_w1, conf_b1, conf_dw_w, conf_dw_b, conf_ln_g, conf_ln_b, conf_w2, conf_b2, ffn_w_gate, ffn_w_up, ffn_w_down, loss_target, m_mix_pre_g, m_mix_post_g, m_ffn_pre_g, m_ffn_post_g, m_hgrn_lb_logits, m_even_w_in, m_hgrn_norm_g, m_ssd_conv_w, m_ssd_conv_b, m_ssd_dt_bias, m_ssd_a_log, m_ssd_d, m_ssd_norm_g, m_even_w_out, m_conf_w1, m_conf_b1, m_conf_dw_w, m_conf_dw_b, m_conf_ln_g, m_conf_ln_b, m_conf_w2, m_conf_b2, m_ffn_w_gate, m_ffn_w_up, m_ffn_w_down, v_mix_pre_g, v_mix_post_g, v_ffn_pre_g, v_ffn_post_g, v_hgrn_lb_logits, v_even_w_in, v_hgrn_norm_g, v_ssd_conv_w, v_ssd_conv_b, v_ssd_dt_bias, v_ssd_a_log, v_ssd_d, v_ssd_norm_g, v_even_w_out, v_conf_w1, v_conf_b1, v_conf_dw_w, v_conf_dw_b, v_conf_ln_g, v_conf_ln_b, v_conf_w2, v_conf_b2, v_ffn_w_gate, v_ffn_w_up, v_ffn_w_down):
    args = locals()
    w = {n: args[n] for n in WEIGHT_NAMES}
    m = {n: args["m_" + n] for n in WEIGHT_NAMES}
    v = {n: args["v_" + n] for n in WEIGHT_NAMES}
    return _train_step(x, loss_target, w, m, v)
```

```python
import functools
import math

import jax
import jax.numpy as jnp
from jax import lax
from jax.experimental import pallas as pl
from jax.experimental.pallas import tpu as pltpu

f32 = jnp.float32
bf16 = jnp.bfloat16
MESH = pl.DeviceIdType.MESH
HI = lax.Precision.HIGHEST

A_HEAD = 128
A_CHUNK = 64
A_SUB = 8
A_REF = 4
A_EXP_CAP = 60.0
A_HEADS_PER_STEP = 8
A_F_MIN = 1e-6
B_HEAD = 64
B_GROUPS = 4
B_STATE = 128
B_CONV = 4
B_CHUNK = 128
C_KERNEL = 31
RMS_EPS = 1e-6
LN_EPS = 1e-5
ADAM_LR = 0.001
ADAM_B1 = 0.9
ADAM_B2 = 0.999
ADAM_EPS = 1e-08
ADAM_WD = 0.01
ADAM_STEP = 10

N_CHIPS = 4
N_DEV = 8
V7X_VMEM_LIMIT = 56 * 1024 * 1024
LANES = 128
CONV_PAD = 32
CONV_ROWS = 128
CONV_CH = 256

NN = (((1,), (0,)), ((), ()))
NT = (((1,), (1,)), ((), ()))
TN = (((0,), (0,)), ((), ()))


def _tile(n, cap, unit=LANES):
    best = None
    for t in range(unit, min(n, cap) + 1, unit):
        if n % t == 0:
            best = t
    return n if best is None else best


def _cp(*sem):
    return pltpu.CompilerParams(dimension_semantics=sem, vmem_limit_bytes=V7X_VMEM_LIMIT)


def _sigmoid(x):
    return jax.nn.sigmoid(x)


def _silu(x):
    return x * jax.nn.sigmoid(x)


def _rms(x, g):
    return x * lax.rsqrt(jnp.mean(x * x, axis=-1, keepdims=True) + RMS_EPS) * g


def _pair(a):
    return a if isinstance(a, tuple) else (a, 0)


def _stage(name, fn, rows, params, outs, par_outs=(), *, tm, tn=None):
    rows = [_pair(r) for r in rows]
    params = [_pair(p) for p in params]
    S = rows[0][0].shape[0]
    n_in, n_o = len(rows) + len(params), len(outs)
    if tn is None:
        grid = (S // tm,)
        in_specs = [pl.BlockSpec((tm, a.shape[1]), lambda i: (i, 0)) for a, _ in rows]
        in_specs += [pl.BlockSpec(a.shape, lambda i: (0, 0)) for a, _ in params]
        out_specs = [pl.BlockSpec((tm, w), lambda i: (i, 0)) for w, _ in outs]
        out_specs += [pl.BlockSpec((k, w), lambda i: (0, 0)) for k, w in par_outs]
        row_axis = 0
        sem = ("arbitrary",) if par_outs else ("parallel",)
    else:
        grid = (outs[0][0] // tn, S // tm)
        in_specs = [pl.BlockSpec((tm, tn), lambda j, i, o=o: (i, j + o)) for _, o in rows]
        in_specs += [pl.BlockSpec((a.shape[0], tn), lambda j, i, o=o: (0, j + o)) for a, o in params]
        out_specs = [pl.BlockSpec((tm, tn), lambda j, i: (i, j)) for _ in outs]
        out_specs += [pl.BlockSpec((k, tn), lambda j, i: (0, j)) for k, _ in par_outs]
        row_axis = 1
        sem = ("parallel", "arbitrary") if par_outs else ("parallel", "parallel")
    out_shape = [jax.ShapeDtypeStruct((S, w), d) for w, d in outs]
    out_shape += [jax.ShapeDtypeStruct((k, w), f32) for k, w in par_outs]

    def body(*refs):
        res = fn(*[r[...] for r in refs[:n_in]])
        for r, v in zip(refs[n_in:n_in + n_o], res[:n_o]):
            r[...] = v.astype(r.dtype)
        if par_outs:
            acc_refs = refs[n_in + n_o:]

            @pl.when(pl.program_id(row_axis) == 0)
            def _():
                for r in acc_refs:
                    r[...] = jnp.zeros_like(r)

            for r, v in zip(acc_refs, res[n_o:]):
                r[...] += v

    return pl.pallas_call(
        body, name=name, grid=grid, in_specs=in_specs, out_specs=out_specs, out_shape=out_shape,
        compiler_params=_cp(*sem))(*[a for a, _ in rows], *[a for a, _ in params])


def _stage_fwd(name, fn, rows, params, outs, *, tm, tn=None):
    n_r = len(rows)

    def ffn(*t):
        return fn(*[v.astype(f32) for v in t[:n_r]], *t[n_r:])

    return _stage(name, ffn, rows, params, outs, tm=tm, tn=tn)


def _stage_bwd(name, fn, rows, params, cts, drow, *, tm, tn=None):
    rows = [_pair(r) for r in rows]
    params = [_pair(p) for p in params]
    n_r, n_p = len(rows), len(params)
    flat_ct = [_pair(c) for group in cts for c in group]
    counts = [len(group) for group in cts]
    need = [i for i, d in enumerate(drow) if d is not None]

    def bfn(*t):
        r = [v.astype(f32) for v in t[:n_r]]
        c = t[n_r:n_r + len(flat_ct)]
        p = list(t[n_r + len(flat_ct):])
        res, vjp = jax.vjp(fn, *r, *p)
        ct, pos = [], 0
        for o, k in zip(res, counts):
            s = c[pos].astype(f32)
            for e in range(1, k):
                s = s + c[pos + e].astype(f32)
            pos += k
            ct.append(s.astype(o.dtype))
        g = vjp(tuple(ct))
        return tuple(g[i] for i in need) + tuple(g[n_r:])

    if tn is None:
        outs = [(rows[i][0].shape[1], drow[i]) for i in need]
        par_outs = [p.shape for p, _ in params]
    else:
        w_all = flat_ct[0][0].shape[1]
        outs = [(w_all, drow[i]) for i in need]
        par_outs = [(p.shape[0], w_all) for p, _ in params]
    res = _stage(name, bfn, rows + flat_ct, params, outs, par_outs, tm=tm, tn=tn)
    return res[:len(need)], res[len(need):]


def _mm(name, dims, a, b, grid, a_spec, b_spec, o_spec, out_shape, acc_shape, a_fn=None, tail=None):
    nk = grid[2]
    a_list = list(a) if isinstance(a, (list, tuple)) else [a]
    na = len(a_list)
    t_fn, t_arrays = tail if tail is not None else (None, [])
    ne = len(t_arrays)
    multi = isinstance(out_shape, (list, tuple))

    def body(*refs):
        a_refs, b_ref, t_refs = refs[:na], refs[na], refs[na + 1:na + 1 + ne]
        o_refs, acc = refs[na + 1 + ne:-1], refs[-1]
        k = pl.program_id(2)

        @pl.when(k == 0)
        def _():
            acc[...] = jnp.zeros_like(acc)

        lhs = a_refs[0][...] if a_fn is None else a_fn(*[r[...] for r in a_refs])
        acc[...] += lax.dot_general(lhs.astype(bf16), b_ref[...].astype(bf16), dims, preferred_element_type=f32)

        @pl.when(k == nk - 1)
        def _():
            res = (acc[...],) if t_fn is None else t_fn(acc[...], *[r[...] for r in t_refs])
            for r, val in zip(o_refs, res):
                r[...] = val.astype(r.dtype)

    return pl.pallas_call(
        body, name=name, grid=grid, in_specs=[a_spec] * na + [b_spec] + [o_spec] * ne,
        out_specs=[o_spec] * len(out_shape) if multi else o_spec, out_shape=out_shape,
        scratch_shapes=[pltpu.VMEM(acc_shape, f32)],
        compiler_params=_cp("parallel", "parallel", "arbitrary"))(*a_list, b, *t_arrays)


def _mm_tiles(S):
    return _tile(S, 512)


def mm_nn_col(name, a, W, li, out_dtype=f32):
    P, _, K, C = W.shape
    S = a.shape[0]
    tm, tn, tk = _mm_tiles(S), _tile(C, 1536), _tile(K, 2048)
    nc = C // tn
    return _mm(name, NN, a, W, (S // tm, P * nc, K // tk),
               pl.BlockSpec((tm, tk), lambda i, j, k: (i, k)),
               pl.BlockSpec((None, None, tk, tn), lambda i, j, k: (j // nc, li, k, j % nc)),
               pl.BlockSpec((tm, tn), lambda i, j, k: (i, j)),
               jax.ShapeDtypeStruct((S, P * C), out_dtype), (tm, tn))


def mm_nn_row(name, a, W, li, out_dtype=f32, a_fn=None):
    P, _, R, N = W.shape
    S = (a[0] if a_fn is not None else a).shape[0]
    tm, tn, tk = _mm_tiles(S), _tile(N, 1024), _tile(R, 2048)
    nr = R // tk
    return _mm(name, NN, a, W, (S // tm, N // tn, P * nr),
               pl.BlockSpec((tm, tk), lambda i, j, k: (i, k)),
               pl.BlockSpec((None, None, tk, tn), lambda i, j, k: (k // nr, li, k % nr, j)),
               pl.BlockSpec((tm, tn), lambda i, j, k: (i, j)),
               jax.ShapeDtypeStruct((S, N), out_dtype), (tm, tn), a_fn=a_fn)


def mm_nt_col(name, dy, W, li, out_dtype=f32):
    P, _, K, C = W.shape
    S = dy.shape[0]
    tm, tn, tk = _mm_tiles(S), _tile(K, 1024), _tile(C, 2048)
    nc = C // tk
    return _mm(name, NT, dy, W, (S // tm, K // tn, P * nc),
               pl.BlockSpec((tm, tk), lambda i, j, k: (i, k)),
               pl.BlockSpec((None, None, tn, tk), lambda i, j, k: (k // nc, li, j, k % nc)),
               pl.BlockSpec((tm, tn), lambda i, j, k: (i, j)),
               jax.ShapeDtypeStruct((S, K), out_dtype), (tm, tn))


def mm_nt_row(name, dy, W, li, out_dtype=f32, tail=None, n_out=None):
    P, _, R, N = W.shape
    S = dy.shape[0]
    tm, tn, tk = _mm_tiles(S), _tile(R, 1536), _tile(N, 2048)
    nr = R // tn
    out = jax.ShapeDtypeStruct((S, P * R), out_dtype)
    return _mm(name, NT, dy, W, (S // tm, P * nr, N // tk),
               pl.BlockSpec((tm, tk), lambda i, j, k: (i, k)),
               pl.BlockSpec((None, None, tn, tk), lambda i, j, k: (j // nr, li, j % nr, k)),
               pl.BlockSpec((tm, tn), lambda i, j, k: (i, j)),
               out if n_out is None else [out] * n_out, (tm, tn), tail=tail)


def mm_tn_col(name, a, dy, P):
    S, K = a.shape
    C = dy.shape[1] // P
    tm, tn, tk = _tile(K, 512), _tile(C, 1536), _tile(S, 1024)
    nc = C // tn
    return _mm(name, TN, a, dy, (K // tm, P * nc, S // tk),
               pl.BlockSpec((tk, tm), lambda i, j, k: (k, i)),
               pl.BlockSpec((tk, tn), lambda i, j, k: (k, j)),
               pl.BlockSpec((None, tm, tn), lambda i, j, k: (j // nc, i, j % nc)),
               jax.ShapeDtypeStruct((P, K, C), bf16), (tm, tn))


def mm_tn_row(name, a, dy, P, a_fn=None):
    S, N = dy.shape
    R = (a[0] if a_fn is not None else a).shape[1] // P
    tm, tn, tk = _tile(R, 1536), _tile(N, 1024), _tile(S, 1024)
    nr = R // tm
    return _mm(name, TN, a, dy, (P * nr, N // tn, S // tk),
               pl.BlockSpec((tk, tm), lambda i, j, k: (k, i)),
               pl.BlockSpec((tk, tn), lambda i, j, k: (k, j)),
               pl.BlockSpec((None, tm, tn), lambda i, j, k: (i // nr, i % nr, j)),
               jax.ShapeDtypeStruct((P, R, N), bf16), (tm, tn), a_fn=a_fn)


def _hgrn_chunk(st, q, fp, v, gt, lb, an):
    C = q.shape[0]
    sig = _sigmoid(fp)
    f = lb + (1.0 - lb) * sig
    kk = (1.0 - lb) * (1.0 - sig)
    g = jnp.log(jnp.maximum(f, A_F_MIN))
    qs = _silu(q)
    row = lax.broadcasted_iota(jnp.int32, (C, C), 0)
    col = lax.broadcasted_iota(jnp.int32, (C, C), 1)
    tri = (col <= row).astype(f32)
    b = jnp.dot(tri, g, precision=HI, preferred_element_type=f32)
    o_inter = lax.dot_general((qs * jnp.exp(b)).astype(bf16), st.astype(bf16), NT,
                              preferred_element_type=f32)
    T, NB = A_SUB, C // A_SUB
    refs = [b[i * T + A_REF:i * T + A_REF + 1, :] for i in range(NB)]
    ref_q = jnp.concatenate([jnp.broadcast_to(r, (T, A_HEAD)) for r in refs], axis=0)
    ref_k = jnp.concatenate([jnp.broadcast_to(r, (C, A_HEAD)) for r in refs], axis=0)
    q_t = qs * jnp.exp(b - ref_q)
    k_t = jnp.concatenate([kk] * NB, axis=0) * jnp.exp(
        jnp.minimum(ref_k - jnp.concatenate([b] * NB, axis=0), A_EXP_CAP))
    s = lax.dot_general(q_t.astype(bf16), k_t.astype(bf16), NT, preferred_element_type=f32)
    trow = lax.broadcasted_iota(jnp.int32, (C, NB * C), 0)
    scol = lax.broadcasted_iota(jnp.int32, (C, NB * C), 1)
    keep = jnp.logical_and(scol // C == trow // T, scol % C <= trow)
    s = jnp.where(keep, s, 0.0)
    o_intra = jnp.dot(s.astype(bf16), jnp.concatenate([v.astype(bf16)] * NB, axis=0),
                      preferred_element_type=f32)
    bl = b[C - 1:C, :]
    kd = kk * jnp.exp(bl - b)
    st_new = st * jnp.exp(bl) + lax.dot_general(v.astype(bf16), kd.astype(bf16), TN,
                                                preferred_element_type=f32)
    o = o_inter + o_intra
    y = o * lax.rsqrt(jnp.mean(o * o, axis=-1, keepdims=True) + RMS_EPS) * an * _silu(gt)
    return st_new, y


def _hgrn_heads_per_step(HA):
    return A_HEADS_PER_STEP if HA % A_HEADS_PER_STEP == 0 else 1


def _hgrn_in_specs(HA, HP, cidx):
    W = HP * A_HEAD
    specs = [pl.BlockSpec((A_CHUNK, W), lambda h, c, s=s: (cidx(c), s * (HA // HP) + h)) for s in range(4)]
    specs += [pl.BlockSpec((1, W), lambda h, c: (0, h))] * 2
    return specs


def _head(ref, j):
    return ref[:, j * A_HEAD:(j + 1) * A_HEAD]


def hgrn_fwd(name, ymain, lb, an, D):
    S = ymain.shape[0]
    HA, nc = D // A_HEAD, S // A_CHUNK
    HP = _hgrn_heads_per_step(HA)
    W = HP * A_HEAD

    def body(q, fp, v, gt, lb_ref, an_ref, o_ref, sv_ref, st):
        @pl.when(pl.program_id(1) == 0)
        def _():
            st[...] = jnp.zeros_like(st)

        sv_ref[...] = st[...]
        for j in range(HP):
            st_new, y = _hgrn_chunk(st[j], _head(q, j), _head(fp, j), _head(v, j), _head(gt, j),
                                    _head(lb_ref, j), _head(an_ref, j))
            st[j] = st_new
            o_ref[:, j * A_HEAD:(j + 1) * A_HEAD] = y.astype(o_ref.dtype)

    return pl.pallas_call(
        body, name=name, grid=(HA // HP, nc),
        in_specs=_hgrn_in_specs(HA, HP, lambda c: c),
        out_specs=[pl.BlockSpec((A_CHUNK, W), lambda h, c: (c, h)),
                   pl.BlockSpec((HP, None, A_HEAD, A_HEAD), lambda h, c: (h, c, 0, 0))],
        out_shape=[jax.ShapeDtypeStruct((S, D), bf16),
                   jax.ShapeDtypeStruct((HA, nc, A_HEAD, A_HEAD), f32)],
        scratch_shapes=[pltpu.VMEM((HP, A_HEAD, A_HEAD), f32)],
        compiler_params=_cp("parallel", "arbitrary"))(ymain, ymain, ymain, ymain, lb, an)


def hgrn_bwd(name, ymain, lb, an, saved, dmixed, D):
    S = ymain.shape[0]
    HA, nc = D // A_HEAD, S // A_CHUNK
    HP = _hgrn_heads_per_step(HA)
    W = HP * A_HEAD
    rev = lambda c: nc - 1 - c

    def body(q, fp, v, gt, lb_ref, an_ref, sv_ref, do_ref, dq, df, dv, dg, dlb, dan, dst):
        @pl.when(pl.program_id(1) == 0)
        def _():
            dst[...] = jnp.zeros_like(dst)
            dlb[...] = jnp.zeros_like(dlb)
            dan[...] = jnp.zeros_like(dan)

        for j in range(HP):
            cols = slice(j * A_HEAD, (j + 1) * A_HEAD)
            _, vjp = jax.vjp(_hgrn_chunk, sv_ref[j], _head(q, j), _head(fp, j), _head(v, j), _head(gt, j),
                             _head(lb_ref, j), _head(an_ref, j))
            g = vjp((dst[j], _head(do_ref, j).astype(f32)))
            dst[j] = g[0]
            for r, x in zip((dq, df, dv, dg), g[1:5]):
                r[:, cols] = x.astype(r.dtype)
            dlb[:, cols] += g[5]
            dan[:, cols] += g[6]

    blk = pl.BlockSpec((A_CHUNK, W), lambda h, c: (rev(c), h))
    vec = pl.BlockSpec((1, W), lambda h, c: (0, h))
    return pl.pallas_call(
        body, name=name, grid=(HA // HP, nc),
        in_specs=_hgrn_in_specs(HA, HP, rev) + [
            pl.BlockSpec((HP, None, A_HEAD, A_HEAD), lambda h, c: (h, rev(c), 0, 0)), blk],
        out_specs=[blk] * 4 + [vec] * 2,
        out_shape=[jax.ShapeDtypeStruct((S, D), bf16)] * 4 + [jax.ShapeDtypeStruct((1, D), f32)] * 2,
        scratch_shapes=[pltpu.VMEM((HP, A_HEAD, A_HEAD), f32)],
        compiler_params=_cp("parallel", "arbitrary"))(ymain, ymain, ymain, ymain, lb, an, saved, dmixed)


def _ssd_chunk(hp, xs, bm, cm, z, dtr, dtb, alog, dsk, bn, g, R):
    L, GW = xs.shape
    HB = dtr.shape[1]
    R8 = max(R, 8)
    dt = jax.nn.softplus(dtr + dtb)
    a = -jnp.exp(alog)
    row = lax.broadcasted_iota(jnp.int32, (L, L), 0)
    col = lax.broadcasted_iota(jnp.int32, (L, L), 1)
    causal = col <= row
    cs = jnp.dot(causal.astype(f32), dt * a, precision=HI, preferred_element_type=f32)
    eh = lax.broadcasted_iota(jnp.int32, (HB, GW), 0)
    ec = lax.broadcasted_iota(jnp.int32, (HB, GW), 1)
    spread = (eh == g * R + ec // B_HEAD).astype(f32)
    sh = lax.broadcasted_iota(jnp.int32, (R8, HB), 1)
    sr = lax.broadcasted_iota(jnp.int32, (R8, HB), 0)
    pick_t = jnp.logical_and(sh == g * R + sr, sr < R).astype(f32)
    dtf = jnp.dot(dt, spread, precision=HI, preferred_element_type=f32)
    csf = jnp.dot(cs, spread, precision=HI, preferred_element_type=f32)
    dsf = jnp.dot(jnp.broadcast_to(dsk, (8, HB)), spread, precision=HI, preferred_element_type=f32)[0:1, :]
    cs_col = lax.dot_general(cs, pick_t, NT, precision=HI, preferred_element_type=f32)
    cs_row = lax.dot_general(pick_t, cs, NT, precision=HI, preferred_element_type=f32)
    xdt = xs * dtf
    cb = lax.dot_general(cm.astype(bf16), bm.astype(bf16), NT, preferred_element_type=f32)
    lane_head = lax.broadcasted_iota(jnp.int32, (1, GW), 1) // B_HEAD
    y = jnp.zeros((L, GW), f32)
    for r in range(R):
        seg = cs_col[:, r:r + 1] - cs_row[r:r + 1, :]
        dec = jnp.where(causal, jnp.exp(jnp.where(causal, seg, 0.0)), 0.0)
        xm = jnp.where(lane_head == r, xdt, 0.0)
        y = y + jnp.dot((cb * dec).astype(bf16), xm.astype(bf16), preferred_element_type=f32)
    csl = csf[L - 1:L, :]
    dte = jnp.exp(csl - csf)
    states = lax.dot_general(bm.astype(bf16), (xdt * dte).astype(bf16), TN, preferred_element_type=f32)
    y_off = jnp.dot(cm.astype(bf16), hp.astype(bf16), preferred_element_type=f32) * jnp.exp(csf)
    hn = hp * jnp.exp(csl) + states
    gated = (y + y_off + dsf * xs) * _silu(z)
    out = gated * lax.rsqrt(jnp.mean(gated * gated, axis=-1, keepdims=True) + RMS_EPS) * bn
    return hn, out


def _ssd_in_specs(D, HB, cidx):
    L, GW, N = B_CHUNK, D // B_GROUPS, B_STATE
    zoff, boff = 4 * D // GW, D // N
    return [
        pl.BlockSpec((L, GW), lambda c, g: (cidx(c), g)),
        pl.BlockSpec((L, N), lambda c, g: (cidx(c), boff + g)),
        pl.BlockSpec((L, N), lambda c, g: (cidx(c), boff + B_GROUPS + g)),
        pl.BlockSpec((L, GW), lambda c, g: (cidx(c), zoff + g)),
        pl.BlockSpec((L, HB), lambda c, g: (cidx(c), 0)),
        pl.BlockSpec((1, HB), lambda c, g: (0, 0)),
        pl.BlockSpec((1, HB), lambda c, g: (0, 0)),
        pl.BlockSpec((1, HB), lambda c, g: (0, 0)),
        pl.BlockSpec((1, GW), lambda c, g: (0, g)),
    ]


def ssd_fwd(name, xact, ymain, dtr, dtb, alog, dsk, bn, D):
    S, HB = dtr.shape
    nc, GW, R = S // B_CHUNK, D // B_GROUPS, HB // B_GROUPS

    def body(xs, bm, cm, z, dt_ref, dtb_ref, al_ref, ds_ref, bn_ref, o_ref, sv_ref, hs):
        g = pl.program_id(1)

        @pl.when(pl.program_id(0) == 0)
        def _():
            hs[g] = jnp.zeros((B_STATE, GW), f32)

        hp = hs[g]
        sv_ref[...] = hp
        hn, out = _ssd_chunk(hp, xs[...], bm[...], cm[...], z[...], dt_ref[...], dtb_ref[...], al_ref[...],
                             ds_ref[...], bn_ref[...], g, R)
        hs[g] = hn
        o_ref[...] = out.astype(o_ref.dtype)

    return pl.pallas_call(
        body, name=name, grid=(nc, B_GROUPS),
        in_specs=_ssd_in_specs(D, HB, lambda c: c),
        out_specs=[pl.BlockSpec((B_CHUNK, GW), lambda c, g: (c, g)),
                   pl.BlockSpec((None, None, B_STATE, GW), lambda c, g: (c, g, 0, 0))],
        out_shape=[jax.ShapeDtypeStruct((S, D), bf16),
                   jax.ShapeDtypeStruct((nc, B_GROUPS, B_STATE, GW), f32)],
        scratch_shapes=[pltpu.VMEM((B_GROUPS, B_STATE, GW), f32)],
        compiler_params=_cp("arbitrary", "arbitrary"))(xact, xact, xact, ymain, dtr, dtb, alog, dsk, bn)


def ssd_bwd(name, xact, ymain, dtr, dtb, alog, dsk, bn, saved, dmixed, D):
    S, HB = dtr.shape
    nc, GW, R = S // B_CHUNK, D // B_GROUPS, HB // B_GROUPS
    rev = lambda c: nc - 1 - c
    ooff = D // GW

    def body(xs, bm, cm, z, dt_ref, dtb_ref, al_ref, ds_ref, bn_ref, sv_ref, do_ref,
             dxs, dbm, dcm, dz, ddt, ddtb, dal, dds, dbn, dhs):
        c, g = pl.program_id(0), pl.program_id(1)

        @pl.when(c == 0)
        def _():
            dhs[g] = jnp.zeros((B_STATE, GW), f32)
            dbn[g] = jnp.zeros((1, GW), f32)

        @pl.when(jnp.logical_and(c == 0, g == 0))
        def _():
            ddtb[...] = jnp.zeros_like(ddtb)
            dal[...] = jnp.zeros_like(dal)
            dds[...] = jnp.zeros_like(dds)

        @pl.when(g == 0)
        def _():
            ddt[...] = jnp.zeros_like(ddt)

        fn = functools.partial(_ssd_chunk, g=g, R=R)
        _, vjp = jax.vjp(fn, sv_ref[...], xs[...], bm[...], cm[...], z[...], dt_ref[...], dtb_ref[...],
                         al_ref[...], ds_ref[...], bn_ref[...])
        gr = vjp((dhs[g], do_ref[...].astype(f32)))
        dhs[g] = gr[0]
        dxs[...] = gr[1]
        dbm[...] = gr[2]
        dcm[...] = gr[3]
        dz[...] = gr[4].astype(dz.dtype)
        ddt[...] += gr[5]
        ddtb[...] += gr[6]
        dal[...] += gr[7]
        dds[...] += gr[8]
        dbn[g] += gr[9]

    hb_vec = pl.BlockSpec((1, HB), lambda c, g: (0, 0))
    return pl.pallas_call(
        body, name=name, grid=(nc, B_GROUPS),
        in_specs=_ssd_in_specs(D, HB, rev) + [
            pl.BlockSpec((None, None, B_STATE, GW), lambda c, g: (rev(c), g, 0, 0)),
            pl.BlockSpec((B_CHUNK, GW), lambda c, g: (rev(c), ooff + g))],
        out_specs=[pl.BlockSpec((B_CHUNK, GW), lambda c, g: (rev(c), g)),
                   pl.BlockSpec((B_CHUNK, B_STATE), lambda c, g: (rev(c), g)),
                   pl.BlockSpec((B_CHUNK, B_STATE), lambda c, g: (rev(c), g)),
                   pl.BlockSpec((B_CHUNK, GW), lambda c, g: (rev(c), g)),
                   pl.BlockSpec((B_CHUNK, HB), lambda c, g: (rev(c), 0)),
                   hb_vec, hb_vec, hb_vec,
                   pl.BlockSpec((B_GROUPS, 1, GW), lambda c, g: (0, 0, 0))],
        out_shape=[jax.ShapeDtypeStruct((S, D), f32),
                   jax.ShapeDtypeStruct((S, B_GROUPS * B_STATE), f32),
                   jax.ShapeDtypeStruct((S, B_GROUPS * B_STATE), f32),
                   jax.ShapeDtypeStruct((S, D), bf16),
                   jax.ShapeDtypeStruct((S, HB), f32),
                   jax.ShapeDtypeStruct((1, HB), f32), jax.ShapeDtypeStruct((1, HB), f32),
                   jax.ShapeDtypeStruct((1, HB), f32),
                   jax.ShapeDtypeStruct((B_GROUPS, 1, GW), f32)],
        scratch_shapes=[pltpu.VMEM((B_GROUPS, B_STATE, GW), f32)],
        compiler_params=_cp("arbitrary", "arbitrary"))(
            xact, xact, xact, ymain, dtr, dtb, alog, dsk, bn, saved, dmixed)


def _conv_taps(xp, w_ref, b_ref, r0, K):
    acc = jnp.broadcast_to(b_ref[...], (CONV_ROWS, b_ref.shape[1]))
    for k in range(K):
        acc = acc + w_ref[k:k + 1, :] * xp[r0 + CONV_PAD - (K - 1) + k:r0 + CONV_PAD - (K - 1) + k + CONV_ROWS, :]
    return acc


def conv_fwd(name, x, xoff, w, b, act, out_dtype):
    S = x.shape[0]
    K, CW = w.shape
    tc = CONV_CH

    def body(x_ref, w_ref, b_ref, o_ref, xp):
        xp[0:CONV_PAD, :] = jnp.zeros((CONV_PAD, tc), f32)
        xp[CONV_PAD:CONV_PAD + S, :] = x_ref[...].astype(f32)
        for r0 in range(0, S, CONV_ROWS):
            acc = _conv_taps(xp, w_ref, b_ref, r0, K)
            if act:
                acc = _silu(acc)
            o_ref[r0:r0 + CONV_ROWS, :] = acc.astype(o_ref.dtype)

    return pl.pallas_call(
        body, name=name, grid=(CW // tc,),
        in_specs=[pl.BlockSpec((S, tc), lambda j: (0, j + xoff)),
                  pl.BlockSpec((K, tc), lambda j: (0, j)),
                  pl.BlockSpec((1, tc), lambda j: (0, j))],
        out_specs=pl.BlockSpec((S, tc), lambda j: (0, j)),
        out_shape=jax.ShapeDtypeStruct((S, CW), out_dtype),
        scratch_shapes=[pltpu.VMEM((S + CONV_PAD, tc), f32)],
        compiler_params=_cp("parallel"))(x, w, b)


def conv_bwd(name, x, xoff, w, b, dout, act, dx_dtype):
    S = x.shape[0]
    K, CW = w.shape
    tc = CONV_CH

    def body(x_ref, w_ref, b_ref, d_ref, dx_ref, dw_ref, db_ref, xp, dp):
        xp[0:CONV_PAD, :] = jnp.zeros((CONV_PAD, tc), f32)
        xp[CONV_PAD:CONV_PAD + S, :] = x_ref[...].astype(f32)
        dp[S:S + CONV_PAD, :] = jnp.zeros((CONV_PAD, tc), f32)
        db = jnp.zeros((1, tc), f32)
        for r0 in range(0, S, CONV_ROWS):
            d = d_ref[r0:r0 + CONV_ROWS, :].astype(f32)
            if act:
                pre = _conv_taps(xp, w_ref, b_ref, r0, K)
                s = _sigmoid(pre)
                d = d * (s + pre * s * (1.0 - s))
            dp[r0:r0 + CONV_ROWS, :] = d
            db = db + jnp.sum(d, axis=0, keepdims=True)
        db_ref[...] = db
        for r0 in range(0, S, CONV_ROWS):
            acc = jnp.zeros((CONV_ROWS, tc), f32)
            for k in range(K):
                acc = acc + w_ref[k:k + 1, :] * dp[r0 + (K - 1 - k):r0 + (K - 1 - k) + CONV_ROWS, :]
            dx_ref[r0:r0 + CONV_ROWS, :] = acc.astype(dx_ref.dtype)
        for k in range(K):
            acc = jnp.zeros((1, tc), f32)
            for r0 in range(0, S, CONV_ROWS):
                lo = r0 + CONV_PAD - (K - 1) + k
                acc = acc + jnp.sum(dp[r0:r0 + CONV_ROWS, :] * xp[lo:lo + CONV_ROWS, :], axis=0, keepdims=True)
            dw_ref[k:k + 1, :] = acc

    return pl.pallas_call(
        body, name=name, grid=(CW // tc,),
        in_specs=[pl.BlockSpec((S, tc), lambda j: (0, j + xoff)),
                  pl.BlockSpec((K, tc), lambda j: (0, j)),
                  pl.BlockSpec((1, tc), lambda j: (0, j)),
                  pl.BlockSpec((S, tc), lambda j: (0, j))],
        out_specs=[pl.BlockSpec((S, tc), lambda j: (0, j)),
                   pl.BlockSpec((K, tc), lambda j: (0, j)),
                   pl.BlockSpec((1, tc), lambda j: (0, j))],
        out_shape=[jax.ShapeDtypeStruct((S, CW), dx_dtype),
                   jax.ShapeDtypeStruct((K, CW), f32),
                   jax.ShapeDtypeStruct((1, CW), f32)],
        scratch_shapes=[pltpu.VMEM((S + CONV_PAD, tc), f32), pltpu.VMEM((S + CONV_PAD, tc), f32)],
        compiler_params=_cp("parallel"))(x, w, b, dout)


def loss_head(name, y, target, tm):
    S, D = y.shape

    def body(y_ref, t_ref, dy_ref, l_ref):
        @pl.when(pl.program_id(0) == 0)
        def _():
            l_ref[...] = jnp.zeros_like(l_ref)

        err = y_ref[...] - t_ref[...]
        dy_ref[...] = err * (1.0 / D)
        l_ref[...] += jnp.sum(err * err) * (0.5 / D)

    dy, l = pl.pallas_call(
        body, name=name, grid=(S // tm,),
        in_specs=[pl.BlockSpec((tm, D), lambda i: (i, 0))] * 2,
        out_specs=[pl.BlockSpec((tm, D), lambda i: (i, 0)), pl.BlockSpec((8, LANES), lambda i: (0, 0))],
        out_shape=[jax.ShapeDtypeStruct((S, D), f32), jax.ShapeDtypeStruct((8, LANES), f32)],
        compiler_params=_cp("arbitrary"))(y, target)
    return dy, l[0, 0]


def _flat2d_tiles(rows, cols, itemsize, target_bytes):
    tc = _tile(cols, 1024) if cols % LANES == 0 else cols
    cap = max(8, target_bytes // (tc * itemsize))
    tr = _tile(rows, cap, 16) if rows % 16 == 0 else rows
    return tr, tc


def adamw(name, w, g, m, v):
    shape = w.shape
    cols = shape[-1]
    rows = math.prod(shape[:-1])
    tr, tc = _flat2d_tiles(rows, cols, 4, 1 << 20)
    c1 = 1.0 - ADAM_B1 ** ADAM_STEP
    c2 = 1.0 - ADAM_B2 ** ADAM_STEP

    def body(w_ref, g_ref, m_ref, v_ref, d_ref, nm_ref, nv_ref):
        gg = g_ref[...]
        nm = ADAM_B1 * m_ref[...] + (1.0 - ADAM_B1) * gg
        nv = ADAM_B2 * v_ref[...] + (1.0 - ADAM_B2) * (gg * gg)
        d_ref[...] = -ADAM_LR * ((nm / c1) / (jnp.sqrt(nv / c2) + ADAM_EPS) + ADAM_WD * w_ref[...])
        nm_ref[...] = nm
        nv_ref[...] = nv

    spec = pl.BlockSpec((tr, tc), lambda i, j: (i, j))
    outs = pl.pallas_call(
        body, name=name, grid=(rows // tr, cols // tc), in_specs=[spec] * 4, out_specs=[spec] * 3,
        out_shape=[jax.ShapeDtypeStruct((rows, cols), f32)] * 3,
        compiler_params=_cp("parallel", "parallel"))(*[a.reshape(rows, cols) for a in (w, g, m, v)])
    return [o.reshape(shape) for o in outs]


def _core_index():
    return lax.axis_index("c").astype(jnp.int32).reshape(1)


def _half_rows_tile(Rh, C):
    return _tile(Rh, max(16, (2 << 20) // (C * 2)), 16)


def rs_add(name, G, buf):
    P, _, Rh, C = G.shape
    tr = _half_rows_tile(Rh, C)

    def body(c_ref, g_ref, b_ref, o_ref):
        o_ref[...] = (g_ref[...].astype(f32) + b_ref[...].astype(f32)).astype(o_ref.dtype)

    return pl.pallas_call(
        body, name=name,
        grid_spec=pltpu.PrefetchScalarGridSpec(
            num_scalar_prefetch=1, grid=(P, Rh // tr),
            in_specs=[pl.BlockSpec((None, None, tr, C), lambda p, i, c: (p, c[0], i, 0)),
                      pl.BlockSpec((None, tr, C), lambda p, i, c: (p, i, 0))],
            out_specs=pl.BlockSpec((None, tr, C), lambda p, i, c: (p, i, 0))),
        out_shape=jax.ShapeDtypeStruct((P, Rh, C), bf16),
        compiler_params=_cp("parallel", "parallel"))(_core_index(), G, buf)


def _chip_indices():
    x, y, c = lax.axis_index("x"), lax.axis_index("y"), lax.axis_index("c")
    ids = [2 * x + y] + [2 * _flip(x, fx) + _flip(y, fy) for fx, fy in _CHIP_FLIPS] + [c]
    return [i.astype(jnp.int32).reshape(1) for i in ids]


def rs_sum4(name, pair, buf, final, layer):
    P, Rh, C = buf.shape
    tr = _half_rows_tile(Rh, C)

    def body(i0, i1, i2, i3, ic, b0, b1, b2, b3, f_ref, o_ref):
        o_ref[...] = ((b0[...].astype(f32) + b1[...].astype(f32)) + b2[...].astype(f32)) + b3[...].astype(f32)

    blk = (None, tr, C)
    return pl.pallas_call(
        body, name=name,
        grid_spec=pltpu.PrefetchScalarGridSpec(
            num_scalar_prefetch=5, grid=(Rh // tr,),
            in_specs=[pl.BlockSpec(blk, lambda i, *ids, k=k: (ids[k][0], i, 0)) for k in range(P)] + [_ANY],
            out_specs=pl.BlockSpec((None, None, tr, C), lambda i, *ids: (layer, ids[4][0], i, 0))),
        out_shape=jax.ShapeDtypeStruct(final.shape, final.dtype),
        input_output_aliases={9: 0},
        compiler_params=_cp("parallel"))(*_chip_indices(), pair, buf, buf, buf, final)


def place_own(name, w, layer):
    _, R, C = w.shape
    tr = _tile(R, max(16, (2 << 20) // (C * 2)), 16)

    def body(q, w_ref, own_ref, land_ref):
        wb = w_ref[...].astype(bf16)
        own_ref[...] = wb
        land_ref[...] = wb

    return pl.pallas_call(
        body, name=name,
        grid_spec=pltpu.PrefetchScalarGridSpec(
            num_scalar_prefetch=1, grid=(R // tr,),
            in_specs=[pl.BlockSpec((None, tr, C), lambda i, q: (layer, i, 0))],
            out_specs=[pl.BlockSpec((tr, C), lambda i, q: (i, 0)),
                       pl.BlockSpec((None, tr, C), lambda i, q: (q[0], i, 0))]),
        out_shape=[jax.ShapeDtypeStruct((R, C), bf16), jax.ShapeDtypeStruct((N_CHIPS, R, C), bf16)],
        compiler_params=_cp("parallel"))(_chip_indices()[0], w)


_ANY = pl.BlockSpec(memory_space=pl.ANY)
_CHIP_FLIPS = ((1, 0), (0, 1), (1, 1))


def _place():
    return lax.axis_index("x"), lax.axis_index("y"), lax.axis_index("c")


def _flip(v, f):
    return 1 - v if f else v


def _remote(src, dst, ssem, rsem, dev):
    return pltpu.make_async_remote_copy(src_ref=src, dst_ref=dst, send_sem=ssem, recv_sem=rsem,
                                        device_id=dev, device_id_type=MESH)


_HBM = pl.BlockSpec(memory_space=pltpu.HBM)
_SEM = pl.BlockSpec(memory_space=pltpu.SEMAPHORE)
_DATAFLOW = pltpu.SideEffectType.DATAFLOW_SIDE_EFFECTING
_TOKEN = jax.ShapeDtypeStruct((8, LANES), f32)


def _in_hbm(a):
    return pltpu.with_memory_space_constraint(a, pltpu.HBM)


def _hbm_like(a):
    return pltpu.HBM(a.shape, a.dtype)


def gather_ici_start(name, groups):
    sizes = [len(g) for g in groups]
    owns = [o for g in groups for o, _ in g]
    lands = [l for g in groups for _, l in g]
    n, ng = len(owns), len(groups)

    def body(*refs):
        own, land = refs[:n], refs[n:2 * n]
        sems = refs[2 * n:2 * n + 2 * ng]
        token = refs[-1]
        x, y, c = _place()
        q = 2 * x + y
        t = 0
        for gi, size in enumerate(sizes):
            for j in range(size):
                for k, (fx, fy) in enumerate(_CHIP_FLIPS):
                    _remote(own[t].at[c], land[t].at[q, c], sems[2 * gi].at[3 * j + k], sems[2 * gi + 1].at[3 * j + k],
                            (_flip(x, fx), _flip(y, fy), c)).start()
                t += 1
        token[...] = jnp.zeros_like(token)

    sem_shapes = [pltpu.SemaphoreType.DMA((3 * size,)) for size in sizes for _ in range(2)]
    res = pl.pallas_call(
        body, name=name,
        in_specs=[_HBM] * (2 * n),
        out_specs=[_SEM] * (2 * ng) + [_HBM] * (2 * n) + [pl.BlockSpec(memory_space=pltpu.VMEM)],
        out_shape=sem_shapes + [_hbm_like(a) for a in owns + lands] + [_TOKEN],
        input_output_aliases={i: 2 * ng + i for i in range(2 * n)},
        compiler_params=pltpu.CompilerParams(has_side_effects=_DATAFLOW),
    )(*[_in_hbm(a) for a in owns + lands])
    own_thru, land_thru = res[2 * ng:2 * ng + n], res[2 * ng + n:2 * ng + 2 * n]
    handles, t = [], 0
    for gi, size in enumerate(sizes):
        handles.append((res[2 * gi], res[2 * gi + 1], list(own_thru[t:t + size]), list(land_thru[t:t + size])))
        t += size
    return handles


def gather_ici_wait(name, handle, after):
    send, recv, owns, lands = handle
    n = len(owns)

    def body(*refs):
        own, land = refs[:n], refs[n:2 * n]
        send_ref, recv_ref = refs[2 * n], refs[2 * n + 1]
        x, y, c = _place()
        for j in range(n):
            for k, (fx, fy) in enumerate(_CHIP_FLIPS):
                px, py = _flip(x, fx), _flip(y, fy)
                cp = _remote(own[j].at[c], land[j].at[2 * px + py, c], send_ref.at[3 * j + k], recv_ref.at[3 * j + k],
                             (px, py, c))
                cp.wait_send()
                cp.wait_recv()

    res = pl.pallas_call(
        body, name=name,
        in_specs=[_HBM] * (2 * n) + [_SEM, _SEM, _ANY],
        out_specs=[_HBM] * (2 * n),
        out_shape=[_hbm_like(a) for a in owns + lands],
        input_output_aliases={i: i for i in range(2 * n)},
        compiler_params=pltpu.CompilerParams(has_side_effects=_DATAFLOW),
    )(*owns, *lands, send, recv, after)
    return list(res[n:])


def gather_forward(name, lands):
    T = len(lands)

    def body(*refs):
        o = refs[T:2 * T]
        send, recv = refs[2 * T:]
        x, y, c = _place()
        sib = (x, y, 1 - c)
        cps = []
        for t in range(T):
            for k, (fx, fy) in enumerate(_CHIP_FLIPS):
                slab = o[t].at[2 * _flip(x, fx) + _flip(y, fy), c]
                cp = _remote(slab, slab, send.at[t, k], recv.at[t, k], sib)
                cp.start()
                cps.append(cp)
        for t in range(T):
            for k, (fx, fy) in enumerate(_CHIP_FLIPS):
                slab = o[t].at[2 * _flip(x, fx) + _flip(y, fy), 1 - c]
                _remote(slab, slab, send.at[t, k], recv.at[t, k], sib).wait_recv()
        for cp in cps:
            cp.wait_send()

    return pl.pallas_call(
        body, name=name, in_specs=[_ANY] * T, out_specs=[_ANY] * T,
        out_shape=[jax.ShapeDtypeStruct(a.shape, a.dtype) for a in lands],
        input_output_aliases={t: t for t in range(T)},
        scratch_shapes=[pltpu.SemaphoreType.DMA((T, 3)), pltpu.SemaphoreType.DMA((T, 3))],
        )(*lands)


def rs_sibling(name, Gs):
    T = len(Gs)

    def body(*refs):
        g, o = refs[:T], refs[T:2 * T]
        send, recv = refs[2 * T:]
        x, y, c = _place()
        cps = []
        for t in range(T):
            cp = _remote(g[t].at[:, 1 - c], o[t], send.at[t], recv.at[t], (x, y, 1 - c))
            cp.start()
            cps.append(cp)
        for cp in cps:
            cp.wait()

    return pl.pallas_call(
        body, name=name, in_specs=[_ANY] * T, out_specs=[_ANY] * T,
        out_shape=[jax.ShapeDtypeStruct((G.shape[0],) + G.shape[2:], G.dtype) for G in Gs],
        scratch_shapes=[pltpu.SemaphoreType.DMA((T,)), pltpu.SemaphoreType.DMA((T,))],
        )(*Gs)


def reduce_ici_start(name, Ss):
    T = len(Ss)
    lands = [lax.empty(S.shape, S.dtype) for S in Ss]

    def body(*refs):
        s, land = refs[:T], refs[T:2 * T]
        send, recv = refs[2 * T], refs[2 * T + 1]
        token = refs[-1]
        x, y, c = _place()
        q = 2 * x + y
        for t in range(T):
            for k, (fx, fy) in enumerate(_CHIP_FLIPS):
                px, py = _flip(x, fx), _flip(y, fy)
                _remote(s[t].at[2 * px + py], land[t].at[q], send.at[3 * t + k], recv.at[3 * t + k],
                        (px, py, c)).start()
        token[...] = jnp.zeros_like(token)

    res = pl.pallas_call(
        body, name=name,
        in_specs=[_HBM] * (2 * T),
        out_specs=[_SEM, _SEM] + [_HBM] * (2 * T) + [pl.BlockSpec(memory_space=pltpu.VMEM)],
        out_shape=[pltpu.SemaphoreType.DMA((3 * T,)), pltpu.SemaphoreType.DMA((3 * T,))]
        + [_hbm_like(a) for a in Ss + lands] + [_TOKEN],
        input_output_aliases={i: 2 + i for i in range(2 * T)},
        compiler_params=pltpu.CompilerParams(has_side_effects=_DATAFLOW),
    )(*[_in_hbm(a) for a in Ss + lands])
    return res[0], res[1], list(res[2:2 + T]), list(res[2 + T:2 + 2 * T]), res[-1]


def reduce_ici_wait(name, handle, after):
    send, recv, Ss, lands, _ = handle
    T = len(Ss)

    def body(*refs):
        s, land = refs[:T], refs[T:2 * T]
        send_ref, recv_ref = refs[2 * T], refs[2 * T + 1]
        x, y, c = _place()
        for t in range(T):
            for k, (fx, fy) in enumerate(_CHIP_FLIPS):
                px, py = _flip(x, fx), _flip(y, fy)
                cp = _remote(s[t].at[2 * px + py], land[t].at[2 * px + py], send_ref.at[3 * t + k], recv_ref.at[3 * t + k],
                             (px, py, c))
                cp.wait_send()
                cp.wait_recv()

    res = pl.pallas_call(
        body, name=name,
        in_specs=[_HBM] * (2 * T) + [_SEM, _SEM, _ANY],
        out_specs=[_HBM] * (2 * T),
        out_shape=[_hbm_like(a) for a in Ss + lands],
        input_output_aliases={i: i for i in range(2 * T)},
        compiler_params=pltpu.CompilerParams(has_side_effects=_DATAFLOW),
    )(*Ss, *lands, send, recv, after)
    return list(res[:T]), list(res[T:])


def rs_share(name, tots, layers):
    T = len(tots)

    def body(*refs):
        s, o = refs[:T], refs[T:2 * T]
        send, recv = refs[2 * T:]
        x, y, c = _place()
        cps = []
        for t in range(T):
            mine = o[t].at[layers[t], c]
            cp = _remote(mine, mine, send.at[t], recv.at[t], (x, y, 1 - c))
            cp.start()
            cps.append(cp)
        for t in range(T):
            other = o[t].at[layers[t], 1 - c]
            _remote(other, other, send.at[t], recv.at[t], (x, y, 1 - c)).wait_recv()
        for cp in cps:
            cp.wait_send()

    return pl.pallas_call(
        body, name=name, in_specs=[_ANY] * T, out_specs=[_ANY] * T,
        out_shape=[jax.ShapeDtypeStruct(s.shape, s.dtype) for s in tots],
        input_output_aliases={t: t for t in range(T)},
        scratch_shapes=[pltpu.SemaphoreType.DMA((T,)), pltpu.SemaphoreType.DMA((T,))],
        )(*tots)


def all_reduce_small(name, vec):
    rows = vec.shape[0]
    flips = [(fx, fy, fc) for fx in (0, 1) for fy in (0, 1) for fc in (0, 1)][1:]

    def body(v_ref, o_ref, buf, send, recv):
        x, y, c = _place()
        me = 4 * x + 2 * y + c
        buf[me] = v_ref[...]
        cps = []
        for k, (fx, fy, fc) in enumerate(flips):
            cp = _remote(buf.at[me], buf.at[me], send.at[k], recv.at[k],
                         (_flip(x, fx), _flip(y, fy), _flip(c, fc)))
            cp.start()
            cps.append(cp)
        for k, (fx, fy, fc) in enumerate(flips):
            slab = buf.at[4 * _flip(x, fx) + 2 * _flip(y, fy) + _flip(c, fc)]
            _remote(slab, slab, send.at[k], recv.at[k], (x, y, c)).wait_recv()
        for cp in cps:
            cp.wait_send()
        acc = buf[0]
        for d in range(1, N_DEV):
            acc = acc + buf[d]
        o_ref[...] = acc

    return pl.pallas_call(
        body, name=name,
        in_specs=[pl.BlockSpec(memory_space=pltpu.VMEM)], out_specs=pl.BlockSpec(memory_space=pltpu.VMEM),
        out_shape=jax.ShapeDtypeStruct((rows, LANES), f32),
        scratch_shapes=[pltpu.VMEM((N_DEV, rows, LANES), f32),
                        pltpu.SemaphoreType.DMA((N_DEV - 1,)), pltpu.SemaphoreType.DMA((N_DEV - 1,))],
        compiler_params=pltpu.CompilerParams(vmem_limit_bytes=V7X_VMEM_LIMIT))(vec)


def _pack(arrays):
    flat = jnp.concatenate([a.reshape(-1) for a in arrays])
    n = flat.shape[0]
    rows = -(-n // (8 * LANES)) * 8
    return jnp.pad(flat, (0, rows * LANES - n)).reshape(rows, LANES)


def _unpack(vec, shapes):
    flat = vec.reshape(-1)
    out, pos = [], 0
    for s in shapes:
        n = math.prod(s)
        out.append(flat[pos:pos + n].reshape(s))
        pos += n
    return out


def _f_first(x, g):
    return x, _rms(x, g)


def _f_mid(h, m, gp, gn):
    h1 = h + _rms(m, gp)
    return h1, _rms(h1, gn)


def _f_mid_bias(h, m, b, gp, gn):
    h1 = h + _rms(m + b, gp)
    return h1, _rms(h1, gn)


def _f_last(h, m, gp):
    return (h + _rms(m, gp),)


def _f_swiglu(gate, up):
    return (_silu(gate) * up,)


def _swiglu_tile(gate, up):
    return _silu(gate.astype(f32)) * up.astype(f32)


def _swiglu_bwd_tile(d_act, gate, up):
    _, vjp = jax.vjp(_f_swiglu, gate.astype(f32), up.astype(f32))
    return vjp((d_act,))


def _f_glu(a, g, ba, bg):
    return ((a + ba) * _sigmoid(g + bg),)


def _f_ln_silu(x, g, b):
    mu = jnp.mean(x, axis=-1, keepdims=True)
    xc = x - mu
    y = xc * lax.rsqrt(jnp.mean(xc * xc, axis=-1, keepdims=True) + LN_EPS) * g + b
    return (_silu(y),)


def _f_lower_bounds(logits):
    n = logits.shape[0]
    e = jnp.exp(logits - jnp.max(logits, axis=0, keepdims=True))
    p = e / jnp.sum(e, axis=0, keepdims=True)
    layer = lax.broadcasted_iota(jnp.int32, logits.shape, 0)
    out = -jnp.broadcast_to(p[0:1, :], logits.shape)
    for j in range(n):
        out = out + jnp.where(layer >= j, p[j:j + 1, :], 0.0)
    return (out,)


WEIGHT_NAMES = ['mix_pre_g', 'mix_post_g', 'ffn_pre_g', 'ffn_post_g', 'hgrn_lb_logits', 'even_w_in',
                'hgrn_norm_g', 'ssd_conv_w', 'ssd_conv_b', 'ssd_dt_bias', 'ssd_a_log', 'ssd_d', 'ssd_norm_g',
                'even_w_out', 'conf_w1', 'conf_b1', 'conf_dw_w', 'conf_dw_b', 'conf_ln_g', 'conf_ln_b',
                'conf_w2', 'conf_b2', 'ffn_w_gate', 'ffn_w_up', 'ffn_w_down']
BIG = ['even_w_in', 'even_w_out', 'conf_w1', 'conf_w2', 'ffn_w_gate', 'ffn_w_up', 'ffn_w_down']
SMALL_SHARDED = {'ssd_conv_w': 2, 'conf_b1': 1, 'conf_dw_w': 2, 'conf_dw_b': 1, 'conf_ln_g': 1,
                 'conf_ln_b': 1, 'conf_b2': 1}


def _train_step(x, target, w, m, v):
    S, D = x.shape[1], x.shape[2]
    x2, t2 = x[0], target[0]
    NL = w['mix_pre_g'].shape[0]
    HB = w['ssd_dt_bias'].shape[1]
    GN = B_GROUPS * B_STATE
    xw, yw, cw = _place()
    chip = 2 * xw + yw
    tm = _tile(S, 128, 8)
    row1 = lambda a, i: a[i:i + 1]

    sharded = list(SMALL_SHARDED)
    placed = []
    for n in sharded:
        ax, a = SMALL_SHARDED[n], w[n]
        full = jnp.zeros(a.shape[:ax] + (a.shape[ax] * N_CHIPS,) + a.shape[ax + 1:], f32)
        start = [0] * a.ndim
        start[ax] = chip * a.shape[ax]
        placed.append(lax.dynamic_update_slice(full, jnp.where(cw == 0, a, 0.0), start))
    whole = dict(zip(sharded, _unpack(all_reduce_small("gather_small", _pack(placed)), [p.shape for p in placed])))
    small = {n: whole.get(n, w[n]) for n in WEIGHT_NAMES if n not in BIG}

    own, land = {}, {}
    for n in BIG:
        for l in range(w[n].shape[0]):
            own[n, l], land[n, l] = place_own("place_own", w[n], l)

    def mixer_keys(layer):
        names = ('even_w_in', 'even_w_out') if layer % 2 == 0 else ('conf_w1', 'conf_w2')
        return [(n, layer // 2) for n in names]

    def ffn_keys(layer):
        return [(n, layer) for n in ('ffn_w_gate', 'ffn_w_up', 'ffn_w_down')]

    groups = [keys(layer) for layer in range(NL) for keys in (mixer_keys, ffn_keys)]
    halves = lambda a: a.reshape(a.shape[:-2] + (2, a.shape[-2] // 2, a.shape[-1]))
    handles = gather_ici_start("gather_start", [[(halves(own[k]), halves(land[k])) for k in g] for g in groups])
    W = {}

    def fetch(gi, after):
        arrived = gather_ici_wait(f"gather_wait_{gi}", handles[gi], after)
        for k, a in zip(groups[gi], gather_forward("gather_forward", arrived)):
            W[k] = a.reshape((N_CHIPS, 1) + own[k].shape)

    WM = 6 * D + 2 * GN
    w_main, w_dt = {}, {}

    n_even = small['hgrn_lb_logits'].shape[0]
    (lbs,) = _stage_fwd("lower_bounds", _f_lower_bounds, [small['hgrn_lb_logits']], [], [(D, f32)], tm=n_even)
    saved = []
    h = x2
    (u,) = _stage_fwd("pre_norm", lambda a, g: (_rms(a, g),), [h], [row1(small['mix_pre_g'], 0)],
                      [(D, bf16)], tm=tm)
    for layer in range(NL):
        li = layer // 2
        r = {'h': h, 'u': u}
        fetch(2 * layer, u)
        if layer % 2 == 0:
            win = jnp.concatenate([W['even_w_in', li][j, 0] for j in range(N_CHIPS)], axis=-1)
            w_main[li], w_dt[li] = win[None, None, :, :WM], win[None, None, :, WM:]
            r['ymain'] = mm_nn_col("in_proj", u, w_main[li], 0)
            r['dtr'] = mm_nn_col("in_proj_dt", u, w_dt[li], 0)
            r['xact'] = conv_fwd("ssd_conv", r['ymain'], 5 * D // CONV_CH, small['ssd_conv_w'][li],
                                 row1(small['ssd_conv_b'], li), True, f32)
            o_a, r['hg_st'] = hgrn_fwd("hgrn", r['ymain'], row1(lbs, li), row1(small['hgrn_norm_g'], li), D)
            o_b, r['ssd_st'] = ssd_fwd("ssd", r['xact'], r['ymain'], r['dtr'], row1(small['ssd_dt_bias'], li),
                                       row1(small['ssd_a_log'], li), row1(small['ssd_d'], li),
                                       row1(small['ssd_norm_g'], li), D)
            r['mixed'] = jnp.concatenate([o_a, o_b], axis=1)
            r['m'] = mm_nn_row("out_proj", r['mixed'], W['even_w_out', li], 0)
            mid_fn, mid_par = _f_mid, []
        else:
            r['c1'] = mm_nn_col("conf_in", u, W['conf_w1', li], 0)
            b1 = row1(small['conf_b1'], li)
            tn = _tile(D, 512)
            (r['glu'],) = _stage_fwd("conf_glu", _f_glu, [r['c1'], (r['c1'], D // tn)], [b1, (b1, D // tn)],
                                     [(D, f32)], tm=tm, tn=tn)
            r['cc'] = conv_fwd("conf_conv", r['glu'], 0, small['conf_dw_w'][li], row1(small['conf_dw_b'], li),
                               False, f32)
            (r['c2'],) = _stage_fwd("conf_ln", _f_ln_silu, [r['cc']],
                                    [row1(small['conf_ln_g'], li), row1(small['conf_ln_b'], li)],
                                    [(D, bf16)], tm=tm)
            r['m'] = mm_nn_row("conf_out", r['c2'], W['conf_w2', li], 0)
            mid_fn, mid_par = _f_mid_bias, [row1(small['conf_b2'], li)]
        r['mid_fn'] = mid_fn
        r['mid_par'] = mid_par + [row1(small['mix_post_g'], layer), row1(small['ffn_pre_g'], layer)]
        r['h1'], r['u2'] = _stage_fwd("mid_norm", mid_fn, [h, r['m']], r['mid_par'], [(D, f32), (D, bf16)], tm=tm)
        fetch(2 * layer + 1, r['u2'])
        r['gate'] = mm_nn_col("ffn_gate", r['u2'], W['ffn_w_gate', layer], 0, bf16)
        r['up'] = mm_nn_col("ffn_up", r['u2'], W['ffn_w_up', layer], 0, bf16)
        r['dn'] = mm_nn_row("ffn_down", [r['gate'], r['up']], W['ffn_w_down', layer], 0, a_fn=_swiglu_tile)
        if layer + 1 < NL:
            r['end_fn'] = _f_mid
            r['end_par'] = [row1(small['ffn_post_g'], layer), row1(small['mix_pre_g'], layer + 1)]
            h, u = _stage_fwd("end_norm", _f_mid, [r['h1'], r['dn']], r['end_par'], [(D, f32), (D, bf16)], tm=tm)
        else:
            r['end_fn'] = _f_last
            r['end_par'] = [row1(small['ffn_post_g'], layer)]
            (h,) = _stage_fwd("last_norm", _f_last, [r['h1'], r['dn']], r['end_par'], [(D, f32)], tm=tm)
        saved.append(r)

    dy, loss_local = loss_head("loss_head", h, t2, tm)
    loss = lax.psum(loss_local, ("x", "y", "c"))

    gs = {n: jnp.zeros(small[n].shape, f32) for n in small}

    def put(n, i, val):
        gs[n] = gs[n].at[i].add(val.reshape(gs[n].shape[1:]))

    final = {n: lax.empty((w[n].shape[0], 2, w[n].shape[1] // 2, w[n].shape[2]), f32) for n in BIG}
    pending = []

    def reduce_start(gi, parts):
        parts = [halves(p) for p in parts]
        from_sib = rs_sibling("reduce_sibling", parts)
        sums = [rs_add("reduce_add", a, b) for a, b in zip(parts, from_sib)]
        handle = reduce_ici_start(f"reduce_start_{gi}", sums)
        pending.append((gi, handle))
        return handle[-1][0, 0]

    def reduce_finish(after):
        gi, handle = pending.pop(0)
        keys = groups[gi]
        sums, lands = reduce_ici_wait(f"reduce_wait_{gi}", handle, after)
        for (n, l), s_, b_ in zip(keys, sums, lands):
            final[n] = rs_sum4("reduce_sum", s_, b_, final[n], l)
        names = [n for n, _ in keys]
        shared = rs_share("reduce_share", [final[n] for n in names], [l for _, l in keys])
        final.update(zip(names, shared))

    dh = dy
    du_parts = None
    started = None
    for layer in reversed(range(NL)):
        li = layer // 2
        r = saved[layer]
        cts = [[dh]] if du_parts is None else [[dh], du_parts]
        par = r['end_par'] if started is None else [r['end_par'][0] + started] + r['end_par'][1:]
        (dh1, d_dn), pg = _stage_bwd("end_norm_bwd", r['end_fn'], [r['h1'], r['dn']], par, cts,
                                     [f32, bf16], tm=tm)
        put('ffn_post_g', layer, pg[0])
        if du_parts is not None:
            put('mix_pre_g', layer + 1, pg[1])
        d_gate, d_up = mm_nt_row("ffn_down_dx", d_dn, W['ffn_w_down', layer], 0, bf16,
                                 tail=(_swiglu_bwd_tile, [r['gate'], r['up']]), n_out=2)
        g_down = mm_tn_row("ffn_down_dw", [r['gate'], r['up']], d_dn, N_CHIPS, a_fn=_swiglu_tile)
        du_a = mm_nt_col("ffn_gate_dx", d_gate, W['ffn_w_gate', layer], 0)
        du_b = mm_nt_col("ffn_up_dx", d_up, W['ffn_w_up', layer], 0)
        g_gate = mm_tn_col("ffn_gate_dw", r['u2'], d_gate, N_CHIPS)
        g_up = mm_tn_col("ffn_up_dw", r['u2'], d_up, N_CHIPS)
        started = reduce_start(2 * layer + 1, [g_gate, g_up, g_down])
        if len(pending) > 1:
            reduce_finish(du_b)
        par = r['mid_par'][:-1] + [r['mid_par'][-1] + started]
        (dh, dm), pg = _stage_bwd("mid_norm_bwd", r['mid_fn'], [r['h'], r['m']], par,
                                  [[dh1], [du_a, du_b]], [f32, bf16], tm=tm)
        put('mix_post_g', layer, pg[-2])
        put('ffn_pre_g', layer, pg[-1])
        if layer % 2 == 0:
            d_mixed = mm_nt_row("out_proj_dx", dm, W['even_w_out', li], 0)
            g_out = mm_tn_row("out_proj_dw", r['mixed'], dm, N_CHIPS)
            dxs, dbm, dcm, dz, ddt, ddtb, dal, dds, dbn = ssd_bwd(
                "ssd_bwd", r['xact'], r['ymain'], r['dtr'], row1(small['ssd_dt_bias'], li),
                row1(small['ssd_a_log'], li), row1(small['ssd_d'], li), row1(small['ssd_norm_g'], li),
                r['ssd_st'], d_mixed, D)
            put('ssd_dt_bias', li, ddtb)
            put('ssd_a_log', li, dal)
            put('ssd_d', li, dds)
            put('ssd_norm_g', li, dbn)
            d_xact = jnp.concatenate([dxs, dbm, dcm], axis=1)
            d_xbc, dcw, dcb = conv_bwd("ssd_conv_bwd", r['ymain'], 5 * D // CONV_CH, small['ssd_conv_w'][li],
                                       row1(small['ssd_conv_b'], li), d_xact, True, bf16)
            put('ssd_conv_w', li, dcw)
            put('ssd_conv_b', li, dcb)
            dq, df, dv, dg, dlb, dan = hgrn_bwd("hgrn_bwd", r['ymain'], row1(lbs, li), row1(small['hgrn_norm_g'], li),
                                               r['hg_st'], d_mixed, D)
            put('hgrn_norm_g', li, dan)
            r['dlb'] = dlb
            d_main = jnp.concatenate([dq, df, dv, dg, dz, d_xbc], axis=1)
            du_parts = [mm_nt_col("in_proj_dx", d_main, w_main[li], 0),
                        mm_nt_col("in_proj_dt_dx", ddt, w_dt[li], 0)]
            g_in = jnp.concatenate([mm_tn_col("in_proj_dw", r['u'], d_main, 1)[0],
                                    mm_tn_col("in_proj_dt_dw", r['u'], ddt, 1)[0]], axis=-1)
            CI = w['even_w_in'].shape[2]
            g_in = jnp.stack([g_in[:, j * CI:(j + 1) * CI] for j in range(N_CHIPS)])
            started = reduce_start(2 * layer, [g_in, g_out])
        else:
            put('conf_b2', li, pg[0])
            d_c2 = mm_nt_row("conf_out_dx", dm, W['conf_w2', li], 0)
            g_w2 = mm_tn_row("conf_out_dw", r['c2'], dm, N_CHIPS)
            (d_cc,), pl_ = _stage_bwd("conf_ln_bwd", _f_ln_silu, [r['cc']],
                                      [row1(small['conf_ln_g'], li), row1(small['conf_ln_b'], li)],
                                      [[d_c2]], [f32], tm=tm)
            put('conf_ln_g', li, pl_[0])
            put('conf_ln_b', li, pl_[1])
            d_glu, ddw, ddb = conv_bwd("conf_conv_bwd", r['glu'], 0, small['conf_dw_w'][li],
                                       row1(small['conf_dw_b'], li), d_cc, False, f32)
            put('conf_dw_w', li, ddw)
            put('conf_dw_b', li, ddb)
            b1 = row1(small['conf_b1'], li)
            tn = _tile(D, 512)
            (da, dg_), pb = _stage_bwd("conf_glu_bwd", _f_glu, [r['c1'], (r['c1'], D // tn)], [b1, (b1, D // tn)],
                                       [[d_glu]], [bf16, bf16], tm=tm, tn=tn)
            put('conf_b1', li, jnp.concatenate([pb[0], pb[1]], axis=1))
            d_c1 = jnp.concatenate([da, dg_], axis=1)
            du_parts = [mm_nt_col("conf_in_dx", d_c1, W['conf_w1', li], 0)]
            g_w1 = mm_tn_col("conf_in_dw", r['u'], d_c1, N_CHIPS)
            started = reduce_start(2 * layer, [g_w1, g_w2])
        reduce_finish(du_parts[0])
    (grad_x2,), pg = _stage_bwd("pre_norm_bwd", _f_first, [x2], [row1(small['mix_pre_g'], 0) + started],
                                [[dh], du_parts], [f32], tm=tm)
    put('mix_pre_g', 0, pg[0])
    dlbs = jnp.concatenate([saved[2 * i]['dlb'] for i in range(n_even)], axis=0)
    (dlogits,), _ = _stage_bwd("lower_bounds_bwd", _f_lower_bounds, [small['hgrn_lb_logits']], [],
                               [[dlbs]], [f32], tm=n_even)
    gs['hgrn_lb_logits'] = dlogits

    names_s = [n for n in WEIGHT_NAMES if n not in BIG]
    summed = _unpack(all_reduce_small("reduce_small", _pack([gs[n] for n in names_s])),
                     [gs[n].shape for n in names_s])
    grads = {}
    for n, a in zip(names_s, summed):
        if n in SMALL_SHARDED:
            ax = SMALL_SHARDED[n]
            size = w[n].shape[ax]
            start = [0] * a.ndim
            start[ax] = chip * size
            a = lax.dynamic_slice(a, start, a.shape[:ax] + (size,) + a.shape[ax + 1:])
        grads[n] = a

    delta, new_m, new_v = {}, {}, {}
    last = [n for n, _ in groups[pending[0][0]]]
    for n in WEIGHT_NAMES:
        if n not in last:
            if n in BIG:
                grads[n] = final[n].reshape(w[n].shape)
            delta[n], new_m[n], new_v[n] = adamw("adamw", w[n], grads[n], m[n], v[n])
    reduce_finish(delta['ffn_w_down'])
    for n in last:
        grads[n] = final[n].reshape(w[n].shape)
        delta[n], new_m[n], new_v[n] = adamw("adamw", w[n], grads[n], m[n], v[n])
    return (loss, grad_x2[None], *[grads[n] for n in WEIGHT_NAMES], *[delta[n] for n in WEIGHT_NAMES],
            *[new_m[n] for n in WEIGHT_NAMES], *[new_v[n] for n in WEIGHT_NAMES])


def kernel(x, mix_pre_g, mix_post_g, ffn_pre_g, ffn_post_g, hgrn_lb_logits, even_w_in, hgrn_norm_g, ssd_conv_w, ssd_conv_b, ssd_dt_bias, ssd_a_log, ssd_d, ssd_norm_g, even_w_out, conf_w1, conf_b1, conf_dw_w, conf_dw_b, conf_ln_g, conf_ln_b, conf_w2, conf_b2, ffn_w_gate, ffn_w_up, ffn_w_down, loss_target, m_mix_pre_g, m_mix_post_g, m_ffn_pre_g, m_ffn_post_g, m_hgrn_lb_logits, m_even_w_in, m_hgrn_norm_g, m_ssd_conv_w, m_ssd_conv_b, m_ssd_dt_bias, m_ssd_a_log, m_ssd_d, m_ssd_norm_g, m_even_w_out, m_conf_w1, m_conf_b1, m_conf_dw_w, m_conf_dw_b, m_conf_ln_g, m_conf_ln_b, m_conf_w2, m_conf_b2, m_ffn_w_gate, m_ffn_w_up, m_ffn_w_down, v_mix_pre_g, v_mix_post_g, v_ffn_pre_g, v_ffn_post_g, v_hgrn_lb_logits, v_even_w_in, v_hgrn_norm_g, v_ssd_conv_w, v_ssd_conv_b, v_ssd_dt_bias, v_ssd_a_log, v_ssd_d, v_ssd_norm_g, v_even_w_out, v_conf_w1, v_conf_b1, v_conf_dw_w, v_conf_dw_b, v_conf_ln_g, v_conf_ln_b, v_conf_w2, v_conf_b2, v_ffn_w_gate, v_ffn_w_up, v_ffn_w_down):
    args = locals()
    w = {n: args[n] for n in WEIGHT_NAMES}
    m = {n: args["m_" + n] for n in WEIGHT_NAMES}
    v = {n: args["v_" + n] for n in WEIGHT_NAMES}
    return _train_step(x, loss_target, w, m, v)
```

```python
import functools
import math

import jax
import jax.numpy as jnp
from jax import lax
from jax.experimental import pallas as pl
from jax.experimental.pallas import tpu as pltpu

f32 = jnp.float32
bf16 = jnp.bfloat16
MESH = pl.DeviceIdType.MESH
HI = lax.Precision.HIGHEST

A_HEAD = 128
A_CHUNK = 64
A_SUB = 8
A_REF = 4
A_EXP_CAP = 60.0
A_HEADS_PER_STEP = 8
A_F_MIN = 1e-6
B_HEAD = 64
B_GROUPS = 4
B_STATE = 128
B_CONV = 4
B_CHUNK = 128
C_KERNEL = 31
RMS_EPS = 1e-6
LN_EPS = 1e-5
ADAM_LR = 0.001
ADAM_B1 = 0.9
ADAM_B2 = 0.999
ADAM_EPS = 1e-08
ADAM_WD = 0.01
ADAM_STEP = 10

N_CHIPS = 4
N_DEV = 8
V7X_VMEM_LIMIT = 56 * 1024 * 1024
LANES = 128
CONV_PAD = 32
CONV_ROWS = 128
CONV_CH = 256

NN = (((1,), (0,)), ((), ()))
NT = (((1,), (1,)), ((), ()))
TN = (((0,), (0,)), ((), ()))


def _tile(n, cap, unit=LANES):
    best = None
    for t in range(unit, min(n, cap) + 1, unit):
        if n % t == 0:
            best = t
    return n if best is None else best


def _cp(*sem):
    return pltpu.CompilerParams(dimension_semantics=sem, vmem_limit_bytes=V7X_VMEM_LIMIT)


def _sigmoid(x):
    return jax.nn.sigmoid(x)


def _silu(x):
    return x * jax.nn.sigmoid(x)


def _rms(x, g):
    return x * lax.rsqrt(jnp.mean(x * x, axis=-1, keepdims=True) + RMS_EPS) * g


def _pair(a):
    return a if isinstance(a, tuple) else (a, 0)


def _stage(name, fn, rows, params, outs, par_outs=(), *, tm, tn=None):
    rows = [_pair(r) for r in rows]
    params = [_pair(p) for p in params]
    S = rows[0][0].shape[0]
    n_in, n_o = len(rows) + len(params), len(outs)
    if tn is None:
        grid = (S // tm,)
        in_specs = [pl.BlockSpec((tm, a.shape[1]), lambda i: (i, 0)) for a, _ in rows]
        in_specs += [pl.BlockSpec(a.shape, lambda i: (0, 0)) for a, _ in params]
        out_specs = [pl.BlockSpec((tm, w), lambda i: (i, 0)) for w, _ in outs]
        out_specs += [pl.BlockSpec((k, w), lambda i: (0, 0)) for k, w in par_outs]
        row_axis = 0
        sem = ("arbitrary",) if par_outs else ("parallel",)
    else:
        grid = (outs[0][0] // tn, S // tm)
        in_specs = [pl.BlockSpec((tm, tn), lambda j, i, o=o: (i, j + o)) for _, o in rows]
        in_specs += [pl.BlockSpec((a.shape[0], tn), lambda j, i, o=o: (0, j + o)) for a, o in params]
        out_specs = [pl.BlockSpec((tm, tn), lambda j, i: (i, j)) for _ in outs]
        out_specs += [pl.BlockSpec((k, tn), lambda j, i: (0, j)) for k, _ in par_outs]
        row_axis = 1
        sem = ("parallel", "arbitrary") if par_outs else ("parallel", "parallel")
    out_shape = [jax.ShapeDtypeStruct((S, w), d) for w, d in outs]
    out_shape += [jax.ShapeDtypeStruct((k, w), f32) for k, w in par_outs]

    def body(*refs):
        res = fn(*[r[...] for r in refs[:n_in]])
        for r, v in zip(refs[n_in:n_in + n_o], res[:n_o]):
            r[...] = v.astype(r.dtype)
        if par_outs:
            acc_refs = refs[n_in + n_o:]

            @pl.when(pl.program_id(row_axis) == 0)
            def _():
                for r in acc_refs:
                    r[...] = jnp.zeros_like(r)

            for r, v in zip(acc_refs, res[n_o:]):
                r[...] += v

    return pl.pallas_call(
        body, name=name, grid=grid, in_specs=in_specs, out_specs=out_specs, out_shape=out_shape,
        compiler_params=_cp(*sem))(*[a for a, _ in rows], *[a for a, _ in params])


def _stage_fwd(name, fn, rows, params, outs, *, tm, tn=None):
    n_r = len(rows)

    def ffn(*t):
        return fn(*[v.astype(f32) for v in t[:n_r]], *t[n_r:])

    return _stage(name, ffn, rows, params, outs, tm=tm, tn=tn)


def _stage_bwd(name, fn, rows, params, cts, drow, *, tm, tn=None):
    rows = [_pair(r) for r in rows]
    params = [_pair(p) for p in params]
    n_r, n_p = len(rows), len(params)
    flat_ct = [_pair(c) for group in cts for c in group]
    counts = [len(group) for group in cts]
    need = [i for i, d in enumerate(drow) if d is not None]

    def bfn(*t):
        r = [v.astype(f32) for v in t[:n_r]]
        c = t[n_r:n_r + len(flat_ct)]
        p = list(t[n_r + len(flat_ct):])
        res, vjp = jax.vjp(fn, *r, *p)
        ct, pos = [], 0
        for o, k in zip(res, counts):
            s = c[pos].astype(f32)
            for e in range(1, k):
                s = s + c[pos + e].astype(f32)
            pos += k
            ct.append(s.astype(o.dtype))
        g = vjp(tuple(ct))
        return tuple(g[i] for i in need) + tuple(g[n_r:])

    if tn is None:
        outs = [(rows[i][0].shape[1], drow[i]) for i in need]
        par_outs = [p.shape for p, _ in params]
    else:
        w_all = flat_ct[0][0].shape[1]
        outs = [(w_all, drow[i]) for i in need]
        par_outs = [(p.shape[0], w_all) for p, _ in params]
    res = _stage(name, bfn, rows + flat_ct, params, outs, par_outs, tm=tm, tn=tn)
    return res[:len(need)], res[len(need):]


def _mm(name, dims, a, b, grid, a_spec, b_spec, o_spec, out_shape, acc_shape, a_fn=None, tail=None, dep=None):
    nk = grid[2]
    a_list = list(a) if isinstance(a, (list, tuple)) else [a]
    na = len(a_list)
    t_fn, t_arrays = tail if tail is not None else (None, [])
    ne = len(t_arrays)
    deps = [] if dep is None else [dep]
    multi = isinstance(out_shape, (list, tuple))

    def body(*refs):
        a_refs, b_ref, t_refs = refs[:na], refs[na], refs[na + 1:na + 1 + ne]
        o_refs, acc = refs[na + 1 + ne + len(deps):-1], refs[-1]
        k = pl.program_id(2)

        @pl.when(k == 0)
        def _():
            acc[...] = jnp.zeros_like(acc)

        lhs = a_refs[0][...] if a_fn is None else a_fn(*[r[...] for r in a_refs])
        acc[...] += lax.dot_general(lhs.astype(bf16), b_ref[...].astype(bf16), dims, preferred_element_type=f32)

        @pl.when(k == nk - 1)
        def _():
            res = (acc[...],) if t_fn is None else t_fn(acc[...], *[r[...] for r in t_refs])
            for r, val in zip(o_refs, res):
                r[...] = val.astype(r.dtype)

    return pl.pallas_call(
        body, name=name, grid=grid,
        in_specs=[a_spec] * na + [b_spec] + [o_spec] * ne + [pl.BlockSpec(memory_space=pl.ANY)] * len(deps),
        out_specs=[o_spec] * len(out_shape) if multi else o_spec, out_shape=out_shape,
        scratch_shapes=[pltpu.VMEM(acc_shape, f32)],
        compiler_params=_cp("parallel", "parallel", "arbitrary"))(*a_list, b, *t_arrays, *deps)


def _mm_tiles(S):
    return _tile(S, 512)


def mm_nn_col(name, a, W, li, out_dtype=f32, dep=None):
    P, _, K, C = W.shape
    S = a.shape[0]
    tm, tn, tk = _mm_tiles(S), _tile(C, 1536), _tile(K, 2048)
    nc = C // tn
    return _mm(name, NN, a, W, (S // tm, P * nc, K // tk),
               pl.BlockSpec((tm, tk), lambda i, j, k: (i, k)),
               pl.BlockSpec((None, None, tk, tn), lambda i, j, k: (j // nc, li, k, j % nc)),
               pl.BlockSpec((tm, tn), lambda i, j, k: (i, j)),
               jax.ShapeDtypeStruct((S, P * C), out_dtype), (tm, tn), dep=dep)


def mm_nn_row(name, a, W, li, out_dtype=f32, a_fn=None):
    P, _, R, N = W.shape
    S = (a[0] if a_fn is not None else a).shape[0]
    tm, tn, tk = _mm_tiles(S), _tile(N, 1024), _tile(R, 2048)
    nr = R // tk
    return _mm(name, NN, a, W, (S // tm, N // tn, P * nr),
               pl.BlockSpec((tm, tk), lambda i, j, k: (i, k)),
               pl.BlockSpec((None, None, tk, tn), lambda i, j, k: (k // nr, li, k % nr, j)),
               pl.BlockSpec((tm, tn), lambda i, j, k: (i, j)),
               jax.ShapeDtypeStruct((S, N), out_dtype), (tm, tn), a_fn=a_fn)


def mm_nt_col(name, dy, W, li, out_dtype=f32):
    P, _, K, C = W.shape
    S = dy.shape[0]
    tm, tn, tk = _mm_tiles(S), _tile(K, 1024), _tile(C, 2048)
    nc = C // tk
    return _mm(name, NT, dy, W, (S // tm, K // tn, P * nc),
               pl.BlockSpec((tm, tk), lambda i, j, k: (i, k)),
               pl.BlockSpec((None, None, tn, tk), lambda i, j, k: (k // nc, li, j, k % nc)),
               pl.BlockSpec((tm, tn), lambda i, j, k: (i, j)),
               jax.ShapeDtypeStruct((S, K), out_dtype), (tm, tn))


def mm_nt_row(name, dy, W, li, out_dtype=f32, tail=None, n_out=None):
    P, _, R, N = W.shape
    S = dy.shape[0]
    tm, tn, tk = _mm_tiles(S), _tile(R, 1536), _tile(N, 2048)
    nr = R // tn
    out = jax.ShapeDtypeStruct((S, P * R), out_dtype)
    return _mm(name, NT, dy, W, (S // tm, P * nr, N // tk),
               pl.BlockSpec((tm, tk), lambda i, j, k: (i, k)),
               pl.BlockSpec((None, None, tn, tk), lambda i, j, k: (j // nr, li, j % nr, k)),
               pl.BlockSpec((tm, tn), lambda i, j, k: (i, j)),
               out if n_out is None else [out] * n_out, (tm, tn), tail=tail)


def mm_tn_col(name, a, dy, P):
    S, K = a.shape
    C = dy.shape[1] // P
    tm, tn, tk = _tile(K, 512), _tile(C, 1536), _tile(S, 1024)
    nc = C // tn
    return _mm(name, TN, a, dy, (K // tm, P * nc, S // tk),
               pl.BlockSpec((tk, tm), lambda i, j, k: (k, i)),
               pl.BlockSpec((tk, tn), lambda i, j, k: (k, j)),
               pl.BlockSpec((None, tm, tn), lambda i, j, k: (j // nc, i, j % nc)),
               jax.ShapeDtypeStruct((P, K, C), bf16), (tm, tn))


def mm_tn_row(name, a, dy, P, a_fn=None):
    S, N = dy.shape
    R = (a[0] if a_fn is not None else a).shape[1] // P
    tm, tn, tk = _tile(R, 1536), _tile(N, 1024), _tile(S, 1024)
    nr = R // tm
    return _mm(name, TN, a, dy, (P * nr, N // tn, S // tk),
               pl.BlockSpec((tk, tm), lambda i, j, k: (k, i)),
               pl.BlockSpec((tk, tn), lambda i, j, k: (k, j)),
               pl.BlockSpec((None, tm, tn), lambda i, j, k: (i // nr, i % nr, j)),
               jax.ShapeDtypeStruct((P, R, N), bf16), (tm, tn), a_fn=a_fn)


def _hgrn_chunk(st, q, fp, v, gt, lb, an):
    C = q.shape[0]
    sig = _sigmoid(fp)
    f = lb + (1.0 - lb) * sig
    kk = (1.0 - lb) * (1.0 - sig)
    g = jnp.log(jnp.maximum(f, A_F_MIN))
    qs = _silu(q)
    row = lax.broadcasted_iota(jnp.int32, (C, C), 0)
    col = lax.broadcasted_iota(jnp.int32, (C, C), 1)
    tri = (col <= row).astype(f32)
    b = jnp.dot(tri, g, precision=HI, preferred_element_type=f32)
    o_inter = lax.dot_general((qs * jnp.exp(b)).astype(bf16), st.astype(bf16), NT,
                              preferred_element_type=f32)
    T, NB = A_SUB, C // A_SUB
    refs = [b[i * T + A_REF:i * T + A_REF + 1, :] for i in range(NB)]
    ref_q = jnp.concatenate([jnp.broadcast_to(r, (T, A_HEAD)) for r in refs], axis=0)
    ref_k = jnp.concatenate([jnp.broadcast_to(r, (C, A_HEAD)) for r in refs], axis=0)
    q_t = qs * jnp.exp(b - ref_q)
    k_t = jnp.concatenate([kk] * NB, axis=0) * jnp.exp(
        jnp.minimum(ref_k - jnp.concatenate([b] * NB, axis=0), A_EXP_CAP))
    s = lax.dot_general(q_t.astype(bf16), k_t.astype(bf16), NT, preferred_element_type=f32)
    trow = lax.broadcasted_iota(jnp.int32, (C, NB * C), 0)
    scol = lax.broadcasted_iota(jnp.int32, (C, NB * C), 1)
    keep = jnp.logical_and(scol // C == trow // T, scol % C <= trow)
    s = jnp.where(keep, s, 0.0)
    o_intra = jnp.dot(s.astype(bf16), jnp.concatenate([v.astype(bf16)] * NB, axis=0),
                      preferred_element_type=f32)
    bl = b[C - 1:C, :]
    kd = kk * jnp.exp(bl - b)
    st_new = st * jnp.exp(bl) + lax.dot_general(v.astype(bf16), kd.astype(bf16), TN,
                                                preferred_element_type=f32)
    o = o_inter + o_intra
    y = o * lax.rsqrt(jnp.mean(o * o, axis=-1, keepdims=True) + RMS_EPS) * an * _silu(gt)
    return st_new, y


def _hgrn_heads_per_step(HA):
    return A_HEADS_PER_STEP if HA % A_HEADS_PER_STEP == 0 else 1


def _hgrn_in_specs(HA, HP, cidx):
    W = HP * A_HEAD
    specs = [pl.BlockSpec((A_CHUNK, W), lambda h, c, s=s: (cidx(c), s * (HA // HP) + h)) for s in range(4)]
    specs += [pl.BlockSpec((1, W), lambda h, c: (0, h))] * 2
    return specs


def _head(ref, j):
    return ref[:, j * A_HEAD:(j + 1) * A_HEAD]


def hgrn_fwd(name, ymain, lb, an, D):
    S = ymain.shape[0]
    HA, nc = D // A_HEAD, S // A_CHUNK
    HP = _hgrn_heads_per_step(HA)
    W = HP * A_HEAD

    def body(q, fp, v, gt, lb_ref, an_ref, o_ref, sv_ref, st):
        @pl.when(pl.program_id(1) == 0)
        def _():
            st[...] = jnp.zeros_like(st)

        sv_ref[...] = st[...]
        for j in range(HP):
            st_new, y = _hgrn_chunk(st[j], _head(q, j), _head(fp, j), _head(v, j), _head(gt, j),
                                    _head(lb_ref, j), _head(an_ref, j))
            st[j] = st_new
            o_ref[:, j * A_HEAD:(j + 1) * A_HEAD] = y.astype(o_ref.dtype)

    return pl.pallas_call(
        body, name=name, grid=(HA // HP, nc),
        in_specs=_hgrn_in_specs(HA, HP, lambda c: c),
        out_specs=[pl.BlockSpec((A_CHUNK, W), lambda h, c: (c, h)),
                   pl.BlockSpec((HP, None, A_HEAD, A_HEAD), lambda h, c: (h, c, 0, 0))],
        out_shape=[jax.ShapeDtypeStruct((S, D), bf16),
                   jax.ShapeDtypeStruct((HA, nc, A_HEAD, A_HEAD), f32)],
        scratch_shapes=[pltpu.VMEM((HP, A_HEAD, A_HEAD), f32)],
        compiler_params=_cp("parallel", "arbitrary"))(ymain, ymain, ymain, ymain, lb, an)


def hgrn_bwd(name, ymain, lb, an, saved, dmixed, D):
    S = ymain.shape[0]
    HA, nc = D // A_HEAD, S // A_CHUNK
    HP = _hgrn_heads_per_step(HA)
    W = HP * A_HEAD
    rev = lambda c: nc - 1 - c

    def body(q, fp, v, gt, lb_ref, an_ref, sv_ref, do_ref, dq, df, dv, dg, dlb, dan, dst):
        @pl.when(pl.program_id(1) == 0)
        def _():
            dst[...] = jnp.zeros_like(dst)
            dlb[...] = jnp.zeros_like(dlb)
            dan[...] = jnp.zeros_like(dan)

        for j in range(HP):
            cols = slice(j * A_HEAD, (j + 1) * A_HEAD)
            _, vjp = jax.vjp(_hgrn_chunk, sv_ref[j], _head(q, j), _head(fp, j), _head(v, j), _head(gt, j),
                             _head(lb_ref, j), _head(an_ref, j))
            g = vjp((dst[j], _head(do_ref, j).astype(f32)))
            dst[j] = g[0]
            for r, x in zip((dq, df, dv, dg), g[1:5]):
                r[:, cols] = x.astype(r.dtype)
            dlb[:, cols] += g[5]
            dan[:, cols] += g[6]

    blk = pl.BlockSpec((A_CHUNK, W), lambda h, c: (rev(c), h))
    vec = pl.BlockSpec((1, W), lambda h, c: (0, h))
    return pl.pallas_call(
        body, name=name, grid=(HA // HP, nc),
        in_specs=_hgrn_in_specs(HA, HP, rev) + [
            pl.BlockSpec((HP, None, A_HEAD, A_HEAD), lambda h, c: (h, rev(c), 0, 0)), blk],
        out_specs=[blk] * 4 + [vec] * 2,
        out_shape=[jax.ShapeDtypeStruct((S, D), bf16)] * 4 + [jax.ShapeDtypeStruct((1, D), f32)] * 2,
        scratch_shapes=[pltpu.VMEM((HP, A_HEAD, A_HEAD), f32)],
        compiler_params=_cp("parallel", "arbitrary"))(ymain, ymain, ymain, ymain, lb, an, saved, dmixed)


def _ssd_chunk(hp, xs, bm, cm, z, dtr, dtb, alog, dsk, bn, g, R):
    L, GW = xs.shape
    HB = dtr.shape[1]
    R8 = max(R, 8)
    dt = jax.nn.softplus(dtr + dtb)
    a = -jnp.exp(alog)
    row = lax.broadcasted_iota(jnp.int32, (L, L), 0)
    col = lax.broadcasted_iota(jnp.int32, (L, L), 1)
    causal = col <= row
    cs = jnp.dot(causal.astype(f32), dt * a, precision=HI, preferred_element_type=f32)
    eh = lax.broadcasted_iota(jnp.int32, (HB, GW), 0)
    ec = lax.broadcasted_iota(jnp.int32, (HB, GW), 1)
    spread = (eh == g * R + ec // B_HEAD).astype(f32)
    sh = lax.broadcasted_iota(jnp.int32, (R8, HB), 1)
    sr = lax.broadcasted_iota(jnp.int32, (R8, HB), 0)
    pick_t = jnp.logical_and(sh == g * R + sr, sr < R).astype(f32)
    dtf = jnp.dot(dt, spread, precision=HI, preferred_element_type=f32)
    csf = jnp.dot(cs, spread, precision=HI, preferred_element_type=f32)
    dsf = jnp.dot(jnp.broadcast_to(dsk, (8, HB)), spread, precision=HI, preferred_element_type=f32)[0:1, :]
    cs_col = lax.dot_general(cs, pick_t, NT, precision=HI, preferred_element_type=f32)
    cs_row = lax.dot_general(pick_t, cs, NT, precision=HI, preferred_element_type=f32)
    xdt = xs * dtf
    cb = lax.dot_general(cm.astype(bf16), bm.astype(bf16), NT, preferred_element_type=f32)
    lane_head = lax.broadcasted_iota(jnp.int32, (1, GW), 1) // B_HEAD
    y = jnp.zeros((L, GW), f32)
    for r in range(R):
        seg = cs_col[:, r:r + 1] - cs_row[r:r + 1, :]
        dec = jnp.where(causal, jnp.exp(jnp.where(causal, seg, 0.0)), 0.0)
        xm = jnp.where(lane_head == r, xdt, 0.0)
        y = y + jnp.dot((cb * dec).astype(bf16), xm.astype(bf16), preferred_element_type=f32)
    csl = csf[L - 1:L, :]
    dte = jnp.exp(csl - csf)
    states = lax.dot_general(bm.astype(bf16), (xdt * dte).astype(bf16), TN, preferred_element_type=f32)
    y_off = jnp.dot(cm.astype(bf16), hp.astype(bf16), preferred_element_type=f32) * jnp.exp(csf)
    hn = hp * jnp.exp(csl) + states
    gated = (y + y_off + dsf * xs) * _silu(z)
    out = gated * lax.rsqrt(jnp.mean(gated * gated, axis=-1, keepdims=True) + RMS_EPS) * bn
    return hn, out


def _ssd_in_specs(D, HB, cidx):
    L, GW, N = B_CHUNK, D // B_GROUPS, B_STATE
    zoff, boff = 4 * D // GW, D // N
    return [
        pl.BlockSpec((L, GW), lambda c, g: (cidx(c), g)),
        pl.BlockSpec((L, N), lambda c, g: (cidx(c), boff + g)),
        pl.BlockSpec((L, N), lambda c, g: (cidx(c), boff + B_GROUPS + g)),
        pl.BlockSpec((L, GW), lambda c, g: (cidx(c), zoff + g)),
        pl.BlockSpec((L, HB), lambda c, g: (cidx(c), 0)),
        pl.BlockSpec((1, HB), lambda c, g: (0, 0)),
        pl.BlockSpec((1, HB), lambda c, g: (0, 0)),
        pl.BlockSpec((1, HB), lambda c, g: (0, 0)),
        pl.BlockSpec((1, GW), lambda c, g: (0, g)),
    ]


def ssd_fwd(name, xact, ymain, dtr, dtb, alog, dsk, bn, D):
    S, HB = dtr.shape
    nc, GW, R = S // B_CHUNK, D // B_GROUPS, HB // B_GROUPS

    def body(xs, bm, cm, z, dt_ref, dtb_ref, al_ref, ds_ref, bn_ref, o_ref, sv_ref, hs):
        g = pl.program_id(1)

        @pl.when(pl.program_id(0) == 0)
        def _():
            hs[g] = jnp.zeros((B_STATE, GW), f32)

        hp = hs[g]
        sv_ref[...] = hp
        hn, out = _ssd_chunk(hp, xs[...], bm[...], cm[...], z[...], dt_ref[...], dtb_ref[...], al_ref[...],
                             ds_ref[...], bn_ref[...], g, R)
        hs[g] = hn
        o_ref[...] = out.astype(o_ref.dtype)

    return pl.pallas_call(
        body, name=name, grid=(nc, B_GROUPS),
        in_specs=_ssd_in_specs(D, HB, lambda c: c),
        out_specs=[pl.BlockSpec((B_CHUNK, GW), lambda c, g: (c, g)),
                   pl.BlockSpec((None, None, B_STATE, GW), lambda c, g: (c, g, 0, 0))],
        out_shape=[jax.ShapeDtypeStruct((S, D), bf16),
                   jax.ShapeDtypeStruct((nc, B_GROUPS, B_STATE, GW), f32)],
        scratch_shapes=[pltpu.VMEM((B_GROUPS, B_STATE, GW), f32)],
        compiler_params=_cp("arbitrary", "arbitrary"))(xact, xact, xact, ymain, dtr, dtb, alog, dsk, bn)


def ssd_bwd(name, xact, ymain, dtr, dtb, alog, dsk, bn, saved, dmixed, D):
    S, HB = dtr.shape
    nc, GW, R = S // B_CHUNK, D // B_GROUPS, HB // B_GROUPS
    rev = lambda c: nc - 1 - c
    ooff = D // GW

    def body(xs, bm, cm, z, dt_ref, dtb_ref, al_ref, ds_ref, bn_ref, sv_ref, do_ref,
             dxs, dbm, dcm, dz, ddt, ddtb, dal, dds, dbn, dhs):
        c, g = pl.program_id(0), pl.program_id(1)

        @pl.when(c == 0)
        def _():
            dhs[g] = jnp.zeros((B_STATE, GW), f32)
            dbn[g] = jnp.zeros((1, GW), f32)

        @pl.when(jnp.logical_and(c == 0, g == 0))
        def _():
            ddtb[...] = jnp.zeros_like(ddtb)
            dal[...] = jnp.zeros_like(dal)
            dds[...] = jnp.zeros_like(dds)

        @pl.when(g == 0)
        def _():
            ddt[...] = jnp.zeros_like(ddt)

        fn = functools.partial(_ssd_chunk, g=g, R=R)
        _, vjp = jax.vjp(fn, sv_ref[...], xs[...], bm[...], cm[...], z[...], dt_ref[...], dtb_ref[...],
                         al_ref[...], ds_ref[...], bn_ref[...])
        gr = vjp((dhs[g], do_ref[...].astype(f32)))
        dhs[g] = gr[0]
        dxs[...] = gr[1]
        dbm[...] = gr[2]
        dcm[...] = gr[3]
        dz[...] = gr[4].astype(dz.dtype)
        ddt[...] += gr[5]
        ddtb[...] += gr[6]
        dal[...] += gr[7]
        dds[...] += gr[8]
        dbn[g] += gr[9]

    hb_vec = pl.BlockSpec((1, HB), lambda c, g: (0, 0))
    return pl.pallas_call(
        body, name=name, grid=(nc, B_GROUPS),
        in_specs=_ssd_in_specs(D, HB, rev) + [
            pl.BlockSpec((None, None, B_STATE, GW), lambda c, g: (rev(c), g, 0, 0)),
            pl.BlockSpec((B_CHUNK, GW), lambda c, g: (rev(c), ooff + g))],
        out_specs=[pl.BlockSpec((B_CHUNK, GW), lambda c, g: (rev(c), g)),
                   pl.BlockSpec((B_CHUNK, B_STATE), lambda c, g: (rev(c), g)),
                   pl.BlockSpec((B_CHUNK, B_STATE), lambda c, g: (rev(c), g)),
                   pl.BlockSpec((B_CHUNK, GW), lambda c, g: (rev(c), g)),
                   pl.BlockSpec((B_CHUNK, HB), lambda c, g: (rev(c), 0)),
                   hb_vec, hb_vec, hb_vec,
                   pl.BlockSpec((B_GROUPS, 1, GW), lambda c, g: (0, 0, 0))],
        out_shape=[jax.ShapeDtypeStruct((S, D), f32),
                   jax.ShapeDtypeStruct((S, B_GROUPS * B_STATE), f32),
                   jax.ShapeDtypeStruct((S, B_GROUPS * B_STATE), f32),
                   jax.ShapeDtypeStruct((S, D), bf16),
                   jax.ShapeDtypeStruct((S, HB), f32),
                   jax.ShapeDtypeStruct((1, HB), f32), jax.ShapeDtypeStruct((1, HB), f32),
                   jax.ShapeDtypeStruct((1, HB), f32),
                   jax.ShapeDtypeStruct((B_GROUPS, 1, GW), f32)],
        scratch_shapes=[pltpu.VMEM((B_GROUPS, B_STATE, GW), f32)],
        compiler_params=_cp("arbitrary", "arbitrary"))(
            xact, xact, xact, ymain, dtr, dtb, alog, dsk, bn, saved, dmixed)


def _conv_taps(xp, w_ref, b_ref, r0, K):
    acc = jnp.broadcast_to(b_ref[...], (CONV_ROWS, b_ref.shape[1]))
    for k in range(K):
        acc = acc + w_ref[k:k + 1, :] * xp[r0 + CONV_PAD - (K - 1) + k:r0 + CONV_PAD - (K - 1) + k + CONV_ROWS, :]
    return acc


def conv_fwd(name, x, xoff, w, b, act, out_dtype):
    S = x.shape[0]
    K, CW = w.shape
    tc = CONV_CH

    def body(x_ref, w_ref, b_ref, o_ref, xp):
        xp[0:CONV_PAD, :] = jnp.zeros((CONV_PAD, tc), f32)
        xp[CONV_PAD:CONV_PAD + S, :] = x_ref[...].astype(f32)
        for r0 in range(0, S, CONV_ROWS):
            acc = _conv_taps(xp, w_ref, b_ref, r0, K)
            if act:
                acc = _silu(acc)
            o_ref[r0:r0 + CONV_ROWS, :] = acc.astype(o_ref.dtype)

    return pl.pallas_call(
        body, name=name, grid=(CW // tc,),
        in_specs=[pl.BlockSpec((S, tc), lambda j: (0, j + xoff)),
                  pl.BlockSpec((K, tc), lambda j: (0, j)),
                  pl.BlockSpec((1, tc), lambda j: (0, j))],
        out_specs=pl.BlockSpec((S, tc), lambda j: (0, j)),
        out_shape=jax.ShapeDtypeStruct((S, CW), out_dtype),
        scratch_shapes=[pltpu.VMEM((S + CONV_PAD, tc), f32)],
        compiler_params=_cp("parallel"))(x, w, b)


def conv_bwd(name, x, xoff, w, b, dout, act, dx_dtype):
    S = x.shape[0]
    K, CW = w.shape
    tc = CONV_CH

    def body(x_ref, w_ref, b_ref, d_ref, dx_ref, dw_ref, db_ref, xp, dp):
        xp[0:CONV_PAD, :] = jnp.zeros((CONV_PAD, tc), f32)
        xp[CONV_PAD:CONV_PAD + S, :] = x_ref[...].astype(f32)
        dp[S:S + CONV_PAD, :] = jnp.zeros((CONV_PAD, tc), f32)
        db = jnp.zeros((1, tc), f32)
        for r0 in range(0, S, CONV_ROWS):
            d = d_ref[r0:r0 + CONV_ROWS, :].astype(f32)
            if act:
                pre = _conv_taps(xp, w_ref, b_ref, r0, K)
                s = _sigmoid(pre)
                d = d * (s + pre * s * (1.0 - s))
            dp[r0:r0 + CONV_ROWS, :] = d
            db = db + jnp.sum(d, axis=0, keepdims=True)
        db_ref[...] = db
        for r0 in range(0, S, CONV_ROWS):
            acc = jnp.zeros((CONV_ROWS, tc), f32)
            for k in range(K):
                acc = acc + w_ref[k:k + 1, :] * dp[r0 + (K - 1 - k):r0 + (K - 1 - k) + CONV_ROWS, :]
            dx_ref[r0:r0 + CONV_ROWS, :] = acc.astype(dx_ref.dtype)
        for k in range(K):
            acc = jnp.zeros((1, tc), f32)
            for r0 in range(0, S, CONV_ROWS):
                lo = r0 + CONV_PAD - (K - 1) + k
                acc = acc + jnp.sum(dp[r0:r0 + CONV_ROWS, :] * xp[lo:lo + CONV_ROWS, :], axis=0, keepdims=True)
            dw_ref[k:k + 1, :] = acc

    return pl.pallas_call(
        body, name=name, grid=(CW // tc,),
        in_specs=[pl.BlockSpec((S, tc), lambda j: (0, j + xoff)),
                  pl.BlockSpec((K, tc), lambda j: (0, j)),
                  pl.BlockSpec((1, tc), lambda j: (0, j)),
                  pl.BlockSpec((S, tc), lambda j: (0, j))],
        out_specs=[pl.BlockSpec((S, tc), lambda j: (0, j)),
                   pl.BlockSpec((K, tc), lambda j: (0, j)),
                   pl.BlockSpec((1, tc), lambda j: (0, j))],
        out_shape=[jax.ShapeDtypeStruct((S, CW), dx_dtype),
                   jax.ShapeDtypeStruct((K, CW), f32),
                   jax.ShapeDtypeStruct((1, CW), f32)],
        scratch_shapes=[pltpu.VMEM((S + CONV_PAD, tc), f32), pltpu.VMEM((S + CONV_PAD, tc), f32)],
        compiler_params=_cp("parallel"))(x, w, b, dout)


def loss_head(name, y, target, tm):
    S, D = y.shape

    def body(y_ref, t_ref, dy_ref, l_ref):
        @pl.when(pl.program_id(0) == 0)
        def _():
            l_ref[...] = jnp.zeros_like(l_ref)

        err = y_ref[...] - t_ref[...]
        dy_ref[...] = err * (1.0 / D)
        l_ref[...] += jnp.sum(err * err) * (0.5 / D)

    dy, l = pl.pallas_call(
        body, name=name, grid=(S // tm,),
        in_specs=[pl.BlockSpec((tm, D), lambda i: (i, 0))] * 2,
        out_specs=[pl.BlockSpec((tm, D), lambda i: (i, 0)), pl.BlockSpec((8, LANES), lambda i: (0, 0))],
        out_shape=[jax.ShapeDtypeStruct((S, D), f32), jax.ShapeDtypeStruct((8, LANES), f32)],
        compiler_params=_cp("arbitrary"))(y, target)
    return dy, l[0, 0]


def _flat2d_tiles(rows, cols, itemsize, target_bytes):
    tc = _tile(cols, 1024) if cols % LANES == 0 else cols
    cap = max(8, target_bytes // (tc * itemsize))
    tr = _tile(rows, cap, 16) if rows % 16 == 0 else rows
    return tr, tc


def adamw(name, w, g, m, v):
    shape = w.shape
    cols = shape[-1]
    rows = math.prod(shape[:-1])
    tr, tc = _flat2d_tiles(rows, cols, 4, 1 << 20)
    c1 = 1.0 - ADAM_B1 ** ADAM_STEP
    c2 = 1.0 - ADAM_B2 ** ADAM_STEP

    def body(w_ref, g_ref, m_ref, v_ref, d_ref, nm_ref, nv_ref):
        gg = g_ref[...]
        nm = ADAM_B1 * m_ref[...] + (1.0 - ADAM_B1) * gg
        nv = ADAM_B2 * v_ref[...] + (1.0 - ADAM_B2) * (gg * gg)
        d_ref[...] = -ADAM_LR * ((nm / c1) / (jnp.sqrt(nv / c2) + ADAM_EPS) + ADAM_WD * w_ref[...])
        nm_ref[...] = nm
        nv_ref[...] = nv

    spec = pl.BlockSpec((tr, tc), lambda i, j: (i, j))
    outs = pl.pallas_call(
        body, name=name, grid=(rows // tr, cols // tc), in_specs=[spec] * 4, out_specs=[spec] * 3,
        out_shape=[jax.ShapeDtypeStruct((rows, cols), f32)] * 3,
        compiler_params=_cp("parallel", "parallel"))(*[a.reshape(rows, cols) for a in (w, g, m, v)])
    return [o.reshape(shape) for o in outs]


def _core_index():
    return lax.axis_index("c").astype(jnp.int32).reshape(1)


def _half_rows_tile(Rh, C):
    return _tile(Rh, max(16, (2 << 20) // (C * 2)), 16)


def rs_add(name, G, buf):
    P, _, Rh, C = G.shape
    tr = _half_rows_tile(Rh, C)

    def body(c_ref, g_ref, b_ref, o_ref):
        o_ref[...] = (g_ref[...].astype(f32) + b_ref[...].astype(f32)).astype(o_ref.dtype)

    return pl.pallas_call(
        body, name=name,
        grid_spec=pltpu.PrefetchScalarGridSpec(
            num_scalar_prefetch=1, grid=(P, Rh // tr),
            in_specs=[pl.BlockSpec((None, None, tr, C), lambda p, i, c: (p, c[0], i, 0)),
                      pl.BlockSpec((None, tr, C), lambda p, i, c: (p, i, 0))],
            out_specs=pl.BlockSpec((None, tr, C), lambda p, i, c: (p, i, 0))),
        out_shape=jax.ShapeDtypeStruct((P, Rh, C), bf16),
        compiler_params=_cp("parallel", "parallel"))(_core_index(), G, buf)


def _chip_indices():
    x, y, c = lax.axis_index("x"), lax.axis_index("y"), lax.axis_index("c")
    ids = [2 * x + y] + [2 * _flip(x, fx) + _flip(y, fy) for fx, fy in _CHIP_FLIPS] + [c]
    return [i.astype(jnp.int32).reshape(1) for i in ids]


def rs_sum4(name, pair, buf, final, layer):
    P, Rh, C = buf.shape
    tr = _half_rows_tile(Rh, C)

    def body(i0, i1, i2, i3, ic, b0, b1, b2, b3, f_ref, o_ref):
        o_ref[...] = ((b0[...].astype(f32) + b1[...].astype(f32)) + b2[...].astype(f32)) + b3[...].astype(f32)

    blk = (None, tr, C)
    return pl.pallas_call(
        body, name=name,
        grid_spec=pltpu.PrefetchScalarGridSpec(
            num_scalar_prefetch=5, grid=(Rh // tr,),
            in_specs=[pl.BlockSpec(blk, lambda i, *ids, k=k: (ids[k][0], i, 0)) for k in range(P)] + [_ANY],
            out_specs=pl.BlockSpec((None, None, tr, C), lambda i, *ids: (layer, ids[4][0], i, 0))),
        out_shape=jax.ShapeDtypeStruct(final.shape, final.dtype),
        input_output_aliases={9: 0},
        compiler_params=_cp("parallel"))(*_chip_indices(), pair, buf, buf, buf, final)


def place_own(name, w, layer):
    _, R, C = w.shape
    tr = _tile(R, max(16, (2 << 20) // (C * 2)), 16)

    def body(q, w_ref, own_ref, land_ref):
        wb = w_ref[...].astype(bf16)
        own_ref[...] = wb
        land_ref[...] = wb

    return pl.pallas_call(
        body, name=name,
        grid_spec=pltpu.PrefetchScalarGridSpec(
            num_scalar_prefetch=1, grid=(R // tr,),
            in_specs=[pl.BlockSpec((None, tr, C), lambda i, q: (layer, i, 0))],
            out_specs=[pl.BlockSpec((tr, C), lambda i, q: (i, 0)),
                       pl.BlockSpec((None, tr, C), lambda i, q: (q[0], i, 0))]),
        out_shape=[jax.ShapeDtypeStruct((R, C), bf16), jax.ShapeDtypeStruct((N_CHIPS, R, C), bf16)],
        compiler_params=_cp("parallel"))(_chip_indices()[0], w)


_ANY = pl.BlockSpec(memory_space=pl.ANY)
_CHIP_FLIPS = ((1, 0), (0, 1), (1, 1))


def _place():
    return lax.axis_index("x"), lax.axis_index("y"), lax.axis_index("c")


def _flip(v, f):
    return 1 - v if f else v


def _remote(src, dst, ssem, rsem, dev):
    return pltpu.make_async_remote_copy(src_ref=src, dst_ref=dst, send_sem=ssem, recv_sem=rsem,
                                        device_id=dev, device_id_type=MESH)


_HBM = pl.BlockSpec(memory_space=pltpu.HBM)
_SEM = pl.BlockSpec(memory_space=pltpu.SEMAPHORE)
_DATAFLOW = pltpu.SideEffectType.DATAFLOW_SIDE_EFFECTING
_TOKEN = jax.ShapeDtypeStruct((8, LANES), f32)


def _in_hbm(a):
    return pltpu.with_memory_space_constraint(a, pltpu.HBM)


def _hbm_like(a):
    return pltpu.HBM(a.shape, a.dtype)


def gather_ici_start(name, groups):
    sizes = [len(g) for g in groups]
    owns = [o for g in groups for o, _ in g]
    lands = [l for g in groups for _, l in g]
    n, ng = len(owns), len(groups)

    def body(*refs):
        own, land = refs[:n], refs[n:2 * n]
        sems = refs[2 * n:2 * n + 2 * ng]
        token = refs[-1]
        x, y, c = _place()
        q = 2 * x + y
        t = 0
        for gi, size in enumerate(sizes):
            for j in range(size):
                for k, (fx, fy) in enumerate(_CHIP_FLIPS):
                    _remote(own[t].at[c], land[t].at[q, c], sems[2 * gi].at[3 * j + k], sems[2 * gi + 1].at[3 * j + k],
                            (_flip(x, fx), _flip(y, fy), c)).start()
                t += 1
        token[...] = jnp.zeros_like(token)

    sem_shapes = [pltpu.SemaphoreType.DMA((3 * size,)) for size in sizes for _ in range(2)]
    res = pl.pallas_call(
        body, name=name,
        in_specs=[_HBM] * (2 * n),
        out_specs=[_SEM] * (2 * ng) + [_HBM] * (2 * n) + [pl.BlockSpec(memory_space=pltpu.VMEM)],
        out_shape=sem_shapes + [_hbm_like(a) for a in owns + lands] + [_TOKEN],
        input_output_aliases={i: 2 * ng + i for i in range(2 * n)},
        compiler_params=pltpu.CompilerParams(has_side_effects=_DATAFLOW),
    )(*[_in_hbm(a) for a in owns + lands])
    own_thru, land_thru = res[2 * ng:2 * ng + n], res[2 * ng + n:2 * ng + 2 * n]
    handles, t = [], 0
    for gi, size in enumerate(sizes):
        handles.append((res[2 * gi], res[2 * gi + 1], list(own_thru[t:t + size]), list(land_thru[t:t + size])))
        t += size
    return handles, res[-1]


def gather_ici_wait(name, handle, after):
    send, recv, owns, lands = handle
    n = len(owns)

    def body(*refs):
        own, land = refs[:n], refs[n:2 * n]
        send_ref, recv_ref = refs[2 * n], refs[2 * n + 1]
        x, y, c = _place()
        for j in range(n):
            for k, (fx, fy) in enumerate(_CHIP_FLIPS):
                px, py = _flip(x, fx), _flip(y, fy)
                cp = _remote(own[j].at[c], land[j].at[2 * px + py, c], send_ref.at[3 * j + k], recv_ref.at[3 * j + k],
                             (px, py, c))
                cp.wait_send()
                cp.wait_recv()

    res = pl.pallas_call(
        body, name=name,
        in_specs=[_HBM] * (2 * n) + [_SEM, _SEM, _ANY],
        out_specs=[_HBM] * (2 * n),
        out_shape=[_hbm_like(a) for a in owns + lands],
        input_output_aliases={i: i for i in range(2 * n)},
        compiler_params=pltpu.CompilerParams(has_side_effects=_DATAFLOW),
    )(*owns, *lands, send, recv, after)
    return list(res[n:])


def _split_start(name, body, arrays, n_sems):
    n = len(arrays)

    def kernel_body(*refs):
        body(refs[:n], refs[n], refs[n + 1])
        refs[-1][...] = jnp.zeros_like(refs[-1])

    res = pl.pallas_call(
        kernel_body, name=name,
        in_specs=[_HBM] * n,
        out_specs=[_SEM, _SEM] + [_HBM] * n + [pl.BlockSpec(memory_space=pltpu.VMEM)],
        out_shape=[pltpu.SemaphoreType.DMA((n_sems,)), pltpu.SemaphoreType.DMA((n_sems,))]
        + [_hbm_like(a) for a in arrays] + [_TOKEN],
        input_output_aliases={i: 2 + i for i in range(n)},
        compiler_params=pltpu.CompilerParams(has_side_effects=_DATAFLOW),
    )(*[_in_hbm(a) for a in arrays])
    return res[0], res[1], list(res[2:2 + n]), res[-1]


def _split_wait(name, body, handle, after):
    send, recv, arrays, _ = handle
    n = len(arrays)

    def kernel_body(*refs):
        body(refs[:n], refs[n], refs[n + 1])

    res = pl.pallas_call(
        kernel_body, name=name,
        in_specs=[_HBM] * n + [_SEM, _SEM, _ANY],
        out_specs=[_HBM] * n,
        out_shape=[_hbm_like(a) for a in arrays],
        input_output_aliases={i: i for i in range(n)},
        compiler_params=pltpu.CompilerParams(has_side_effects=_DATAFLOW),
    )(*arrays, send, recv, after)
    return list(res)


def _forward_copies(land, send, recv):
    x, y, c = _place()
    for t in range(len(land)):
        for k, (fx, fy) in enumerate(_CHIP_FLIPS):
            slab = land[t].at[2 * _flip(x, fx) + _flip(y, fy), c]
            yield _remote(slab, slab, send.at[3 * t + k], recv.at[3 * t + k], (x, y, 1 - c))


def forward_start(name, lands):
    def body(land, send, recv):
        for cp in _forward_copies(land, send, recv):
            cp.start()

    return _split_start(name, body, lands, 3 * len(lands))


def forward_wait(name, handle, after):
    def body(land, send, recv):
        for cp in _forward_copies(land, send, recv):
            cp.wait_send()
            cp.wait_recv()

    return _split_wait(name, body, handle, after)


def sibling_start(name, Gs):
    T = len(Gs)
    bufs = [lax.empty((G.shape[0],) + G.shape[2:], G.dtype) for G in Gs]

    def body(refs, send, recv):
        x, y, c = _place()
        for t in range(T):
            _remote(refs[t].at[:, 1 - c], refs[T + t], send.at[t], recv.at[t], (x, y, 1 - c)).start()

    return _split_start(name, body, list(Gs) + bufs, T)


def sibling_wait(name, handle, after):
    T = len(handle[2]) // 2

    def body(refs, send, recv):
        x, y, c = _place()
        for t in range(T):
            cp = _remote(refs[t].at[:, 1 - c], refs[T + t], send.at[t], recv.at[t], (x, y, 1 - c))
            cp.wait_send()
            cp.wait_recv()

    res = _split_wait(name, body, handle, after)
    return res[:T], res[T:]


def reduce_ici_start(name, Ss):
    T = len(Ss)
    lands = [lax.empty(S.shape, S.dtype) for S in Ss]

    def body(*refs):
        s, land = refs[:T], refs[T:2 * T]
        send, recv = refs[2 * T], refs[2 * T + 1]
        token = refs[-1]
        x, y, c = _place()
        q = 2 * x + y
        for t in range(T):
            for k, (fx, fy) in enumerate(_CHIP_FLIPS):
                px, py = _flip(x, fx), _flip(y, fy)
                _remote(s[t].at[2 * px + py], land[t].at[q], send.at[3 * t + k], recv.at[3 * t + k],
                        (px, py, c)).start()
        token[...] = jnp.zeros_like(token)

    res = pl.pallas_call(
        body, name=name,
        in_specs=[_HBM] * (2 * T),
        out_specs=[_SEM, _SEM] + [_HBM] * (2 * T) + [pl.BlockSpec(memory_space=pltpu.VMEM)],
        out_shape=[pltpu.SemaphoreType.DMA((3 * T,)), pltpu.SemaphoreType.DMA((3 * T,))]
        + [_hbm_like(a) for a in Ss + lands] + [_TOKEN],
        input_output_aliases={i: 2 + i for i in range(2 * T)},
        compiler_params=pltpu.CompilerParams(has_side_effects=_DATAFLOW),
    )(*[_in_hbm(a) for a in Ss + lands])
    return res[0], res[1], list(res[2:2 + T]), list(res[2 + T:2 + 2 * T]), res[-1]


def reduce_ici_wait(name, handle, after):
    send, recv, Ss, lands, _ = handle
    T = len(Ss)

    def body(*refs):
        s, land = refs[:T], refs[T:2 * T]
        send_ref, recv_ref = refs[2 * T], refs[2 * T + 1]
        x, y, c = _place()
        for t in range(T):
            for k, (fx, fy) in enumerate(_CHIP_FLIPS):
                px, py = _flip(x, fx), _flip(y, fy)
                cp = _remote(s[t].at[2 * px + py], land[t].at[2 * px + py], send_ref.at[3 * t + k], recv_ref.at[3 * t + k],
                             (px, py, c))
                cp.wait_send()
                cp.wait_recv()

    res = pl.pallas_call(
        body, name=name,
        in_specs=[_HBM] * (2 * T) + [_SEM, _SEM, _ANY],
        out_specs=[_HBM] * (2 * T),
        out_shape=[_hbm_like(a) for a in Ss + lands],
        input_output_aliases={i: i for i in range(2 * T)},
        compiler_params=pltpu.CompilerParams(has_side_effects=_DATAFLOW),
    )(*Ss, *lands, send, recv, after)
    return list(res[:T]), list(res[T:])


def rs_share(name, tots, layers):
    T = len(tots)

    def body(*refs):
        s, o = refs[:T], refs[T:2 * T]
        send, recv = refs[2 * T:]
        x, y, c = _place()
        cps = []
        for t in range(T):
            mine = o[t].at[layers[t], c]
            cp = _remote(mine, mine, send.at[t], recv.at[t], (x, y, 1 - c))
            cp.start()
            cps.append(cp)
        for t in range(T):
            other = o[t].at[layers[t], 1 - c]
            _remote(other, other, send.at[t], recv.at[t], (x, y, 1 - c)).wait_recv()
        for cp in cps:
            cp.wait_send()

    return pl.pallas_call(
        body, name=name, in_specs=[_ANY] * T, out_specs=[_ANY] * T,
        out_shape=[jax.ShapeDtypeStruct(s.shape, s.dtype) for s in tots],
        input_output_aliases={t: t for t in range(T)},
        scratch_shapes=[pltpu.SemaphoreType.DMA((T,)), pltpu.SemaphoreType.DMA((T,))],
        )(*tots)


def all_reduce_small(name, vec):
    rows = vec.shape[0]
    flips = [(fx, fy, fc) for fx in (0, 1) for fy in (0, 1) for fc in (0, 1)][1:]

    def body(v_ref, o_ref, buf, send, recv):
        x, y, c = _place()
        me = 4 * x + 2 * y + c
        buf[me] = v_ref[...]
        cps = []
        for k, (fx, fy, fc) in enumerate(flips):
            cp = _remote(buf.at[me], buf.at[me], send.at[k], recv.at[k],
                         (_flip(x, fx), _flip(y, fy), _flip(c, fc)))
            cp.start()
            cps.append(cp)
        for k, (fx, fy, fc) in enumerate(flips):
            slab = buf.at[4 * _flip(x, fx) + 2 * _flip(y, fy) + _flip(c, fc)]
            _remote(slab, slab, send.at[k], recv.at[k], (x, y, c)).wait_recv()
        for cp in cps:
            cp.wait_send()
        acc = buf[0]
        for d in range(1, N_DEV):
            acc = acc + buf[d]
        o_ref[...] = acc

    return pl.pallas_call(
        body, name=name,
        in_specs=[pl.BlockSpec(memory_space=pltpu.VMEM)], out_specs=pl.BlockSpec(memory_space=pltpu.VMEM),
        out_shape=jax.ShapeDtypeStruct((rows, LANES), f32),
        scratch_shapes=[pltpu.VMEM((N_DEV, rows, LANES), f32),
                        pltpu.SemaphoreType.DMA((N_DEV - 1,)), pltpu.SemaphoreType.DMA((N_DEV - 1,))],
        compiler_params=pltpu.CompilerParams(vmem_limit_bytes=V7X_VMEM_LIMIT))(vec)


def _all_done(arrays):
    return jnp.stack([a[(0,) * a.ndim].astype(f32) for a in arrays]).sum(keepdims=True)


def _pack(arrays):
    flat = jnp.concatenate([a.reshape(-1) for a in arrays])
    n = flat.shape[0]
    rows = -(-n // (8 * LANES)) * 8
    return jnp.pad(flat, (0, rows * LANES - n)).reshape(rows, LANES)


def _unpack(vec, shapes):
    flat = vec.reshape(-1)
    out, pos = [], 0
    for s in shapes:
        n = math.prod(s)
        out.append(flat[pos:pos + n].reshape(s))
        pos += n
    return out


def _f_first(x, g):
    return x, _rms(x, g)


def _f_mid(h, m, gp, gn):
    h1 = h + _rms(m, gp)
    return h1, _rms(h1, gn)


def _f_mid_bias(h, m, b, gp, gn):
    h1 = h + _rms(m + b, gp)
    return h1, _rms(h1, gn)


def _f_last(h, m, gp):
    return (h + _rms(m, gp),)


def _f_swiglu(gate, up):
    return (_silu(gate) * up,)


def _swiglu_tile(gate, up):
    return _silu(gate.astype(f32)) * up.astype(f32)


def _swiglu_bwd_tile(d_act, gate, up):
    _, vjp = jax.vjp(_f_swiglu, gate.astype(f32), up.astype(f32))
    return vjp((d_act,))


def _f_glu(a, g, ba, bg):
    return ((a + ba) * _sigmoid(g + bg),)


def _f_ln_silu(x, g, b):
    mu = jnp.mean(x, axis=-1, keepdims=True)
    xc = x - mu
    y = xc * lax.rsqrt(jnp.mean(xc * xc, axis=-1, keepdims=True) + LN_EPS) * g + b
    return (_silu(y),)


def _f_lower_bounds(logits):
    n = logits.shape[0]
    e = jnp.exp(logits - jnp.max(logits, axis=0, keepdims=True))
    p = e / jnp.sum(e, axis=0, keepdims=True)
    layer = lax.broadcasted_iota(jnp.int32, logits.shape, 0)
    out = -jnp.broadcast_to(p[0:1, :], logits.shape)
    for j in range(n):
        out = out + jnp.where(layer >= j, p[j:j + 1, :], 0.0)
    return (out,)


WEIGHT_NAMES = ['mix_pre_g', 'mix_post_g', 'ffn_pre_g', 'ffn_post_g', 'hgrn_lb_logits', 'even_w_in',
                'hgrn_norm_g', 'ssd_conv_w', 'ssd_conv_b', 'ssd_dt_bias', 'ssd_a_log', 'ssd_d', 'ssd_norm_g',
                'even_w_out', 'conf_w1', 'conf_b1', 'conf_dw_w', 'conf_dw_b', 'conf_ln_g', 'conf_ln_b',
                'conf_w2', 'conf_b2', 'ffn_w_gate', 'ffn_w_up', 'ffn_w_down']
BIG = ['even_w_in', 'even_w_out', 'conf_w1', 'conf_w2', 'ffn_w_gate', 'ffn_w_up', 'ffn_w_down']
SMALL_SHARDED = {'ssd_conv_w': 2, 'conf_b1': 1, 'conf_dw_w': 2, 'conf_dw_b': 1, 'conf_ln_g': 1,
                 'conf_ln_b': 1, 'conf_b2': 1}


def _train_step(x, target, w, m, v):
    S, D = x.shape[1], x.shape[2]
    x2, t2 = x[0], target[0]
    NL = w['mix_pre_g'].shape[0]
    HB = w['ssd_dt_bias'].shape[1]
    GN = B_GROUPS * B_STATE
    xw, yw, cw = _place()
    chip = 2 * xw + yw
    tm = _tile(S, 128, 8)
    row1 = lambda a, i: a[i:i + 1]

    sharded = list(SMALL_SHARDED)
    placed = []
    for n in sharded:
        ax, a = SMALL_SHARDED[n], w[n]
        full = jnp.zeros(a.shape[:ax] + (a.shape[ax] * N_CHIPS,) + a.shape[ax + 1:], f32)
        start = [0] * a.ndim
        start[ax] = chip * a.shape[ax]
        placed.append(lax.dynamic_update_slice(full, jnp.where(cw == 0, a, 0.0), start))
    whole = dict(zip(sharded, _unpack(all_reduce_small("gather_small", _pack(placed)), [p.shape for p in placed])))
    small = {n: whole.get(n, w[n]) for n in WEIGHT_NAMES if n not in BIG}

    def mixer_keys(layer):
        names = ('even_w_in', 'even_w_out') if layer % 2 == 0 else ('conf_w1', 'conf_w2')
        return [(n, layer // 2) for n in names]

    def ffn_keys(layer):
        return [(n, layer) for n in ('ffn_w_gate', 'ffn_w_up', 'ffn_w_down')]

    groups = [keys(layer) for layer in range(NL) for keys in (mixer_keys, ffn_keys)]
    halves = lambda a: a.reshape(a.shape[:-2] + (2, a.shape[-2] // 2, a.shape[-1]))
    own, land = {}, {}

    def start_groups(name, some):
        for g in some:
            for n, l in g:
                own[n, l], land[n, l] = place_own("place_own", w[n], l)
        return gather_ici_start(name, [[(halves(own[k]), halves(land[k])) for k in g] for g in some])

    first, _ = start_groups("gather_start_first", groups[:1])
    rest, rest_started = start_groups("gather_start_rest", groups[1:])
    handles = first + rest
    W, handed = {}, {}

    def fetch_begin(gi, after):
        arrived = gather_ici_wait(f"gather_wait_{gi}", handles[gi], after)
        handed[gi] = forward_start(f"forward_start_{gi}", arrived)
        return handed[gi][-1]

    def fetch_end(gi, after):
        for k, a in zip(groups[gi], forward_wait(f"forward_wait_{gi}", handed[gi], after)):
            W[k] = a.reshape((N_CHIPS, 1) + own[k].shape)

    WM = 6 * D + 2 * GN
    w_main, w_dt = {}, {}

    n_even = small['hgrn_lb_logits'].shape[0]
    (lbs,) = _stage_fwd("lower_bounds", _f_lower_bounds, [small['hgrn_lb_logits']], [], [(D, f32)], tm=n_even)
    saved = []
    h = x2
    (u,) = _stage_fwd("pre_norm", lambda a, g: (_rms(a, g),), [h],
                      [row1(small['mix_pre_g'], 0) + rest_started[0, 0]], [(D, bf16)], tm=tm)
    fetch_begin(0, u)
    for layer in range(NL):
        li = layer // 2
        r = {'h': h, 'u': u}
        fetch_end(2 * layer, u)
        if layer % 2 == 0:
            win = jnp.concatenate([W['even_w_in', li][j, 0] for j in range(N_CHIPS)], axis=-1)
            w_main[li], w_dt[li] = win[None, None, :, :WM], win[None, None, :, WM:]
            r['ymain'] = mm_nn_col("in_proj", u, w_main[li], 0)
            r['dtr'] = mm_nn_col("in_proj_dt", u, w_dt[li], 0)
            r['xact'] = conv_fwd("ssd_conv", r['ymain'], 5 * D // CONV_CH, small['ssd_conv_w'][li],
                                 row1(small['ssd_conv_b'], li), True, f32)
            o_a, r['hg_st'] = hgrn_fwd("hgrn", r['ymain'], row1(lbs, li), row1(small['hgrn_norm_g'], li), D)
            begun = fetch_begin(2 * layer + 1, o_a)[0, 0]
            o_b, r['ssd_st'] = ssd_fwd("ssd", r['xact'], r['ymain'], r['dtr'], row1(small['ssd_dt_bias'], li),
                                       row1(small['ssd_a_log'], li), row1(small['ssd_d'], li),
                                       row1(small['ssd_norm_g'], li) + begun, D)
            r['mixed'] = jnp.concatenate([o_a, o_b], axis=1)
            r['m'] = mm_nn_row("out_proj", r['mixed'], W['even_w_out', li], 0)
            mid_fn, mid_par = _f_mid, []
        else:
            r['c1'] = mm_nn_col("conf_in", u, W['conf_w1', li], 0)
            b1 = row1(small['conf_b1'], li)
            tn = _tile(D, 512)
            (r['glu'],) = _stage_fwd("conf_glu", _f_glu, [r['c1'], (r['c1'], D // tn)], [b1, (b1, D // tn)],
                                     [(D, f32)], tm=tm, tn=tn)
            r['cc'] = conv_fwd("conf_conv", r['glu'], 0, small['conf_dw_w'][li], row1(small['conf_dw_b'], li),
                               False, f32)
            begun = fetch_begin(2 * layer + 1, r['cc'])[0, 0]
            (r['c2'],) = _stage_fwd("conf_ln", _f_ln_silu, [r['cc']],
                                    [row1(small['conf_ln_g'], li) + begun, row1(small['conf_ln_b'], li)],
                                    [(D, bf16)], tm=tm)
            r['m'] = mm_nn_row("conf_out", r['c2'], W['conf_w2', li], 0)
            mid_fn, mid_par = _f_mid_bias, [row1(small['conf_b2'], li)]
        r['mid_fn'] = mid_fn
        r['mid_par'] = mid_par + [row1(small['mix_post_g'], layer), row1(small['ffn_pre_g'], layer)]
        r['h1'], r['u2'] = _stage_fwd("mid_norm", mid_fn, [h, r['m']], r['mid_par'], [(D, f32), (D, bf16)], tm=tm)
        fetch_end(2 * layer + 1, r['u2'])
        r['gate'] = mm_nn_col("ffn_gate", r['u2'], W['ffn_w_gate', layer], 0, bf16)
        begun = fetch_begin(2 * layer + 2, r['gate']) if layer + 1 < NL else None
        r['up'] = mm_nn_col("ffn_up", r['u2'], W['ffn_w_up', layer], 0, bf16, dep=begun)
        r['dn'] = mm_nn_row("ffn_down", [r['gate'], r['up']], W['ffn_w_down', layer], 0, a_fn=_swiglu_tile)
        if layer + 1 < NL:
            r['end_fn'] = _f_mid
            r['end_par'] = [row1(small['ffn_post_g'], layer), row1(small['mix_pre_g'], layer + 1)]
            h, u = _stage_fwd("end_norm", _f_mid, [r['h1'], r['dn']], r['end_par'], [(D, f32), (D, bf16)], tm=tm)
        else:
            r['end_fn'] = _f_last
            r['end_par'] = [row1(small['ffn_post_g'], layer)]
            (h,) = _stage_fwd("last_norm", _f_last, [r['h1'], r['dn']], r['end_par'], [(D, f32)], tm=tm)
        saved.append(r)

    dy, loss_local = loss_head("loss_head", h, t2, tm)
    loss = lax.psum(loss_local, ("x", "y", "c"))

    gs = {n: jnp.zeros(small[n].shape, f32) for n in small}

    def put(n, i, val):
        gs[n] = gs[n].at[i].add(val.reshape(gs[n].shape[1:]))

    final = {n: lax.empty((w[n].shape[0], 2, w[n].shape[1] // 2, w[n].shape[2]), f32) for n in BIG}
    to_sibling, to_chips = [], []

    def reduce_begin(gi, parts):
        handle = sibling_start(f"sibling_start_{gi}", [halves(p) for p in parts])
        to_sibling.append((gi, handle))
        return handle[-1][0, 0]

    def reduce_middle(after):
        gi, handle = to_sibling.pop(0)
        parts, from_sib = sibling_wait(f"sibling_wait_{gi}", handle, after)
        sums = [rs_add("reduce_add", a, b) for a, b in zip(parts, from_sib)]
        handle = reduce_ici_start(f"reduce_start_{gi}", sums)
        to_chips.append((gi, handle))
        return handle[-1][0, 0]

    def reduce_finish(after):
        gi, handle = to_chips.pop(0)
        keys = groups[gi]
        sums, lands = reduce_ici_wait(f"reduce_wait_{gi}", handle, after)
        for (n, l), s_, b_ in zip(keys, sums, lands):
            final[n] = rs_sum4("reduce_sum", s_, b_, final[n], l)
        names = [n for n, _ in keys]
        shared = rs_share("reduce_share", [final[n] for n in names], [l for _, l in keys])
        final.update(zip(names, shared))

    def reduce_step(gi, parts, newest):
        zero = reduce_begin(gi, parts)
        if to_chips:
            reduce_finish(newest)
        if len(to_sibling) > 1:
            zero = zero + reduce_middle(newest)
        return zero

    dh = dy
    du_parts = None
    started = None
    for layer in reversed(range(NL)):
        li = layer // 2
        r = saved[layer]
        cts = [[dh]] if du_parts is None else [[dh], du_parts]
        par = r['end_par'] if started is None else [r['end_par'][0] + started] + r['end_par'][1:]
        (dh1, d_dn), pg = _stage_bwd("end_norm_bwd", r['end_fn'], [r['h1'], r['dn']], par, cts,
                                     [f32, bf16], tm=tm)
        put('ffn_post_g', layer, pg[0])
        if du_parts is not None:
            put('mix_pre_g', layer + 1, pg[1])
        d_gate, d_up = mm_nt_row("ffn_down_dx", d_dn, W['ffn_w_down', layer], 0, bf16,
                                 tail=(_swiglu_bwd_tile, [r['gate'], r['up']]), n_out=2)
        g_down = mm_tn_row("ffn_down_dw", [r['gate'], r['up']], d_dn, N_CHIPS, a_fn=_swiglu_tile)
        du_a = mm_nt_col("ffn_gate_dx", d_gate, W['ffn_w_gate', layer], 0)
        du_b = mm_nt_col("ffn_up_dx", d_up, W['ffn_w_up', layer], 0)
        g_gate = mm_tn_col("ffn_gate_dw", r['u2'], d_gate, N_CHIPS)
        g_up = mm_tn_col("ffn_up_dw", r['u2'], d_up, N_CHIPS)
        started = reduce_step(2 * layer + 1, [g_gate, g_up, g_down], g_up)
        par = r['mid_par'][:-1] + [r['mid_par'][-1] + started]
        (dh, dm), pg = _stage_bwd("mid_norm_bwd", r['mid_fn'], [r['h'], r['m']], par,
                                  [[dh1], [du_a, du_b]], [f32, bf16], tm=tm)
        put('mix_post_g', layer, pg[-2])
        put('ffn_pre_g', layer, pg[-1])
        if layer % 2 == 0:
            d_mixed = mm_nt_row("out_proj_dx", dm, W['even_w_out', li], 0)
            g_out = mm_tn_row("out_proj_dw", r['mixed'], dm, N_CHIPS)
            dxs, dbm, dcm, dz, ddt, ddtb, dal, dds, dbn = ssd_bwd(
                "ssd_bwd", r['xact'], r['ymain'], r['dtr'], row1(small['ssd_dt_bias'], li),
                row1(small['ssd_a_log'], li), row1(small['ssd_d'], li), row1(small['ssd_norm_g'], li),
                r['ssd_st'], d_mixed, D)
            put('ssd_dt_bias', li, ddtb)
            put('ssd_a_log', li, dal)
            put('ssd_d', li, dds)
            put('ssd_norm_g', li, dbn)
            d_xact = jnp.concatenate([dxs, dbm, dcm], axis=1)
            d_xbc, dcw, dcb = conv_bwd("ssd_conv_bwd", r['ymain'], 5 * D // CONV_CH, small['ssd_conv_w'][li],
                                       row1(small['ssd_conv_b'], li), d_xact, True, bf16)
            put('ssd_conv_w', li, dcw)
            put('ssd_conv_b', li, dcb)
            dq, df, dv, dg, dlb, dan = hgrn_bwd("hgrn_bwd", r['ymain'], row1(lbs, li), row1(small['hgrn_norm_g'], li),
                                               r['hg_st'], d_mixed, D)
            put('hgrn_norm_g', li, dan)
            r['dlb'] = dlb
            d_main = jnp.concatenate([dq, df, dv, dg, dz, d_xbc], axis=1)
            du_parts = [mm_nt_col("in_proj_dx", d_main, w_main[li], 0),
                        mm_nt_col("in_proj_dt_dx", ddt, w_dt[li], 0)]
            g_in = jnp.concatenate([mm_tn_col("in_proj_dw", r['u'], d_main, 1)[0],
                                    mm_tn_col("in_proj_dt_dw", r['u'], ddt, 1)[0]], axis=-1)
            CI = w['even_w_in'].shape[2]
            g_in = jnp.stack([g_in[:, j * CI:(j + 1) * CI] for j in range(N_CHIPS)])
            started = reduce_step(2 * layer, [g_in, g_out], g_in)
        else:
            put('conf_b2', li, pg[0])
            d_c2 = mm_nt_row("conf_out_dx", dm, W['conf_w2', li], 0)
            g_w2 = mm_tn_row("conf_out_dw", r['c2'], dm, N_CHIPS)
            (d_cc,), pl_ = _stage_bwd("conf_ln_bwd", _f_ln_silu, [r['cc']],
                                      [row1(small['conf_ln_g'], li), row1(small['conf_ln_b'], li)],
                                      [[d_c2]], [f32], tm=tm)
            put('conf_ln_g', li, pl_[0])
            put('conf_ln_b', li, pl_[1])
            d_glu, ddw, ddb = conv_bwd("conf_conv_bwd", r['glu'], 0, small['conf_dw_w'][li],
                                       row1(small['conf_dw_b'], li), d_cc, False, f32)
            put('conf_dw_w', li, ddw)
            put('conf_dw_b', li, ddb)
            b1 = row1(small['conf_b1'], li)
            tn = _tile(D, 512)
            (da, dg_), pb = _stage_bwd("conf_glu_bwd", _f_glu, [r['c1'], (r['c1'], D // tn)], [b1, (b1, D // tn)],
                                       [[d_glu]], [bf16, bf16], tm=tm, tn=tn)
            put('conf_b1', li, jnp.concatenate([pb[0], pb[1]], axis=1))
            d_c1 = jnp.concatenate([da, dg_], axis=1)
            du_parts = [mm_nt_col("conf_in_dx", d_c1, W['conf_w1', li], 0)]
            g_w1 = mm_tn_col("conf_in_dw", r['u'], d_c1, N_CHIPS)
            started = reduce_step(2 * layer, [g_w1, g_w2], g_w1)
    started = started + reduce_middle(du_parts[0])
    (grad_x2,), pg = _stage_bwd("pre_norm_bwd", _f_first, [x2], [row1(small['mix_pre_g'], 0) + started],
                                [[dh], du_parts], [f32], tm=tm)
    put('mix_pre_g', 0, pg[0])
    dlbs = jnp.concatenate([saved[2 * i]['dlb'] for i in range(n_even)], axis=0)
    (dlogits,), _ = _stage_bwd("lower_bounds_bwd", _f_lower_bounds, [small['hgrn_lb_logits']], [],
                               [[dlbs]], [f32], tm=n_even)
    gs['hgrn_lb_logits'] = dlogits

    names_s = [n for n in WEIGHT_NAMES if n not in BIG]
    summed = _unpack(all_reduce_small("reduce_small", _pack([gs[n] for n in names_s])),
                     [gs[n].shape for n in names_s])
    grads = {}
    for n, a in zip(names_s, summed):
        if n in SMALL_SHARDED:
            ax = SMALL_SHARDED[n]
            size = w[n].shape[ax]
            start = [0] * a.ndim
            start[ax] = chip * size
            a = lax.dynamic_slice(a, start, a.shape[:ax] + (size,) + a.shape[ax + 1:])
        grads[n] = a

    delta, new_m, new_v = {}, {}, {}

    def update(names):
        for n in names:
            if n in BIG:
                grads[n] = final[n].reshape(w[n].shape)
            delta[n], new_m[n], new_v[n] = adamw("adamw", w[n], grads[n], m[n], v[n])
        return _all_done([delta[n] for n in names])

    second_last = [n for n, _ in groups[to_chips[0][0]]]
    last = [n for n, _ in groups[to_chips[1][0]]]
    done = update([n for n in WEIGHT_NAMES if n not in last + second_last])
    reduce_finish(done)
    done = update(second_last)
    reduce_finish(done)
    update(last)
    return (loss, grad_x2[None], *[grads[n] for n in WEIGHT_NAMES], *[delta[n] for n in WEIGHT_NAMES],
            *[new_m[n] for n in WEIGHT_NAMES], *[new_v[n] for n in WEIGHT_NAMES])


def kernel(x, mix_pre_g, mix_post_g, ffn_pre_g, ffn_post_g, hgrn_lb_logits, even_w_in, hgrn_norm_g, ssd_conv_w, ssd_conv_b, ssd_dt_bias, ssd_a_log, ssd_d, ssd_norm_g, even_w_out, conf_w1, conf_b1, conf_dw_w, conf_dw_b, conf_ln_g, conf_ln_b, conf_w2, conf_b2, ffn_w_gate, ffn_w_up, ffn_w_down, loss_target, m_mix_pre_g, m_mix_post_g, m_ffn_pre_g, m_ffn_post_g, m_hgrn_lb_logits, m_even_w_in, m_hgrn_norm_g, m_ssd_conv_w, m_ssd_conv_b, m_ssd_dt_bias, m_ssd_a_log, m_ssd_d, m_ssd_norm_g, m_even_w_out, m_conf_w1, m_conf_b1, m_conf_dw_w, m_conf_dw_b, m_conf_ln_g, m_conf_ln_b, m_conf_w2, m_conf_b2, m_ffn_w_gate, m_ffn_w_up, m_ffn_w_down, v_mix_pre_g, v_mix_post_g, v_ffn_pre_g, v_ffn_post_g, v_hgrn_lb_logits, v_even_w_in, v_hgrn_norm_g, v_ssd_conv_w, v_ssd_conv_b, v_ssd_dt_bias, v_ssd_a_log, v_ssd_d, v_ssd_norm_g, v_even_w_out, v_conf_w1, v_conf_b1, v_conf_dw_w, v_conf_dw_b, v_conf_ln_g, v_conf_ln_b, v_conf_w2, v_conf_b2, v_ffn_w_gate, v_ffn_w_up, v_ffn_w_down):
    args = locals()
    w = {n: args[n] for n in WEIGHT_NAMES}
    m = {n: args["m_" + n] for n in WEIGHT_NAMES}
    v = {n: args["v_" + n] for n in WEIGHT_NAMES}
    return _train_step(x, loss_target, w, m, v)
```

```python
import functools
import math

import jax
import jax.numpy as jnp
from jax import lax
from jax.experimental import pallas as pl
from jax.experimental.pallas import tpu as pltpu

f32 = jnp.float32
bf16 = jnp.bfloat16
MESH = pl.DeviceIdType.MESH
HI = lax.Precision.HIGHEST

A_HEAD = 128
A_CHUNK = 64
A_SUB = 8
A_REF = 4
A_EXP_CAP = 60.0
A_HEADS_PER_STEP = 8
A_F_MIN = 1e-6
B_HEAD = 64
B_GROUPS = 4
B_STATE = 128
B_CONV = 4
B_CHUNK = 128
C_KERNEL = 31
RMS_EPS = 1e-6
LN_EPS = 1e-5
ADAM_LR = 0.001
ADAM_B1 = 0.9
ADAM_B2 = 0.999
ADAM_EPS = 1e-08
ADAM_WD = 0.01
ADAM_STEP = 10

N_CHIPS = 4
N_DEV = 8
V7X_VMEM_LIMIT = 56 * 1024 * 1024
LANES = 128
CONV_PAD = 32
CONV_ROWS = 128
CONV_CH = 256

NN = (((1,), (0,)), ((), ()))
NT = (((1,), (1,)), ((), ()))
TN = (((0,), (0,)), ((), ()))


def _tile(n, cap, unit=LANES):
    best = None
    for t in range(unit, min(n, cap) + 1, unit):
        if n % t == 0:
            best = t
    return n if best is None else best


def _cp(*sem):
    return pltpu.CompilerParams(dimension_semantics=sem, vmem_limit_bytes=V7X_VMEM_LIMIT)


def _sigmoid(x):
    return jax.nn.sigmoid(x)


def _silu(x):
    return x * jax.nn.sigmoid(x)


def _rms(x, g):
    return x * lax.rsqrt(jnp.mean(x * x, axis=-1, keepdims=True) + RMS_EPS) * g


def _pair(a):
    return a if isinstance(a, tuple) else (a, 0)


def _stage(name, fn, rows, params, outs, par_outs=(), *, tm, tn=None):
    rows = [_pair(r) for r in rows]
    params = [_pair(p) for p in params]
    S = rows[0][0].shape[0]
    n_in, n_o = len(rows) + len(params), len(outs)
    if tn is None:
        grid = (S // tm,)
        in_specs = [pl.BlockSpec((tm, a.shape[1]), lambda i: (i, 0)) for a, _ in rows]
        in_specs += [pl.BlockSpec(a.shape, lambda i: (0, 0)) for a, _ in params]
        out_specs = [pl.BlockSpec((tm, w), lambda i: (i, 0)) for w, _ in outs]
        out_specs += [pl.BlockSpec((k, w), lambda i: (0, 0)) for k, w in par_outs]
        row_axis = 0
        sem = ("arbitrary",) if par_outs else ("parallel",)
    else:
        grid = (outs[0][0] // tn, S // tm)
        in_specs = [pl.BlockSpec((tm, tn), lambda j, i, o=o: (i, j + o)) for _, o in rows]
        in_specs += [pl.BlockSpec((a.shape[0], tn), lambda j, i, o=o: (0, j + o)) for a, o in params]
        out_specs = [pl.BlockSpec((tm, tn), lambda j, i: (i, j)) for _ in outs]
        out_specs += [pl.BlockSpec((k, tn), lambda j, i: (0, j)) for k, _ in par_outs]
        row_axis = 1
        sem = ("parallel", "arbitrary") if par_outs else ("parallel", "parallel")
    out_shape = [jax.ShapeDtypeStruct((S, w), d) for w, d in outs]
    out_shape += [jax.ShapeDtypeStruct((k, w), f32) for k, w in par_outs]

    def body(*refs):
        res = fn(*[r[...] for r in refs[:n_in]])
        for r, v in zip(refs[n_in:n_in + n_o], res[:n_o]):
            r[...] = v.astype(r.dtype)
        if par_outs:
            acc_refs = refs[n_in + n_o:]

            @pl.when(pl.program_id(row_axis) == 0)
            def _():
                for r in acc_refs:
                    r[...] = jnp.zeros_like(r)

            for r, v in zip(acc_refs, res[n_o:]):
                r[...] += v

    return pl.pallas_call(
        body, name=name, grid=grid, in_specs=in_specs, out_specs=out_specs, out_shape=out_shape,
        compiler_params=_cp(*sem))(*[a for a, _ in rows], *[a for a, _ in params])


def _stage_fwd(name, fn, rows, params, outs, *, tm, tn=None):
    n_r = len(rows)

    def ffn(*t):
        return fn(*[v.astype(f32) for v in t[:n_r]], *t[n_r:])

    return _stage(name, ffn, rows, params, outs, tm=tm, tn=tn)


def _stage_bwd(name, fn, rows, params, cts, drow, *, tm, tn=None):
    rows = [_pair(r) for r in rows]
    params = [_pair(p) for p in params]
    n_r, n_p = len(rows), len(params)
    flat_ct = [_pair(c) for group in cts for c in group]
    counts = [len(group) for group in cts]
    need = [i for i, d in enumerate(drow) if d is not None]

    def bfn(*t):
        r = [v.astype(f32) for v in t[:n_r]]
        c = t[n_r:n_r + len(flat_ct)]
        p = list(t[n_r + len(flat_ct):])
        res, vjp = jax.vjp(fn, *r, *p)
        ct, pos = [], 0
        for o, k in zip(res, counts):
            s = c[pos].astype(f32)
            for e in range(1, k):
                s = s + c[pos + e].astype(f32)
            pos += k
            ct.append(s.astype(o.dtype))
        g = vjp(tuple(ct))
        return tuple(g[i] for i in need) + tuple(g[n_r:])

    if tn is None:
        outs = [(rows[i][0].shape[1], drow[i]) for i in need]
        par_outs = [p.shape for p, _ in params]
    else:
        w_all = flat_ct[0][0].shape[1]
        outs = [(w_all, drow[i]) for i in need]
        par_outs = [(p.shape[0], w_all) for p, _ in params]
    res = _stage(name, bfn, rows + flat_ct, params, outs, par_outs, tm=tm, tn=tn)
    return res[:len(need)], res[len(need):]


def _mm(name, dims, a, b, grid, a_spec, b_spec, o_spec, out_shape, acc_shape, a_fn=None, tail=None, dep=None):
    nk = grid[2]
    a_list = list(a) if isinstance(a, (list, tuple)) else [a]
    na = len(a_list)
    t_fn, t_arrays = tail if tail is not None else (None, [])
    ne = len(t_arrays)
    deps = [] if dep is None else [dep]
    multi = isinstance(out_shape, (list, tuple))

    n_out = len(out_shape) if multi else 1

    def body(*refs):
        a_refs, b_ref, t_refs = refs[:na], refs[na], refs[na + 1:na + 1 + ne]
        first_out = na + 1 + ne + len(deps)
        o_refs = refs[first_out:first_out + n_out]
        k = pl.program_id(2)

        def prod():
            lhs = a_refs[0][...] if a_fn is None else a_fn(*[r[...] for r in a_refs])
            return lax.dot_general(lhs.astype(bf16), b_ref[...].astype(bf16), dims, preferred_element_type=f32)

        def finish(total):
            res = (total,) if t_fn is None else t_fn(total, *[r[...] for r in t_refs])
            for r, val in zip(o_refs, res):
                r[...] = val.astype(r.dtype)

        if nk == 1:
            finish(prod())
        else:
            acc = refs[-1]

            @pl.when(k == 0)
            def _():
                acc[...] = prod()

            if nk > 2:
                @pl.when(jnp.logical_and(k > 0, k < nk - 1))
                def _():
                    acc[...] += prod()

            @pl.when(k == nk - 1)
            def _():
                finish(acc[...] + prod())

    return pl.pallas_call(
        body, name=name, grid=grid,
        in_specs=[a_spec] * na + [b_spec] + [o_spec] * ne + [pl.BlockSpec(memory_space=pl.ANY)] * len(deps),
        out_specs=[o_spec] * len(out_shape) if multi else o_spec, out_shape=out_shape,
        scratch_shapes=[pltpu.VMEM(acc_shape, f32)] if nk > 1 else [],
        compiler_params=_cp("parallel", "parallel", "arbitrary"))(*a_list, b, *t_arrays, *deps)


def _mm_tiles(S):
    return _tile(S, 512)


def mm_nn_col(name, a, W, li, out_dtype=f32, dep=None):
    P, _, K, C = W.shape
    S = a.shape[0]
    tm, tn, tk = _mm_tiles(S), _tile(C, 1536), _tile(K, 2048)
    nc = C // tn
    return _mm(name, NN, a, W, (S // tm, P * nc, K // tk),
               pl.BlockSpec((tm, tk), lambda i, j, k: (i, k)),
               pl.BlockSpec((None, None, tk, tn), lambda i, j, k: (j // nc, li, k, j % nc)),
               pl.BlockSpec((tm, tn), lambda i, j, k: (i, j)),
               jax.ShapeDtypeStruct((S, P * C), out_dtype), (tm, tn), dep=dep)


def mm_nn_row(name, a, W, li, out_dtype=f32, a_fn=None):
    P, _, R, N = W.shape
    S = (a[0] if a_fn is not None else a).shape[0]
    tm, tn, tk = _mm_tiles(S), _tile(N, 1024), _tile(R, 2048)
    nr = R // tk
    return _mm(name, NN, a, W, (S // tm, N // tn, P * nr),
               pl.BlockSpec((tm, tk), lambda i, j, k: (i, k)),
               pl.BlockSpec((None, None, tk, tn), lambda i, j, k: (k // nr, li, k % nr, j)),
               pl.BlockSpec((tm, tn), lambda i, j, k: (i, j)),
               jax.ShapeDtypeStruct((S, N), out_dtype), (tm, tn), a_fn=a_fn)


def mm_nt_col(name, dy, W, li, out_dtype=f32):
    P, _, K, C = W.shape
    S = dy.shape[0]
    tm, tn, tk = _mm_tiles(S), _tile(K, 1024), _tile(C, 2048)
    nc = C // tk
    return _mm(name, NT, dy, W, (S // tm, K // tn, P * nc),
               pl.BlockSpec((tm, tk), lambda i, j, k: (i, k)),
               pl.BlockSpec((None, None, tn, tk), lambda i, j, k: (k // nc, li, j, k % nc)),
               pl.BlockSpec((tm, tn), lambda i, j, k: (i, j)),
               jax.ShapeDtypeStruct((S, K), out_dtype), (tm, tn))


def mm_nt_row(name, dy, W, li, out_dtype=f32, tail=None, n_out=None):
    P, _, R, N = W.shape
    S = dy.shape[0]
    tm, tn, tk = _mm_tiles(S), _tile(R, 1536), _tile(N, 2048)
    nr = R // tn
    out = jax.ShapeDtypeStruct((S, P * R), out_dtype)
    return _mm(name, NT, dy, W, (S // tm, P * nr, N // tk),
               pl.BlockSpec((tm, tk), lambda i, j, k: (i, k)),
               pl.BlockSpec((None, None, tn, tk), lambda i, j, k: (j // nr, li, j % nr, k)),
               pl.BlockSpec((tm, tn), lambda i, j, k: (i, j)),
               out if n_out is None else [out] * n_out, (tm, tn), tail=tail)


def mm_tn_col(name, a, dy, P):
    S, K = a.shape
    C = dy.shape[1] // P
    tm, tn, tk = _tile(K, 512), _tile(C, 1536), _tile(S, 2048)
    nc = C // tn
    return _mm(name, TN, a, dy, (K // tm, P * nc, S // tk),
               pl.BlockSpec((tk, tm), lambda i, j, k: (k, i)),
               pl.BlockSpec((tk, tn), lambda i, j, k: (k, j)),
               pl.BlockSpec((None, tm, tn), lambda i, j, k: (j // nc, i, j % nc)),
               jax.ShapeDtypeStruct((P, K, C), bf16), (tm, tn))


def mm_tn_row(name, a, dy, P, a_fn=None):
    S, N = dy.shape
    R = (a[0] if a_fn is not None else a).shape[1] // P
    tm, tn, tk = _tile(R, 1536), _tile(N, 1024), _tile(S, 2048 if a_fn is None else 1024)
    nr = R // tm
    return _mm(name, TN, a, dy, (P * nr, N // tn, S // tk),
               pl.BlockSpec((tk, tm), lambda i, j, k: (k, i)),
               pl.BlockSpec((tk, tn), lambda i, j, k: (k, j)),
               pl.BlockSpec((None, tm, tn), lambda i, j, k: (i // nr, i % nr, j)),
               jax.ShapeDtypeStruct((P, R, N), bf16), (tm, tn), a_fn=a_fn)


def _hgrn_chunk(st, q, fp, v, gt, lb, an):
    C = q.shape[0]
    sig = _sigmoid(fp)
    f = lb + (1.0 - lb) * sig
    kk = (1.0 - lb) * (1.0 - sig)
    g = jnp.log(jnp.maximum(f, A_F_MIN))
    qs = _silu(q)
    row = lax.broadcasted_iota(jnp.int32, (C, C), 0)
    col = lax.broadcasted_iota(jnp.int32, (C, C), 1)
    tri = (col <= row).astype(f32)
    b = jnp.dot(tri, g, precision=HI, preferred_element_type=f32)
    o_inter = lax.dot_general((qs * jnp.exp(b)).astype(bf16), st.astype(bf16), NT,
                              preferred_element_type=f32)
    T, NB = A_SUB, C // A_SUB
    refs = [b[i * T + A_REF:i * T + A_REF + 1, :] for i in range(NB)]
    ref_q = jnp.concatenate([jnp.broadcast_to(r, (T, A_HEAD)) for r in refs], axis=0)
    ref_k = jnp.concatenate([jnp.broadcast_to(r, (C, A_HEAD)) for r in refs], axis=0)
    q_t = qs * jnp.exp(b - ref_q)
    k_t = jnp.concatenate([kk] * NB, axis=0) * jnp.exp(
        jnp.minimum(ref_k - jnp.concatenate([b] * NB, axis=0), A_EXP_CAP))
    s = lax.dot_general(q_t.astype(bf16), k_t.astype(bf16), NT, preferred_element_type=f32)
    trow = lax.broadcasted_iota(jnp.int32, (C, NB * C), 0)
    scol = lax.broadcasted_iota(jnp.int32, (C, NB * C), 1)
    keep = jnp.logical_and(scol // C == trow // T, scol % C <= trow)
    s = jnp.where(keep, s, 0.0)
    o_intra = jnp.dot(s.astype(bf16), jnp.concatenate([v.astype(bf16)] * NB, axis=0),
                      preferred_element_type=f32)
    bl = b[C - 1:C, :]
    kd = kk * jnp.exp(bl - b)
    st_new = st * jnp.exp(bl) + lax.dot_general(v.astype(bf16), kd.astype(bf16), TN,
                                                preferred_element_type=f32)
    o = o_inter + o_intra
    y = o * lax.rsqrt(jnp.mean(o * o, axis=-1, keepdims=True) + RMS_EPS) * an * _silu(gt)
    return st_new, y


def _hgrn_heads_per_step(HA):
    return A_HEADS_PER_STEP if HA % A_HEADS_PER_STEP == 0 else 1


def _hgrn_in_specs(HA, HP, cidx):
    W = HP * A_HEAD
    specs = [pl.BlockSpec((A_CHUNK, W), lambda h, c, s=s: (cidx(c), s * (HA // HP) + h)) for s in range(4)]
    specs += [pl.BlockSpec((1, W), lambda h, c: (0, h))] * 2
    return specs


def _head(ref, j):
    return ref[:, j * A_HEAD:(j + 1) * A_HEAD]


def hgrn_fwd(name, ymain, lb, an, D):
    S = ymain.shape[0]
    HA, nc = D // A_HEAD, S // A_CHUNK
    HP = _hgrn_heads_per_step(HA)
    W = HP * A_HEAD

    def body(q, fp, v, gt, lb_ref, an_ref, o_ref, sv_ref, st):
        @pl.when(pl.program_id(1) == 0)
        def _():
            st[...] = jnp.zeros_like(st)

        sv_ref[...] = st[...]
        for j in range(HP):
            st_new, y = _hgrn_chunk(st[j], _head(q, j), _head(fp, j), _head(v, j), _head(gt, j),
                                    _head(lb_ref, j), _head(an_ref, j))
            st[j] = st_new
            o_ref[:, j * A_HEAD:(j + 1) * A_HEAD] = y.astype(o_ref.dtype)

    return pl.pallas_call(
        body, name=name, grid=(HA // HP, nc),
        in_specs=_hgrn_in_specs(HA, HP, lambda c: c),
        out_specs=[pl.BlockSpec((A_CHUNK, W), lambda h, c: (c, h)),
                   pl.BlockSpec((HP, None, A_HEAD, A_HEAD), lambda h, c: (h, c, 0, 0))],
        out_shape=[jax.ShapeDtypeStruct((S, D), bf16),
                   jax.ShapeDtypeStruct((HA, nc, A_HEAD, A_HEAD), f32)],
        scratch_shapes=[pltpu.VMEM((HP, A_HEAD, A_HEAD), f32)],
        compiler_params=_cp("parallel", "arbitrary"))(ymain, ymain, ymain, ymain, lb, an)


def hgrn_bwd(name, ymain, lb, an, saved, dmixed, D):
    S = ymain.shape[0]
    HA, nc = D // A_HEAD, S // A_CHUNK
    HP = _hgrn_heads_per_step(HA)
    W = HP * A_HEAD
    rev = lambda c: nc - 1 - c

    def body(q, fp, v, gt, lb_ref, an_ref, sv_ref, do_ref, dq, df, dv, dg, dlb, dan, dst):
        @pl.when(pl.program_id(1) == 0)
        def _():
            dst[...] = jnp.zeros_like(dst)
            dlb[...] = jnp.zeros_like(dlb)
            dan[...] = jnp.zeros_like(dan)

        for j in range(HP):
            cols = slice(j * A_HEAD, (j + 1) * A_HEAD)
            _, vjp = jax.vjp(_hgrn_chunk, sv_ref[j], _head(q, j), _head(fp, j), _head(v, j), _head(gt, j),
                             _head(lb_ref, j), _head(an_ref, j))
            g = vjp((dst[j], _head(do_ref, j).astype(f32)))
            dst[j] = g[0]
            for r, x in zip((dq, df, dv, dg), g[1:5]):
                r[:, cols] = x.astype(r.dtype)
            dlb[:, cols] += g[5]
            dan[:, cols] += g[6]

    blk = pl.BlockSpec((A_CHUNK, W), lambda h, c: (rev(c), h))
    vec = pl.BlockSpec((1, W), lambda h, c: (0, h))
    return pl.pallas_call(
        body, name=name, grid=(HA // HP, nc),
        in_specs=_hgrn_in_specs(HA, HP, rev) + [
            pl.BlockSpec((HP, None, A_HEAD, A_HEAD), lambda h, c: (h, rev(c), 0, 0)), blk],
        out_specs=[blk] * 4 + [vec] * 2,
        out_shape=[jax.ShapeDtypeStruct((S, D), bf16)] * 4 + [jax.ShapeDtypeStruct((1, D), f32)] * 2,
        scratch_shapes=[pltpu.VMEM((HP, A_HEAD, A_HEAD), f32)],
        compiler_params=_cp("parallel", "arbitrary"))(ymain, ymain, ymain, ymain, lb, an, saved, dmixed)


def _ssd_chunk(hp, xs, bm, cm, z, dtr, dtb, alog, dsk, bn, g, R):
    L, GW = xs.shape
    HB = dtr.shape[1]
    R8 = max(R, 8)
    dt = jax.nn.softplus(dtr + dtb)
    a = -jnp.exp(alog)
    row = lax.broadcasted_iota(jnp.int32, (L, L), 0)
    col = lax.broadcasted_iota(jnp.int32, (L, L), 1)
    causal = col <= row
    cs = jnp.dot(causal.astype(f32), dt * a, precision=HI, preferred_element_type=f32)
    eh = lax.broadcasted_iota(jnp.int32, (HB, GW), 0)
    ec = lax.broadcasted_iota(jnp.int32, (HB, GW), 1)
    spread = (eh == g * R + ec // B_HEAD).astype(f32)
    sh = lax.broadcasted_iota(jnp.int32, (R8, HB), 1)
    sr = lax.broadcasted_iota(jnp.int32, (R8, HB), 0)
    pick_t = jnp.logical_and(sh == g * R + sr, sr < R).astype(f32)
    dtf = jnp.dot(dt, spread, precision=HI, preferred_element_type=f32)
    csf = jnp.dot(cs, spread, precision=HI, preferred_element_type=f32)
    dsf = jnp.dot(jnp.broadcast_to(dsk, (8, HB)), spread, precision=HI, preferred_element_type=f32)[0:1, :]
    cs_col = lax.dot_general(cs, pick_t, NT, precision=HI, preferred_element_type=f32)
    cs_row = lax.dot_general(pick_t, cs, NT, precision=HI, preferred_element_type=f32)
    xdt = xs * dtf
    cb = lax.dot_general(cm.astype(bf16), bm.astype(bf16), NT, preferred_element_type=f32)
    lane_head = lax.broadcasted_iota(jnp.int32, (1, GW), 1) // B_HEAD
    y = jnp.zeros((L, GW), f32)
    for r in range(R):
        seg = cs_col[:, r:r + 1] - cs_row[r:r + 1, :]
        dec = jnp.where(causal, jnp.exp(jnp.where(causal, seg, 0.0)), 0.0)
        xm = jnp.where(lane_head == r, xdt, 0.0)
        y = y + jnp.dot((cb * dec).astype(bf16), xm.astype(bf16), preferred_element_type=f32)
    csl = csf[L - 1:L, :]
    dte = jnp.exp(csl - csf)
    states = lax.dot_general(bm.astype(bf16), (xdt * dte).astype(bf16), TN, preferred_element_type=f32)
    y_off = jnp.dot(cm.astype(bf16), hp.astype(bf16), preferred_element_type=f32) * jnp.exp(csf)
    hn = hp * jnp.exp(csl) + states
    gated = (y + y_off + dsf * xs) * _silu(z)
    out = gated * lax.rsqrt(jnp.mean(gated * gated, axis=-1, keepdims=True) + RMS_EPS) * bn
    return hn, out


def _ssd_in_specs(D, HB, cidx):
    L, GW, N = B_CHUNK, D // B_GROUPS, B_STATE
    zoff, boff = 4 * D // GW, D // N
    return [
        pl.BlockSpec((L, GW), lambda c, g: (cidx(c), g)),
        pl.BlockSpec((L, N), lambda c, g: (cidx(c), boff + g)),
        pl.BlockSpec((L, N), lambda c, g: (cidx(c), boff + B_GROUPS + g)),
        pl.BlockSpec((L, GW), lambda c, g: (cidx(c), zoff + g)),
        pl.BlockSpec((L, HB), lambda c, g: (cidx(c), 0)),
        pl.BlockSpec((1, HB), lambda c, g: (0, 0)),
        pl.BlockSpec((1, HB), lambda c, g: (0, 0)),
        pl.BlockSpec((1, HB), lambda c, g: (0, 0)),
        pl.BlockSpec((1, GW), lambda c, g: (0, g)),
    ]


def ssd_fwd(name, xact, ymain, dtr, dtb, alog, dsk, bn, D):
    S, HB = dtr.shape
    nc, GW, R = S // B_CHUNK, D // B_GROUPS, HB // B_GROUPS

    def body(xs, bm, cm, z, dt_ref, dtb_ref, al_ref, ds_ref, bn_ref, o_ref, sv_ref, hs):
        g = pl.program_id(1)

        @pl.when(pl.program_id(0) == 0)
        def _():
            hs[g] = jnp.zeros((B_STATE, GW), f32)

        hp = hs[g]
        sv_ref[...] = hp
        hn, out = _ssd_chunk(hp, xs[...], bm[...], cm[...], z[...], dt_ref[...], dtb_ref[...], al_ref[...],
                             ds_ref[...], bn_ref[...], g, R)
        hs[g] = hn
        o_ref[...] = out.astype(o_ref.dtype)

    return pl.pallas_call(
        body, name=name, grid=(nc, B_GROUPS),
        in_specs=_ssd_in_specs(D, HB, lambda c: c),
        out_specs=[pl.BlockSpec((B_CHUNK, GW), lambda c, g: (c, g)),
                   pl.BlockSpec((None, None, B_STATE, GW), lambda c, g: (c, g, 0, 0))],
        out_shape=[jax.ShapeDtypeStruct((S, D), bf16),
                   jax.ShapeDtypeStruct((nc, B_GROUPS, B_STATE, GW), f32)],
        scratch_shapes=[pltpu.VMEM((B_GROUPS, B_STATE, GW), f32)],
        compiler_params=_cp("arbitrary", "arbitrary"))(xact, xact, xact, ymain, dtr, dtb, alog, dsk, bn)


def ssd_bwd(name, xact, ymain, dtr, dtb, alog, dsk, bn, saved, dmixed, D):
    S, HB = dtr.shape
    nc, GW, R = S // B_CHUNK, D // B_GROUPS, HB // B_GROUPS
    rev = lambda c: nc - 1 - c
    ooff = D // GW

    def body(xs, bm, cm, z, dt_ref, dtb_ref, al_ref, ds_ref, bn_ref, sv_ref, do_ref,
             dxs, dbm, dcm, dz, ddt, ddtb, dal, dds, dbn, dhs):
        c, g = pl.program_id(0), pl.program_id(1)

        @pl.when(c == 0)
        def _():
            dhs[g] = jnp.zeros((B_STATE, GW), f32)
            dbn[g] = jnp.zeros((1, GW), f32)

        @pl.when(jnp.logical_and(c == 0, g == 0))
        def _():
            ddtb[...] = jnp.zeros_like(ddtb)
            dal[...] = jnp.zeros_like(dal)
            dds[...] = jnp.zeros_like(dds)

        @pl.when(g == 0)
        def _():
            ddt[...] = jnp.zeros_like(ddt)

        fn = functools.partial(_ssd_chunk, g=g, R=R)
        _, vjp = jax.vjp(fn, sv_ref[...], xs[...], bm[...], cm[...], z[...], dt_ref[...], dtb_ref[...],
                         al_ref[...], ds_ref[...], bn_ref[...])
        gr = vjp((dhs[g], do_ref[...].astype(f32)))
        dhs[g] = gr[0]
        dxs[...] = gr[1]
        dbm[...] = gr[2]
        dcm[...] = gr[3]
        dz[...] = gr[4].astype(dz.dtype)
        ddt[...] += gr[5]
        ddtb[...] += gr[6]
        dal[...] += gr[7]
        dds[...] += gr[8]
        dbn[g] += gr[9]

    hb_vec = pl.BlockSpec((1, HB), lambda c, g: (0, 0))
    return pl.pallas_call(
        body, name=name, grid=(nc, B_GROUPS),
        in_specs=_ssd_in_specs(D, HB, rev) + [
            pl.BlockSpec((None, None, B_STATE, GW), lambda c, g: (rev(c), g, 0, 0)),
            pl.BlockSpec((B_CHUNK, GW), lambda c, g: (rev(c), ooff + g))],
        out_specs=[pl.BlockSpec((B_CHUNK, GW), lambda c, g: (rev(c), g)),
                   pl.BlockSpec((B_CHUNK, B_STATE), lambda c, g: (rev(c), g)),
                   pl.BlockSpec((B_CHUNK, B_STATE), lambda c, g: (rev(c), g)),
                   pl.BlockSpec((B_CHUNK, GW), lambda c, g: (rev(c), g)),
                   pl.BlockSpec((B_CHUNK, HB), lambda c, g: (rev(c), 0)),
                   hb_vec, hb_vec, hb_vec,
                   pl.BlockSpec((B_GROUPS, 1, GW), lambda c, g: (0, 0, 0))],
        out_shape=[jax.ShapeDtypeStruct((S, D), f32),
                   jax.ShapeDtypeStruct((S, B_GROUPS * B_STATE), f32),
                   jax.ShapeDtypeStruct((S, B_GROUPS * B_STATE), f32),
                   jax.ShapeDtypeStruct((S, D), bf16),
                   jax.ShapeDtypeStruct((S, HB), f32),
                   jax.ShapeDtypeStruct((1, HB), f32), jax.ShapeDtypeStruct((1, HB), f32),
                   jax.ShapeDtypeStruct((1, HB), f32),
                   jax.ShapeDtypeStruct((B_GROUPS, 1, GW), f32)],
        scratch_shapes=[pltpu.VMEM((B_GROUPS, B_STATE, GW), f32)],
        compiler_params=_cp("arbitrary", "arbitrary"))(
            xact, xact, xact, ymain, dtr, dtb, alog, dsk, bn, saved, dmixed)


def _conv_taps(xp, w_ref, b_ref, r0, K):
    acc = jnp.broadcast_to(b_ref[...], (CONV_ROWS, b_ref.shape[1]))
    for k in range(K):
        acc = acc + w_ref[k:k + 1, :] * xp[r0 + CONV_PAD - (K - 1) + k:r0 + CONV_PAD - (K - 1) + k + CONV_ROWS, :]
    return acc


def conv_fwd(name, x, xoff, w, b, act, out_dtype):
    S = x.shape[0]
    K, CW = w.shape
    tc = CONV_CH

    def body(x_ref, w_ref, b_ref, o_ref, xp):
        xp[0:CONV_PAD, :] = jnp.zeros((CONV_PAD, tc), f32)
        xp[CONV_PAD:CONV_PAD + S, :] = x_ref[...].astype(f32)
        for r0 in range(0, S, CONV_ROWS):
            acc = _conv_taps(xp, w_ref, b_ref, r0, K)
            if act:
                acc = _silu(acc)
            o_ref[r0:r0 + CONV_ROWS, :] = acc.astype(o_ref.dtype)

    return pl.pallas_call(
        body, name=name, grid=(CW // tc,),
        in_specs=[pl.BlockSpec((S, tc), lambda j: (0, j + xoff)),
                  pl.BlockSpec((K, tc), lambda j: (0, j)),
                  pl.BlockSpec((1, tc), lambda j: (0, j))],
        out_specs=pl.BlockSpec((S, tc), lambda j: (0, j)),
        out_shape=jax.ShapeDtypeStruct((S, CW), out_dtype),
        scratch_shapes=[pltpu.VMEM((S + CONV_PAD, tc), f32)],
        compiler_params=_cp("parallel"))(x, w, b)


def conv_bwd(name, x, xoff, w, b, dout, act, dx_dtype):
    S = x.shape[0]
    K, CW = w.shape
    tc = CONV_CH

    def body(x_ref, w_ref, b_ref, d_ref, dx_ref, dw_ref, db_ref, xp, dp):
        xp[0:CONV_PAD, :] = jnp.zeros((CONV_PAD, tc), f32)
        xp[CONV_PAD:CONV_PAD + S, :] = x_ref[...].astype(f32)
        dp[S:S + CONV_PAD, :] = jnp.zeros((CONV_PAD, tc), f32)
        db = jnp.zeros((1, tc), f32)
        for r0 in range(0, S, CONV_ROWS):
            d = d_ref[r0:r0 + CONV_ROWS, :].astype(f32)
            if act:
                pre = _conv_taps(xp, w_ref, b_ref, r0, K)
                s = _sigmoid(pre)
                d = d * (s + pre * s * (1.0 - s))
            dp[r0:r0 + CONV_ROWS, :] = d
            db = db + jnp.sum(d, axis=0, keepdims=True)
        db_ref[...] = db
        for r0 in range(0, S, CONV_ROWS):
            acc = jnp.zeros((CONV_ROWS, tc), f32)
            for k in range(K):
                acc = acc + w_ref[k:k + 1, :] * dp[r0 + (K - 1 - k):r0 + (K - 1 - k) + CONV_ROWS, :]
            dx_ref[r0:r0 + CONV_ROWS, :] = acc.astype(dx_ref.dtype)
        for k in range(K):
            acc = jnp.zeros((1, tc), f32)
            for r0 in range(0, S, CONV_ROWS):
                lo = r0 + CONV_PAD - (K - 1) + k
                acc = acc + jnp.sum(dp[r0:r0 + CONV_ROWS, :] * xp[lo:lo + CONV_ROWS, :], axis=0, keepdims=True)
            dw_ref[k:k + 1, :] = acc

    return pl.pallas_call(
        body, name=name, grid=(CW // tc,),
        in_specs=[pl.BlockSpec((S, tc), lambda j: (0, j + xoff)),
                  pl.BlockSpec((K, tc), lambda j: (0, j)),
                  pl.BlockSpec((1, tc), lambda j: (0, j)),
                  pl.BlockSpec((S, tc), lambda j: (0, j))],
        out_specs=[pl.BlockSpec((S, tc), lambda j: (0, j)),
                   pl.BlockSpec((K, tc), lambda j: (0, j)),
                   pl.BlockSpec((1, tc), lambda j: (0, j))],
        out_shape=[jax.ShapeDtypeStruct((S, CW), dx_dtype),
                   jax.ShapeDtypeStruct((K, CW), f32),
                   jax.ShapeDtypeStruct((1, CW), f32)],
        scratch_shapes=[pltpu.VMEM((S + CONV_PAD, tc), f32), pltpu.VMEM((S + CONV_PAD, tc), f32)],
        compiler_params=_cp("parallel"))(x, w, b, dout)


def loss_head(name, y, target, tm):
    S, D = y.shape

    def body(y_ref, t_ref, dy_ref, l_ref):
        @pl.when(pl.program_id(0) == 0)
        def _():
            l_ref[...] = jnp.zeros_like(l_ref)

        err = y_ref[...] - t_ref[...]
        dy_ref[...] = err * (1.0 / D)
        l_ref[...] += jnp.sum(err * err) * (0.5 / D)

    dy, l = pl.pallas_call(
        body, name=name, grid=(S // tm,),
        in_specs=[pl.BlockSpec((tm, D), lambda i: (i, 0))] * 2,
        out_specs=[pl.BlockSpec((tm, D), lambda i: (i, 0)), pl.BlockSpec((8, LANES), lambda i: (0, 0))],
        out_shape=[jax.ShapeDtypeStruct((S, D), f32), jax.ShapeDtypeStruct((8, LANES), f32)],
        compiler_params=_cp("arbitrary"))(y, target)
    return dy, l[0, 0]


def _flat2d_tiles(rows, cols, itemsize, target_bytes):
    tc = _tile(cols, 1024) if cols % LANES == 0 else cols
    cap = max(8, target_bytes // (tc * itemsize))
    tr = _tile(rows, cap, 16) if rows % 16 == 0 else rows
    return tr, tc


def adamw(name, w, g, m, v):
    shape = w.shape
    cols = shape[-1]
    rows = math.prod(shape[:-1])
    tr, tc = _flat2d_tiles(rows, cols, 4, 1 << 20)
    c1 = 1.0 - ADAM_B1 ** ADAM_STEP
    c2 = 1.0 - ADAM_B2 ** ADAM_STEP

    def body(w_ref, g_ref, m_ref, v_ref, d_ref, nm_ref, nv_ref, g_out_ref):
        gg = g_ref[...]
        nm = ADAM_B1 * m_ref[...] + (1.0 - ADAM_B1) * gg
        nv = ADAM_B2 * v_ref[...] + (1.0 - ADAM_B2) * (gg * gg)
        d_ref[...] = -ADAM_LR * ((nm / c1) / (jnp.sqrt(nv / c2) + ADAM_EPS) + ADAM_WD * w_ref[...])
        nm_ref[...] = nm
        nv_ref[...] = nv
        g_out_ref[...] = gg

    spec = pl.BlockSpec((tr, tc), lambda i, j: (i, j))
    outs = pl.pallas_call(
        body, name=name, grid=(rows // tr, cols // tc), in_specs=[spec] * 4, out_specs=[spec] * 4,
        out_shape=[jax.ShapeDtypeStruct((rows, cols), f32)] * 4,
        compiler_params=_cp("parallel", "parallel"))(*[a.reshape(rows, cols) for a in (w, g, m, v)])
    return [o.reshape(shape) for o in outs]


def _core_index():
    return lax.axis_index("c").astype(jnp.int32).reshape(1)


def _half_rows_tile(Rh, C):
    return _tile(Rh, max(16, (2 << 20) // (C * 2)), 16)


def rs_add(name, G, buf):
    P, _, Rh, C = G.shape
    tr = _half_rows_tile(Rh, C)

    def body(c_ref, g_ref, b_ref, o_ref):
        o_ref[...] = (g_ref[...].astype(f32) + b_ref[...].astype(f32)).astype(o_ref.dtype)

    return pl.pallas_call(
        body, name=name,
        grid_spec=pltpu.PrefetchScalarGridSpec(
            num_scalar_prefetch=1, grid=(P, Rh // tr),
            in_specs=[pl.BlockSpec((None, None, tr, C), lambda p, i, c: (p, c[0], i, 0)),
                      pl.BlockSpec((None, tr, C), lambda p, i, c: (p, i, 0))],
            out_specs=pl.BlockSpec((None, tr, C), lambda p, i, c: (p, i, 0))),
        out_shape=jax.ShapeDtypeStruct((P, Rh, C), bf16),
        compiler_params=_cp("parallel", "parallel"))(_core_index(), G, buf)


def _chip_indices():
    x, y, c = lax.axis_index("x"), lax.axis_index("y"), lax.axis_index("c")
    ids = [2 * x + y] + [2 * _flip(x, fx) + _flip(y, fy) for fx, fy in _CHIP_FLIPS] + [c]
    return [i.astype(jnp.int32).reshape(1) for i in ids]


def rs_sum4(name, pair, buf, final, layer):
    P, Rh, C = buf.shape
    tr = _half_rows_tile(Rh, C)

    def body(i0, i1, i2, i3, ic, b0, b1, b2, b3, f_ref, o_ref):
        o_ref[...] = ((b0[...].astype(f32) + b1[...].astype(f32)) + b2[...].astype(f32)) + b3[...].astype(f32)

    blk = (None, tr, C)
    return pl.pallas_call(
        body, name=name,
        grid_spec=pltpu.PrefetchScalarGridSpec(
            num_scalar_prefetch=5, grid=(Rh // tr,),
            in_specs=[pl.BlockSpec(blk, lambda i, *ids, k=k: (ids[k][0], i, 0)) for k in range(P)] + [_ANY],
            out_specs=pl.BlockSpec((None, None, tr, C), lambda i, *ids: (layer, ids[4][0], i, 0))),
        out_shape=jax.ShapeDtypeStruct(final.shape, final.dtype),
        input_output_aliases={9: 0},
        compiler_params=_cp("parallel"))(*_chip_indices(), pair, buf, buf, buf, final)


def place_own(name, w, layer):
    _, R, C = w.shape
    tr = _tile(R, max(16, (2 << 20) // (C * 2)), 16)

    def body(q, w_ref, own_ref, land_ref):
        wb = w_ref[...].astype(bf16)
        own_ref[...] = wb
        land_ref[...] = wb

    return pl.pallas_call(
        body, name=name,
        grid_spec=pltpu.PrefetchScalarGridSpec(
            num_scalar_prefetch=1, grid=(R // tr,),
            in_specs=[pl.BlockSpec((None, tr, C), lambda i, q: (layer, i, 0))],
            out_specs=[pl.BlockSpec((tr, C), lambda i, q: (i, 0)),
                       pl.BlockSpec((None, tr, C), lambda i, q: (q[0], i, 0))]),
        out_shape=[jax.ShapeDtypeStruct((R, C), bf16), jax.ShapeDtypeStruct((N_CHIPS, R, C), bf16)],
        compiler_params=_cp("parallel"))(_chip_indices()[0], w)


_ANY = pl.BlockSpec(memory_space=pl.ANY)
_CHIP_FLIPS = ((1, 0), (0, 1), (1, 1))


def _place():
    return lax.axis_index("x"), lax.axis_index("y"), lax.axis_index("c")


def _flip(v, f):
    return 1 - v if f else v


def _remote(src, dst, ssem, rsem, dev):
    return pltpu.make_async_remote_copy(src_ref=src, dst_ref=dst, send_sem=ssem, recv_sem=rsem,
                                        device_id=dev, device_id_type=MESH)


_HBM = pl.BlockSpec(memory_space=pltpu.HBM)
_SEM = pl.BlockSpec(memory_space=pltpu.SEMAPHORE)
_DATAFLOW = pltpu.SideEffectType.DATAFLOW_SIDE_EFFECTING
_TOKEN = jax.ShapeDtypeStruct((8, LANES), f32)


def _in_hbm(a):
    return pltpu.with_memory_space_constraint(a, pltpu.HBM)


def _hbm_like(a):
    return pltpu.HBM(a.shape, a.dtype)


def gather_ici_start(name, groups, dep=None):
    sizes = [len(g) for g in groups]
    owns = [o for g in groups for o, _ in g]
    lands = [l for g in groups for _, l in g]
    n, ng = len(owns), len(groups)
    deps = [] if dep is None else [dep]

    def body(*refs):
        own, land = refs[:n], refs[n:2 * n]
        sems = refs[2 * n + len(deps):2 * n + len(deps) + 2 * ng]
        token = refs[-1]
        x, y, c = _place()
        q = 2 * x + y
        t = 0
        for gi, size in enumerate(sizes):
            for j in range(size):
                for k, (fx, fy) in enumerate(_CHIP_FLIPS):
                    _remote(own[t].at[c], land[t].at[q, c], sems[2 * gi].at[3 * j + k], sems[2 * gi + 1].at[3 * j + k],
                            (_flip(x, fx), _flip(y, fy), c)).start()
                t += 1
        token[...] = jnp.zeros_like(token)

    sem_shapes = [pltpu.SemaphoreType.DMA((3 * size,)) for size in sizes for _ in range(2)]
    res = pl.pallas_call(
        body, name=name,
        in_specs=[_HBM] * (2 * n) + [_ANY] * len(deps),
        out_specs=[_SEM] * (2 * ng) + [_HBM] * (2 * n) + [pl.BlockSpec(memory_space=pltpu.VMEM)],
        out_shape=sem_shapes + [_hbm_like(a) for a in owns + lands] + [_TOKEN],
        input_output_aliases={i: 2 * ng + i for i in range(2 * n)},
        compiler_params=pltpu.CompilerParams(has_side_effects=_DATAFLOW),
    )(*[_in_hbm(a) for a in owns + lands], *deps)
    own_thru, land_thru = res[2 * ng:2 * ng + n], res[2 * ng + n:2 * ng + 2 * n]
    handles, t = [], 0
    for gi, size in enumerate(sizes):
        handles.append((res[2 * gi], res[2 * gi + 1], list(own_thru[t:t + size]), list(land_thru[t:t + size])))
        t += size
    return handles, res[-1]


def gather_ici_wait(name, handle, after):
    send, recv, owns, lands = handle
    n = len(owns)

    def body(*refs):
        own, land = refs[:n], refs[n:2 * n]
        send_ref, recv_ref = refs[2 * n], refs[2 * n + 1]
        x, y, c = _place()
        for j in range(n):
            for k, (fx, fy) in enumerate(_CHIP_FLIPS):
                px, py = _flip(x, fx), _flip(y, fy)
                cp = _remote(own[j].at[c], land[j].at[2 * px + py, c], send_ref.at[3 * j + k], recv_ref.at[3 * j + k],
                             (px, py, c))
                cp.wait_send()
                cp.wait_recv()

    res = pl.pallas_call(
        body, name=name,
        in_specs=[_HBM] * (2 * n) + [_SEM, _SEM, _ANY],
        out_specs=[_HBM] * (2 * n),
        out_shape=[_hbm_like(a) for a in owns + lands],
        input_output_aliases={i: i for i in range(2 * n)},
        compiler_params=pltpu.CompilerParams(has_side_effects=_DATAFLOW),
    )(*owns, *lands, send, recv, after)
    return list(res[n:])


def _split_start(name, body, arrays, n_sems, dep=None):
    n = len(arrays)
    deps = [] if dep is None else [dep]

    def kernel_body(*refs):
        m = n + len(deps)
        body(refs[:n], refs[m], refs[m + 1])
        refs[-1][...] = jnp.zeros_like(refs[-1])

    res = pl.pallas_call(
        kernel_body, name=name,
        in_specs=[_HBM] * n + [_ANY] * len(deps),
        out_specs=[_SEM, _SEM] + [_HBM] * n + [pl.BlockSpec(memory_space=pltpu.VMEM)],
        out_shape=[pltpu.SemaphoreType.DMA((n_sems,)), pltpu.SemaphoreType.DMA((n_sems,))]
        + [_hbm_like(a) for a in arrays] + [_TOKEN],
        input_output_aliases={i: 2 + i for i in range(n)},
        compiler_params=pltpu.CompilerParams(has_side_effects=_DATAFLOW),
    )(*[_in_hbm(a) for a in arrays], *deps)
    return res[0], res[1], list(res[2:2 + n]), res[-1]


def _split_wait(name, body, handle, after):
    send, recv, arrays, _ = handle
    n = len(arrays)

    def kernel_body(*refs):
        body(refs[:n], refs[n], refs[n + 1])

    res = pl.pallas_call(
        kernel_body, name=name,
        in_specs=[_HBM] * n + [_SEM, _SEM, _ANY],
        out_specs=[_HBM] * n,
        out_shape=[_hbm_like(a) for a in arrays],
        input_output_aliases={i: i for i in range(n)},
        compiler_params=pltpu.CompilerParams(has_side_effects=_DATAFLOW),
    )(*arrays, send, recv, after)
    return list(res)


def _forward_copies(land, send, recv):
    x, y, c = _place()
    for t in range(len(land)):
        for k, (fx, fy) in enumerate(_CHIP_FLIPS):
            slab = land[t].at[2 * _flip(x, fx) + _flip(y, fy), c]
            yield _remote(slab, slab, send.at[3 * t + k], recv.at[3 * t + k], (x, y, 1 - c))


def forward_start(name, lands):
    def body(land, send, recv):
        for cp in _forward_copies(land, send, recv):
            cp.start()

    return _split_start(name, body, lands, 3 * len(lands))


def forward_wait(name, handle, after):
    def body(land, send, recv):
        for cp in _forward_copies(land, send, recv):
            cp.wait_send()
            cp.wait_recv()

    return _split_wait(name, body, handle, after)


def sibling_start(name, Gs, dep=None):
    T = len(Gs)
    bufs = [lax.empty((G.shape[0],) + G.shape[2:], G.dtype) for G in Gs]

    def body(refs, send, recv):
        x, y, c = _place()
        for t in range(T):
            _remote(refs[t].at[:, 1 - c], refs[T + t], send.at[t], recv.at[t], (x, y, 1 - c)).start()

    return _split_start(name, body, list(Gs) + bufs, T, dep)


def sibling_wait(name, handle, after):
    T = len(handle[2]) // 2

    def body(refs, send, recv):
        x, y, c = _place()
        for t in range(T):
            cp = _remote(refs[t].at[:, 1 - c], refs[T + t], send.at[t], recv.at[t], (x, y, 1 - c))
            cp.wait_send()
            cp.wait_recv()

    res = _split_wait(name, body, handle, after)
    return res[:T], res[T:]


def reduce_ici_start(name, Ss):
    T = len(Ss)
    lands = [lax.empty(S.shape, S.dtype) for S in Ss]

    def body(*refs):
        s, land = refs[:T], refs[T:2 * T]
        send, recv = refs[2 * T], refs[2 * T + 1]
        token = refs[-1]
        x, y, c = _place()
        q = 2 * x + y
        for t in range(T):
            for k, (fx, fy) in enumerate(_CHIP_FLIPS):
                px, py = _flip(x, fx), _flip(y, fy)
                _remote(s[t].at[2 * px + py], land[t].at[q], send.at[3 * t + k], recv.at[3 * t + k],
                        (px, py, c)).start()
        token[...] = jnp.zeros_like(token)

    res = pl.pallas_call(
        body, name=name,
        in_specs=[_HBM] * (2 * T),
        out_specs=[_SEM, _SEM] + [_HBM] * (2 * T) + [pl.BlockSpec(memory_space=pltpu.VMEM)],
        out_shape=[pltpu.SemaphoreType.DMA((3 * T,)), pltpu.SemaphoreType.DMA((3 * T,))]
        + [_hbm_like(a) for a in Ss + lands] + [_TOKEN],
        input_output_aliases={i: 2 + i for i in range(2 * T)},
        compiler_params=pltpu.CompilerParams(has_side_effects=_DATAFLOW),
    )(*[_in_hbm(a) for a in Ss + lands])
    return res[0], res[1], list(res[2:2 + T]), list(res[2 + T:2 + 2 * T]), res[-1]


def reduce_ici_wait(name, handle, after):
    send, recv, Ss, lands, _ = handle
    T = len(Ss)

    def body(*refs):
        s, land = refs[:T], refs[T:2 * T]
        send_ref, recv_ref = refs[2 * T], refs[2 * T + 1]
        x, y, c = _place()
        for t in range(T):
            for k, (fx, fy) in enumerate(_CHIP_FLIPS):
                px, py = _flip(x, fx), _flip(y, fy)
                cp = _remote(s[t].at[2 * px + py], land[t].at[2 * px + py], send_ref.at[3 * t + k], recv_ref.at[3 * t + k],
                             (px, py, c))
                cp.wait_send()
                cp.wait_recv()

    res = pl.pallas_call(
        body, name=name,
        in_specs=[_HBM] * (2 * T) + [_SEM, _SEM, _ANY],
        out_specs=[_HBM] * (2 * T),
        out_shape=[_hbm_like(a) for a in Ss + lands],
        input_output_aliases={i: i for i in range(2 * T)},
        compiler_params=pltpu.CompilerParams(has_side_effects=_DATAFLOW),
    )(*Ss, *lands, send, recv, after)
    return list(res[:T]), list(res[T:])


def rs_share(name, tots, layers):
    T = len(tots)

    def body(*refs):
        s, o = refs[:T], refs[T:2 * T]
        send, recv = refs[2 * T:]
        x, y, c = _place()
        cps = []
        for t in range(T):
            mine = o[t].at[layers[t], c]
            cp = _remote(mine, mine, send.at[t], recv.at[t], (x, y, 1 - c))
            cp.start()
            cps.append(cp)
        for t in range(T):
            other = o[t].at[layers[t], 1 - c]
            _remote(other, other, send.at[t], recv.at[t], (x, y, 1 - c)).wait_recv()
        for cp in cps:
            cp.wait_send()

    return pl.pallas_call(
        body, name=name, in_specs=[_ANY] * T, out_specs=[_ANY] * T,
        out_shape=[jax.ShapeDtypeStruct(s.shape, s.dtype) for s in tots],
        input_output_aliases={t: t for t in range(T)},
        scratch_shapes=[pltpu.SemaphoreType.DMA((T,)), pltpu.SemaphoreType.DMA((T,))],
        )(*tots)


def all_reduce_small(name, vec):
    rows = vec.shape[0]
    flips = [(fx, fy, fc) for fx in (0, 1) for fy in (0, 1) for fc in (0, 1)][1:]

    def body(v_ref, o_ref, buf, send, recv):
        x, y, c = _place()
        me = 4 * x + 2 * y + c
        buf[me] = v_ref[...]
        cps = []
        for k, (fx, fy, fc) in enumerate(flips):
            cp = _remote(buf.at[me], buf.at[me], send.at[k], recv.at[k],
                         (_flip(x, fx), _flip(y, fy), _flip(c, fc)))
            cp.start()
            cps.append(cp)
        for k, (fx, fy, fc) in enumerate(flips):
            slab = buf.at[4 * _flip(x, fx) + 2 * _flip(y, fy) + _flip(c, fc)]
            _remote(slab, slab, send.at[k], recv.at[k], (x, y, c)).wait_recv()
        for cp in cps:
            cp.wait_send()
        acc = buf[0]
        for d in range(1, N_DEV):
            acc = acc + buf[d]
        o_ref[...] = acc

    return pl.pallas_call(
        body, name=name,
        in_specs=[pl.BlockSpec(memory_space=pltpu.VMEM)], out_specs=pl.BlockSpec(memory_space=pltpu.VMEM),
        out_shape=jax.ShapeDtypeStruct((rows, LANES), f32),
        scratch_shapes=[pltpu.VMEM((N_DEV, rows, LANES), f32),
                        pltpu.SemaphoreType.DMA((N_DEV - 1,)), pltpu.SemaphoreType.DMA((N_DEV - 1,))],
        compiler_params=pltpu.CompilerParams(vmem_limit_bytes=V7X_VMEM_LIMIT))(vec)


def _all_done(arrays):
    return jnp.stack([a[(0,) * a.ndim].astype(f32) for a in arrays]).sum(keepdims=True)


def _pack(arrays):
    flat = jnp.concatenate([a.reshape(-1) for a in arrays])
    n = flat.shape[0]
    rows = -(-n // (8 * LANES)) * 8
    return jnp.pad(flat, (0, rows * LANES - n)).reshape(rows, LANES)


def _unpack(vec, shapes):
    flat = vec.reshape(-1)
    out, pos = [], 0
    for s in shapes:
        n = math.prod(s)
        out.append(flat[pos:pos + n].reshape(s))
        pos += n
    return out


def _f_first(x, g):
    return x, _rms(x, g)


def _f_mid(h, m, gp, gn):
    h1 = h + _rms(m, gp)
    return h1, _rms(h1, gn)


def _f_mid_bias(h, m, b, gp, gn):
    h1 = h + _rms(m + b, gp)
    return h1, _rms(h1, gn)


def _f_last(h, m, gp):
    return (h + _rms(m, gp),)


def _f_swiglu(gate, up):
    return (_silu(gate) * up,)


def _swiglu_tile(gate, up):
    return _silu(gate.astype(f32)) * up.astype(f32)


def _swiglu_bwd_tile(d_act, gate, up):
    _, vjp = jax.vjp(_f_swiglu, gate.astype(f32), up.astype(f32))
    return vjp((d_act,))


def _f_glu(a, g, ba, bg):
    return ((a + ba) * _sigmoid(g + bg),)


def _f_ln_silu(x, g, b):
    mu = jnp.mean(x, axis=-1, keepdims=True)
    xc = x - mu
    y = xc * lax.rsqrt(jnp.mean(xc * xc, axis=-1, keepdims=True) + LN_EPS) * g + b
    return (_silu(y),)


def _f_lower_bounds(logits):
    n = logits.shape[0]
    e = jnp.exp(logits - jnp.max(logits, axis=0, keepdims=True))
    p = e / jnp.sum(e, axis=0, keepdims=True)
    layer = lax.broadcasted_iota(jnp.int32, logits.shape, 0)
    out = -jnp.broadcast_to(p[0:1, :], logits.shape)
    for j in range(n):
        out = out + jnp.where(layer >= j, p[j:j + 1, :], 0.0)
    return (out,)


WEIGHT_NAMES = ['mix_pre_g', 'mix_post_g', 'ffn_pre_g', 'ffn_post_g', 'hgrn_lb_logits', 'even_w_in',
                'hgrn_norm_g', 'ssd_conv_w', 'ssd_conv_b', 'ssd_dt_bias', 'ssd_a_log', 'ssd_d', 'ssd_norm_g',
                'even_w_out', 'conf_w1', 'conf_b1', 'conf_dw_w', 'conf_dw_b', 'conf_ln_g', 'conf_ln_b',
                'conf_w2', 'conf_b2', 'ffn_w_gate', 'ffn_w_up', 'ffn_w_down']
BIG = ['even_w_in', 'even_w_out', 'conf_w1', 'conf_w2', 'ffn_w_gate', 'ffn_w_up', 'ffn_w_down']
SMALL_SHARDED = {'ssd_conv_w': 2, 'conf_b1': 1, 'conf_dw_w': 2, 'conf_dw_b': 1, 'conf_ln_g': 1,
                 'conf_ln_b': 1, 'conf_b2': 1}


def _train_step(x, target, w, m, v):
    S, D = x.shape[1], x.shape[2]
    x2, t2 = x[0], target[0]
    NL = w['mix_pre_g'].shape[0]
    HB = w['ssd_dt_bias'].shape[1]
    GN = B_GROUPS * B_STATE
    xw, yw, cw = _place()
    chip = 2 * xw + yw
    tm = _tile(S, 128, 8)
    row1 = lambda a, i: a[i:i + 1]

    sharded = list(SMALL_SHARDED)
    placed = []
    for n in sharded:
        ax, a = SMALL_SHARDED[n], w[n]
        full = jnp.zeros(a.shape[:ax] + (a.shape[ax] * N_CHIPS,) + a.shape[ax + 1:], f32)
        start = [0] * a.ndim
        start[ax] = chip * a.shape[ax]
        placed.append(lax.dynamic_update_slice(full, jnp.where(cw == 0, a, 0.0), start))
    whole = dict(zip(sharded, _unpack(all_reduce_small("gather_small", _pack(placed)), [p.shape for p in placed])))
    small = {n: whole.get(n, w[n]) for n in WEIGHT_NAMES if n not in BIG}

    def mixer_keys(layer):
        names = ('even_w_in', 'even_w_out') if layer % 2 == 0 else ('conf_w1', 'conf_w2')
        return [(n, layer // 2) for n in names]

    def ffn_keys(layer):
        return [(n, layer) for n in ('ffn_w_gate', 'ffn_w_up', 'ffn_w_down')]

    groups = [keys(layer) for layer in range(NL) for keys in (mixer_keys, ffn_keys)]
    halves = lambda a: a.reshape(a.shape[:-2] + (2, a.shape[-2] // 2, a.shape[-1]))
    own, land = {}, {}

    def start_groups(name, some, dep=None):
        for g in some:
            for n, l in g:
                own[n, l], land[n, l] = place_own("place_own", w[n], l)
        return gather_ici_start(name, [[(halves(own[k]), halves(land[k])) for k in g] for g in some], dep)

    first, first_started = start_groups("gather_start_first", groups[:1])
    rest, rest_started = start_groups("gather_start_rest", groups[1:], first_started)
    handles = first + rest
    W, handed = {}, {}

    def fetch_begin(gi, after):
        arrived = gather_ici_wait(f"gather_wait_{gi}", handles[gi], after)
        handed[gi] = forward_start(f"forward_start_{gi}", arrived)
        return handed[gi][-1]

    def fetch_end(gi, after):
        for k, a in zip(groups[gi], forward_wait(f"forward_wait_{gi}", handed[gi], after)):
            W[k] = a.reshape((N_CHIPS, 1) + own[k].shape)

    WM = 6 * D + 2 * GN
    w_main, w_dt = {}, {}

    n_even = small['hgrn_lb_logits'].shape[0]
    (lbs,) = _stage_fwd("lower_bounds", _f_lower_bounds, [small['hgrn_lb_logits']], [], [(D, f32)], tm=n_even)
    saved = []
    h = x2
    (u,) = _stage_fwd("pre_norm", lambda a, g: (_rms(a, g),), [h],
                      [row1(small['mix_pre_g'], 0) + rest_started[0, 0]], [(D, bf16)], tm=tm)
    fetch_begin(0, u)
    for layer in range(NL):
        li = layer // 2
        r = {'h': h, 'u': u}
        fetch_end(2 * layer, u)
        if layer % 2 == 0:
            win = jnp.concatenate([W['even_w_in', li][j, 0] for j in range(N_CHIPS)], axis=-1)
            w_main[li], w_dt[li] = win[None, None, :, :WM], win[None, None, :, WM:]
            r['ymain'] = mm_nn_col("in_proj", u, w_main[li], 0)
            r['dtr'] = mm_nn_col("in_proj_dt", u, w_dt[li], 0)
            r['xact'] = conv_fwd("ssd_conv", r['ymain'], 5 * D // CONV_CH, small['ssd_conv_w'][li],
                                 row1(small['ssd_conv_b'], li), True, f32)
            o_a, r['hg_st'] = hgrn_fwd("hgrn", r['ymain'], row1(lbs, li), row1(small['hgrn_norm_g'], li), D)
            begun = fetch_begin(2 * layer + 1, o_a)[0, 0]
            o_b, r['ssd_st'] = ssd_fwd("ssd", r['xact'], r['ymain'], r['dtr'], row1(small['ssd_dt_bias'], li),
                                       row1(small['ssd_a_log'], li), row1(small['ssd_d'], li),
                                       row1(small['ssd_norm_g'], li) + begun, D)
            r['mixed'] = jnp.concatenate([o_a, o_b], axis=1)
            r['m'] = mm_nn_row("out_proj", r['mixed'], W['even_w_out', li], 0)
            mid_fn, mid_par = _f_mid, []
        else:
            r['c1'] = mm_nn_col("conf_in", u, W['conf_w1', li], 0)
            b1 = row1(small['conf_b1'], li)
            tn = _tile(D, 512)
            (r['glu'],) = _stage_fwd("conf_glu", _f_glu, [r['c1'], (r['c1'], D // tn)], [b1, (b1, D // tn)],
                                     [(D, f32)], tm=tm, tn=tn)
            r['cc'] = conv_fwd("conf_conv", r['glu'], 0, small['conf_dw_w'][li], row1(small['conf_dw_b'], li),
                               False, f32)
            begun = fetch_begin(2 * layer + 1, r['cc'])[0, 0]
            (r['c2'],) = _stage_fwd("conf_ln", _f_ln_silu, [r['cc']],
                                    [row1(small['conf_ln_g'], li) + begun, row1(small['conf_ln_b'], li)],
                                    [(D, bf16)], tm=tm)
            r['m'] = mm_nn_row("conf_out", r['c2'], W['conf_w2', li], 0)
            mid_fn, mid_par = _f_mid_bias, [row1(small['conf_b2'], li)]
        r['mid_fn'] = mid_fn
        r['mid_par'] = mid_par + [row1(small['mix_post_g'], layer), row1(small['ffn_pre_g'], layer)]
        r['h1'], r['u2'] = _stage_fwd("mid_norm", mid_fn, [h, r['m']], r['mid_par'], [(D, f32), (D, bf16)], tm=tm)
        fetch_end(2 * layer + 1, r['u2'])
        r['gate'] = mm_nn_col("ffn_gate", r['u2'], W['ffn_w_gate', layer], 0, bf16)
        begun = fetch_begin(2 * layer + 2, r['gate']) if layer + 1 < NL else None
        r['up'] = mm_nn_col("ffn_up", r['u2'], W['ffn_w_up', layer], 0, bf16, dep=begun)
        r['dn'] = mm_nn_row("ffn_down", [r['gate'], r['up']], W['ffn_w_down', layer], 0, a_fn=_swiglu_tile)
        if layer + 1 < NL:
            r['end_fn'] = _f_mid
            r['end_par'] = [row1(small['ffn_post_g'], layer), row1(small['mix_pre_g'], layer + 1)]
            h, u = _stage_fwd("end_norm", _f_mid, [r['h1'], r['dn']], r['end_par'], [(D, f32), (D, bf16)], tm=tm)
        else:
            r['end_fn'] = _f_last
            r['end_par'] = [row1(small['ffn_post_g'], layer)]
            (h,) = _stage_fwd("last_norm", _f_last, [r['h1'], r['dn']], r['end_par'], [(D, f32)], tm=tm)
        saved.append(r)

    dy, loss_local = loss_head("loss_head", h, t2, tm)
    loss = lax.psum(loss_local, ("x", "y", "c"))

    gs = {n: jnp.zeros(small[n].shape, f32) for n in small}

    def put(n, i, val):
        gs[n] = gs[n].at[i].add(val.reshape(gs[n].shape[1:]))

    final = {n: lax.empty((w[n].shape[0], 2, w[n].shape[1] // 2, w[n].shape[2]), f32) for n in BIG}
    to_sibling, to_chips = [], []

    def reduce_begin(gi, parts, dep=None):
        handle = sibling_start(f"sibling_start_{gi}", [halves(p) for p in parts], dep)
        to_sibling.append((gi, handle))
        return handle[-1][0, 0]

    def reduce_middle(after):
        gi, handle = to_sibling.pop(0)
        parts, from_sib = sibling_wait(f"sibling_wait_{gi}", handle, after)
        sums = [rs_add("reduce_add", a, b) for a, b in zip(parts, from_sib)]
        handle = reduce_ici_start(f"reduce_start_{gi}", sums)
        to_chips.append((gi, handle))
        return handle[-1][0, 0]

    def reduce_finish(after):
        gi, handle = to_chips.pop(0)
        keys = groups[gi]
        sums, lands = reduce_ici_wait(f"reduce_wait_{gi}", handle, after)
        for (n, l), s_, b_ in zip(keys, sums, lands):
            final[n] = rs_sum4("reduce_sum", s_, b_, final[n], l)
        names = [n for n, _ in keys]
        shared = rs_share("reduce_share", [final[n] for n in names], [l for _, l in keys])
        final.update(zip(names, shared))

    def reduce_step(gi, parts, newest, dep=None):
        zero = reduce_begin(gi, parts, dep)
        if to_chips:
            reduce_finish(newest)
        if len(to_sibling) > 1:
            zero = zero + reduce_middle(newest)
        return zero

    grads, first_layer = {}, {}

    def small_gradients(dh, du_parts, started):
        (first_layer['grad_x'],), pg = _stage_bwd("pre_norm_bwd", _f_first, [x2],
                                                   [row1(small['mix_pre_g'], 0) + started],
                                                   [[dh], du_parts], [f32], tm=tm)
        put('mix_pre_g', 0, pg[0])
        dlbs = jnp.concatenate([saved[2 * i]['dlb'] for i in range(n_even)], axis=0)
        (dlogits,), _ = _stage_bwd("lower_bounds_bwd", _f_lower_bounds, [small['hgrn_lb_logits']], [],
                                   [[dlbs]], [f32], tm=n_even)
        gs['hgrn_lb_logits'] = dlogits
        names_s = [n for n in WEIGHT_NAMES if n not in BIG]
        total = all_reduce_small("reduce_small", _pack([gs[n] for n in names_s]))
        for n, a in zip(names_s, _unpack(total, [gs[n].shape for n in names_s])):
            if n in SMALL_SHARDED:
                ax = SMALL_SHARDED[n]
                size = w[n].shape[ax]
                start = [0] * a.ndim
                start[ax] = chip * size
                a = lax.dynamic_slice(a, start, a.shape[:ax] + (size,) + a.shape[ax + 1:])
            grads[n] = a
        return total

    dh = dy
    du_parts = None
    started = None
    for layer in reversed(range(NL)):
        li = layer // 2
        r = saved[layer]
        cts = [[dh]] if du_parts is None else [[dh], du_parts]
        par = r['end_par'] if started is None else [r['end_par'][0] + started] + r['end_par'][1:]
        (dh1, d_dn), pg = _stage_bwd("end_norm_bwd", r['end_fn'], [r['h1'], r['dn']], par, cts,
                                     [f32, bf16], tm=tm)
        put('ffn_post_g', layer, pg[0])
        if du_parts is not None:
            put('mix_pre_g', layer + 1, pg[1])
        d_gate, d_up = mm_nt_row("ffn_down_dx", d_dn, W['ffn_w_down', layer], 0, bf16,
                                 tail=(_swiglu_bwd_tile, [r['gate'], r['up']]), n_out=2)
        g_down = mm_tn_row("ffn_down_dw", [r['gate'], r['up']], d_dn, N_CHIPS, a_fn=_swiglu_tile)
        du_a = mm_nt_col("ffn_gate_dx", d_gate, W['ffn_w_gate', layer], 0)
        du_b = mm_nt_col("ffn_up_dx", d_up, W['ffn_w_up', layer], 0)
        g_gate = mm_tn_col("ffn_gate_dw", r['u2'], d_gate, N_CHIPS)
        g_up = mm_tn_col("ffn_up_dw", r['u2'], d_up, N_CHIPS)
        started = reduce_step(2 * layer + 1, [g_gate, g_up, g_down], g_up)
        par = r['mid_par'][:-1] + [r['mid_par'][-1] + started]
        (dh, dm), pg = _stage_bwd("mid_norm_bwd", r['mid_fn'], [r['h'], r['m']], par,
                                  [[dh1], [du_a, du_b]], [f32, bf16], tm=tm)
        put('mix_post_g', layer, pg[-2])
        put('ffn_pre_g', layer, pg[-1])
        if layer % 2 == 0:
            d_mixed = mm_nt_row("out_proj_dx", dm, W['even_w_out', li], 0)
            g_out = mm_tn_row("out_proj_dw", r['mixed'], dm, N_CHIPS)
            dxs, dbm, dcm, dz, ddt, ddtb, dal, dds, dbn = ssd_bwd(
                "ssd_bwd", r['xact'], r['ymain'], r['dtr'], row1(small['ssd_dt_bias'], li),
                row1(small['ssd_a_log'], li), row1(small['ssd_d'], li), row1(small['ssd_norm_g'], li),
                r['ssd_st'], d_mixed, D)
            put('ssd_dt_bias', li, ddtb)
            put('ssd_a_log', li, dal)
            put('ssd_d', li, dds)
            put('ssd_norm_g', li, dbn)
            d_xact = jnp.concatenate([dxs, dbm, dcm], axis=1)
            d_xbc, dcw, dcb = conv_bwd("ssd_conv_bwd", r['ymain'], 5 * D // CONV_CH, small['ssd_conv_w'][li],
                                       row1(small['ssd_conv_b'], li), d_xact, True, bf16)
            put('ssd_conv_w', li, dcw)
            put('ssd_conv_b', li, dcb)
            dq, df, dv, dg, dlb, dan = hgrn_bwd("hgrn_bwd", r['ymain'], row1(lbs, li), row1(small['hgrn_norm_g'], li),
                                               r['hg_st'], d_mixed, D)
            put('hgrn_norm_g', li, dan)
            r['dlb'] = dlb
            d_main = jnp.concatenate([dq, df, dv, dg, dz, d_xbc], axis=1)
            du_parts = [mm_nt_col("in_proj_dx", d_main, w_main[li], 0),
                        mm_nt_col("in_proj_dt_dx", ddt, w_dt[li], 0)]
            g_in = jnp.concatenate([mm_tn_col("in_proj_dw", r['u'], d_main, 1)[0],
                                    mm_tn_col("in_proj_dt_dw", r['u'], ddt, 1)[0]], axis=-1)
            CI = w['even_w_in'].shape[2]
            g_in = jnp.stack([g_in[:, j * CI:(j + 1) * CI] for j in range(N_CHIPS)])
            mixer_parts, newest = [g_in, g_out], g_in
        else:
            put('conf_b2', li, pg[0])
            d_c2 = mm_nt_row("conf_out_dx", dm, W['conf_w2', li], 0)
            g_w2 = mm_tn_row("conf_out_dw", r['c2'], dm, N_CHIPS)
            (d_cc,), pl_ = _stage_bwd("conf_ln_bwd", _f_ln_silu, [r['cc']],
                                      [row1(small['conf_ln_g'], li), row1(small['conf_ln_b'], li)],
                                      [[d_c2]], [f32], tm=tm)
            put('conf_ln_g', li, pl_[0])
            put('conf_ln_b', li, pl_[1])
            d_glu, ddw, ddb = conv_bwd("conf_conv_bwd", r['glu'], 0, small['conf_dw_w'][li],
                                       row1(small['conf_dw_b'], li), d_cc, False, f32)
            put('conf_dw_w', li, ddw)
            put('conf_dw_b', li, ddb)
            b1 = row1(small['conf_b1'], li)
            tn = _tile(D, 512)
            (da, dg_), pb = _stage_bwd("conf_glu_bwd", _f_glu, [r['c1'], (r['c1'], D // tn)], [b1, (b1, D // tn)],
                                       [[d_glu]], [bf16, bf16], tm=tm, tn=tn)
            put('conf_b1', li, jnp.concatenate([pb[0], pb[1]], axis=1))
            d_c1 = jnp.concatenate([da, dg_], axis=1)
            du_parts = [mm_nt_col("conf_in_dx", d_c1, W['conf_w1', li], 0)]
            g_w1 = mm_tn_col("conf_in_dw", r['u'], d_c1, N_CHIPS)
            mixer_parts, newest = [g_w1, g_w2], g_w1
        small_done = small_gradients(dh, du_parts, started) if layer == 0 else None
        started = reduce_step(2 * layer, mixer_parts, newest, small_done)
    started = started + reduce_middle(mixer_parts[0])

    delta, new_m, new_v = {}, {}, {}

    def update(names):
        for n in names:
            g = final[n].reshape(w[n].shape) if n in BIG else grads[n]
            delta[n], new_m[n], new_v[n], grads[n] = adamw("adamw", w[n], g, m[n], v[n])
        return _all_done([delta[n] for n in names])

    second_last = [n for n, _ in groups[to_chips[0][0]]]
    last = [n for n, _ in groups[to_chips[1][0]]]
    grads[WEIGHT_NAMES[0]] = grads[WEIGHT_NAMES[0]] + started
    grad_x2 = first_layer['grad_x']
    done = update([n for n in WEIGHT_NAMES if n not in last + second_last])
    reduce_finish(done)
    done = update(second_last)
    reduce_finish(done)
    update(last)
    return (loss, grad_x2[None], *[grads[n] for n in WEIGHT_NAMES], *[delta[n] for n in WEIGHT_NAMES],
            *[new_m[n] for n in WEIGHT_NAMES], *[new_v[n] for n in WEIGHT_NAMES])


def kernel(x, mix_pre_g, mix_post_g, ffn_pre_g, ffn_post_g, hgrn_lb_logits, even_w_in, hgrn_norm_g, ssd_conv_w, ssd_conv_b, ssd_dt_bias, ssd_a_log, ssd_d, ssd_norm_g, even_w_out, conf_w1, conf_b1, conf_dw_w, conf_dw_b, conf_ln_g, conf_ln_b, conf_w2, conf_b2, ffn_w_gate, ffn_w_up, ffn_w_down, loss_target, m_mix_pre_g, m_mix_post_g, m_ffn_pre_g, m_ffn_post_g, m_hgrn_lb_logits, m_even_w_in, m_hgrn_norm_g, m_ssd_conv_w, m_ssd_conv_b, m_ssd_dt_bias, m_ssd_a_log, m_ssd_d, m_ssd_norm_g, m_even_w_out, m_conf_w1, m_conf_b1, m_conf_dw_w, m_conf_dw_b, m_conf_ln_g, m_conf_ln_b, m_conf_w2, m_conf_b2, m_ffn_w_gate, m_ffn_w_up, m_ffn_w_down, v_mix_pre_g, v_mix_post_g, v_ffn_pre_g, v_ffn_post_g, v_hgrn_lb_logits, v_even_w_in, v_hgrn_norm_g, v_ssd_conv_w, v_ssd_conv_b, v_ssd_dt_bias, v_ssd_a_log, v_ssd_d, v_ssd_norm_g, v_even_w_out, v_conf_w1, v_conf_b1, v_conf_dw_w, v_conf_dw_b, v_conf_ln_g, v_conf_ln_b, v_conf_w2, v_conf_b2, v_ffn_w_gate, v_ffn_w_up, v_ffn_w_down):
    args = locals()
    w = {n: args[n] for n in WEIGHT_NAMES}
    m = {n: args["m_" + n] for n in WEIGHT_NAMES}
    v = {n: args["v_" + n] for n in WEIGHT_NAMES}
    return _train_step(x, loss_target, w, m, v)
```

```python
import functools
import math

import jax
import jax.numpy as jnp
from jax import lax
from jax.experimental import pallas as pl
from jax.experimental.pallas import tpu as pltpu

f32 = jnp.float32
bf16 = jnp.bfloat16
MESH = pl.DeviceIdType.MESH
HI = lax.Precision.HIGHEST

A_HEAD = 128
A_CHUNK = 64
A_SUB = 8
A_REF = 4
A_EXP_CAP = 60.0
A_HEADS_PER_STEP = 8
A_F_MIN = 1e-6
B_HEAD = 64
B_GROUPS = 4
B_STATE = 128
B_CONV = 4
B_CHUNK = 128
C_KERNEL = 31
RMS_EPS = 1e-6
LN_EPS = 1e-5
ADAM_LR = 0.001
ADAM_B1 = 0.9
ADAM_B2 = 0.999
ADAM_EPS = 1e-08
ADAM_WD = 0.01
ADAM_STEP = 10

N_CHIPS = 4
N_DEV = 8
V7X_VMEM_LIMIT = 56 * 1024 * 1024
LANES = 128
CONV_PAD = 32
CONV_ROWS = 128
CONV_CH = 256

NN = (((1,), (0,)), ((), ()))
NT = (((1,), (1,)), ((), ()))
TN = (((0,), (0,)), ((), ()))


def _tile(n, cap, unit=LANES):
    best = None
    for t in range(unit, min(n, cap) + 1, unit):
        if n % t == 0:
            best = t
    return n if best is None else best


def _cp(*sem):
    return pltpu.CompilerParams(dimension_semantics=sem, vmem_limit_bytes=V7X_VMEM_LIMIT)


def _sigmoid(x):
    return jax.nn.sigmoid(x)


def _silu(x):
    return x * jax.nn.sigmoid(x)


def _rms(x, g):
    return x * lax.rsqrt(jnp.mean(x * x, axis=-1, keepdims=True) + RMS_EPS) * g


def _pair(a):
    return a if isinstance(a, tuple) else (a, 0)


def _stage(name, fn, rows, params, outs, par_outs=(), *, tm, tn=None):
    rows = [_pair(r) for r in rows]
    params = [_pair(p) for p in params]
    S = rows[0][0].shape[0]
    n_in, n_o = len(rows) + len(params), len(outs)
    if tn is None:
        grid = (S // tm,)
        in_specs = [pl.BlockSpec((tm, a.shape[1]), lambda i: (i, 0)) for a, _ in rows]
        in_specs += [pl.BlockSpec(a.shape, lambda i: (0, 0)) for a, _ in params]
        out_specs = [pl.BlockSpec((tm, w), lambda i: (i, 0)) for w, _ in outs]
        out_specs += [pl.BlockSpec((k, w), lambda i: (0, 0)) for k, w in par_outs]
        row_axis = 0
        sem = ("arbitrary",) if par_outs else ("parallel",)
    else:
        grid = (outs[0][0] // tn, S // tm)
        in_specs = [pl.BlockSpec((tm, tn), lambda j, i, o=o: (i, j + o)) for _, o in rows]
        in_specs += [pl.BlockSpec((a.shape[0], tn), lambda j, i, o=o: (0, j + o)) for a, o in params]
        out_specs = [pl.BlockSpec((tm, tn), lambda j, i: (i, j)) for _ in outs]
        out_specs += [pl.BlockSpec((k, tn), lambda j, i: (0, j)) for k, _ in par_outs]
        row_axis = 1
        sem = ("parallel", "arbitrary") if par_outs else ("parallel", "parallel")
    out_shape = [jax.ShapeDtypeStruct((S, w), d) for w, d in outs]
    out_shape += [jax.ShapeDtypeStruct((k, w), f32) for k, w in par_outs]

    def body(*refs):
        res = fn(*[r[...] for r in refs[:n_in]])
        for r, v in zip(refs[n_in:n_in + n_o], res[:n_o]):
            r[...] = v.astype(r.dtype)
        if par_outs:
            acc_refs = refs[n_in + n_o:]

            @pl.when(pl.program_id(row_axis) == 0)
            def _():
                for r in acc_refs:
                    r[...] = jnp.zeros_like(r)

            for r, v in zip(acc_refs, res[n_o:]):
                r[...] += v

    return pl.pallas_call(
        body, name=name, grid=grid, in_specs=in_specs, out_specs=out_specs, out_shape=out_shape,
        compiler_params=_cp(*sem))(*[a for a, _ in rows], *[a for a, _ in params])


def _stage_fwd(name, fn, rows, params, outs, *, tm, tn=None):
    n_r = len(rows)

    def ffn(*t):
        return fn(*[v.astype(f32) for v in t[:n_r]], *t[n_r:])

    return _stage(name, ffn, rows, params, outs, tm=tm, tn=tn)


def _stage_bwd(name, fn, rows, params, cts, drow, *, tm, tn=None):
    rows = [_pair(r) for r in rows]
    params = [_pair(p) for p in params]
    n_r, n_p = len(rows), len(params)
    flat_ct = [_pair(c) for group in cts for c in group]
    counts = [len(group) for group in cts]
    need = [i for i, d in enumerate(drow) if d is not None]

    def bfn(*t):
        r = [v.astype(f32) for v in t[:n_r]]
        c = t[n_r:n_r + len(flat_ct)]
        p = list(t[n_r + len(flat_ct):])
        res, vjp = jax.vjp(fn, *r, *p)
        ct, pos = [], 0
        for o, k in zip(res, counts):
            s = c[pos].astype(f32)
            for e in range(1, k):
                s = s + c[pos + e].astype(f32)
            pos += k
            ct.append(s.astype(o.dtype))
        g = vjp(tuple(ct))
        return tuple(g[i] for i in need) + tuple(g[n_r:])

    if tn is None:
        outs = [(rows[i][0].shape[1], drow[i]) for i in need]
        par_outs = [p.shape for p, _ in params]
    else:
        w_all = flat_ct[0][0].shape[1]
        outs = [(w_all, drow[i]) for i in need]
        par_outs = [(p.shape[0], w_all) for p, _ in params]
    res = _stage(name, bfn, rows + flat_ct, params, outs, par_outs, tm=tm, tn=tn)
    return res[:len(need)], res[len(need):]


def _mm(name, dims, a, b, grid, a_spec, b_spec, o_spec, out_shape, acc_shape, a_fn=None, tail=None, dep=None):
    nk = grid[2]
    a_list = list(a) if isinstance(a, (list, tuple)) else [a]
    na = len(a_list)
    t_fn, t_arrays = tail if tail is not None else (None, [])
    ne = len(t_arrays)
    deps = [] if dep is None else [dep]
    multi = isinstance(out_shape, (list, tuple))

    n_out = len(out_shape) if multi else 1

    def body(*refs):
        a_refs, b_ref, t_refs = refs[:na], refs[na], refs[na + 1:na + 1 + ne]
        first_out = na + 1 + ne + len(deps)
        o_refs = refs[first_out:first_out + n_out]
        k = pl.program_id(2)

        def prod():
            lhs = a_refs[0][...] if a_fn is None else a_fn(*[r[...] for r in a_refs])
            return lax.dot_general(lhs.astype(bf16), b_ref[...].astype(bf16), dims, preferred_element_type=f32)

        def finish(total):
            res = (total,) if t_fn is None else t_fn(total, *[r[...] for r in t_refs])
            for r, val in zip(o_refs, res):
                r[...] = val.astype(r.dtype)

        if nk == 1:
            finish(prod())
        else:
            acc = refs[-1]

            @pl.when(k == 0)
            def _():
                acc[...] = prod()

            if nk > 2:
                @pl.when(jnp.logical_and(k > 0, k < nk - 1))
                def _():
                    acc[...] += prod()

            @pl.when(k == nk - 1)
            def _():
                finish(acc[...] + prod())

    return pl.pallas_call(
        body, name=name, grid=grid,
        in_specs=[a_spec] * na + [b_spec] + [o_spec] * ne + [pl.BlockSpec(memory_space=pl.ANY)] * len(deps),
        out_specs=[o_spec] * len(out_shape) if multi else o_spec, out_shape=out_shape,
        scratch_shapes=[pltpu.VMEM(acc_shape, f32)] if nk > 1 else [],
        compiler_params=_cp("parallel", "parallel", "arbitrary"))(*a_list, b, *t_arrays, *deps)


def _mm_tiles(S, roomy=True):
    return _tile(S, 1024 if roomy else 512)


def join_in_proj(name, parts, width_main):
    P, D, CI = parts.shape
    rest = P * CI - width_main
    tr = _tile(D, 128, 16)

    def body(p_ref, m_ref, d_ref):
        for j in range(P - 1):
            m_ref[:, j * CI:(j + 1) * CI] = p_ref[j]
        m_ref[:, (P - 1) * CI:width_main] = p_ref[P - 1][:, :CI - rest]
        d_ref[...] = p_ref[P - 1][:, CI - rest:]

    return pl.pallas_call(
        body, name=name, grid=(D // tr,),
        in_specs=[pl.BlockSpec((P, tr, CI), lambda i: (0, i, 0))],
        out_specs=[pl.BlockSpec((tr, width_main), lambda i: (i, 0)), pl.BlockSpec((tr, rest), lambda i: (i, 0))],
        out_shape=[jax.ShapeDtypeStruct((D, width_main), parts.dtype), jax.ShapeDtypeStruct((D, rest), parts.dtype)],
        compiler_params=_cp("parallel"))(parts)


def split_in_proj(name, main, rest, P):
    D, WM = main.shape
    nr = rest.shape[1]
    CI = (WM + nr) // P
    tr = _tile(D, 128, 16)

    def body(m_ref, d_ref, p_ref):
        for j in range(P - 1):
            p_ref[j] = m_ref[:, j * CI:(j + 1) * CI]
        p_ref[P - 1, :, :CI - nr] = m_ref[:, (P - 1) * CI:WM]
        p_ref[P - 1, :, CI - nr:] = d_ref[...]

    return pl.pallas_call(
        body, name=name, grid=(D // tr,),
        in_specs=[pl.BlockSpec((tr, WM), lambda i: (i, 0)), pl.BlockSpec((tr, nr), lambda i: (i, 0))],
        out_specs=pl.BlockSpec((P, tr, CI), lambda i: (0, i, 0)),
        out_shape=jax.ShapeDtypeStruct((P, D, CI), main.dtype),
        compiler_params=_cp("parallel"))(main, rest)


def mm_nn_col(name, a, W, li, out_dtype=f32, dep=None):
    P, _, K, C = W.shape
    S = a.shape[0]
    tm, tn, tk = _mm_tiles(S), _tile(C, 1536), _tile(K, 2048)
    nc = C // tn
    return _mm(name, NN, a, W, (S // tm, P * nc, K // tk),
               pl.BlockSpec((tm, tk), lambda i, j, k: (i, k)),
               pl.BlockSpec((None, None, tk, tn), lambda i, j, k: (j // nc, li, k, j % nc)),
               pl.BlockSpec((tm, tn), lambda i, j, k: (i, j)),
               jax.ShapeDtypeStruct((S, P * C), out_dtype), (tm, tn), dep=dep)


def mm_nn_row(name, a, W, li, out_dtype=f32, a_fn=None):
    P, _, R, N = W.shape
    S = (a[0] if a_fn is not None else a).shape[0]
    tm, tn, tk = _mm_tiles(S, a_fn is None), _tile(N, 1024), _tile(R, 2048)
    nr = R // tk
    return _mm(name, NN, a, W, (S // tm, N // tn, P * nr),
               pl.BlockSpec((tm, tk), lambda i, j, k: (i, k)),
               pl.BlockSpec((None, None, tk, tn), lambda i, j, k: (k // nr, li, k % nr, j)),
               pl.BlockSpec((tm, tn), lambda i, j, k: (i, j)),
               jax.ShapeDtypeStruct((S, N), out_dtype), (tm, tn), a_fn=a_fn)


def mm_nt_col(name, dy, W, li, out_dtype=f32):
    P, _, K, C = W.shape
    S = dy.shape[0]
    tm, tn, tk = _mm_tiles(S), _tile(K, 1024), _tile(C, 2048)
    nc = C // tk
    return _mm(name, NT, dy, W, (S // tm, K // tn, P * nc),
               pl.BlockSpec((tm, tk), lambda i, j, k: (i, k)),
               pl.BlockSpec((None, None, tn, tk), lambda i, j, k: (k // nc, li, j, k % nc)),
               pl.BlockSpec((tm, tn), lambda i, j, k: (i, j)),
               jax.ShapeDtypeStruct((S, K), out_dtype), (tm, tn))


def mm_nt_row(name, dy, W, li, out_dtype=f32, tail=None, n_out=None):
    P, _, R, N = W.shape
    S = dy.shape[0]
    tm, tn, tk = _mm_tiles(S, tail is None), _tile(R, 1536), _tile(N, 2048)
    nr = R // tn
    out = jax.ShapeDtypeStruct((S, P * R), out_dtype)
    return _mm(name, NT, dy, W, (S // tm, P * nr, N // tk),
               pl.BlockSpec((tm, tk), lambda i, j, k: (i, k)),
               pl.BlockSpec((None, None, tn, tk), lambda i, j, k: (j // nr, li, j % nr, k)),
               pl.BlockSpec((tm, tn), lambda i, j, k: (i, j)),
               out if n_out is None else [out] * n_out, (tm, tn), tail=tail)


def mm_tn_col(name, a, dy, P):
    S, K = a.shape
    C = dy.shape[1] // P
    tm, tn, tk = _tile(K, 1024), _tile(C, 1536), _tile(S, 2048)
    nc = C // tn
    return _mm(name, TN, a, dy, (K // tm, P * nc, S // tk),
               pl.BlockSpec((tk, tm), lambda i, j, k: (k, i)),
               pl.BlockSpec((tk, tn), lambda i, j, k: (k, j)),
               pl.BlockSpec((None, tm, tn), lambda i, j, k: (j // nc, i, j % nc)),
               jax.ShapeDtypeStruct((P, K, C), bf16), (tm, tn))


def mm_tn_row(name, a, dy, P, a_fn=None):
    S, N = dy.shape
    R = (a[0] if a_fn is not None else a).shape[1] // P
    tm, tn, tk = _tile(R, 1536), _tile(N, 1024), _tile(S, 2048 if a_fn is None else 1024)
    nr = R // tm
    return _mm(name, TN, a, dy, (P * nr, N // tn, S // tk),
               pl.BlockSpec((tk, tm), lambda i, j, k: (k, i)),
               pl.BlockSpec((tk, tn), lambda i, j, k: (k, j)),
               pl.BlockSpec((None, tm, tn), lambda i, j, k: (i // nr, i % nr, j)),
               jax.ShapeDtypeStruct((P, R, N), bf16), (tm, tn), a_fn=a_fn)


def _hgrn_chunk(st, q, fp, v, gt, lb, an):
    C = q.shape[0]
    sig = _sigmoid(fp)
    f = lb + (1.0 - lb) * sig
    kk = (1.0 - lb) * (1.0 - sig)
    g = jnp.log(jnp.maximum(f, A_F_MIN))
    qs = _silu(q)
    row = lax.broadcasted_iota(jnp.int32, (C, C), 0)
    col = lax.broadcasted_iota(jnp.int32, (C, C), 1)
    tri = (col <= row).astype(f32)
    b = jnp.dot(tri, g, precision=HI, preferred_element_type=f32)
    o_inter = lax.dot_general((qs * jnp.exp(b)).astype(bf16), st.astype(bf16), NT,
                              preferred_element_type=f32)
    T, NB = A_SUB, C // A_SUB
    refs = [b[i * T + A_REF:i * T + A_REF + 1, :] for i in range(NB)]
    ref_q = jnp.concatenate([jnp.broadcast_to(r, (T, A_HEAD)) for r in refs], axis=0)
    ref_k = jnp.concatenate([jnp.broadcast_to(r, (C, A_HEAD)) for r in refs], axis=0)
    q_t = qs * jnp.exp(b - ref_q)
    k_t = jnp.concatenate([kk] * NB, axis=0) * jnp.exp(
        jnp.minimum(ref_k - jnp.concatenate([b] * NB, axis=0), A_EXP_CAP))
    s = lax.dot_general(q_t.astype(bf16), k_t.astype(bf16), NT, preferred_element_type=f32)
    trow = lax.broadcasted_iota(jnp.int32, (C, NB * C), 0)
    scol = lax.broadcasted_iota(jnp.int32, (C, NB * C), 1)
    keep = jnp.logical_and(scol // C == trow // T, scol % C <= trow)
    s = jnp.where(keep, s, 0.0)
    o_intra = jnp.dot(s.astype(bf16), jnp.concatenate([v.astype(bf16)] * NB, axis=0),
                      preferred_element_type=f32)
    bl = b[C - 1:C, :]
    kd = kk * jnp.exp(bl - b)
    st_new = st * jnp.exp(bl) + lax.dot_general(v.astype(bf16), kd.astype(bf16), TN,
                                                preferred_element_type=f32)
    o = o_inter + o_intra
    y = o * lax.rsqrt(jnp.mean(o * o, axis=-1, keepdims=True) + RMS_EPS) * an * _silu(gt)
    return st_new, y


def _hgrn_heads_per_step(HA):
    return A_HEADS_PER_STEP if HA % A_HEADS_PER_STEP == 0 else 1


def _hgrn_in_specs(HA, HP, cidx):
    W = HP * A_HEAD
    specs = [pl.BlockSpec((A_CHUNK, W), lambda h, c, s=s: (cidx(c), s * (HA // HP) + h)) for s in range(4)]
    specs += [pl.BlockSpec((1, W), lambda h, c: (0, h))] * 2
    return specs


def _head(ref, j):
    return ref[:, j * A_HEAD:(j + 1) * A_HEAD]


def hgrn_fwd(name, ymain, lb, an, D):
    S = ymain.shape[0]
    HA, nc = D // A_HEAD, S // A_CHUNK
    HP = _hgrn_heads_per_step(HA)
    W = HP * A_HEAD

    def body(q, fp, v, gt, lb_ref, an_ref, o_ref, sv_ref, st):
        @pl.when(pl.program_id(1) == 0)
        def _():
            st[...] = jnp.zeros_like(st)

        sv_ref[...] = st[...]
        for j in range(HP):
            st_new, y = _hgrn_chunk(st[j], _head(q, j), _head(fp, j), _head(v, j), _head(gt, j),
                                    _head(lb_ref, j), _head(an_ref, j))
            st[j] = st_new
            o_ref[:, j * A_HEAD:(j + 1) * A_HEAD] = y.astype(o_ref.dtype)

    return pl.pallas_call(
        body, name=name, grid=(HA // HP, nc),
        in_specs=_hgrn_in_specs(HA, HP, lambda c: c),
        out_specs=[pl.BlockSpec((A_CHUNK, W), lambda h, c: (c, h)),
                   pl.BlockSpec((HP, None, A_HEAD, A_HEAD), lambda h, c: (h, c, 0, 0))],
        out_shape=[jax.ShapeDtypeStruct((S, D), bf16),
                   jax.ShapeDtypeStruct((HA, nc, A_HEAD, A_HEAD), f32)],
        scratch_shapes=[pltpu.VMEM((HP, A_HEAD, A_HEAD), f32)],
        compiler_params=_cp("parallel", "arbitrary"))(ymain, ymain, ymain, ymain, lb, an)


def hgrn_bwd(name, ymain, lb, an, saved, dmixed, D):
    S = ymain.shape[0]
    HA, nc = D // A_HEAD, S // A_CHUNK
    HP = _hgrn_heads_per_step(HA)
    W = HP * A_HEAD
    rev = lambda c: nc - 1 - c

    def body(q, fp, v, gt, lb_ref, an_ref, sv_ref, do_ref, dq, df, dv, dg, dlb, dan, dst):
        @pl.when(pl.program_id(1) == 0)
        def _():
            dst[...] = jnp.zeros_like(dst)
            dlb[...] = jnp.zeros_like(dlb)
            dan[...] = jnp.zeros_like(dan)

        for j in range(HP):
            cols = slice(j * A_HEAD, (j + 1) * A_HEAD)
            _, vjp = jax.vjp(_hgrn_chunk, sv_ref[j], _head(q, j), _head(fp, j), _head(v, j), _head(gt, j),
                             _head(lb_ref, j), _head(an_ref, j))
            g = vjp((dst[j], _head(do_ref, j).astype(f32)))
            dst[j] = g[0]
            for r, x in zip((dq, df, dv, dg), g[1:5]):
                r[:, cols] = x.astype(r.dtype)
            dlb[:, cols] += g[5]
            dan[:, cols] += g[6]

    blk = pl.BlockSpec((A_CHUNK, W), lambda h, c: (rev(c), h))
    vec = pl.BlockSpec((1, W), lambda h, c: (0, h))
    return pl.pallas_call(
        body, name=name, grid=(HA // HP, nc),
        in_specs=_hgrn_in_specs(HA, HP, rev) + [
            pl.BlockSpec((HP, None, A_HEAD, A_HEAD), lambda h, c: (h, rev(c), 0, 0)), blk],
        out_specs=[blk] * 4 + [vec] * 2,
        out_shape=[jax.ShapeDtypeStruct((S, D), bf16)] * 4 + [jax.ShapeDtypeStruct((1, D), f32)] * 2,
        scratch_shapes=[pltpu.VMEM((HP, A_HEAD, A_HEAD), f32)],
        compiler_params=_cp("parallel", "arbitrary"))(ymain, ymain, ymain, ymain, lb, an, saved, dmixed)


def _ssd_chunk(hp, xs, bm, cm, z, dtr, dtb, alog, dsk, bn, g, R):
    L, GW = xs.shape
    HB = dtr.shape[1]
    R8 = max(R, 8)
    dt = jax.nn.softplus(dtr + dtb)
    a = -jnp.exp(alog)
    row = lax.broadcasted_iota(jnp.int32, (L, L), 0)
    col = lax.broadcasted_iota(jnp.int32, (L, L), 1)
    causal = col <= row
    cs = jnp.dot(causal.astype(f32), dt * a, precision=HI, preferred_element_type=f32)
    eh = lax.broadcasted_iota(jnp.int32, (HB, GW), 0)
    ec = lax.broadcasted_iota(jnp.int32, (HB, GW), 1)
    spread = (eh == g * R + ec // B_HEAD).astype(f32)
    sh = lax.broadcasted_iota(jnp.int32, (R8, HB), 1)
    sr = lax.broadcasted_iota(jnp.int32, (R8, HB), 0)
    pick_t = jnp.logical_and(sh == g * R + sr, sr < R).astype(f32)
    dtf = jnp.dot(dt, spread, precision=HI, preferred_element_type=f32)
    csf = jnp.dot(cs, spread, precision=HI, preferred_element_type=f32)
    dsf = jnp.dot(jnp.broadcast_to(dsk, (8, HB)), spread, precision=HI, preferred_element_type=f32)[0:1, :]
    cs_col = lax.dot_general(cs, pick_t, NT, precision=HI, preferred_element_type=f32)
    cs_row = lax.dot_general(pick_t, cs, NT, precision=HI, preferred_element_type=f32)
    xdt = xs * dtf
    cb = lax.dot_general(cm.astype(bf16), bm.astype(bf16), NT, preferred_element_type=f32)
    lane_head = lax.broadcasted_iota(jnp.int32, (1, GW), 1) // B_HEAD
    y = jnp.zeros((L, GW), f32)
    for r in range(R):
        seg = cs_col[:, r:r + 1] - cs_row[r:r + 1, :]
        dec = jnp.where(causal, jnp.exp(jnp.where(causal, seg, 0.0)), 0.0)
        xm = jnp.where(lane_head == r, xdt, 0.0)
        y = y + jnp.dot((cb * dec).astype(bf16), xm.astype(bf16), preferred_element_type=f32)
    csl = csf[L - 1:L, :]
    dte = jnp.exp(csl - csf)
    states = lax.dot_general(bm.astype(bf16), (xdt * dte).astype(bf16), TN, preferred_element_type=f32)
    y_off = jnp.dot(cm.astype(bf16), hp.astype(bf16), preferred_element_type=f32) * jnp.exp(csf)
    hn = hp * jnp.exp(csl) + states
    gated = (y + y_off + dsf * xs) * _silu(z)
    out = gated * lax.rsqrt(jnp.mean(gated * gated, axis=-1, keepdims=True) + RMS_EPS) * bn
    return hn, out


def _ssd_in_specs(D, HB, cidx):
    L, GW, N = B_CHUNK, D // B_GROUPS, B_STATE
    zoff, boff = 4 * D // GW, D // N
    return [
        pl.BlockSpec((L, GW), lambda c, g: (cidx(c), g)),
        pl.BlockSpec((L, N), lambda c, g: (cidx(c), boff + g)),
        pl.BlockSpec((L, N), lambda c, g: (cidx(c), boff + B_GROUPS + g)),
        pl.BlockSpec((L, GW), lambda c, g: (cidx(c), zoff + g)),
        pl.BlockSpec((L, HB), lambda c, g: (cidx(c), 0)),
        pl.BlockSpec((1, HB), lambda c, g: (0, 0)),
        pl.BlockSpec((1, HB), lambda c, g: (0, 0)),
        pl.BlockSpec((1, HB), lambda c, g: (0, 0)),
        pl.BlockSpec((1, GW), lambda c, g: (0, g)),
    ]


def ssd_fwd(name, xact, ymain, dtr, dtb, alog, dsk, bn, D):
    S, HB = dtr.shape
    nc, GW, R = S // B_CHUNK, D // B_GROUPS, HB // B_GROUPS

    def body(xs, bm, cm, z, dt_ref, dtb_ref, al_ref, ds_ref, bn_ref, o_ref, sv_ref, hs):
        g = pl.program_id(1)

        @pl.when(pl.program_id(0) == 0)
        def _():
            hs[g] = jnp.zeros((B_STATE, GW), f32)

        hp = hs[g]
        sv_ref[...] = hp
        hn, out = _ssd_chunk(hp, xs[...], bm[...], cm[...], z[...], dt_ref[...], dtb_ref[...], al_ref[...],
                             ds_ref[...], bn_ref[...], g, R)
        hs[g] = hn
        o_ref[...] = out.astype(o_ref.dtype)

    return pl.pallas_call(
        body, name=name, grid=(nc, B_GROUPS),
        in_specs=_ssd_in_specs(D, HB, lambda c: c),
        out_specs=[pl.BlockSpec((B_CHUNK, GW), lambda c, g: (c, g)),
                   pl.BlockSpec((None, None, B_STATE, GW), lambda c, g: (c, g, 0, 0))],
        out_shape=[jax.ShapeDtypeStruct((S, D), bf16),
                   jax.ShapeDtypeStruct((nc, B_GROUPS, B_STATE, GW), f32)],
        scratch_shapes=[pltpu.VMEM((B_GROUPS, B_STATE, GW), f32)],
        compiler_params=_cp("arbitrary", "arbitrary"))(xact, xact, xact, ymain, dtr, dtb, alog, dsk, bn)


def ssd_bwd(name, xact, ymain, dtr, dtb, alog, dsk, bn, saved, dmixed, D):
    S, HB = dtr.shape
    nc, GW, R = S // B_CHUNK, D // B_GROUPS, HB // B_GROUPS
    rev = lambda c: nc - 1 - c
    ooff = D // GW

    def body(xs, bm, cm, z, dt_ref, dtb_ref, al_ref, ds_ref, bn_ref, sv_ref, do_ref,
             dxs, dbm, dcm, dz, ddt, ddtb, dal, dds, dbn, dhs):
        c, g = pl.program_id(0), pl.program_id(1)

        @pl.when(c == 0)
        def _():
            dhs[g] = jnp.zeros((B_STATE, GW), f32)
            dbn[g] = jnp.zeros((1, GW), f32)

        @pl.when(jnp.logical_and(c == 0, g == 0))
        def _():
            ddtb[...] = jnp.zeros_like(ddtb)
            dal[...] = jnp.zeros_like(dal)
            dds[...] = jnp.zeros_like(dds)

        @pl.when(g == 0)
        def _():
            ddt[...] = jnp.zeros_like(ddt)

        fn = functools.partial(_ssd_chunk, g=g, R=R)
        _, vjp = jax.vjp(fn, sv_ref[...], xs[...], bm[...], cm[...], z[...], dt_ref[...], dtb_ref[...],
                         al_ref[...], ds_ref[...], bn_ref[...])
        gr = vjp((dhs[g], do_ref[...].astype(f32)))
        dhs[g] = gr[0]
        dxs[...] = gr[1]
        dbm[...] = gr[2]
        dcm[...] = gr[3]
        dz[...] = gr[4].astype(dz.dtype)
        ddt[...] += gr[5]
        ddtb[...] += gr[6]
        dal[...] += gr[7]
        dds[...] += gr[8]
        dbn[g] += gr[9]

    hb_vec = pl.BlockSpec((1, HB), lambda c, g: (0, 0))
    return pl.pallas_call(
        body, name=name, grid=(nc, B_GROUPS),
        in_specs=_ssd_in_specs(D, HB, rev) + [
            pl.BlockSpec((None, None, B_STATE, GW), lambda c, g: (rev(c), g, 0, 0)),
            pl.BlockSpec((B_CHUNK, GW), lambda c, g: (rev(c), ooff + g))],
        out_specs=[pl.BlockSpec((B_CHUNK, GW), lambda c, g: (rev(c), g)),
                   pl.BlockSpec((B_CHUNK, B_STATE), lambda c, g: (rev(c), g)),
                   pl.BlockSpec((B_CHUNK, B_STATE), lambda c, g: (rev(c), g)),
                   pl.BlockSpec((B_CHUNK, GW), lambda c, g: (rev(c), g)),
                   pl.BlockSpec((B_CHUNK, HB), lambda c, g: (rev(c), 0)),
                   hb_vec, hb_vec, hb_vec,
                   pl.BlockSpec((B_GROUPS, 1, GW), lambda c, g: (0, 0, 0))],
        out_shape=[jax.ShapeDtypeStruct((S, D), f32),
                   jax.ShapeDtypeStruct((S, B_GROUPS * B_STATE), f32),
                   jax.ShapeDtypeStruct((S, B_GROUPS * B_STATE), f32),
                   jax.ShapeDtypeStruct((S, D), bf16),
                   jax.ShapeDtypeStruct((S, HB), f32),
                   jax.ShapeDtypeStruct((1, HB), f32), jax.ShapeDtypeStruct((1, HB), f32),
                   jax.ShapeDtypeStruct((1, HB), f32),
                   jax.ShapeDtypeStruct((B_GROUPS, 1, GW), f32)],
        scratch_shapes=[pltpu.VMEM((B_GROUPS, B_STATE, GW), f32)],
        compiler_params=_cp("arbitrary", "arbitrary"))(
            xact, xact, xact, ymain, dtr, dtb, alog, dsk, bn, saved, dmixed)


def _conv_taps(xp, w_ref, b_ref, r0, K):
    acc = jnp.broadcast_to(b_ref[...], (CONV_ROWS, b_ref.shape[1]))
    for k in range(K):
        acc = acc + w_ref[k:k + 1, :] * xp[r0 + CONV_PAD - (K - 1) + k:r0 + CONV_PAD - (K - 1) + k + CONV_ROWS, :]
    return acc


def conv_fwd(name, x, xoff, w, b, act, out_dtype):
    S = x.shape[0]
    K, CW = w.shape
    tc = CONV_CH

    def body(x_ref, w_ref, b_ref, o_ref, xp):
        xp[0:CONV_PAD, :] = jnp.zeros((CONV_PAD, tc), f32)
        xp[CONV_PAD:CONV_PAD + S, :] = x_ref[...].astype(f32)
        for r0 in range(0, S, CONV_ROWS):
            acc = _conv_taps(xp, w_ref, b_ref, r0, K)
            if act:
                acc = _silu(acc)
            o_ref[r0:r0 + CONV_ROWS, :] = acc.astype(o_ref.dtype)

    return pl.pallas_call(
        body, name=name, grid=(CW // tc,),
        in_specs=[pl.BlockSpec((S, tc), lambda j: (0, j + xoff)),
                  pl.BlockSpec((K, tc), lambda j: (0, j)),
                  pl.BlockSpec((1, tc), lambda j: (0, j))],
        out_specs=pl.BlockSpec((S, tc), lambda j: (0, j)),
        out_shape=jax.ShapeDtypeStruct((S, CW), out_dtype),
        scratch_shapes=[pltpu.VMEM((S + CONV_PAD, tc), f32)],
        compiler_params=_cp("parallel"))(x, w, b)


def conv_bwd(name, x, xoff, w, b, dout, act, dx_dtype):
    S = x.shape[0]
    K, CW = w.shape
    tc = CONV_CH

    def body(x_ref, w_ref, b_ref, d_ref, dx_ref, dw_ref, db_ref, xp, dp):
        xp[0:CONV_PAD, :] = jnp.zeros((CONV_PAD, tc), f32)
        xp[CONV_PAD:CONV_PAD + S, :] = x_ref[...].astype(f32)
        dp[S:S + CONV_PAD, :] = jnp.zeros((CONV_PAD, tc), f32)
        db = jnp.zeros((1, tc), f32)
        for r0 in range(0, S, CONV_ROWS):
            d = d_ref[r0:r0 + CONV_ROWS, :].astype(f32)
            if act:
                pre = _conv_taps(xp, w_ref, b_ref, r0, K)
                s = _sigmoid(pre)
                d = d * (s + pre * s * (1.0 - s))
            dp[r0:r0 + CONV_ROWS, :] = d
            db = db + jnp.sum(d, axis=0, keepdims=True)
        db_ref[...] = db
        for r0 in range(0, S, CONV_ROWS):
            acc = jnp.zeros((CONV_ROWS, tc), f32)
            for k in range(K):
                acc = acc + w_ref[k:k + 1, :] * dp[r0 + (K - 1 - k):r0 + (K - 1 - k) + CONV_ROWS, :]
            dx_ref[r0:r0 + CONV_ROWS, :] = acc.astype(dx_ref.dtype)
        for k in range(K):
            acc = jnp.zeros((1, tc), f32)
            for r0 in range(0, S, CONV_ROWS):
                lo = r0 + CONV_PAD - (K - 1) + k
                acc = acc + jnp.sum(dp[r0:r0 + CONV_ROWS, :] * xp[lo:lo + CONV_ROWS, :], axis=0, keepdims=True)
            dw_ref[k:k + 1, :] = acc

    return pl.pallas_call(
        body, name=name, grid=(CW // tc,),
        in_specs=[pl.BlockSpec((S, tc), lambda j: (0, j + xoff)),
                  pl.BlockSpec((K, tc), lambda j: (0, j)),
                  pl.BlockSpec((1, tc), lambda j: (0, j)),
                  pl.BlockSpec((S, tc), lambda j: (0, j))],
        out_specs=[pl.BlockSpec((S, tc), lambda j: (0, j)),
                   pl.BlockSpec((K, tc), lambda j: (0, j)),
                   pl.BlockSpec((1, tc), lambda j: (0, j))],
        out_shape=[jax.ShapeDtypeStruct((S, CW), dx_dtype),
                   jax.ShapeDtypeStruct((K, CW), f32),
                   jax.ShapeDtypeStruct((1, CW), f32)],
        scratch_shapes=[pltpu.VMEM((S + CONV_PAD, tc), f32), pltpu.VMEM((S + CONV_PAD, tc), f32)],
        compiler_params=_cp("parallel"))(x, w, b, dout)


def loss_head(name, y, target, tm):
    S, D = y.shape

    def body(y_ref, t_ref, dy_ref, l_ref):
        @pl.when(pl.program_id(0) == 0)
        def _():
            l_ref[...] = jnp.zeros_like(l_ref)

        err = y_ref[...] - t_ref[...]
        dy_ref[...] = err * (1.0 / D)
        l_ref[...] += jnp.sum(err * err) * (0.5 / D)

    dy, l = pl.pallas_call(
        body, name=name, grid=(S // tm,),
        in_specs=[pl.BlockSpec((tm, D), lambda i: (i, 0))] * 2,
        out_specs=[pl.BlockSpec((tm, D), lambda i: (i, 0)), pl.BlockSpec((8, LANES), lambda i: (0, 0))],
        out_shape=[jax.ShapeDtypeStruct((S, D), f32), jax.ShapeDtypeStruct((8, LANES), f32)],
        compiler_params=_cp("arbitrary"))(y, target)
    return dy, l[0, 0]


def _flat2d_tiles(rows, cols, itemsize, target_bytes):
    tc = _tile(cols, 1024) if cols % LANES == 0 else cols
    cap = max(8, target_bytes // (tc * itemsize))
    tr = _tile(rows, cap, 16) if rows % 16 == 0 else rows
    return tr, tc


def adamw(name, w, g, m, v):
    shape = w.shape
    cols = shape[-1]
    rows = math.prod(shape[:-1])
    tr, tc = _flat2d_tiles(rows, cols, 4, 1 << 20)
    c1 = 1.0 - ADAM_B1 ** ADAM_STEP
    c2 = 1.0 - ADAM_B2 ** ADAM_STEP

    def body(w_ref, g_ref, m_ref, v_ref, d_ref, nm_ref, nv_ref, g_out_ref):
        gg = g_ref[...]
        nm = ADAM_B1 * m_ref[...] + (1.0 - ADAM_B1) * gg
        nv = ADAM_B2 * v_ref[...] + (1.0 - ADAM_B2) * (gg * gg)
        d_ref[...] = -ADAM_LR * ((nm / c1) / (jnp.sqrt(nv / c2) + ADAM_EPS) + ADAM_WD * w_ref[...])
        nm_ref[...] = nm
        nv_ref[...] = nv
        g_out_ref[...] = gg

    spec = pl.BlockSpec((tr, tc), lambda i, j: (i, j))
    outs = pl.pallas_call(
        body, name=name, grid=(rows // tr, cols // tc), in_specs=[spec] * 4, out_specs=[spec] * 4,
        out_shape=[jax.ShapeDtypeStruct((rows, cols), f32)] * 4,
        compiler_params=_cp("parallel", "parallel"))(*[a.reshape(rows, cols) for a in (w, g, m, v)])
    return [o.reshape(shape) for o in outs]


def _core_index():
    return lax.axis_index("c").astype(jnp.int32).reshape(1)


def _half_rows_tile(Rh, C):
    return _tile(Rh, max(16, (2 << 20) // (C * 2)), 16)


def rs_add(name, G, buf):
    P, _, Rh, C = G.shape
    tr = _half_rows_tile(Rh, C)

    def body(c_ref, g_ref, b_ref, o_ref):
        o_ref[...] = (g_ref[...].astype(f32) + b_ref[...].astype(f32)).astype(o_ref.dtype)

    return pl.pallas_call(
        body, name=name,
        grid_spec=pltpu.PrefetchScalarGridSpec(
            num_scalar_prefetch=1, grid=(P, Rh // tr),
            in_specs=[pl.BlockSpec((None, None, tr, C), lambda p, i, c: (p, c[0], i, 0)),
                      pl.BlockSpec((None, tr, C), lambda p, i, c: (p, i, 0))],
            out_specs=pl.BlockSpec((None, tr, C), lambda p, i, c: (p, i, 0))),
        out_shape=jax.ShapeDtypeStruct((P, Rh, C), bf16),
        compiler_params=_cp("parallel", "parallel"))(_core_index(), G, buf)


def _chip_indices():
    x, y, c = lax.axis_index("x"), lax.axis_index("y"), lax.axis_index("c")
    ids = [2 * x + y] + [2 * _flip(x, fx) + _flip(y, fy) for fx, fy in _CHIP_FLIPS] + [c]
    return [i.astype(jnp.int32).reshape(1) for i in ids]


def rs_sum4(name, pair, buf, final, layer):
    P, Rh, C = buf.shape
    tr = _half_rows_tile(Rh, C)

    def body(i0, i1, i2, i3, ic, b0, b1, b2, b3, f_ref, o_ref):
        o_ref[...] = ((b0[...].astype(f32) + b1[...].astype(f32)) + b2[...].astype(f32)) + b3[...].astype(f32)

    blk = (None, tr, C)
    return pl.pallas_call(
        body, name=name,
        grid_spec=pltpu.PrefetchScalarGridSpec(
            num_scalar_prefetch=5, grid=(Rh // tr,),
            in_specs=[pl.BlockSpec(blk, lambda i, *ids, k=k: (ids[k][0], i, 0)) for k in range(P)] + [_ANY],
            out_specs=pl.BlockSpec((None, None, tr, C), lambda i, *ids: (layer, ids[4][0], i, 0))),
        out_shape=jax.ShapeDtypeStruct(final.shape, final.dtype),
        input_output_aliases={9: 0},
        compiler_params=_cp("parallel"))(*_chip_indices(), pair, buf, buf, buf, final)


def place_own(name, w, layer):
    _, R, C = w.shape
    tr = _tile(R, max(16, (2 << 20) // (C * 2)), 16)

    def body(q, w_ref, own_ref, land_ref):
        wb = w_ref[...].astype(bf16)
        own_ref[...] = wb
        land_ref[...] = wb

    return pl.pallas_call(
        body, name=name,
        grid_spec=pltpu.PrefetchScalarGridSpec(
            num_scalar_prefetch=1, grid=(R // tr,),
            in_specs=[pl.BlockSpec((None, tr, C), lambda i, q: (layer, i, 0))],
            out_specs=[pl.BlockSpec((tr, C), lambda i, q: (i, 0)),
                       pl.BlockSpec((None, tr, C), lambda i, q: (q[0], i, 0))]),
        out_shape=[jax.ShapeDtypeStruct((R, C), bf16), jax.ShapeDtypeStruct((N_CHIPS, R, C), bf16)],
        compiler_params=_cp("parallel"))(_chip_indices()[0], w)


_ANY = pl.BlockSpec(memory_space=pl.ANY)
_CHIP_FLIPS = ((1, 0), (0, 1), (1, 1))


def _place():
    return lax.axis_index("x"), lax.axis_index("y"), lax.axis_index("c")


def _flip(v, f):
    return 1 - v if f else v


def _remote(src, dst, ssem, rsem, dev):
    return pltpu.make_async_remote_copy(src_ref=src, dst_ref=dst, send_sem=ssem, recv_sem=rsem,
                                        device_id=dev, device_id_type=MESH)


_HBM = pl.BlockSpec(memory_space=pltpu.HBM)
_SEM = pl.BlockSpec(memory_space=pltpu.SEMAPHORE)
_DATAFLOW = pltpu.SideEffectType.DATAFLOW_SIDE_EFFECTING
_TOKEN = jax.ShapeDtypeStruct((8, LANES), f32)


def _in_hbm(a):
    return pltpu.with_memory_space_constraint(a, pltpu.HBM)


def _hbm_like(a):
    return pltpu.HBM(a.shape, a.dtype)


def gather_ici_start(name, groups, dep=None):
    sizes = [len(g) for g in groups]
    owns = [o for g in groups for o, _ in g]
    lands = [l for g in groups for _, l in g]
    n, ng = len(owns), len(groups)
    deps = [] if dep is None else [dep]

    def body(*refs):
        own, land = refs[:n], refs[n:2 * n]
        sems = refs[2 * n + len(deps):2 * n + len(deps) + 2 * ng]
        token = refs[-1]
        x, y, c = _place()
        q = 2 * x + y
        t = 0
        for gi, size in enumerate(sizes):
            for j in range(size):
                for k, (fx, fy) in enumerate(_CHIP_FLIPS):
                    _remote(own[t].at[c], land[t].at[q, c], sems[2 * gi].at[3 * j + k], sems[2 * gi + 1].at[3 * j + k],
                            (_flip(x, fx), _flip(y, fy), c)).start()
                t += 1
        token[...] = jnp.zeros_like(token)

    sem_shapes = [pltpu.SemaphoreType.DMA((3 * size,)) for size in sizes for _ in range(2)]
    res = pl.pallas_call(
        body, name=name,
        in_specs=[_HBM] * (2 * n) + [_ANY] * len(deps),
        out_specs=[_SEM] * (2 * ng) + [_HBM] * (2 * n) + [pl.BlockSpec(memory_space=pltpu.VMEM)],
        out_shape=sem_shapes + [_hbm_like(a) for a in owns + lands] + [_TOKEN],
        input_output_aliases={i: 2 * ng + i for i in range(2 * n)},
        compiler_params=pltpu.CompilerParams(has_side_effects=_DATAFLOW),
    )(*[_in_hbm(a) for a in owns + lands], *deps)
    own_thru, land_thru = res[2 * ng:2 * ng + n], res[2 * ng + n:2 * ng + 2 * n]
    handles, t = [], 0
    for gi, size in enumerate(sizes):
        handles.append((res[2 * gi], res[2 * gi + 1], list(own_thru[t:t + size]), list(land_thru[t:t + size])))
        t += size
    return handles, res[-1]


def gather_ici_wait(name, handle, after):
    send, recv, owns, lands = handle
    n = len(owns)

    def body(*refs):
        own, land = refs[:n], refs[n:2 * n]
        send_ref, recv_ref = refs[2 * n], refs[2 * n + 1]
        x, y, c = _place()
        for j in range(n):
            for k, (fx, fy) in enumerate(_CHIP_FLIPS):
                px, py = _flip(x, fx), _flip(y, fy)
                cp = _remote(own[j].at[c], land[j].at[2 * px + py, c], send_ref.at[3 * j + k], recv_ref.at[3 * j + k],
                             (px, py, c))
                cp.wait_send()
                cp.wait_recv()

    res = pl.pallas_call(
        body, name=name,
        in_specs=[_HBM] * (2 * n) + [_SEM, _SEM, _ANY],
        out_specs=[_HBM] * (2 * n),
        out_shape=[_hbm_like(a) for a in owns + lands],
        input_output_aliases={i: i for i in range(2 * n)},
        compiler_params=pltpu.CompilerParams(has_side_effects=_DATAFLOW),
    )(*owns, *lands, send, recv, after)
    return list(res[n:])


def _split_start(name, body, arrays, n_sems, dep=None):
    n = len(arrays)
    deps = [] if dep is None else [dep]

    def kernel_body(*refs):
        m = n + len(deps)
        body(refs[:n], refs[m], refs[m + 1])
        refs[-1][...] = jnp.zeros_like(refs[-1])

    res = pl.pallas_call(
        kernel_body, name=name,
        in_specs=[_HBM] * n + [_ANY] * len(deps),
        out_specs=[_SEM, _SEM] + [_HBM] * n + [pl.BlockSpec(memory_space=pltpu.VMEM)],
        out_shape=[pltpu.SemaphoreType.DMA((n_sems,)), pltpu.SemaphoreType.DMA((n_sems,))]
        + [_hbm_like(a) for a in arrays] + [_TOKEN],
        input_output_aliases={i: 2 + i for i in range(n)},
        compiler_params=pltpu.CompilerParams(has_side_effects=_DATAFLOW),
    )(*[_in_hbm(a) for a in arrays], *deps)
    return res[0], res[1], list(res[2:2 + n]), res[-1]


def _split_wait(name, body, handle, after):
    send, recv, arrays, _ = handle
    n = len(arrays)

    def kernel_body(*refs):
        body(refs[:n], refs[n], refs[n + 1])

    res = pl.pallas_call(
        kernel_body, name=name,
        in_specs=[_HBM] * n + [_SEM, _SEM, _ANY],
        out_specs=[_HBM] * n,
        out_shape=[_hbm_like(a) for a in arrays],
        input_output_aliases={i: i for i in range(n)},
        compiler_params=pltpu.CompilerParams(has_side_effects=_DATAFLOW),
    )(*arrays, send, recv, after)
    return list(res)


def _forward_copies(land, send, recv):
    x, y, c = _place()
    for t in range(len(land)):
        for k, (fx, fy) in enumerate(_CHIP_FLIPS):
            slab = land[t].at[2 * _flip(x, fx) + _flip(y, fy), c]
            yield _remote(slab, slab, send.at[3 * t + k], recv.at[3 * t + k], (x, y, 1 - c))


def forward_start(name, lands):
    def body(land, send, recv):
        for cp in _forward_copies(land, send, recv):
            cp.start()

    return _split_start(name, body, lands, 3 * len(lands))


def forward_wait(name, handle, after):
    def body(land, send, recv):
        for cp in _forward_copies(land, send, recv):
            cp.wait_send()
            cp.wait_recv()

    return _split_wait(name, body, handle, after)


def sibling_start(name, Gs, dep=None):
    T = len(Gs)
    bufs = [lax.empty((G.shape[0],) + G.shape[2:], G.dtype) for G in Gs]

    def body(refs, send, recv):
        x, y, c = _place()
        for t in range(T):
            _remote(refs[t].at[:, 1 - c], refs[T + t], send.at[t], recv.at[t], (x, y, 1 - c)).start()

    return _split_start(name, body, list(Gs) + bufs, T, dep)


def sibling_wait(name, handle, after):
    T = len(handle[2]) // 2

    def body(refs, send, recv):
        x, y, c = _place()
        for t in range(T):
            cp = _remote(refs[t].at[:, 1 - c], refs[T + t], send.at[t], recv.at[t], (x, y, 1 - c))
            cp.wait_send()
            cp.wait_recv()

    res = _split_wait(name, body, handle, after)
    return res[:T], res[T:]


def reduce_ici_start(name, Ss):
    T = len(Ss)
    lands = [lax.empty(S.shape, S.dtype) for S in Ss]

    def body(*refs):
        s, land = refs[:T], refs[T:2 * T]
        send, recv = refs[2 * T], refs[2 * T + 1]
        token = refs[-1]
        x, y, c = _place()
        q = 2 * x + y
        for t in range(T):
            for k, (fx, fy) in enumerate(_CHIP_FLIPS):
                px, py = _flip(x, fx), _flip(y, fy)
                _remote(s[t].at[2 * px + py], land[t].at[q], send.at[3 * t + k], recv.at[3 * t + k],
                        (px, py, c)).start()
        token[...] = jnp.zeros_like(token)

    res = pl.pallas_call(
        body, name=name,
        in_specs=[_HBM] * (2 * T),
        out_specs=[_SEM, _SEM] + [_HBM] * (2 * T) + [pl.BlockSpec(memory_space=pltpu.VMEM)],
        out_shape=[pltpu.SemaphoreType.DMA((3 * T,)), pltpu.SemaphoreType.DMA((3 * T,))]
        + [_hbm_like(a) for a in Ss + lands] + [_TOKEN],
        input_output_aliases={i: 2 + i for i in range(2 * T)},
        compiler_params=pltpu.CompilerParams(has_side_effects=_DATAFLOW),
    )(*[_in_hbm(a) for a in Ss + lands])
    return res[0], res[1], list(res[2:2 + T]), list(res[2 + T:2 + 2 * T]), res[-1]


def reduce_ici_wait(name, handle, after):
    send, recv, Ss, lands, _ = handle
    T = len(Ss)

    def body(*refs):
        s, land = refs[:T], refs[T:2 * T]
        send_ref, recv_ref = refs[2 * T], refs[2 * T + 1]
        x, y, c = _place()
        for t in range(T):
            for k, (fx, fy) in enumerate(_CHIP_FLIPS):
                px, py = _flip(x, fx), _flip(y, fy)
                cp = _remote(s[t].at[2 * px + py], land[t].at[2 * px + py], send_ref.at[3 * t + k], recv_ref.at[3 * t + k],
                             (px, py, c))
                cp.wait_send()
                cp.wait_recv()

    res = pl.pallas_call(
        body, name=name,
        in_specs=[_HBM] * (2 * T) + [_SEM, _SEM, _ANY],
        out_specs=[_HBM] * (2 * T),
        out_shape=[_hbm_like(a) for a in Ss + lands],
        input_output_aliases={i: i for i in range(2 * T)},
        compiler_params=pltpu.CompilerParams(has_side_effects=_DATAFLOW),
    )(*Ss, *lands, send, recv, after)
    return list(res[:T]), list(res[T:])


def rs_share(name, tots, layers):
    T = len(tots)

    def body(*refs):
        s, o = refs[:T], refs[T:2 * T]
        send, recv = refs[2 * T:]
        x, y, c = _place()
        cps = []
        for t in range(T):
            mine = o[t].at[layers[t], c]
            cp = _remote(mine, mine, send.at[t], recv.at[t], (x, y, 1 - c))
            cp.start()
            cps.append(cp)
        for t in range(T):
            other = o[t].at[layers[t], 1 - c]
            _remote(other, other, send.at[t], recv.at[t], (x, y, 1 - c)).wait_recv()
        for cp in cps:
            cp.wait_send()

    return pl.pallas_call(
        body, name=name, in_specs=[_ANY] * T, out_specs=[_ANY] * T,
        out_shape=[jax.ShapeDtypeStruct(s.shape, s.dtype) for s in tots],
        input_output_aliases={t: t for t in range(T)},
        scratch_shapes=[pltpu.SemaphoreType.DMA((T,)), pltpu.SemaphoreType.DMA((T,))],
        )(*tots)


def all_reduce_small(name, vec):
    rows = vec.shape[0]
    flips = [(fx, fy, fc) for fx in (0, 1) for fy in (0, 1) for fc in (0, 1)][1:]

    def body(v_ref, o_ref, buf, send, recv):
        x, y, c = _place()
        me = 4 * x + 2 * y + c
        buf[me] = v_ref[...]
        cps = []
        for k, (fx, fy, fc) in enumerate(flips):
            cp = _remote(buf.at[me], buf.at[me], send.at[k], recv.at[k],
                         (_flip(x, fx), _flip(y, fy), _flip(c, fc)))
            cp.start()
            cps.append(cp)
        for k, (fx, fy, fc) in enumerate(flips):
            slab = buf.at[4 * _flip(x, fx) + 2 * _flip(y, fy) + _flip(c, fc)]
            _remote(slab, slab, send.at[k], recv.at[k], (x, y, c)).wait_recv()
        for cp in cps:
            cp.wait_send()
        acc = buf[0]
        for d in range(1, N_DEV):
            acc = acc + buf[d]
        o_ref[...] = acc

    return pl.pallas_call(
        body, name=name,
        in_specs=[pl.BlockSpec(memory_space=pltpu.VMEM)], out_specs=pl.BlockSpec(memory_space=pltpu.VMEM),
        out_shape=jax.ShapeDtypeStruct((rows, LANES), f32),
        scratch_shapes=[pltpu.VMEM((N_DEV, rows, LANES), f32),
                        pltpu.SemaphoreType.DMA((N_DEV - 1,)), pltpu.SemaphoreType.DMA((N_DEV - 1,))],
        compiler_params=pltpu.CompilerParams(vmem_limit_bytes=V7X_VMEM_LIMIT))(vec)


def _all_done(arrays):
    return jnp.stack([a[(0,) * a.ndim].astype(f32) for a in arrays]).sum(keepdims=True)


def _pack(arrays):
    flat = jnp.concatenate([a.reshape(-1) for a in arrays])
    n = flat.shape[0]
    rows = -(-n // (8 * LANES)) * 8
    return jnp.pad(flat, (0, rows * LANES - n)).reshape(rows, LANES)


def _unpack(vec, shapes):
    flat = vec.reshape(-1)
    out, pos = [], 0
    for s in shapes:
        n = math.prod(s)
        out.append(flat[pos:pos + n].reshape(s))
        pos += n
    return out


def _f_first(x, g):
    return x, _rms(x, g)


def _f_mid(h, m, gp, gn):
    h1 = h + _rms(m, gp)
    return h1, _rms(h1, gn)


def _f_mid_bias(h, m, b, gp, gn):
    h1 = h + _rms(m + b, gp)
    return h1, _rms(h1, gn)


def _f_last(h, m, gp):
    return (h + _rms(m, gp),)


def _f_swiglu(gate, up):
    return (_silu(gate) * up,)


def _swiglu_tile(gate, up):
    return _silu(gate.astype(f32)) * up.astype(f32)


def _swiglu_bwd_tile(d_act, gate, up):
    _, vjp = jax.vjp(_f_swiglu, gate.astype(f32), up.astype(f32))
    return vjp((d_act,))


def _f_glu(a, g, ba, bg):
    return ((a + ba) * _sigmoid(g + bg),)


def _f_ln_silu(x, g, b):
    mu = jnp.mean(x, axis=-1, keepdims=True)
    xc = x - mu
    y = xc * lax.rsqrt(jnp.mean(xc * xc, axis=-1, keepdims=True) + LN_EPS) * g + b
    return (_silu(y),)


def _f_lower_bounds(logits):
    n = logits.shape[0]
    e = jnp.exp(logits - jnp.max(logits, axis=0, keepdims=True))
    p = e / jnp.sum(e, axis=0, keepdims=True)
    layer = lax.broadcasted_iota(jnp.int32, logits.shape, 0)
    out = -jnp.broadcast_to(p[0:1, :], logits.shape)
    for j in range(n):
        out = out + jnp.where(layer >= j, p[j:j + 1, :], 0.0)
    return (out,)


WEIGHT_NAMES = ['mix_pre_g', 'mix_post_g', 'ffn_pre_g', 'ffn_post_g', 'hgrn_lb_logits', 'even_w_in',
                'hgrn_norm_g', 'ssd_conv_w', 'ssd_conv_b', 'ssd_dt_bias', 'ssd_a_log', 'ssd_d', 'ssd_norm_g',
                'even_w_out', 'conf_w1', 'conf_b1', 'conf_dw_w', 'conf_dw_b', 'conf_ln_g', 'conf_ln_b',
                'conf_w2', 'conf_b2', 'ffn_w_gate', 'ffn_w_up', 'ffn_w_down']
BIG = ['even_w_in', 'even_w_out', 'conf_w1', 'conf_w2', 'ffn_w_gate', 'ffn_w_up', 'ffn_w_down']
SMALL_SHARDED = {'ssd_conv_w': 2, 'conf_b1': 1, 'conf_dw_w': 2, 'conf_dw_b': 1, 'conf_ln_g': 1,
                 'conf_ln_b': 1, 'conf_b2': 1}


def _train_step(x, target, w, m, v):
    S, D = x.shape[1], x.shape[2]
    x2, t2 = x[0], target[0]
    NL = w['mix_pre_g'].shape[0]
    HB = w['ssd_dt_bias'].shape[1]
    GN = B_GROUPS * B_STATE
    xw, yw, cw = _place()
    chip = 2 * xw + yw
    tm = _tile(S, 128, 8)
    row1 = lambda a, i: a[i:i + 1]

    sharded = list(SMALL_SHARDED)
    placed = []
    for n in sharded:
        ax, a = SMALL_SHARDED[n], w[n]
        full = jnp.zeros(a.shape[:ax] + (a.shape[ax] * N_CHIPS,) + a.shape[ax + 1:], f32)
        start = [0] * a.ndim
        start[ax] = chip * a.shape[ax]
        placed.append(lax.dynamic_update_slice(full, jnp.where(cw == 0, a, 0.0), start))
    whole = dict(zip(sharded, _unpack(all_reduce_small("gather_small", _pack(placed)), [p.shape for p in placed])))
    small = {n: whole.get(n, w[n]) for n in WEIGHT_NAMES if n not in BIG}

    def mixer_keys(layer):
        names = ('even_w_in', 'even_w_out') if layer % 2 == 0 else ('conf_w1', 'conf_w2')
        return [(n, layer // 2) for n in names]

    def ffn_keys(layer):
        return [(n, layer) for n in ('ffn_w_gate', 'ffn_w_up', 'ffn_w_down')]

    groups = [keys(layer) for layer in range(NL) for keys in (mixer_keys, ffn_keys)]
    halves = lambda a: a.reshape(a.shape[:-2] + (2, a.shape[-2] // 2, a.shape[-1]))
    own, land = {}, {}

    def start_groups(name, some, dep=None):
        for g in some:
            for n, l in g:
                own[n, l], land[n, l] = place_own("place_own", w[n], l)
        return gather_ici_start(name, [[(halves(own[k]), halves(land[k])) for k in g] for g in some], dep)

    first, first_started = start_groups("gather_start_first", groups[:1])
    rest, rest_started = start_groups("gather_start_rest", groups[1:], first_started)
    handles = first + rest
    W, handed = {}, {}

    def fetch_begin(gi, after):
        arrived = gather_ici_wait(f"gather_wait_{gi}", handles[gi], after)
        handed[gi] = forward_start(f"forward_start_{gi}", arrived)
        return handed[gi][-1]

    def fetch_end(gi, after):
        for k, a in zip(groups[gi], forward_wait(f"forward_wait_{gi}", handed[gi], after)):
            W[k] = a.reshape((N_CHIPS, 1) + own[k].shape)

    WM = 6 * D + 2 * GN
    w_main, w_dt = {}, {}

    n_even = small['hgrn_lb_logits'].shape[0]
    (lbs,) = _stage_fwd("lower_bounds", _f_lower_bounds, [small['hgrn_lb_logits']], [], [(D, f32)], tm=n_even)
    saved = []
    h = x2
    (u,) = _stage_fwd("pre_norm", lambda a, g: (_rms(a, g),), [h],
                      [row1(small['mix_pre_g'], 0) + rest_started[0, 0]], [(D, bf16)], tm=tm)
    fetch_begin(0, u)
    for layer in range(NL):
        li = layer // 2
        r = {'h': h, 'u': u}
        fetch_end(2 * layer, u)
        if layer % 2 == 0:
            wm, wd = join_in_proj("in_proj_join", W['even_w_in', li][:, 0], WM)
            w_main[li], w_dt[li] = wm[None, None], wd[None, None]
            r['ymain'] = mm_nn_col("in_proj", u, w_main[li], 0)
            r['dtr'] = mm_nn_col("in_proj_dt", u, w_dt[li], 0)
            r['xact'] = conv_fwd("ssd_conv", r['ymain'], 5 * D // CONV_CH, small['ssd_conv_w'][li],
                                 row1(small['ssd_conv_b'], li), True, f32)
            o_a, r['hg_st'] = hgrn_fwd("hgrn", r['ymain'], row1(lbs, li), row1(small['hgrn_norm_g'], li), D)
            begun = fetch_begin(2 * layer + 1, o_a)[0, 0]
            o_b, r['ssd_st'] = ssd_fwd("ssd", r['xact'], r['ymain'], r['dtr'], row1(small['ssd_dt_bias'], li),
                                       row1(small['ssd_a_log'], li), row1(small['ssd_d'], li),
                                       row1(small['ssd_norm_g'], li) + begun, D)
            r['mixed'] = jnp.concatenate([o_a, o_b], axis=1)
            r['m'] = mm_nn_row("out_proj", r['mixed'], W['even_w_out', li], 0)
            mid_fn, mid_par = _f_mid, []
        else:
            r['c1'] = mm_nn_col("conf_in", u, W['conf_w1', li], 0)
            b1 = row1(small['conf_b1'], li)
            tn = _tile(D, 512)
            (r['glu'],) = _stage_fwd("conf_glu", _f_glu, [r['c1'], (r['c1'], D // tn)], [b1, (b1, D // tn)],
                                     [(D, f32)], tm=tm, tn=tn)
            r['cc'] = conv_fwd("conf_conv", r['glu'], 0, small['conf_dw_w'][li], row1(small['conf_dw_b'], li),
                               False, f32)
            begun = fetch_begin(2 * layer + 1, r['cc'])[0, 0]
            (r['c2'],) = _stage_fwd("conf_ln", _f_ln_silu, [r['cc']],
                                    [row1(small['conf_ln_g'], li) + begun, row1(small['conf_ln_b'], li)],
                                    [(D, bf16)], tm=tm)
            r['m'] = mm_nn_row("conf_out", r['c2'], W['conf_w2', li], 0)
            mid_fn, mid_par = _f_mid_bias, [row1(small['conf_b2'], li)]
        r['mid_fn'] = mid_fn
        r['mid_par'] = mid_par + [row1(small['mix_post_g'], layer), row1(small['ffn_pre_g'], layer)]
        r['h1'], r['u2'] = _stage_fwd("mid_norm", mid_fn, [h, r['m']], r['mid_par'], [(D, f32), (D, bf16)], tm=tm)
        fetch_end(2 * layer + 1, r['u2'])
        r['gate'] = mm_nn_col("ffn_gate", r['u2'], W['ffn_w_gate', layer], 0, bf16)
        begun = fetch_begin(2 * layer + 2, r['gate']) if layer + 1 < NL else None
        r['up'] = mm_nn_col("ffn_up", r['u2'], W['ffn_w_up', layer], 0, bf16, dep=begun)
        r['dn'] = mm_nn_row("ffn_down", [r['gate'], r['up']], W['ffn_w_down', layer], 0, a_fn=_swiglu_tile)
        if layer + 1 < NL:
            r['end_fn'] = _f_mid
            r['end_par'] = [row1(small['ffn_post_g'], layer), row1(small['mix_pre_g'], layer + 1)]
            h, u = _stage_fwd("end_norm", _f_mid, [r['h1'], r['dn']], r['end_par'], [(D, f32), (D, bf16)], tm=tm)
        else:
            r['end_fn'] = _f_last
            r['end_par'] = [row1(small['ffn_post_g'], layer)]
            (h,) = _stage_fwd("last_norm", _f_last, [r['h1'], r['dn']], r['end_par'], [(D, f32)], tm=tm)
        saved.append(r)

    dy, loss_local = loss_head("loss_head", h, t2, tm)
    loss = lax.psum(loss_local, ("x", "y", "c"))

    gs_rows = {n: [None] * small[n].shape[0] for n in small}
    gs = {}

    def put(n, i, val):
        gs_rows[n][i] = val.reshape(small[n].shape[1:])

    final = {n: lax.empty((w[n].shape[0], 2, w[n].shape[1] // 2, w[n].shape[2]), f32) for n in BIG}
    to_sibling, to_chips = [], []

    def reduce_begin(gi, parts, dep=None):
        handle = sibling_start(f"sibling_start_{gi}", [halves(p) for p in parts], dep)
        to_sibling.append((gi, handle))
        return handle[-1][0, 0]

    def reduce_middle(after):
        gi, handle = to_sibling.pop(0)
        parts, from_sib = sibling_wait(f"sibling_wait_{gi}", handle, after)
        sums = [rs_add("reduce_add", a, b) for a, b in zip(parts, from_sib)]
        handle = reduce_ici_start(f"reduce_start_{gi}", sums)
        to_chips.append((gi, handle))
        return handle[-1][0, 0]

    def reduce_finish(after):
        gi, handle = to_chips.pop(0)
        keys = groups[gi]
        sums, lands = reduce_ici_wait(f"reduce_wait_{gi}", handle, after)
        for (n, l), s_, b_ in zip(keys, sums, lands):
            final[n] = rs_sum4("reduce_sum", s_, b_, final[n], l)
        names = [n for n, _ in keys]
        shared = rs_share("reduce_share", [final[n] for n in names], [l for _, l in keys])
        final.update(zip(names, shared))

    def reduce_step(gi, parts, newest, dep=None):
        zero = reduce_begin(gi, parts, dep)
        if to_chips:
            reduce_finish(newest)
        if len(to_sibling) > 1:
            zero = zero + reduce_middle(newest)
        return zero

    grads, first_layer = {}, {}

    def small_gradients(dh, du_parts, started):
        (first_layer['grad_x'],), pg = _stage_bwd("pre_norm_bwd", _f_first, [x2],
                                                   [row1(small['mix_pre_g'], 0) + started],
                                                   [[dh], du_parts], [f32], tm=tm)
        put('mix_pre_g', 0, pg[0])
        dlbs = jnp.concatenate([saved[2 * i]['dlb'] for i in range(n_even)], axis=0)
        (dlogits,), _ = _stage_bwd("lower_bounds_bwd", _f_lower_bounds, [small['hgrn_lb_logits']], [],
                                   [[dlbs]], [f32], tm=n_even)
        names_s = [n for n in WEIGHT_NAMES if n not in BIG]
        for n in names_s:
            gs[n] = dlogits if n == 'hgrn_lb_logits' else jnp.stack(gs_rows[n])
        total = all_reduce_small("reduce_small", _pack([gs[n] for n in names_s]))
        for n, a in zip(names_s, _unpack(total, [gs[n].shape for n in names_s])):
            if n in SMALL_SHARDED:
                ax = SMALL_SHARDED[n]
                size = w[n].shape[ax]
                start = [0] * a.ndim
                start[ax] = chip * size
                a = lax.dynamic_slice(a, start, a.shape[:ax] + (size,) + a.shape[ax + 1:])
            grads[n] = a
        return total

    dh = dy
    du_parts = None
    started = None
    for layer in reversed(range(NL)):
        li = layer // 2
        r = saved[layer]
        cts = [[dh]] if du_parts is None else [[dh], du_parts]
        par = r['end_par'] if started is None else [r['end_par'][0] + started] + r['end_par'][1:]
        (dh1, d_dn), pg = _stage_bwd("end_norm_bwd", r['end_fn'], [r['h1'], r['dn']], par, cts,
                                     [f32, bf16], tm=tm)
        put('ffn_post_g', layer, pg[0])
        if du_parts is not None:
            put('mix_pre_g', layer + 1, pg[1])
        d_gate, d_up = mm_nt_row("ffn_down_dx", d_dn, W['ffn_w_down', layer], 0, bf16,
                                 tail=(_swiglu_bwd_tile, [r['gate'], r['up']]), n_out=2)
        g_down = mm_tn_row("ffn_down_dw", [r['gate'], r['up']], d_dn, N_CHIPS, a_fn=_swiglu_tile)
        du_a = mm_nt_col("ffn_gate_dx", d_gate, W['ffn_w_gate', layer], 0)
        du_b = mm_nt_col("ffn_up_dx", d_up, W['ffn_w_up', layer], 0)
        g_gate = mm_tn_col("ffn_gate_dw", r['u2'], d_gate, N_CHIPS)
        g_up = mm_tn_col("ffn_up_dw", r['u2'], d_up, N_CHIPS)
        started = reduce_step(2 * layer + 1, [g_gate, g_up, g_down], g_up)
        par = r['mid_par'][:-1] + [r['mid_par'][-1] + started]
        (dh, dm), pg = _stage_bwd("mid_norm_bwd", r['mid_fn'], [r['h'], r['m']], par,
                                  [[dh1], [du_a, du_b]], [f32, bf16], tm=tm)
        put('mix_post_g', layer, pg[-2])
        put('ffn_pre_g', layer, pg[-1])
        if layer % 2 == 0:
            d_mixed = mm_nt_row("out_proj_dx", dm, W['even_w_out', li], 0)
            g_out = mm_tn_row("out_proj_dw", r['mixed'], dm, N_CHIPS)
            dxs, dbm, dcm, dz, ddt, ddtb, dal, dds, dbn = ssd_bwd(
                "ssd_bwd", r['xact'], r['ymain'], r['dtr'], row1(small['ssd_dt_bias'], li),
                row1(small['ssd_a_log'], li), row1(small['ssd_d'], li), row1(small['ssd_norm_g'], li),
                r['ssd_st'], d_mixed, D)
            put('ssd_dt_bias', li, ddtb)
            put('ssd_a_log', li, dal)
            put('ssd_d', li, dds)
            put('ssd_norm_g', li, dbn)
            d_xact = jnp.concatenate([dxs, dbm, dcm], axis=1)
            d_xbc, dcw, dcb = conv_bwd("ssd_conv_bwd", r['ymain'], 5 * D // CONV_CH, small['ssd_conv_w'][li],
                                       row1(small['ssd_conv_b'], li), d_xact, True, bf16)
            put('ssd_conv_w', li, dcw)
            put('ssd_conv_b', li, dcb)
            dq, df, dv, dg, dlb, dan = hgrn_bwd("hgrn_bwd", r['ymain'], row1(lbs, li), row1(small['hgrn_norm_g'], li),
                                               r['hg_st'], d_mixed, D)
            put('hgrn_norm_g', li, dan)
            r['dlb'] = dlb
            d_main = jnp.concatenate([dq, df, dv, dg, dz, d_xbc], axis=1)
            du_parts = [mm_nt_col("in_proj_dx", d_main, w_main[li], 0),
                        mm_nt_col("in_proj_dt_dx", ddt, w_dt[li], 0)]
            g_in = split_in_proj("in_proj_split", mm_tn_col("in_proj_dw", r['u'], d_main, 1)[0],
                                 mm_tn_col("in_proj_dt_dw", r['u'], ddt, 1)[0], N_CHIPS)
            mixer_parts, newest = [g_in, g_out], g_in
        else:
            put('conf_b2', li, pg[0])
            d_c2 = mm_nt_row("conf_out_dx", dm, W['conf_w2', li], 0)
            g_w2 = mm_tn_row("conf_out_dw", r['c2'], dm, N_CHIPS)
            (d_cc,), pl_ = _stage_bwd("conf_ln_bwd", _f_ln_silu, [r['cc']],
                                      [row1(small['conf_ln_g'], li), row1(small['conf_ln_b'], li)],
                                      [[d_c2]], [f32], tm=tm)
            put('conf_ln_g', li, pl_[0])
            put('conf_ln_b', li, pl_[1])
            d_glu, ddw, ddb = conv_bwd("conf_conv_bwd", r['glu'], 0, small['conf_dw_w'][li],
                                       row1(small['conf_dw_b'], li), d_cc, False, f32)
            put('conf_dw_w', li, ddw)
            put('conf_dw_b', li, ddb)
            b1 = row1(small['conf_b1'], li)
            tn = _tile(D, 512)
            (da, dg_), pb = _stage_bwd("conf_glu_bwd", _f_glu, [r['c1'], (r['c1'], D // tn)], [b1, (b1, D // tn)],
                                       [[d_glu]], [bf16, bf16], tm=tm, tn=tn)
            put('conf_b1', li, jnp.concatenate([pb[0], pb[1]], axis=1))
            d_c1 = jnp.concatenate([da, dg_], axis=1)
            du_parts = [mm_nt_col("conf_in_dx", d_c1, W['conf_w1', li], 0)]
            g_w1 = mm_tn_col("conf_in_dw", r['u'], d_c1, N_CHIPS)
            mixer_parts, newest = [g_w1, g_w2], g_w1
        small_done = small_gradients(dh, du_parts, started) if layer == 0 else None
        started = reduce_step(2 * layer, mixer_parts, newest, small_done)
    started = started + reduce_middle(mixer_parts[0])

    delta, new_m, new_v = {}, {}, {}

    def update(names):
        for n in names:
            g = final[n].reshape(w[n].shape) if n in BIG else grads[n]
            delta[n], new_m[n], new_v[n], grads[n] = adamw("adamw", w[n], g, m[n], v[n])
        return _all_done([delta[n] for n in names])

    second_last = [n for n, _ in groups[to_chips[0][0]]]
    last = [n for n, _ in groups[to_chips[1][0]]]
    grads[WEIGHT_NAMES[0]] = grads[WEIGHT_NAMES[0]] + started
    grad_x2 = first_layer['grad_x']
    done = update([n for n in WEIGHT_NAMES if n not in last + second_last])
    reduce_finish(done)
    done = update(second_last)
    reduce_finish(done)
    update(last)
    return (loss, grad_x2[None], *[grads[n] for n in WEIGHT_NAMES], *[delta[n] for n in WEIGHT_NAMES],
            *[new_m[n] for n in WEIGHT_NAMES], *[new_v[n] for n in WEIGHT_NAMES])


def kernel(x, mix_pre_g, mix_post_g, ffn_pre_g, ffn_post_g, hgrn_lb_logits, even_w_in, hgrn_norm_g, ssd_conv_w, ssd_conv_b, ssd_dt_bias, ssd_a_log, ssd_d, ssd_norm_g, even_w_out, conf_w1, conf_b1, conf_dw_w, conf_dw_b, conf_ln_g, conf_ln_b, conf_w2, conf_b2, ffn_w_gate, ffn_w_up, ffn_w_down, loss_target, m_mix_pre_g, m_mix_post_g, m_ffn_pre_g, m_ffn_post_g, m_hgrn_lb_logits, m_even_w_in, m_hgrn_norm_g, m_ssd_conv_w, m_ssd_conv_b, m_ssd_dt_bias, m_ssd_a_log, m_ssd_d, m_ssd_norm_g, m_even_w_out, m_conf_w1, m_conf_b1, m_conf_dw_w, m_conf_dw_b, m_conf_ln_g, m_conf_ln_b, m_conf_w2, m_conf_b2, m_ffn_w_gate, m_ffn_w_up, m_ffn_w_down, v_mix_pre_g, v_mix_post_g, v_ffn_pre_g, v_ffn_post_g, v_hgrn_lb_logits, v_even_w_in, v_hgrn_norm_g, v_ssd_conv_w, v_ssd_conv_b, v_ssd_dt_bias, v_ssd_a_log, v_ssd_d, v_ssd_norm_g, v_even_w_out, v_conf_w1, v_conf_b1, v_conf_dw_w, v_conf_dw_b, v_conf_ln_g, v_conf_ln_b, v_conf_w2, v_conf_b2, v_ffn_w_gate, v_ffn_w_up, v_ffn_w_down):
    args = locals()
    w = {n: args[n] for n in WEIGHT_NAMES}
    m = {n: args["m_" + n] for n in WEIGHT_NAMES}
    v = {n: args["v_" + n] for n in WEIGHT_NAMES}
    return _train_step(x, loss_target, w, m, v)
```

```python
import functools
import math

import jax
import jax.numpy as jnp
from jax import lax
from jax.experimental import pallas as pl
from jax.experimental.pallas import tpu as pltpu

f32 = jnp.float32
bf16 = jnp.bfloat16
MESH = pl.DeviceIdType.MESH
HI = lax.Precision.HIGHEST

A_HEAD = 128
A_CHUNK = 64
A_SUB = 8
A_REF = 4
A_EXP_CAP = 60.0
A_HEADS_PER_STEP = 8
A_F_MIN = 1e-6
B_HEAD = 64
B_GROUPS = 4
B_STATE = 128
B_CONV = 4
B_CHUNK = 128
C_KERNEL = 31
RMS_EPS = 1e-6
LN_EPS = 1e-5
ADAM_LR = 0.001
ADAM_B1 = 0.9
ADAM_B2 = 0.999
ADAM_EPS = 1e-08
ADAM_WD = 0.01
ADAM_STEP = 10

N_CHIPS = 4
N_DEV = 8
V7X_VMEM_LIMIT = 56 * 1024 * 1024
LANES = 128
CONV_PAD = 32
CONV_ROWS = 128
CONV_CH = 256

NN = (((1,), (0,)), ((), ()))
NT = (((1,), (1,)), ((), ()))
TN = (((0,), (0,)), ((), ()))


def _tile(n, cap, unit=LANES):
    best = None
    for t in range(unit, min(n, cap) + 1, unit):
        if n % t == 0:
            best = t
    return n if best is None else best


def _cp(*sem):
    return pltpu.CompilerParams(dimension_semantics=sem, vmem_limit_bytes=V7X_VMEM_LIMIT)


def _sigmoid(x):
    return jax.nn.sigmoid(x)


def _silu(x):
    return x * jax.nn.sigmoid(x)


def _rms(x, g):
    return x * lax.rsqrt(jnp.mean(x * x, axis=-1, keepdims=True) + RMS_EPS) * g


def _pair(a):
    return a if isinstance(a, tuple) else (a, 0)


def _stage(name, fn, rows, params, outs, par_outs=(), *, tm, tn=None):
    rows = [_pair(r) for r in rows]
    params = [_pair(p) for p in params]
    S = rows[0][0].shape[0]
    n_in, n_o = len(rows) + len(params), len(outs)
    if tn is None:
        grid = (S // tm,)
        in_specs = [pl.BlockSpec((tm, a.shape[1]), lambda i: (i, 0)) for a, _ in rows]
        in_specs += [pl.BlockSpec(a.shape, lambda i: (0, 0)) for a, _ in params]
        out_specs = [pl.BlockSpec((tm, w), lambda i: (i, 0)) for w, _ in outs]
        out_specs += [pl.BlockSpec((k, w), lambda i: (0, 0)) for k, w in par_outs]
        row_axis = 0
        sem = ("arbitrary",) if par_outs else ("parallel",)
    else:
        grid = (outs[0][0] // tn, S // tm)
        in_specs = [pl.BlockSpec((tm, tn), lambda j, i, o=o: (i, j + o)) for _, o in rows]
        in_specs += [pl.BlockSpec((a.shape[0], tn), lambda j, i, o=o: (0, j + o)) for a, o in params]
        out_specs = [pl.BlockSpec((tm, tn), lambda j, i: (i, j)) for _ in outs]
        out_specs += [pl.BlockSpec((k, tn), lambda j, i: (0, j)) for k, _ in par_outs]
        row_axis = 1
        sem = ("parallel", "arbitrary") if par_outs else ("parallel", "parallel")
    out_shape = [jax.ShapeDtypeStruct((S, w), d) for w, d in outs]
    out_shape += [jax.ShapeDtypeStruct((k, w), f32) for k, w in par_outs]

    def body(*refs):
        res = fn(*[r[...] for r in refs[:n_in]])
        for r, v in zip(refs[n_in:n_in + n_o], res[:n_o]):
            r[...] = v.astype(r.dtype)
        if par_outs:
            acc_refs = refs[n_in + n_o:]

            @pl.when(pl.program_id(row_axis) == 0)
            def _():
                for r in acc_refs:
                    r[...] = jnp.zeros_like(r)

            for r, v in zip(acc_refs, res[n_o:]):
                r[...] += v

    return pl.pallas_call(
        body, name=name, grid=grid, in_specs=in_specs, out_specs=out_specs, out_shape=out_shape,
        compiler_params=_cp(*sem))(*[a for a, _ in rows], *[a for a, _ in params])


def _stage_fwd(name, fn, rows, params, outs, *, tm, tn=None):
    n_r = len(rows)

    def ffn(*t):
        return fn(*[v.astype(f32) for v in t[:n_r]], *t[n_r:])

    return _stage(name, ffn, rows, params, outs, tm=tm, tn=tn)


def _stage_bwd(name, fn, rows, params, cts, drow, *, tm, tn=None):
    rows = [_pair(r) for r in rows]
    params = [_pair(p) for p in params]
    n_r, n_p = len(rows), len(params)
    flat_ct = [_pair(c) for group in cts for c in group]
    counts = [len(group) for group in cts]
    need = [i for i, d in enumerate(drow) if d is not None]

    def bfn(*t):
        r = [v.astype(f32) for v in t[:n_r]]
        c = t[n_r:n_r + len(flat_ct)]
        p = list(t[n_r + len(flat_ct):])
        res, vjp = jax.vjp(fn, *r, *p)
        ct, pos = [], 0
        for o, k in zip(res, counts):
            s = c[pos].astype(f32)
            for e in range(1, k):
                s = s + c[pos + e].astype(f32)
            pos += k
            ct.append(s.astype(o.dtype))
        g = vjp(tuple(ct))
        return tuple(g[i] for i in need) + tuple(g[n_r:])

    if tn is None:
        outs = [(rows[i][0].shape[1], drow[i]) for i in need]
        par_outs = [p.shape for p, _ in params]
    else:
        w_all = flat_ct[0][0].shape[1]
        outs = [(w_all, drow[i]) for i in need]
        par_outs = [(p.shape[0], w_all) for p, _ in params]
    res = _stage(name, bfn, rows + flat_ct, params, outs, par_outs, tm=tm, tn=tn)
    return res[:len(need)], res[len(need):]


def _mm(name, dims, a, b, grid, a_spec, b_spec, o_spec, out_shape, acc_shape, a_fn=None, tail=None, dep=None):
    nk = grid[2]
    a_list = list(a) if isinstance(a, (list, tuple)) else [a]
    na = len(a_list)
    t_fn, t_arrays = tail if tail is not None else (None, [])
    ne = len(t_arrays)
    deps = [] if dep is None else [dep]
    multi = isinstance(out_shape, (list, tuple))

    n_out = len(out_shape) if multi else 1

    def body(*refs):
        a_refs, b_ref, t_refs = refs[:na], refs[na], refs[na + 1:na + 1 + ne]
        first_out = na + 1 + ne + len(deps)
        o_refs = refs[first_out:first_out + n_out]
        k = pl.program_id(2)

        def prod():
            lhs = a_refs[0][...] if a_fn is None else a_fn(*[r[...] for r in a_refs])
            return lax.dot_general(lhs.astype(bf16), b_ref[...].astype(bf16), dims, preferred_element_type=f32)

        def finish(total):
            res = (total,) if t_fn is None else t_fn(total, *[r[...] for r in t_refs])
            for r, val in zip(o_refs, res):
                r[...] = val.astype(r.dtype)

        if nk == 1:
            finish(prod())
        else:
            acc = refs[-1]

            @pl.when(k == 0)
            def _():
                acc[...] = prod()

            if nk > 2:
                @pl.when(jnp.logical_and(k > 0, k < nk - 1))
                def _():
                    acc[...] += prod()

            @pl.when(k == nk - 1)
            def _():
                finish(acc[...] + prod())

    return pl.pallas_call(
        body, name=name, grid=grid,
        in_specs=[a_spec] * na + [b_spec] + [o_spec] * ne + [pl.BlockSpec(memory_space=pl.ANY)] * len(deps),
        out_specs=[o_spec] * len(out_shape) if multi else o_spec, out_shape=out_shape,
        scratch_shapes=[pltpu.VMEM(acc_shape, f32)] if nk > 1 else [],
        compiler_params=_cp("parallel", "parallel", "arbitrary"))(*a_list, b, *t_arrays, *deps)


def _mm_tiles(S, roomy=True):
    return _tile(S, 1024 if roomy else 512)


def join_in_proj(name, parts, width_main):
    P, D, CI = parts.shape
    rest = P * CI - width_main
    tr = _tile(D, 128, 16)

    def body(p_ref, m_ref, d_ref):
        for j in range(P - 1):
            m_ref[:, j * CI:(j + 1) * CI] = p_ref[j]
        m_ref[:, (P - 1) * CI:width_main] = p_ref[P - 1][:, :CI - rest]
        d_ref[...] = p_ref[P - 1][:, CI - rest:]

    return pl.pallas_call(
        body, name=name, grid=(D // tr,),
        in_specs=[pl.BlockSpec((P, tr, CI), lambda i: (0, i, 0))],
        out_specs=[pl.BlockSpec((tr, width_main), lambda i: (i, 0)), pl.BlockSpec((tr, rest), lambda i: (i, 0))],
        out_shape=[jax.ShapeDtypeStruct((D, width_main), parts.dtype), jax.ShapeDtypeStruct((D, rest), parts.dtype)],
        compiler_params=_cp("parallel"))(parts)


def split_in_proj(name, main, rest, P):
    D, WM = main.shape
    nr = rest.shape[1]
    CI = (WM + nr) // P
    tr = _tile(D, 128, 16)

    def body(m_ref, d_ref, p_ref):
        for j in range(P - 1):
            p_ref[j] = m_ref[:, j * CI:(j + 1) * CI]
        p_ref[P - 1, :, :CI - nr] = m_ref[:, (P - 1) * CI:WM]
        p_ref[P - 1, :, CI - nr:] = d_ref[...]

    return pl.pallas_call(
        body, name=name, grid=(D // tr,),
        in_specs=[pl.BlockSpec((tr, WM), lambda i: (i, 0)), pl.BlockSpec((tr, nr), lambda i: (i, 0))],
        out_specs=pl.BlockSpec((P, tr, CI), lambda i: (0, i, 0)),
        out_shape=jax.ShapeDtypeStruct((P, D, CI), main.dtype),
        compiler_params=_cp("parallel"))(main, rest)


def mm_nn_col(name, a, W, li, out_dtype=f32, dep=None):
    P, _, K, C = W.shape
    S = a.shape[0]
    tm, tn, tk = _mm_tiles(S), _tile(C, 1536), _tile(K, 2048)
    nc = C // tn
    return _mm(name, NN, a, W, (S // tm, P * nc, K // tk),
               pl.BlockSpec((tm, tk), lambda i, j, k: (i, k)),
               pl.BlockSpec((None, None, tk, tn), lambda i, j, k: (j // nc, li, k, j % nc)),
               pl.BlockSpec((tm, tn), lambda i, j, k: (i, j)),
               jax.ShapeDtypeStruct((S, P * C), out_dtype), (tm, tn), dep=dep)


def mm_nn_row(name, a, W, li, out_dtype=f32, a_fn=None):
    P, _, R, N = W.shape
    S = (a[0] if a_fn is not None else a).shape[0]
    tm, tn, tk = _mm_tiles(S, a_fn is None), _tile(N, 1024), _tile(R, 2048)
    nr = R // tk
    return _mm(name, NN, a, W, (S // tm, N // tn, P * nr),
               pl.BlockSpec((tm, tk), lambda i, j, k: (i, k)),
               pl.BlockSpec((None, None, tk, tn), lambda i, j, k: (k // nr, li, k % nr, j)),
               pl.BlockSpec((tm, tn), lambda i, j, k: (i, j)),
               jax.ShapeDtypeStruct((S, N), out_dtype), (tm, tn), a_fn=a_fn)


def mm_nt_col(name, dy, W, li, out_dtype=f32):
    P, _, K, C = W.shape
    S = dy.shape[0]
    tm, tn, tk = _mm_tiles(S), _tile(K, 1024), _tile(C, 2048)
    nc = C // tk
    return _mm(name, NT, dy, W, (S // tm, K // tn, P * nc),
               pl.BlockSpec((tm, tk), lambda i, j, k: (i, k)),
               pl.BlockSpec((None, None, tn, tk), lambda i, j, k: (k // nc, li, j, k % nc)),
               pl.BlockSpec((tm, tn), lambda i, j, k: (i, j)),
               jax.ShapeDtypeStruct((S, K), out_dtype), (tm, tn))


def mm_nt_row(name, dy, W, li, out_dtype=f32, tail=None, n_out=None):
    P, _, R, N = W.shape
    S = dy.shape[0]
    tm, tn, tk = _mm_tiles(S, tail is None), _tile(R, 1536), _tile(N, 2048)
    nr = R // tn
    out = jax.ShapeDtypeStruct((S, P * R), out_dtype)
    return _mm(name, NT, dy, W, (S // tm, P * nr, N // tk),
               pl.BlockSpec((tm, tk), lambda i, j, k: (i, k)),
               pl.BlockSpec((None, None, tn, tk), lambda i, j, k: (j // nr, li, j % nr, k)),
               pl.BlockSpec((tm, tn), lambda i, j, k: (i, j)),
               out if n_out is None else [out] * n_out, (tm, tn), tail=tail)


def mm_tn_col(name, a, dy, P):
    S, K = a.shape
    C = dy.shape[1] // P
    tm, tn, tk = _tile(K, 1024), _tile(C, 1536), _tile(S, 2048)
    nc = C // tn
    return _mm(name, TN, a, dy, (K // tm, P * nc, S // tk),
               pl.BlockSpec((tk, tm), lambda i, j, k: (k, i)),
               pl.BlockSpec((tk, tn), lambda i, j, k: (k, j)),
               pl.BlockSpec((None, tm, tn), lambda i, j, k: (j // nc, i, j % nc)),
               jax.ShapeDtypeStruct((P, K, C), bf16), (tm, tn))


def mm_tn_row(name, a, dy, P, a_fn=None):
    S, N = dy.shape
    R = (a[0] if a_fn is not None else a).shape[1] // P
    tm, tn, tk = _tile(R, 1536), _tile(N, 1024), _tile(S, 2048 if a_fn is None else 1024)
    nr = R // tm
    return _mm(name, TN, a, dy, (P * nr, N // tn, S // tk),
               pl.BlockSpec((tk, tm), lambda i, j, k: (k, i)),
               pl.BlockSpec((tk, tn), lambda i, j, k: (k, j)),
               pl.BlockSpec((None, tm, tn), lambda i, j, k: (i // nr, i % nr, j)),
               jax.ShapeDtypeStruct((P, R, N), bf16), (tm, tn), a_fn=a_fn)


def _hgrn_chunk(st, q, fp, v, gt, lb, an):
    C = q.shape[0]
    sig = _sigmoid(fp)
    f = lb + (1.0 - lb) * sig
    kk = (1.0 - lb) * (1.0 - sig)
    g = jnp.log(jnp.maximum(f, A_F_MIN))
    qs = _silu(q)
    row = lax.broadcasted_iota(jnp.int32, (C, C), 0)
    col = lax.broadcasted_iota(jnp.int32, (C, C), 1)
    tri = (col <= row).astype(f32)
    b = jnp.dot(tri, g, precision=HI, preferred_element_type=f32)
    o_inter = lax.dot_general((qs * jnp.exp(b)).astype(bf16), st.astype(bf16), NT,
                              preferred_element_type=f32)
    T, NB = A_SUB, C // A_SUB
    refs = [b[i * T + A_REF:i * T + A_REF + 1, :] for i in range(NB)]
    ref_q = jnp.concatenate([jnp.broadcast_to(r, (T, A_HEAD)) for r in refs], axis=0)
    ref_k = jnp.concatenate([jnp.broadcast_to(r, (C, A_HEAD)) for r in refs], axis=0)
    q_t = qs * jnp.exp(b - ref_q)
    k_t = jnp.concatenate([kk] * NB, axis=0) * jnp.exp(
        jnp.minimum(ref_k - jnp.concatenate([b] * NB, axis=0), A_EXP_CAP))
    s = lax.dot_general(q_t.astype(bf16), k_t.astype(bf16), NT, preferred_element_type=f32)
    trow = lax.broadcasted_iota(jnp.int32, (C, NB * C), 0)
    scol = lax.broadcasted_iota(jnp.int32, (C, NB * C), 1)
    keep = jnp.logical_and(scol // C == trow // T, scol % C <= trow)
    s = jnp.where(keep, s, 0.0)
    o_intra = jnp.dot(s.astype(bf16), jnp.concatenate([v.astype(bf16)] * NB, axis=0),
                      preferred_element_type=f32)
    bl = b[C - 1:C, :]
    kd = kk * jnp.exp(bl - b)
    st_new = st * jnp.exp(bl) + lax.dot_general(v.astype(bf16), kd.astype(bf16), TN,
                                                preferred_element_type=f32)
    o = o_inter + o_intra
    y = o * lax.rsqrt(jnp.mean(o * o, axis=-1, keepdims=True) + RMS_EPS) * an * _silu(gt)
    return st_new, y


def _hgrn_heads_per_step(HA):
    return A_HEADS_PER_STEP if HA % A_HEADS_PER_STEP == 0 else 1


def _hgrn_in_specs(HA, HP, cidx):
    W = HP * A_HEAD
    specs = [pl.BlockSpec((A_CHUNK, W), lambda h, c, s=s: (cidx(c), s * (HA // HP) + h)) for s in range(4)]
    specs += [pl.BlockSpec((1, W), lambda h, c: (0, h))] * 2
    return specs


def _head(ref, j):
    return ref[:, j * A_HEAD:(j + 1) * A_HEAD]


def hgrn_fwd(name, ymain, lb, an, D):
    S = ymain.shape[0]
    HA, nc = D // A_HEAD, S // A_CHUNK
    HP = _hgrn_heads_per_step(HA)
    W = HP * A_HEAD

    def body(q, fp, v, gt, lb_ref, an_ref, o_ref, sv_ref, st):
        @pl.when(pl.program_id(1) == 0)
        def _():
            st[...] = jnp.zeros_like(st)

        sv_ref[...] = st[...]
        for j in range(HP):
            st_new, y = _hgrn_chunk(st[j], _head(q, j), _head(fp, j), _head(v, j), _head(gt, j),
                                    _head(lb_ref, j), _head(an_ref, j))
            st[j] = st_new
            o_ref[:, j * A_HEAD:(j + 1) * A_HEAD] = y.astype(o_ref.dtype)

    return pl.pallas_call(
        body, name=name, grid=(HA // HP, nc),
        in_specs=_hgrn_in_specs(HA, HP, lambda c: c),
        out_specs=[pl.BlockSpec((A_CHUNK, W), lambda h, c: (c, h)),
                   pl.BlockSpec((HP, None, A_HEAD, A_HEAD), lambda h, c: (h, c, 0, 0))],
        out_shape=[jax.ShapeDtypeStruct((S, D), bf16),
                   jax.ShapeDtypeStruct((HA, nc, A_HEAD, A_HEAD), f32)],
        scratch_shapes=[pltpu.VMEM((HP, A_HEAD, A_HEAD), f32)],
        compiler_params=_cp("parallel", "arbitrary"))(ymain, ymain, ymain, ymain, lb, an)


def hgrn_bwd(name, ymain, lb, an, saved, dmixed, D):
    S = ymain.shape[0]
    HA, nc = D // A_HEAD, S // A_CHUNK
    HP = _hgrn_heads_per_step(HA)
    W = HP * A_HEAD
    rev = lambda c: nc - 1 - c

    def body(q, fp, v, gt, lb_ref, an_ref, sv_ref, do_ref, dq, df, dv, dg, dlb, dan, dst):
        @pl.when(pl.program_id(1) == 0)
        def _():
            dst[...] = jnp.zeros_like(dst)
            dlb[...] = jnp.zeros_like(dlb)
            dan[...] = jnp.zeros_like(dan)

        for j in range(HP):
            cols = slice(j * A_HEAD, (j + 1) * A_HEAD)
            _, vjp = jax.vjp(_hgrn_chunk, sv_ref[j], _head(q, j), _head(fp, j), _head(v, j), _head(gt, j),
                             _head(lb_ref, j), _head(an_ref, j))
            g = vjp((dst[j], _head(do_ref, j).astype(f32)))
            dst[j] = g[0]
            for r, x in zip((dq, df, dv, dg), g[1:5]):
                r[:, cols] = x.astype(r.dtype)
            dlb[:, cols] += g[5]
            dan[:, cols] += g[6]

    blk = pl.BlockSpec((A_CHUNK, W), lambda h, c: (rev(c), h))
    vec = pl.BlockSpec((1, W), lambda h, c: (0, h))
    return pl.pallas_call(
        body, name=name, grid=(HA // HP, nc),
        in_specs=_hgrn_in_specs(HA, HP, rev) + [
            pl.BlockSpec((HP, None, A_HEAD, A_HEAD), lambda h, c: (h, rev(c), 0, 0)), blk],
        out_specs=[blk] * 4 + [vec] * 2,
        out_shape=[jax.ShapeDtypeStruct((S, D), bf16)] * 4 + [jax.ShapeDtypeStruct((1, D), f32)] * 2,
        scratch_shapes=[pltpu.VMEM((HP, A_HEAD, A_HEAD), f32)],
        compiler_params=_cp("parallel", "arbitrary"))(ymain, ymain, ymain, ymain, lb, an, saved, dmixed)


def _ssd_chunk(hp, xs, bm, cm, z, dtr, dtb, alog, dsk, bn, g, R):
    L, GW = xs.shape
    HB = dtr.shape[1]
    R8 = max(R, 8)
    dt = jax.nn.softplus(dtr + dtb)
    a = -jnp.exp(alog)
    row = lax.broadcasted_iota(jnp.int32, (L, L), 0)
    col = lax.broadcasted_iota(jnp.int32, (L, L), 1)
    causal = col <= row
    cs = jnp.dot(causal.astype(f32), dt * a, precision=HI, preferred_element_type=f32)
    eh = lax.broadcasted_iota(jnp.int32, (HB, GW), 0)
    ec = lax.broadcasted_iota(jnp.int32, (HB, GW), 1)
    spread = (eh == g * R + ec // B_HEAD).astype(f32)
    sh = lax.broadcasted_iota(jnp.int32, (R8, HB), 1)
    sr = lax.broadcasted_iota(jnp.int32, (R8, HB), 0)
    pick_t = jnp.logical_and(sh == g * R + sr, sr < R).astype(f32)
    dtf = jnp.dot(dt, spread, precision=HI, preferred_element_type=f32)
    csf = jnp.dot(cs, spread, precision=HI, preferred_element_type=f32)
    dsf = jnp.dot(jnp.broadcast_to(dsk, (8, HB)), spread, precision=HI, preferred_element_type=f32)[0:1, :]
    cs_col = lax.dot_general(cs, pick_t, NT, precision=HI, preferred_element_type=f32)
    cs_row = lax.dot_general(pick_t, cs, NT, precision=HI, preferred_element_type=f32)
    xdt = xs * dtf
    cb = lax.dot_general(cm.astype(bf16), bm.astype(bf16), NT, preferred_element_type=f32)
    lane_head = lax.broadcasted_iota(jnp.int32, (1, GW), 1) // B_HEAD
    y = jnp.zeros((L, GW), f32)
    for r in range(R):
        seg = cs_col[:, r:r + 1] - cs_row[r:r + 1, :]
        dec = jnp.where(causal, jnp.exp(jnp.where(causal, seg, 0.0)), 0.0)
        xm = jnp.where(lane_head == r, xdt, 0.0)
        y = y + jnp.dot((cb * dec).astype(bf16), xm.astype(bf16), preferred_element_type=f32)
    csl = csf[L - 1:L, :]
    dte = jnp.exp(csl - csf)
    states = lax.dot_general(bm.astype(bf16), (xdt * dte).astype(bf16), TN, preferred_element_type=f32)
    y_off = jnp.dot(cm.astype(bf16), hp.astype(bf16), preferred_element_type=f32) * jnp.exp(csf)
    hn = hp * jnp.exp(csl) + states
    gated = (y + y_off + dsf * xs) * _silu(z)
    out = gated * lax.rsqrt(jnp.mean(gated * gated, axis=-1, keepdims=True) + RMS_EPS) * bn
    return hn, out


def _ssd_in_specs(D, HB, cidx):
    L, GW, N = B_CHUNK, D // B_GROUPS, B_STATE
    zoff, boff = 4 * D // GW, D // N
    return [
        pl.BlockSpec((L, GW), lambda c, g: (cidx(c), g)),
        pl.BlockSpec((L, N), lambda c, g: (cidx(c), boff + g)),
        pl.BlockSpec((L, N), lambda c, g: (cidx(c), boff + B_GROUPS + g)),
        pl.BlockSpec((L, GW), lambda c, g: (cidx(c), zoff + g)),
        pl.BlockSpec((L, HB), lambda c, g: (cidx(c), 0)),
        pl.BlockSpec((1, HB), lambda c, g: (0, 0)),
        pl.BlockSpec((1, HB), lambda c, g: (0, 0)),
        pl.BlockSpec((1, HB), lambda c, g: (0, 0)),
        pl.BlockSpec((1, GW), lambda c, g: (0, g)),
    ]


def ssd_fwd(name, xact, ymain, dtr, dtb, alog, dsk, bn, D):
    S, HB = dtr.shape
    nc, GW, R = S // B_CHUNK, D // B_GROUPS, HB // B_GROUPS

    def body(xs, bm, cm, z, dt_ref, dtb_ref, al_ref, ds_ref, bn_ref, o_ref, sv_ref, hs):
        g = pl.program_id(1)

        @pl.when(pl.program_id(0) == 0)
        def _():
            hs[g] = jnp.zeros((B_STATE, GW), f32)

        hp = hs[g]
        sv_ref[...] = hp
        hn, out = _ssd_chunk(hp, xs[...], bm[...], cm[...], z[...], dt_ref[...], dtb_ref[...], al_ref[...],
                             ds_ref[...], bn_ref[...], g, R)
        hs[g] = hn
        o_ref[...] = out.astype(o_ref.dtype)

    return pl.pallas_call(
        body, name=name, grid=(nc, B_GROUPS),
        in_specs=_ssd_in_specs(D, HB, lambda c: c),
        out_specs=[pl.BlockSpec((B_CHUNK, GW), lambda c, g: (c, g)),
                   pl.BlockSpec((None, None, B_STATE, GW), lambda c, g: (c, g, 0, 0))],
        out_shape=[jax.ShapeDtypeStruct((S, D), bf16),
                   jax.ShapeDtypeStruct((nc, B_GROUPS, B_STATE, GW), f32)],
        scratch_shapes=[pltpu.VMEM((B_GROUPS, B_STATE, GW), f32)],
        compiler_params=_cp("arbitrary", "arbitrary"))(xact, xact, xact, ymain, dtr, dtb, alog, dsk, bn)


def ssd_bwd(name, xact, ymain, dtr, dtb, alog, dsk, bn, saved, dmixed, D):
    S, HB = dtr.shape
    nc, GW, R = S // B_CHUNK, D // B_GROUPS, HB // B_GROUPS
    rev = lambda c: nc - 1 - c
    ooff = D // GW

    def body(xs, bm, cm, z, dt_ref, dtb_ref, al_ref, ds_ref, bn_ref, sv_ref, do_ref,
             dxs, dbm, dcm, dz, ddt, ddtb, dal, dds, dbn, dhs):
        c, g = pl.program_id(0), pl.program_id(1)

        @pl.when(c == 0)
        def _():
            dhs[g] = jnp.zeros((B_STATE, GW), f32)
            dbn[g] = jnp.zeros((1, GW), f32)

        @pl.when(jnp.logical_and(c == 0, g == 0))
        def _():
            ddtb[...] = jnp.zeros_like(ddtb)
            dal[...] = jnp.zeros_like(dal)
            dds[...] = jnp.zeros_like(dds)

        @pl.when(g == 0)
        def _():
            ddt[...] = jnp.zeros_like(ddt)

        fn = functools.partial(_ssd_chunk, g=g, R=R)
        _, vjp = jax.vjp(fn, sv_ref[...], xs[...], bm[...], cm[...], z[...], dt_ref[...], dtb_ref[...],
                         al_ref[...], ds_ref[...], bn_ref[...])
        gr = vjp((dhs[g], do_ref[...].astype(f32)))
        dhs[g] = gr[0]
        dxs[...] = gr[1]
        dbm[...] = gr[2]
        dcm[...] = gr[3]
        dz[...] = gr[4].astype(dz.dtype)
        ddt[...] += gr[5]
        ddtb[...] += gr[6]
        dal[...] += gr[7]
        dds[...] += gr[8]
        dbn[g] += gr[9]

    hb_vec = pl.BlockSpec((1, HB), lambda c, g: (0, 0))
    return pl.pallas_call(
        body, name=name, grid=(nc, B_GROUPS),
        in_specs=_ssd_in_specs(D, HB, rev) + [
            pl.BlockSpec((None, None, B_STATE, GW), lambda c, g: (rev(c), g, 0, 0)),
            pl.BlockSpec((B_CHUNK, GW), lambda c, g: (rev(c), ooff + g))],
        out_specs=[pl.BlockSpec((B_CHUNK, GW), lambda c, g: (rev(c), g)),
                   pl.BlockSpec((B_CHUNK, B_STATE), lambda c, g: (rev(c), g)),
                   pl.BlockSpec((B_CHUNK, B_STATE), lambda c, g: (rev(c), g)),
                   pl.BlockSpec((B_CHUNK, GW), lambda c, g: (rev(c), g)),
                   pl.BlockSpec((B_CHUNK, HB), lambda c, g: (rev(c), 0)),
                   hb_vec, hb_vec, hb_vec,
                   pl.BlockSpec((B_GROUPS, 1, GW), lambda c, g: (0, 0, 0))],
        out_shape=[jax.ShapeDtypeStruct((S, D), f32),
                   jax.ShapeDtypeStruct((S, B_GROUPS * B_STATE), f32),
                   jax.ShapeDtypeStruct((S, B_GROUPS * B_STATE), f32),
                   jax.ShapeDtypeStruct((S, D), bf16),
                   jax.ShapeDtypeStruct((S, HB), f32),
                   jax.ShapeDtypeStruct((1, HB), f32), jax.ShapeDtypeStruct((1, HB), f32),
                   jax.ShapeDtypeStruct((1, HB), f32),
                   jax.ShapeDtypeStruct((B_GROUPS, 1, GW), f32)],
        scratch_shapes=[pltpu.VMEM((B_GROUPS, B_STATE, GW), f32)],
        compiler_params=_cp("arbitrary", "arbitrary"))(
            xact, xact, xact, ymain, dtr, dtb, alog, dsk, bn, saved, dmixed)


def _conv_taps(xp, w_ref, b_ref, r0, K):
    acc = jnp.broadcast_to(b_ref[...], (CONV_ROWS, b_ref.shape[1]))
    for k in range(K):
        acc = acc + w_ref[k:k + 1, :] * xp[r0 + CONV_PAD - (K - 1) + k:r0 + CONV_PAD - (K - 1) + k + CONV_ROWS, :]
    return acc


def conv_fwd(name, x, xoff, w, b, act, out_dtype):
    S = x.shape[0]
    K, CW = w.shape
    tc = CONV_CH

    def body(x_ref, w_ref, b_ref, o_ref, xp):
        xp[0:CONV_PAD, :] = jnp.zeros((CONV_PAD, tc), f32)
        xp[CONV_PAD:CONV_PAD + S, :] = x_ref[...].astype(f32)
        for r0 in range(0, S, CONV_ROWS):
            acc = _conv_taps(xp, w_ref, b_ref, r0, K)
            if act:
                acc = _silu(acc)
            o_ref[r0:r0 + CONV_ROWS, :] = acc.astype(o_ref.dtype)

    return pl.pallas_call(
        body, name=name, grid=(CW // tc,),
        in_specs=[pl.BlockSpec((S, tc), lambda j: (0, j + xoff)),
                  pl.BlockSpec((K, tc), lambda j: (0, j)),
                  pl.BlockSpec((1, tc), lambda j: (0, j))],
        out_specs=pl.BlockSpec((S, tc), lambda j: (0, j)),
        out_shape=jax.ShapeDtypeStruct((S, CW), out_dtype),
        scratch_shapes=[pltpu.VMEM((S + CONV_PAD, tc), f32)],
        compiler_params=_cp("parallel"))(x, w, b)


def conv_bwd(name, x, xoff, w, b, dout, act, dx_dtype):
    S = x.shape[0]
    K, CW = w.shape
    tc = CONV_CH

    def body(x_ref, w_ref, b_ref, d_ref, dx_ref, dw_ref, db_ref, xp, dp):
        xp[0:CONV_PAD, :] = jnp.zeros((CONV_PAD, tc), f32)
        xp[CONV_PAD:CONV_PAD + S, :] = x_ref[...].astype(f32)
        dp[S:S + CONV_PAD, :] = jnp.zeros((CONV_PAD, tc), f32)
        db = jnp.zeros((1, tc), f32)
        for r0 in range(0, S, CONV_ROWS):
            d = d_ref[r0:r0 + CONV_ROWS, :].astype(f32)
            if act:
                pre = _conv_taps(xp, w_ref, b_ref, r0, K)
                s = _sigmoid(pre)
                d = d * (s + pre * s * (1.0 - s))
            dp[r0:r0 + CONV_ROWS, :] = d
            db = db + jnp.sum(d, axis=0, keepdims=True)
        db_ref[...] = db
        for r0 in range(0, S, CONV_ROWS):
            acc = jnp.zeros((CONV_ROWS, tc), f32)
            for k in range(K):
                acc = acc + w_ref[k:k + 1, :] * dp[r0 + (K - 1 - k):r0 + (K - 1 - k) + CONV_ROWS, :]
            dx_ref[r0:r0 + CONV_ROWS, :] = acc.astype(dx_ref.dtype)
        for k in range(K):
            acc = jnp.zeros((1, tc), f32)
            for r0 in range(0, S, CONV_ROWS):
                lo = r0 + CONV_PAD - (K - 1) + k
                acc = acc + jnp.sum(dp[r0:r0 + CONV_ROWS, :] * xp[lo:lo + CONV_ROWS, :], axis=0, keepdims=True)
            dw_ref[k:k + 1, :] = acc

    return pl.pallas_call(
        body, name=name, grid=(CW // tc,),
        in_specs=[pl.BlockSpec((S, tc), lambda j: (0, j + xoff)),
                  pl.BlockSpec((K, tc), lambda j: (0, j)),
                  pl.BlockSpec((1, tc), lambda j: (0, j)),
                  pl.BlockSpec((S, tc), lambda j: (0, j))],
        out_specs=[pl.BlockSpec((S, tc), lambda j: (0, j)),
                   pl.BlockSpec((K, tc), lambda j: (0, j)),
                   pl.BlockSpec((1, tc), lambda j: (0, j))],
        out_shape=[jax.ShapeDtypeStruct((S, CW), dx_dtype),
                   jax.ShapeDtypeStruct((K, CW), f32),
                   jax.ShapeDtypeStruct((1, CW), f32)],
        scratch_shapes=[pltpu.VMEM((S + CONV_PAD, tc), f32), pltpu.VMEM((S + CONV_PAD, tc), f32)],
        compiler_params=_cp("parallel"))(x, w, b, dout)


def loss_head(name, y, target, tm):
    S, D = y.shape

    def body(y_ref, t_ref, dy_ref, l_ref):
        @pl.when(pl.program_id(0) == 0)
        def _():
            l_ref[...] = jnp.zeros_like(l_ref)

        err = y_ref[...] - t_ref[...]
        dy_ref[...] = err * (1.0 / D)
        l_ref[...] += jnp.sum(err * err) * (0.5 / D)

    dy, l = pl.pallas_call(
        body, name=name, grid=(S // tm,),
        in_specs=[pl.BlockSpec((tm, D), lambda i: (i, 0))] * 2,
        out_specs=[pl.BlockSpec((tm, D), lambda i: (i, 0)), pl.BlockSpec((8, LANES), lambda i: (0, 0))],
        out_shape=[jax.ShapeDtypeStruct((S, D), f32), jax.ShapeDtypeStruct((8, LANES), f32)],
        compiler_params=_cp("arbitrary"))(y, target)
    return dy, l[0, 0]


def _flat2d_tiles(rows, cols, itemsize, target_bytes):
    tc = _tile(cols, 1024) if cols % LANES == 0 else cols
    cap = max(8, target_bytes // (tc * itemsize))
    tr = _tile(rows, cap, 16) if rows % 16 == 0 else rows
    return tr, tc


def adamw(name, w, g, m, v):
    shape = w.shape
    cols = shape[-1]
    rows = math.prod(shape[:-1])
    tr, tc = _flat2d_tiles(rows, cols, 4, 1 << 20)
    c1 = 1.0 - ADAM_B1 ** ADAM_STEP
    c2 = 1.0 - ADAM_B2 ** ADAM_STEP

    def body(w_ref, g_ref, m_ref, v_ref, d_ref, nm_ref, nv_ref, g_out_ref):
        gg = g_ref[...]
        nm = ADAM_B1 * m_ref[...] + (1.0 - ADAM_B1) * gg
        nv = ADAM_B2 * v_ref[...] + (1.0 - ADAM_B2) * (gg * gg)
        d_ref[...] = -ADAM_LR * ((nm / c1) / (jnp.sqrt(nv / c2) + ADAM_EPS) + ADAM_WD * w_ref[...])
        nm_ref[...] = nm
        nv_ref[...] = nv
        g_out_ref[...] = gg

    spec = pl.BlockSpec((tr, tc), lambda i, j: (i, j))
    outs = pl.pallas_call(
        body, name=name, grid=(rows // tr, cols // tc), in_specs=[spec] * 4, out_specs=[spec] * 4,
        out_shape=[jax.ShapeDtypeStruct((rows, cols), f32)] * 4,
        compiler_params=_cp("parallel", "parallel"))(*[a.reshape(rows, cols) for a in (w, g, m, v)])
    return [o.reshape(shape) for o in outs]


def _core_index():
    return lax.axis_index("c").astype(jnp.int32).reshape(1)


def _half_rows_tile(Rh, C):
    return _tile(Rh, max(16, (2 << 20) // (C * 2)), 16)


def rs_add(name, G, buf):
    P, _, Rh, C = G.shape
    tr = _half_rows_tile(Rh, C)

    def body(c_ref, g_ref, b_ref, o_ref):
        o_ref[...] = (g_ref[...].astype(f32) + b_ref[...].astype(f32)).astype(o_ref.dtype)

    return pl.pallas_call(
        body, name=name,
        grid_spec=pltpu.PrefetchScalarGridSpec(
            num_scalar_prefetch=1, grid=(P, Rh // tr),
            in_specs=[pl.BlockSpec((None, None, tr, C), lambda p, i, c: (p, c[0], i, 0)),
                      pl.BlockSpec((None, tr, C), lambda p, i, c: (p, i, 0))],
            out_specs=pl.BlockSpec((None, tr, C), lambda p, i, c: (p, i, 0))),
        out_shape=jax.ShapeDtypeStruct((P, Rh, C), bf16),
        compiler_params=_cp("parallel", "parallel"))(_core_index(), G, buf)


def _chip_indices():
    x, y, c = lax.axis_index("x"), lax.axis_index("y"), lax.axis_index("c")
    ids = [2 * x + y] + [2 * _flip(x, fx) + _flip(y, fy) for fx, fy in _CHIP_FLIPS] + [c]
    return [i.astype(jnp.int32).reshape(1) for i in ids]


def rs_sum4(name, pair, buf, final, layer):
    P, Rh, C = buf.shape
    tr = _half_rows_tile(Rh, C)

    def body(i0, i1, i2, i3, ic, b0, b1, b2, b3, f_ref, o_ref):
        o_ref[...] = ((b0[...].astype(f32) + b1[...].astype(f32)) + b2[...].astype(f32)) + b3[...].astype(f32)

    blk = (None, tr, C)
    return pl.pallas_call(
        body, name=name,
        grid_spec=pltpu.PrefetchScalarGridSpec(
            num_scalar_prefetch=5, grid=(Rh // tr,),
            in_specs=[pl.BlockSpec(blk, lambda i, *ids, k=k: (ids[k][0], i, 0)) for k in range(P)] + [_ANY],
            out_specs=pl.BlockSpec((None, None, tr, C), lambda i, *ids: (layer, ids[4][0], i, 0))),
        out_shape=jax.ShapeDtypeStruct(final.shape, final.dtype),
        input_output_aliases={9: 0},
        compiler_params=_cp("parallel"))(*_chip_indices(), pair, buf, buf, buf, final)


def place_own(name, w, layer):
    _, R, C = w.shape
    tr = _tile(R, max(16, (2 << 20) // (C * 2)), 16)

    def body(q, w_ref, own_ref, land_ref):
        wb = w_ref[...].astype(bf16)
        own_ref[...] = wb
        land_ref[...] = wb

    return pl.pallas_call(
        body, name=name,
        grid_spec=pltpu.PrefetchScalarGridSpec(
            num_scalar_prefetch=1, grid=(R // tr,),
            in_specs=[pl.BlockSpec((None, tr, C), lambda i, q: (layer, i, 0))],
            out_specs=[pl.BlockSpec((tr, C), lambda i, q: (i, 0)),
                       pl.BlockSpec((None, tr, C), lambda i, q: (q[0], i, 0))]),
        out_shape=[jax.ShapeDtypeStruct((R, C), bf16), jax.ShapeDtypeStruct((N_CHIPS, R, C), bf16)],
        compiler_params=_cp("parallel"))(_chip_indices()[0], w)


_ANY = pl.BlockSpec(memory_space=pl.ANY)
_CHIP_FLIPS = ((1, 0), (0, 1), (1, 1))


def _place():
    return lax.axis_index("x"), lax.axis_index("y"), lax.axis_index("c")


def _flip(v, f):
    return 1 - v if f else v


def _remote(src, dst, ssem, rsem, dev):
    return pltpu.make_async_remote_copy(src_ref=src, dst_ref=dst, send_sem=ssem, recv_sem=rsem,
                                        device_id=dev, device_id_type=MESH)


_HBM = pl.BlockSpec(memory_space=pltpu.HBM)
_SEM = pl.BlockSpec(memory_space=pltpu.SEMAPHORE)
_DATAFLOW = pltpu.SideEffectType.DATAFLOW_SIDE_EFFECTING
_TOKEN = jax.ShapeDtypeStruct((8, LANES), f32)


def _in_hbm(a):
    return pltpu.with_memory_space_constraint(a, pltpu.HBM)


def _uninit(tag, shape, dtype):
    def body(o_ref):
        pass

    return pl.pallas_call(body, name="uninit_" + tag, out_specs=_ANY,
                          out_shape=jax.ShapeDtypeStruct(shape, dtype))()


def _hbm_like(a):
    return pltpu.HBM(a.shape, a.dtype)


def gather_ici_start(name, groups, dep=None):
    sizes = [len(g) for g in groups]
    owns = [o for g in groups for o, _ in g]
    lands = [l for g in groups for _, l in g]
    n, ng = len(owns), len(groups)
    deps = [] if dep is None else [dep]

    def body(*refs):
        own, land = refs[:n], refs[n:2 * n]
        sems = refs[2 * n + len(deps):2 * n + len(deps) + 2 * ng]
        token = refs[-1]
        x, y, c = _place()
        q = 2 * x + y
        t = 0
        for gi, size in enumerate(sizes):
            for j in range(size):
                for k, (fx, fy) in enumerate(_CHIP_FLIPS):
                    _remote(own[t].at[c], land[t].at[q, c], sems[2 * gi].at[3 * j + k], sems[2 * gi + 1].at[3 * j + k],
                            (_flip(x, fx), _flip(y, fy), c)).start()
                t += 1
        token[...] = jnp.zeros_like(token)

    sem_shapes = [pltpu.SemaphoreType.DMA((3 * size,)) for size in sizes for _ in range(2)]
    res = pl.pallas_call(
        body, name=name,
        in_specs=[_HBM] * (2 * n) + [_ANY] * len(deps),
        out_specs=[_SEM] * (2 * ng) + [_HBM] * (2 * n) + [pl.BlockSpec(memory_space=pltpu.VMEM)],
        out_shape=sem_shapes + [_hbm_like(a) for a in owns + lands] + [_TOKEN],
        input_output_aliases={i: 2 * ng + i for i in range(2 * n)},
        compiler_params=pltpu.CompilerParams(has_side_effects=_DATAFLOW),
    )(*[_in_hbm(a) for a in owns + lands], *deps)
    own_thru, land_thru = res[2 * ng:2 * ng + n], res[2 * ng + n:2 * ng + 2 * n]
    handles, t = [], 0
    for gi, size in enumerate(sizes):
        handles.append((res[2 * gi], res[2 * gi + 1], list(own_thru[t:t + size]), list(land_thru[t:t + size])))
        t += size
    return handles, res[-1]


def gather_ici_wait(name, handle, after):
    send, recv, owns, lands = handle
    n = len(owns)

    def body(*refs):
        own, land = refs[:n], refs[n:2 * n]
        send_ref, recv_ref = refs[2 * n], refs[2 * n + 1]
        x, y, c = _place()
        for j in range(n):
            for k, (fx, fy) in enumerate(_CHIP_FLIPS):
                px, py = _flip(x, fx), _flip(y, fy)
                cp = _remote(own[j].at[c], land[j].at[2 * px + py, c], send_ref.at[3 * j + k], recv_ref.at[3 * j + k],
                             (px, py, c))
                cp.wait_send()
                cp.wait_recv()

    res = pl.pallas_call(
        body, name=name,
        in_specs=[_HBM] * (2 * n) + [_SEM, _SEM, _ANY],
        out_specs=[_HBM] * (2 * n),
        out_shape=[_hbm_like(a) for a in owns + lands],
        input_output_aliases={i: i for i in range(2 * n)},
        compiler_params=pltpu.CompilerParams(has_side_effects=_DATAFLOW),
    )(*owns, *lands, send, recv, after)
    return list(res[n:])


def _split_start(name, body, arrays, n_sems, dep=None):
    n = len(arrays)
    deps = [] if dep is None else [dep]

    def kernel_body(*refs):
        m = n + len(deps)
        body(refs[:n], refs[m], refs[m + 1])
        refs[-1][...] = jnp.zeros_like(refs[-1])

    res = pl.pallas_call(
        kernel_body, name=name,
        in_specs=[_HBM] * n + [_ANY] * len(deps),
        out_specs=[_SEM, _SEM] + [_HBM] * n + [pl.BlockSpec(memory_space=pltpu.VMEM)],
        out_shape=[pltpu.SemaphoreType.DMA((n_sems,)), pltpu.SemaphoreType.DMA((n_sems,))]
        + [_hbm_like(a) for a in arrays] + [_TOKEN],
        input_output_aliases={i: 2 + i for i in range(n)},
        compiler_params=pltpu.CompilerParams(has_side_effects=_DATAFLOW),
    )(*[_in_hbm(a) for a in arrays], *deps)
    return res[0], res[1], list(res[2:2 + n]), res[-1]


def _split_wait(name, body, handle, after):
    send, recv, arrays, _ = handle
    n = len(arrays)

    def kernel_body(*refs):
        body(refs[:n], refs[n], refs[n + 1])

    res = pl.pallas_call(
        kernel_body, name=name,
        in_specs=[_HBM] * n + [_SEM, _SEM, _ANY],
        out_specs=[_HBM] * n,
        out_shape=[_hbm_like(a) for a in arrays],
        input_output_aliases={i: i for i in range(n)},
        compiler_params=pltpu.CompilerParams(has_side_effects=_DATAFLOW),
    )(*arrays, send, recv, after)
    return list(res)


def _forward_copies(land, send, recv):
    x, y, c = _place()
    for t in range(len(land)):
        for k, (fx, fy) in enumerate(_CHIP_FLIPS):
            slab = land[t].at[2 * _flip(x, fx) + _flip(y, fy), c]
            yield _remote(slab, slab, send.at[3 * t + k], recv.at[3 * t + k], (x, y, 1 - c))


def forward_start(name, lands):
    def body(land, send, recv):
        for cp in _forward_copies(land, send, recv):
            cp.start()

    return _split_start(name, body, lands, 3 * len(lands))


def forward_wait(name, handle, after):
    def body(land, send, recv):
        for cp in _forward_copies(land, send, recv):
            cp.wait_send()
            cp.wait_recv()

    return _split_wait(name, body, handle, after)


def sibling_start(name, Gs, dep=None):
    T = len(Gs)
    bufs = [_uninit(f"{name}_{t}", (G.shape[0],) + G.shape[2:], G.dtype) for t, G in enumerate(Gs)]

    def body(refs, send, recv):
        x, y, c = _place()
        for t in range(T):
            _remote(refs[t].at[:, 1 - c], refs[T + t], send.at[t], recv.at[t], (x, y, 1 - c)).start()

    return _split_start(name, body, list(Gs) + bufs, T, dep)


def sibling_wait(name, handle, after):
    T = len(handle[2]) // 2

    def body(refs, send, recv):
        x, y, c = _place()
        for t in range(T):
            cp = _remote(refs[t].at[:, 1 - c], refs[T + t], send.at[t], recv.at[t], (x, y, 1 - c))
            cp.wait_send()
            cp.wait_recv()

    res = _split_wait(name, body, handle, after)
    return res[:T], res[T:]


def reduce_ici_start(name, Ss):
    T = len(Ss)
    lands = [_uninit(f"{name}_{t}", S.shape, S.dtype) for t, S in enumerate(Ss)]

    def body(*refs):
        s, land = refs[:T], refs[T:2 * T]
        send, recv = refs[2 * T], refs[2 * T + 1]
        token = refs[-1]
        x, y, c = _place()
        q = 2 * x + y
        for t in range(T):
            for k, (fx, fy) in enumerate(_CHIP_FLIPS):
                px, py = _flip(x, fx), _flip(y, fy)
                _remote(s[t].at[2 * px + py], land[t].at[q], send.at[3 * t + k], recv.at[3 * t + k],
                        (px, py, c)).start()
        token[...] = jnp.zeros_like(token)

    res = pl.pallas_call(
        body, name=name,
        in_specs=[_HBM] * (2 * T),
        out_specs=[_SEM, _SEM] + [_HBM] * (2 * T) + [pl.BlockSpec(memory_space=pltpu.VMEM)],
        out_shape=[pltpu.SemaphoreType.DMA((3 * T,)), pltpu.SemaphoreType.DMA((3 * T,))]
        + [_hbm_like(a) for a in Ss + lands] + [_TOKEN],
        input_output_aliases={i: 2 + i for i in range(2 * T)},
        compiler_params=pltpu.CompilerParams(has_side_effects=_DATAFLOW),
    )(*[_in_hbm(a) for a in Ss + lands])
    return res[0], res[1], list(res[2:2 + T]), list(res[2 + T:2 + 2 * T]), res[-1]


def reduce_ici_wait(name, handle, after):
    send, recv, Ss, lands, _ = handle
    T = len(Ss)

    def body(*refs):
        s, land = refs[:T], refs[T:2 * T]
        send_ref, recv_ref = refs[2 * T], refs[2 * T + 1]
        x, y, c = _place()
        for t in range(T):
            for k, (fx, fy) in enumerate(_CHIP_FLIPS):
                px, py = _flip(x, fx), _flip(y, fy)
                cp = _remote(s[t].at[2 * px + py], land[t].at[2 * px + py], send_ref.at[3 * t + k], recv_ref.at[3 * t + k],
                             (px, py, c))
                cp.wait_send()
                cp.wait_recv()

    res = pl.pallas_call(
        body, name=name,
        in_specs=[_HBM] * (2 * T) + [_SEM, _SEM, _ANY],
        out_specs=[_HBM] * (2 * T),
        out_shape=[_hbm_like(a) for a in Ss + lands],
        input_output_aliases={i: i for i in range(2 * T)},
        compiler_params=pltpu.CompilerParams(has_side_effects=_DATAFLOW),
    )(*Ss, *lands, send, recv, after)
    return list(res[:T]), list(res[T:])


def _share_copies(tots, layers, send, recv):
    x, y, c = _place()
    for t in range(len(tots)):
        mine = tots[t].at[layers[t], c]
        yield _remote(mine, mine, send.at[t], recv.at[t], (x, y, 1 - c))


def share_start(name, tots, layers):
    def body(refs, send, recv):
        for cp in _share_copies(refs, layers, send, recv):
            cp.start()

    return _split_start(name, body, tots, len(tots))


def share_wait(name, handle, layers, after):
    def body(refs, send, recv):
        for cp in _share_copies(refs, layers, send, recv):
            cp.wait_send()
            cp.wait_recv()

    return _split_wait(name, body, handle, after)


def all_reduce_small(name, vec):
    rows = vec.shape[0]
    flips = [(fx, fy, fc) for fx in (0, 1) for fy in (0, 1) for fc in (0, 1)][1:]

    def body(v_ref, o_ref, buf, send, recv):
        x, y, c = _place()
        me = 4 * x + 2 * y + c
        buf[me] = v_ref[...]
        cps = []
        for k, (fx, fy, fc) in enumerate(flips):
            cp = _remote(buf.at[me], buf.at[me], send.at[k], recv.at[k],
                         (_flip(x, fx), _flip(y, fy), _flip(c, fc)))
            cp.start()
            cps.append(cp)
        for k, (fx, fy, fc) in enumerate(flips):
            slab = buf.at[4 * _flip(x, fx) + 2 * _flip(y, fy) + _flip(c, fc)]
            _remote(slab, slab, send.at[k], recv.at[k], (x, y, c)).wait_recv()
        for cp in cps:
            cp.wait_send()
        acc = buf[0]
        for d in range(1, N_DEV):
            acc = acc + buf[d]
        o_ref[...] = acc

    return pl.pallas_call(
        body, name=name,
        in_specs=[pl.BlockSpec(memory_space=pltpu.VMEM)], out_specs=pl.BlockSpec(memory_space=pltpu.VMEM),
        out_shape=jax.ShapeDtypeStruct((rows, LANES), f32),
        scratch_shapes=[pltpu.VMEM((N_DEV, rows, LANES), f32),
                        pltpu.SemaphoreType.DMA((N_DEV - 1,)), pltpu.SemaphoreType.DMA((N_DEV - 1,))],
        compiler_params=pltpu.CompilerParams(vmem_limit_bytes=V7X_VMEM_LIMIT))(vec)


def _all_done(arrays):
    return jnp.stack([a[(0,) * a.ndim].astype(f32) for a in arrays]).sum(keepdims=True)


def _pack(arrays):
    flat = jnp.concatenate([a.reshape(-1) for a in arrays])
    n = flat.shape[0]
    rows = -(-n // (8 * LANES)) * 8
    return jnp.pad(flat, (0, rows * LANES - n)).reshape(rows, LANES)


def _unpack(vec, shapes):
    flat = vec.reshape(-1)
    out, pos = [], 0
    for s in shapes:
        n = math.prod(s)
        out.append(flat[pos:pos + n].reshape(s))
        pos += n
    return out


def _f_first(x, g):
    return x, _rms(x, g)


def _f_mid(h, m, gp, gn):
    h1 = h + _rms(m, gp)
    return h1, _rms(h1, gn)


def _f_mid_bias(h, m, b, gp, gn):
    h1 = h + _rms(m + b, gp)
    return h1, _rms(h1, gn)


def _f_last(h, m, gp):
    return (h + _rms(m, gp),)


def _f_swiglu(gate, up):
    return (_silu(gate) * up,)


def _swiglu_tile(gate, up):
    return _silu(gate.astype(f32)) * up.astype(f32)


def _swiglu_bwd_tile(d_act, gate, up):
    _, vjp = jax.vjp(_f_swiglu, gate.astype(f32), up.astype(f32))
    return vjp((d_act,))


def _f_glu(a, g, ba, bg):
    return ((a + ba) * _sigmoid(g + bg),)


def _f_ln_silu(x, g, b):
    mu = jnp.mean(x, axis=-1, keepdims=True)
    xc = x - mu
    y = xc * lax.rsqrt(jnp.mean(xc * xc, axis=-1, keepdims=True) + LN_EPS) * g + b
    return (_silu(y),)


def _f_lower_bounds(logits):
    n = logits.shape[0]
    e = jnp.exp(logits - jnp.max(logits, axis=0, keepdims=True))
    p = e / jnp.sum(e, axis=0, keepdims=True)
    layer = lax.broadcasted_iota(jnp.int32, logits.shape, 0)
    out = -jnp.broadcast_to(p[0:1, :], logits.shape)
    for j in range(n):
        out = out + jnp.where(layer >= j, p[j:j + 1, :], 0.0)
    return (out,)


WEIGHT_NAMES = ['mix_pre_g', 'mix_post_g', 'ffn_pre_g', 'ffn_post_g', 'hgrn_lb_logits', 'even_w_in',
                'hgrn_norm_g', 'ssd_conv_w', 'ssd_conv_b', 'ssd_dt_bias', 'ssd_a_log', 'ssd_d', 'ssd_norm_g',
                'even_w_out', 'conf_w1', 'conf_b1', 'conf_dw_w', 'conf_dw_b', 'conf_ln_g', 'conf_ln_b',
                'conf_w2', 'conf_b2', 'ffn_w_gate', 'ffn_w_up', 'ffn_w_down']
BIG = ['even_w_in', 'even_w_out', 'conf_w1', 'conf_w2', 'ffn_w_gate', 'ffn_w_up', 'ffn_w_down']
SMALL_SHARDED = {'ssd_conv_w': 2, 'conf_b1': 1, 'conf_dw_w': 2, 'conf_dw_b': 1, 'conf_ln_g': 1,
                 'conf_ln_b': 1, 'conf_b2': 1}


def _train_step(x, target, w, m, v):
    S, D = x.shape[1], x.shape[2]
    x2, t2 = x[0], target[0]
    NL = w['mix_pre_g'].shape[0]
    HB = w['ssd_dt_bias'].shape[1]
    GN = B_GROUPS * B_STATE
    xw, yw, cw = _place()
    chip = 2 * xw + yw
    tm = _tile(S, 128, 8)
    row1 = lambda a, i: a[i:i + 1]

    sharded = list(SMALL_SHARDED)
    placed = []
    for n in sharded:
        ax, a = SMALL_SHARDED[n], w[n]
        full = jnp.zeros(a.shape[:ax] + (a.shape[ax] * N_CHIPS,) + a.shape[ax + 1:], f32)
        start = [0] * a.ndim
        start[ax] = chip * a.shape[ax]
        placed.append(lax.dynamic_update_slice(full, jnp.where(cw == 0, a, 0.0), start))
    whole = dict(zip(sharded, _unpack(all_reduce_small("gather_small", _pack(placed)), [p.shape for p in placed])))
    small = {n: whole.get(n, w[n]) for n in WEIGHT_NAMES if n not in BIG}

    def mixer_keys(layer):
        names = ('even_w_in', 'even_w_out') if layer % 2 == 0 else ('conf_w1', 'conf_w2')
        return [(n, layer // 2) for n in names]

    def ffn_keys(layer):
        return [(n, layer) for n in ('ffn_w_gate', 'ffn_w_up', 'ffn_w_down')]

    groups = [keys(layer) for layer in range(NL) for keys in (mixer_keys, ffn_keys)]
    halves = lambda a: a.reshape(a.shape[:-2] + (2, a.shape[-2] // 2, a.shape[-1]))
    own, land = {}, {}

    def start_groups(name, some, dep=None):
        for g in some:
            for n, l in g:
                own[n, l], land[n, l] = place_own("place_own", w[n], l)
        return gather_ici_start(name, [[(halves(own[k]), halves(land[k])) for k in g] for g in some], dep)

    first, first_started = start_groups("gather_start_first", groups[:1])
    rest, rest_started = start_groups("gather_start_rest", groups[1:], first_started)
    handles = first + rest
    W, handed = {}, {}

    def fetch_begin(gi, after):
        arrived = gather_ici_wait(f"gather_wait_{gi}", handles[gi], after)
        handed[gi] = forward_start(f"forward_start_{gi}", arrived)
        return handed[gi][-1]

    def fetch_end(gi, after):
        for k, a in zip(groups[gi], forward_wait(f"forward_wait_{gi}", handed[gi], after)):
            W[k] = a.reshape((N_CHIPS, 1) + own[k].shape)

    WM = 6 * D + 2 * GN
    w_main, w_dt = {}, {}

    n_even = small['hgrn_lb_logits'].shape[0]
    (lbs,) = _stage_fwd("lower_bounds", _f_lower_bounds, [small['hgrn_lb_logits']], [], [(D, f32)], tm=n_even)
    saved = []
    h = x2
    (u,) = _stage_fwd("pre_norm", lambda a, g: (_rms(a, g),), [h],
                      [row1(small['mix_pre_g'], 0) + rest_started[0, 0]], [(D, bf16)], tm=tm)
    fetch_begin(0, u)
    for layer in range(NL):
        li = layer // 2
        r = {'h': h, 'u': u}
        fetch_end(2 * layer, u)
        if layer % 2 == 0:
            wm, wd = join_in_proj("in_proj_join", W['even_w_in', li][:, 0], WM)
            w_main[li], w_dt[li] = wm[None, None], wd[None, None]
            r['ymain'] = mm_nn_col("in_proj", u, w_main[li], 0)
            r['dtr'] = mm_nn_col("in_proj_dt", u, w_dt[li], 0)
            r['xact'] = conv_fwd("ssd_conv", r['ymain'], 5 * D // CONV_CH, small['ssd_conv_w'][li],
                                 row1(small['ssd_conv_b'], li), True, f32)
            o_a, r['hg_st'] = hgrn_fwd("hgrn", r['ymain'], row1(lbs, li), row1(small['hgrn_norm_g'], li), D)
            begun = fetch_begin(2 * layer + 1, o_a)[0, 0]
            o_b, r['ssd_st'] = ssd_fwd("ssd", r['xact'], r['ymain'], r['dtr'], row1(small['ssd_dt_bias'], li),
                                       row1(small['ssd_a_log'], li), row1(small['ssd_d'], li),
                                       row1(small['ssd_norm_g'], li) + begun, D)
            r['mixed'] = jnp.concatenate([o_a, o_b], axis=1)
            r['m'] = mm_nn_row("out_proj", r['mixed'], W['even_w_out', li], 0)
            mid_fn, mid_par = _f_mid, []
        else:
            r['c1'] = mm_nn_col("conf_in", u, W['conf_w1', li], 0)
            b1 = row1(small['conf_b1'], li)
            tn = _tile(D, 512)
            (r['glu'],) = _stage_fwd("conf_glu", _f_glu, [r['c1'], (r['c1'], D // tn)], [b1, (b1, D // tn)],
                                     [(D, f32)], tm=tm, tn=tn)
            r['cc'] = conv_fwd("conf_conv", r['glu'], 0, small['conf_dw_w'][li], row1(small['conf_dw_b'], li),
                               False, f32)
            begun = fetch_begin(2 * layer + 1, r['cc'])[0, 0]
            (r['c2'],) = _stage_fwd("conf_ln", _f_ln_silu, [r['cc']],
                                    [row1(small['conf_ln_g'], li) + begun, row1(small['conf_ln_b'], li)],
                                    [(D, bf16)], tm=tm)
            r['m'] = mm_nn_row("conf_out", r['c2'], W['conf_w2', li], 0)
            mid_fn, mid_par = _f_mid_bias, [row1(small['conf_b2'], li)]
        r['mid_fn'] = mid_fn
        r['mid_par'] = mid_par + [row1(small['mix_post_g'], layer), row1(small['ffn_pre_g'], layer)]
        r['h1'], r['u2'] = _stage_fwd("mid_norm", mid_fn, [h, r['m']], r['mid_par'], [(D, f32), (D, bf16)], tm=tm)
        fetch_end(2 * layer + 1, r['u2'])
        r['gate'] = mm_nn_col("ffn_gate", r['u2'], W['ffn_w_gate', layer], 0, bf16)
        begun = fetch_begin(2 * layer + 2, r['gate']) if layer + 1 < NL else None
        r['up'] = mm_nn_col("ffn_up", r['u2'], W['ffn_w_up', layer], 0, bf16, dep=begun)
        r['dn'] = mm_nn_row("ffn_down", [r['gate'], r['up']], W['ffn_w_down', layer], 0, a_fn=_swiglu_tile)
        if layer + 1 < NL:
            r['end_fn'] = _f_mid
            r['end_par'] = [row1(small['ffn_post_g'], layer), row1(small['mix_pre_g'], layer + 1)]
            h, u = _stage_fwd("end_norm", _f_mid, [r['h1'], r['dn']], r['end_par'], [(D, f32), (D, bf16)], tm=tm)
        else:
            r['end_fn'] = _f_last
            r['end_par'] = [row1(small['ffn_post_g'], layer)]
            (h,) = _stage_fwd("last_norm", _f_last, [r['h1'], r['dn']], r['end_par'], [(D, f32)], tm=tm)
        saved.append(r)

    dy, loss_local = loss_head("loss_head", h, t2, tm)
    loss = lax.psum(loss_local, ("x", "y", "c"))

    gs_rows = {n: [None] * small[n].shape[0] for n in small}
    gs = {}

    def put(n, i, val):
        gs_rows[n][i] = val.reshape(small[n].shape[1:])

    final = {n: _uninit(n, (w[n].shape[0], 2, w[n].shape[1] // 2, w[n].shape[2]), f32) for n in BIG}
    to_sibling, to_chips = [], []

    def reduce_begin(gi, parts, dep=None):
        handle = sibling_start(f"sibling_start_{gi}", [halves(p) for p in parts], dep)
        to_sibling.append((gi, handle))
        return handle[-1][0, 0]

    def reduce_middle(after):
        gi, handle = to_sibling.pop(0)
        parts, from_sib = sibling_wait(f"sibling_wait_{gi}", handle, after)
        sums = [rs_add("reduce_add", a, b) for a, b in zip(parts, from_sib)]
        handle = reduce_ici_start(f"reduce_start_{gi}", sums)
        to_chips.append((gi, handle))
        return handle[-1][0, 0]

    to_share = []

    def share_finish(after):
        while to_share:
            gi, handle = to_share.pop(0)
            arrived = share_wait(f"share_wait_{gi}", handle, [l for _, l in groups[gi]], after)
            final.update(zip([n for n, _ in groups[gi]], arrived))

    def reduce_finish(after):
        share_finish(after)
        gi, handle = to_chips.pop(0)
        keys = groups[gi]
        sums, lands = reduce_ici_wait(f"reduce_wait_{gi}", handle, after)
        for (n, l), s_, b_ in zip(keys, sums, lands):
            final[n] = rs_sum4("reduce_sum", s_, b_, final[n], l)
        names = [n for n, _ in keys]
        handle = share_start(f"share_start_{gi}", [final[n] for n in names], [l for _, l in keys])
        final.update(zip(names, handle[2]))
        to_share.append((gi, handle))

    def reduce_step(gi, parts, newest, dep=None):
        zero = reduce_begin(gi, parts, dep)
        if to_chips:
            reduce_finish(newest)
        if len(to_sibling) > 1:
            zero = zero + reduce_middle(newest)
        return zero

    grads, first_layer = {}, {}

    def small_gradients(dh, du_parts, started):
        (first_layer['grad_x'],), pg = _stage_bwd("pre_norm_bwd", _f_first, [x2],
                                                   [row1(small['mix_pre_g'], 0) + started],
                                                   [[dh], du_parts], [f32], tm=tm)
        put('mix_pre_g', 0, pg[0])
        dlbs = jnp.concatenate([saved[2 * i]['dlb'] for i in range(n_even)], axis=0)
        (dlogits,), _ = _stage_bwd("lower_bounds_bwd", _f_lower_bounds, [small['hgrn_lb_logits']], [],
                                   [[dlbs]], [f32], tm=n_even)
        names_s = [n for n in WEIGHT_NAMES if n not in BIG]
        for n in names_s:
            gs[n] = dlogits if n == 'hgrn_lb_logits' else jnp.stack(gs_rows[n])
        total = all_reduce_small("reduce_small", _pack([gs[n] for n in names_s]))
        for n, a in zip(names_s, _unpack(total, [gs[n].shape for n in names_s])):
            if n in SMALL_SHARDED:
                ax = SMALL_SHARDED[n]
                size = w[n].shape[ax]
                start = [0] * a.ndim
                start[ax] = chip * size
                a = lax.dynamic_slice(a, start, a.shape[:ax] + (size,) + a.shape[ax + 1:])
            grads[n] = a
        return total

    dh = dy
    du_parts = None
    started = None
    for layer in reversed(range(NL)):
        li = layer // 2
        r = saved[layer]
        cts = [[dh]] if du_parts is None else [[dh], du_parts]
        par = r['end_par'] if started is None else [r['end_par'][0] + started] + r['end_par'][1:]
        (dh1, d_dn), pg = _stage_bwd("end_norm_bwd", r['end_fn'], [r['h1'], r['dn']], par, cts,
                                     [f32, bf16], tm=tm)
        put('ffn_post_g', layer, pg[0])
        if du_parts is not None:
            put('mix_pre_g', layer + 1, pg[1])
        d_gate, d_up = mm_nt_row("ffn_down_dx", d_dn, W['ffn_w_down', layer], 0, bf16,
                                 tail=(_swiglu_bwd_tile, [r['gate'], r['up']]), n_out=2)
        g_down = mm_tn_row("ffn_down_dw", [r['gate'], r['up']], d_dn, N_CHIPS, a_fn=_swiglu_tile)
        du_a = mm_nt_col("ffn_gate_dx", d_gate, W['ffn_w_gate', layer], 0)
        du_b = mm_nt_col("ffn_up_dx", d_up, W['ffn_w_up', layer], 0)
        g_gate = mm_tn_col("ffn_gate_dw", r['u2'], d_gate, N_CHIPS)
        g_up = mm_tn_col("ffn_up_dw", r['u2'], d_up, N_CHIPS)
        started = reduce_step(2 * layer + 1, [g_gate, g_up, g_down], g_up)
        par = r['mid_par'][:-1] + [r['mid_par'][-1] + started]
        (dh, dm), pg = _stage_bwd("mid_norm_bwd", r['mid_fn'], [r['h'], r['m']], par,
                                  [[dh1], [du_a, du_b]], [f32, bf16], tm=tm)
        put('mix_post_g', layer, pg[-2])
        put('ffn_pre_g', layer, pg[-1])
        if layer % 2 == 0:
            d_mixed = mm_nt_row("out_proj_dx", dm, W['even_w_out', li], 0)
            g_out = mm_tn_row("out_proj_dw", r['mixed'], dm, N_CHIPS)
            dxs, dbm, dcm, dz, ddt, ddtb, dal, dds, dbn = ssd_bwd(
                "ssd_bwd", r['xact'], r['ymain'], r['dtr'], row1(small['ssd_dt_bias'], li),
                row1(small['ssd_a_log'], li), row1(small['ssd_d'], li), row1(small['ssd_norm_g'], li),
                r['ssd_st'], d_mixed, D)
            put('ssd_dt_bias', li, ddtb)
            put('ssd_a_log', li, dal)
            put('ssd_d', li, dds)
            put('ssd_norm_g', li, dbn)
            d_xact = jnp.concatenate([dxs, dbm, dcm], axis=1)
            d_xbc, dcw, dcb = conv_bwd("ssd_conv_bwd", r['ymain'], 5 * D // CONV_CH, small['ssd_conv_w'][li],
                                       row1(small['ssd_conv_b'], li), d_xact, True, bf16)
            put('ssd_conv_w', li, dcw)
            put('ssd_conv_b', li, dcb)
            dq, df, dv, dg, dlb, dan = hgrn_bwd("hgrn_bwd", r['ymain'], row1(lbs, li), row1(small['hgrn_norm_g'], li),
                                               r['hg_st'], d_mixed, D)
            put('hgrn_norm_g', li, dan)
            r['dlb'] = dlb
            d_main = jnp.concatenate([dq, df, dv, dg, dz, d_xbc], axis=1)
            du_parts = [mm_nt_col("in_proj_dx", d_main, w_main[li], 0),
                        mm_nt_col("in_proj_dt_dx", ddt, w_dt[li], 0)]
            g_in = split_in_proj("in_proj_split", mm_tn_col("in_proj_dw", r['u'], d_main, 1)[0],
                                 mm_tn_col("in_proj_dt_dw", r['u'], ddt, 1)[0], N_CHIPS)
            mixer_parts, newest = [g_in, g_out], g_in
        else:
            put('conf_b2', li, pg[0])
            d_c2 = mm_nt_row("conf_out_dx", dm, W['conf_w2', li], 0)
            g_w2 = mm_tn_row("conf_out_dw", r['c2'], dm, N_CHIPS)
            (d_cc,), pl_ = _stage_bwd("conf_ln_bwd", _f_ln_silu, [r['cc']],
                                      [row1(small['conf_ln_g'], li), row1(small['conf_ln_b'], li)],
                                      [[d_c2]], [f32], tm=tm)
            put('conf_ln_g', li, pl_[0])
            put('conf_ln_b', li, pl_[1])
            d_glu, ddw, ddb = conv_bwd("conf_conv_bwd", r['glu'], 0, small['conf_dw_w'][li],
                                       row1(small['conf_dw_b'], li), d_cc, False, f32)
            put('conf_dw_w', li, ddw)
            put('conf_dw_b', li, ddb)
            b1 = row1(small['conf_b1'], li)
            tn = _tile(D, 512)
            (da, dg_), pb = _stage_bwd("conf_glu_bwd", _f_glu, [r['c1'], (r['c1'], D // tn)], [b1, (b1, D // tn)],
                                       [[d_glu]], [bf16, bf16], tm=tm, tn=tn)
            put('conf_b1', li, jnp.concatenate([pb[0], pb[1]], axis=1))
            d_c1 = jnp.concatenate([da, dg_], axis=1)
            du_parts = [mm_nt_col("conf_in_dx", d_c1, W['conf_w1', li], 0)]
            g_w1 = mm_tn_col("conf_in_dw", r['u'], d_c1, N_CHIPS)
            mixer_parts, newest = [g_w1, g_w2], g_w1
        small_done = small_gradients(dh, du_parts, started) if layer == 0 else None
        started = reduce_step(2 * layer, mixer_parts, newest, small_done)
    started = started + reduce_middle(mixer_parts[0])

    delta, new_m, new_v = {}, {}, {}

    def update(names):
        for n in names:
            g = final[n].reshape(w[n].shape) if n in BIG else grads[n]
            delta[n], new_m[n], new_v[n], grads[n] = adamw("adamw", w[n], g, m[n], v[n])
        return _all_done([delta[n] for n in names])

    second_last = [n for n, _ in groups[to_chips[0][0]]]
    last = [n for n, _ in groups[to_chips[1][0]]]
    grads[WEIGHT_NAMES[0]] = grads[WEIGHT_NAMES[0]] + started
    grad_x2 = first_layer['grad_x']
    share_finish(grad_x2)
    done = update([n for n in WEIGHT_NAMES if n not in last + second_last])
    reduce_finish(done)
    share_finish(done)
    done = update(second_last)
    reduce_finish(done)
    share_finish(done)
    update(last)
    return (loss, grad_x2[None], *[grads[n] for n in WEIGHT_NAMES], *[delta[n] for n in WEIGHT_NAMES],
            *[new_m[n] for n in WEIGHT_NAMES], *[new_v[n] for n in WEIGHT_NAMES])


def kernel(x, mix_pre_g, mix_post_g, ffn_pre_g, ffn_post_g, hgrn_lb_logits, even_w_in, hgrn_norm_g, ssd_conv_w, ssd_conv_b, ssd_dt_bias, ssd_a_log, ssd_d, ssd_norm_g, even_w_out, conf_w1, conf_b1, conf_dw_w, conf_dw_b, conf_ln_g, conf_ln_b, conf_w2, conf_b2, ffn_w_gate, ffn_w_up, ffn_w_down, loss_target, m_mix_pre_g, m_mix_post_g, m_ffn_pre_g, m_ffn_post_g, m_hgrn_lb_logits, m_even_w_in, m_hgrn_norm_g, m_ssd_conv_w, m_ssd_conv_b, m_ssd_dt_bias, m_ssd_a_log, m_ssd_d, m_ssd_norm_g, m_even_w_out, m_conf_w1, m_conf_b1, m_conf_dw_w, m_conf_dw_b, m_conf_ln_g, m_conf_ln_b, m_conf_w2, m_conf_b2, m_ffn_w_gate, m_ffn_w_up, m_ffn_w_down, v_mix_pre_g, v_mix_post_g, v_ffn_pre_g, v_ffn_post_g, v_hgrn_lb_logits, v_even_w_in, v_hgrn_norm_g, v_ssd_conv_w, v_ssd_conv_b, v_ssd_dt_bias, v_ssd_a_log, v_ssd_d, v_ssd_norm_g, v_even_w_out, v_conf_w1, v_conf_b1, v_conf_dw_w, v_conf_dw_b, v_conf_ln_g, v_conf_ln_b, v_conf_w2, v_conf_b2, v_ffn_w_gate, v_ffn_w_up, v_ffn_w_down):
    args = locals()
    w = {n: args[n] for n in WEIGHT_NAMES}
    m = {n: args["m_" + n] for n in WEIGHT_NAMES}
    v = {n: args["v_" + n] for n in WEIGHT_NAMES}
    return _train_step(x, loss_target, w, m, v)
```

```python
import functools
import math

import jax
import jax.numpy as jnp
from jax import lax
from jax.experimental import pallas as pl
from jax.experimental.pallas import tpu as pltpu

f32 = jnp.float32
bf16 = jnp.bfloat16
MESH = pl.DeviceIdType.MESH
HI = lax.Precision.HIGHEST

A_HEAD = 128
A_CHUNK = 64
A_SUB = 8
A_REF = 4
A_EXP_CAP = 60.0
A_HEADS_PER_STEP = 16
A_F_MIN = 1e-6
B_HEAD = 64
B_GROUPS = 4
B_STATE = 128
B_CONV = 4
B_CHUNK = 128
C_KERNEL = 31
RMS_EPS = 1e-6
LN_EPS = 1e-5
ADAM_LR = 0.001
ADAM_B1 = 0.9
ADAM_B2 = 0.999
ADAM_EPS = 1e-08
ADAM_WD = 0.01
ADAM_STEP = 10

N_CHIPS = 4
N_DEV = 8
V7X_VMEM_LIMIT = 56 * 1024 * 1024
LANES = 128
CONV_PAD = 32
CONV_ROWS = 128
CONV_CH = 256

NN = (((1,), (0,)), ((), ()))
NT = (((1,), (1,)), ((), ()))
TN = (((0,), (0,)), ((), ()))


def _tile(n, cap, unit=LANES):
    best = None
    for t in range(unit, min(n, cap) + 1, unit):
        if n % t == 0:
            best = t
    return n if best is None else best


def _cp(*sem):
    return pltpu.CompilerParams(dimension_semantics=sem, vmem_limit_bytes=V7X_VMEM_LIMIT)


def _sigmoid(x):
    return jax.nn.sigmoid(x)


def _silu(x):
    return x * jax.nn.sigmoid(x)


def _rms(x, g):
    return x * lax.rsqrt(jnp.mean(x * x, axis=-1, keepdims=True) + RMS_EPS) * g


def _pair(a):
    return a if isinstance(a, tuple) else (a, 0)


def _stage(name, fn, rows, params, outs, par_outs=(), *, tm, tn=None):
    rows = [_pair(r) for r in rows]
    params = [_pair(p) for p in params]
    S = rows[0][0].shape[0]
    n_in, n_o = len(rows) + len(params), len(outs)
    if tn is None:
        grid = (S // tm,)
        in_specs = [pl.BlockSpec((tm, a.shape[1]), lambda i: (i, 0)) for a, _ in rows]
        in_specs += [pl.BlockSpec(a.shape, lambda i: (0, 0)) for a, _ in params]
        out_specs = [pl.BlockSpec((tm, w), lambda i: (i, 0)) for w, _ in outs]
        out_specs += [pl.BlockSpec((k, w), lambda i: (0, 0)) for k, w in par_outs]
        row_axis = 0
        sem = ("arbitrary",) if par_outs else ("parallel",)
    else:
        grid = (outs[0][0] // tn, S // tm)
        in_specs = [pl.BlockSpec((tm, tn), lambda j, i, o=o: (i, j + o)) for _, o in rows]
        in_specs += [pl.BlockSpec((a.shape[0], tn), lambda j, i, o=o: (0, j + o)) for a, o in params]
        out_specs = [pl.BlockSpec((tm, tn), lambda j, i: (i, j)) for _ in outs]
        out_specs += [pl.BlockSpec((k, tn), lambda j, i: (0, j)) for k, _ in par_outs]
        row_axis = 1
        sem = ("parallel", "arbitrary") if par_outs else ("parallel", "parallel")
    out_shape = [jax.ShapeDtypeStruct((S, w), d) for w, d in outs]
    out_shape += [jax.ShapeDtypeStruct((k, w), f32) for k, w in par_outs]

    def body(*refs):
        res = fn(*[r[...] for r in refs[:n_in]])
        for r, v in zip(refs[n_in:n_in + n_o], res[:n_o]):
            r[...] = v.astype(r.dtype)
        if par_outs:
            acc_refs = refs[n_in + n_o:]

            @pl.when(pl.program_id(row_axis) == 0)
            def _():
                for r in acc_refs:
                    r[...] = jnp.zeros_like(r)

            for r, v in zip(acc_refs, res[n_o:]):
                r[...] += v

    return pl.pallas_call(
        body, name=name, grid=grid, in_specs=in_specs, out_specs=out_specs, out_shape=out_shape,
        compiler_params=_cp(*sem))(*[a for a, _ in rows], *[a for a, _ in params])


def _stage_fwd(name, fn, rows, params, outs, *, tm, tn=None):
    n_r = len(rows)

    def ffn(*t):
        return fn(*[v.astype(f32) for v in t[:n_r]], *t[n_r:])

    return _stage(name, ffn, rows, params, outs, tm=tm, tn=tn)


def _stage_bwd(name, fn, rows, params, cts, drow, *, tm, tn=None):
    rows = [_pair(r) for r in rows]
    params = [_pair(p) for p in params]
    n_r, n_p = len(rows), len(params)
    flat_ct = [_pair(c) for group in cts for c in group]
    counts = [len(group) for group in cts]
    need = [i for i, d in enumerate(drow) if d is not None]

    def bfn(*t):
        r = [v.astype(f32) for v in t[:n_r]]
        c = t[n_r:n_r + len(flat_ct)]
        p = list(t[n_r + len(flat_ct):])
        res, vjp = jax.vjp(fn, *r, *p)
        ct, pos = [], 0
        for o, k in zip(res, counts):
            s = c[pos].astype(f32)
            for e in range(1, k):
                s = s + c[pos + e].astype(f32)
            pos += k
            ct.append(s.astype(o.dtype))
        g = vjp(tuple(ct))
        return tuple(g[i] for i in need) + tuple(g[n_r:])

    if tn is None:
        outs = [(rows[i][0].shape[1], drow[i]) for i in need]
        par_outs = [p.shape for p, _ in params]
    else:
        w_all = flat_ct[0][0].shape[1]
        outs = [(w_all, drow[i]) for i in need]
        par_outs = [(p.shape[0], w_all) for p, _ in params]
    res = _stage(name, bfn, rows + flat_ct, params, outs, par_outs, tm=tm, tn=tn)
    return res[:len(need)], res[len(need):]


def _mm(name, dims, a, b, grid, a_spec, b_spec, o_spec, out_shape, acc_shape, a_fn=None, tail=None, dep=None):
    nk = grid[2]
    a_list = list(a) if isinstance(a, (list, tuple)) else [a]
    na = len(a_list)
    t_fn, t_arrays = tail if tail is not None else (None, [])
    ne = len(t_arrays)
    deps = [] if dep is None else [dep]
    multi = isinstance(out_shape, (list, tuple))

    n_out = len(out_shape) if multi else 1

    def body(*refs):
        a_refs, b_ref, t_refs = refs[:na], refs[na], refs[na + 1:na + 1 + ne]
        first_out = na + 1 + ne + len(deps)
        o_refs = refs[first_out:first_out + n_out]
        k = pl.program_id(2)

        def prod():
            lhs = a_refs[0][...] if a_fn is None else a_fn(*[r[...] for r in a_refs])
            return lax.dot_general(lhs.astype(bf16), b_ref[...].astype(bf16), dims, preferred_element_type=f32)

        def finish(total):
            res = (total,) if t_fn is None else t_fn(total, *[r[...] for r in t_refs])
            for r, val in zip(o_refs, res):
                r[...] = val.astype(r.dtype)

        if nk == 1:
            finish(prod())
        else:
            acc = refs[-1]

            @pl.when(k == 0)
            def _():
                acc[...] = prod()

            if nk > 2:
                @pl.when(jnp.logical_and(k > 0, k < nk - 1))
                def _():
                    acc[...] += prod()

            @pl.when(k == nk - 1)
            def _():
                finish(acc[...] + prod())

    return pl.pallas_call(
        body, name=name, grid=grid,
        in_specs=[a_spec] * na + [b_spec] + [o_spec] * ne + [pl.BlockSpec(memory_space=pl.ANY)] * len(deps),
        out_specs=[o_spec] * len(out_shape) if multi else o_spec, out_shape=out_shape,
        scratch_shapes=[pltpu.VMEM(acc_shape, f32)] if nk > 1 else [],
        compiler_params=_cp("parallel", "parallel", "arbitrary"))(*a_list, b, *t_arrays, *deps)


def _mm_tiles(S, roomy=True):
    return _tile(S, 1024 if roomy else 512)


def join_in_proj(name, parts, width_main):
    P, D, CI = parts.shape
    rest = P * CI - width_main
    tr = _tile(D, 128, 16)

    def body(p_ref, m_ref, d_ref):
        for j in range(P - 1):
            m_ref[:, j * CI:(j + 1) * CI] = p_ref[j]
        m_ref[:, (P - 1) * CI:width_main] = p_ref[P - 1][:, :CI - rest]
        d_ref[...] = p_ref[P - 1][:, CI - rest:]

    return pl.pallas_call(
        body, name=name, grid=(D // tr,),
        in_specs=[pl.BlockSpec((P, tr, CI), lambda i: (0, i, 0))],
        out_specs=[pl.BlockSpec((tr, width_main), lambda i: (i, 0)), pl.BlockSpec((tr, rest), lambda i: (i, 0))],
        out_shape=[jax.ShapeDtypeStruct((D, width_main), parts.dtype), jax.ShapeDtypeStruct((D, rest), parts.dtype)],
        compiler_params=_cp("parallel"))(parts)


def split_in_proj(name, main, rest, P):
    D, WM = main.shape
    nr = rest.shape[1]
    CI = (WM + nr) // P
    tr = _tile(D, 128, 16)

    def body(m_ref, d_ref, p_ref):
        for j in range(P - 1):
            p_ref[j] = m_ref[:, j * CI:(j + 1) * CI]
        p_ref[P - 1, :, :CI - nr] = m_ref[:, (P - 1) * CI:WM]
        p_ref[P - 1, :, CI - nr:] = d_ref[...]

    return pl.pallas_call(
        body, name=name, grid=(D // tr,),
        in_specs=[pl.BlockSpec((tr, WM), lambda i: (i, 0)), pl.BlockSpec((tr, nr), lambda i: (i, 0))],
        out_specs=pl.BlockSpec((P, tr, CI), lambda i: (0, i, 0)),
        out_shape=jax.ShapeDtypeStruct((P, D, CI), main.dtype),
        compiler_params=_cp("parallel"))(main, rest)


def mm_nn_col(name, a, W, li, out_dtype=f32, dep=None):
    P, _, K, C = W.shape
    S = a.shape[0]
    tm, tn, tk = _mm_tiles(S), _tile(C, 1536), _tile(K, 2048)
    nc = C // tn
    return _mm(name, NN, a, W, (S // tm, P * nc, K // tk),
               pl.BlockSpec((tm, tk), lambda i, j, k: (i, k)),
               pl.BlockSpec((None, None, tk, tn), lambda i, j, k: (j // nc, li, k, j % nc)),
               pl.BlockSpec((tm, tn), lambda i, j, k: (i, j)),
               jax.ShapeDtypeStruct((S, P * C), out_dtype), (tm, tn), dep=dep)


def mm_nn_row(name, a, W, li, out_dtype=f32, a_fn=None):
    P, _, R, N = W.shape
    S = (a[0] if a_fn is not None else a).shape[0]
    tm, tn, tk = _mm_tiles(S, a_fn is None), _tile(N, 1024), _tile(R, 2048)
    nr = R // tk
    return _mm(name, NN, a, W, (S // tm, N // tn, P * nr),
               pl.BlockSpec((tm, tk), lambda i, j, k: (i, k)),
               pl.BlockSpec((None, None, tk, tn), lambda i, j, k: (k // nr, li, k % nr, j)),
               pl.BlockSpec((tm, tn), lambda i, j, k: (i, j)),
               jax.ShapeDtypeStruct((S, N), out_dtype), (tm, tn), a_fn=a_fn)


def mm_nt_col(name, dy, W, li, out_dtype=f32):
    P, _, K, C = W.shape
    S = dy.shape[0]
    tm, tn, tk = _mm_tiles(S), _tile(K, 1024), _tile(C, 2048)
    nc = C // tk
    return _mm(name, NT, dy, W, (S // tm, K // tn, P * nc),
               pl.BlockSpec((tm, tk), lambda i, j, k: (i, k)),
               pl.BlockSpec((None, None, tn, tk), lambda i, j, k: (k // nc, li, j, k % nc)),
               pl.BlockSpec((tm, tn), lambda i, j, k: (i, j)),
               jax.ShapeDtypeStruct((S, K), out_dtype), (tm, tn))


def mm_nt_row(name, dy, W, li, out_dtype=f32, tail=None, n_out=None):
    P, _, R, N = W.shape
    S = dy.shape[0]
    tm, tn, tk = _mm_tiles(S, tail is None), _tile(R, 1536), _tile(N, 2048)
    nr = R // tn
    out = jax.ShapeDtypeStruct((S, P * R), out_dtype)
    return _mm(name, NT, dy, W, (S // tm, P * nr, N // tk),
               pl.BlockSpec((tm, tk), lambda i, j, k: (i, k)),
               pl.BlockSpec((None, None, tn, tk), lambda i, j, k: (j // nr, li, j % nr, k)),
               pl.BlockSpec((tm, tn), lambda i, j, k: (i, j)),
               out if n_out is None else [out] * n_out, (tm, tn), tail=tail)


def mm_tn_col(name, a, dy, P):
    S, K = a.shape
    C = dy.shape[1] // P
    tm, tn, tk = _tile(K, 1024), _tile(C, 1536), _tile(S, 2048)
    nc = C // tn
    return _mm(name, TN, a, dy, (K // tm, P * nc, S // tk),
               pl.BlockSpec((tk, tm), lambda i, j, k: (k, i)),
               pl.BlockSpec((tk, tn), lambda i, j, k: (k, j)),
               pl.BlockSpec((None, tm, tn), lambda i, j, k: (j // nc, i, j % nc)),
               jax.ShapeDtypeStruct((P, K, C), bf16), (tm, tn))


def mm_tn_row(name, a, dy, P, a_fn=None):
    S, N = dy.shape
    R = (a[0] if a_fn is not None else a).shape[1] // P
    tm, tn, tk = _tile(R, 1536), _tile(N, 1024), _tile(S, 2048 if a_fn is None else 1024)
    nr = R // tm
    return _mm(name, TN, a, dy, (P * nr, N // tn, S // tk),
               pl.BlockSpec((tk, tm), lambda i, j, k: (k, i)),
               pl.BlockSpec((tk, tn), lambda i, j, k: (k, j)),
               pl.BlockSpec((None, tm, tn), lambda i, j, k: (i // nr, i % nr, j)),
               jax.ShapeDtypeStruct((P, R, N), bf16), (tm, tn), a_fn=a_fn)


def _hgrn_chunk(st, q, fp, v, gt, lb, an):
    C = q.shape[0]
    sig = _sigmoid(fp)
    f = lb + (1.0 - lb) * sig
    kk = (1.0 - lb) * (1.0 - sig)
    g = jnp.log(jnp.maximum(f, A_F_MIN))
    qs = _silu(q)
    row = lax.broadcasted_iota(jnp.int32, (C, C), 0)
    col = lax.broadcasted_iota(jnp.int32, (C, C), 1)
    tri = (col <= row).astype(f32)
    b = jnp.dot(tri, g, precision=HI, preferred_element_type=f32)
    o_inter = lax.dot_general((qs * jnp.exp(b)).astype(bf16), st.astype(bf16), NT,
                              preferred_element_type=f32)
    T, NB = A_SUB, C // A_SUB
    refs = [b[i * T + A_REF:i * T + A_REF + 1, :] for i in range(NB)]
    ref_q = jnp.concatenate([jnp.broadcast_to(r, (T, A_HEAD)) for r in refs], axis=0)
    ref_k = jnp.concatenate([jnp.broadcast_to(r, (C, A_HEAD)) for r in refs], axis=0)
    q_t = qs * jnp.exp(b - ref_q)
    k_t = jnp.concatenate([kk] * NB, axis=0) * jnp.exp(
        jnp.minimum(ref_k - jnp.concatenate([b] * NB, axis=0), A_EXP_CAP))
    s = lax.dot_general(q_t.astype(bf16), k_t.astype(bf16), NT, preferred_element_type=f32)
    trow = lax.broadcasted_iota(jnp.int32, (C, NB * C), 0)
    scol = lax.broadcasted_iota(jnp.int32, (C, NB * C), 1)
    keep = jnp.logical_and(scol // C == trow // T, scol % C <= trow)
    s = jnp.where(keep, s, 0.0)
    o_intra = jnp.dot(s.astype(bf16), jnp.concatenate([v.astype(bf16)] * NB, axis=0),
                      preferred_element_type=f32)
    bl = b[C - 1:C, :]
    kd = kk * jnp.exp(bl - b)
    st_new = st * jnp.exp(bl) + lax.dot_general(v.astype(bf16), kd.astype(bf16), TN,
                                                preferred_element_type=f32)
    o = o_inter + o_intra
    y = o * lax.rsqrt(jnp.mean(o * o, axis=-1, keepdims=True) + RMS_EPS) * an * _silu(gt)
    return st_new, y


def _hgrn_heads_per_step(HA):
    return A_HEADS_PER_STEP if HA % A_HEADS_PER_STEP == 0 else 1


def _hgrn_in_specs(HA, HP, cidx):
    W = HP * A_HEAD
    specs = [pl.BlockSpec((A_CHUNK, W), lambda h, c, s=s: (cidx(c), s * (HA // HP) + h)) for s in range(4)]
    specs += [pl.BlockSpec((1, W), lambda h, c: (0, h))] * 2
    return specs


def _head(ref, j):
    return ref[:, j * A_HEAD:(j + 1) * A_HEAD]


def hgrn_fwd(name, ymain, lb, an, D):
    S = ymain.shape[0]
    HA, nc = D // A_HEAD, S // A_CHUNK
    HP = _hgrn_heads_per_step(HA)
    W = HP * A_HEAD

    def body(q, fp, v, gt, lb_ref, an_ref, o_ref, sv_ref, st):
        @pl.when(pl.program_id(1) == 0)
        def _():
            st[...] = jnp.zeros_like(st)

        sv_ref[...] = st[...]
        for j in range(HP):
            st_new, y = _hgrn_chunk(st[j], _head(q, j), _head(fp, j), _head(v, j), _head(gt, j),
                                    _head(lb_ref, j), _head(an_ref, j))
            st[j] = st_new
            o_ref[:, j * A_HEAD:(j + 1) * A_HEAD] = y.astype(o_ref.dtype)

    return pl.pallas_call(
        body, name=name, grid=(HA // HP, nc),
        in_specs=_hgrn_in_specs(HA, HP, lambda c: c),
        out_specs=[pl.BlockSpec((A_CHUNK, W), lambda h, c: (c, h)),
                   pl.BlockSpec((HP, None, A_HEAD, A_HEAD), lambda h, c: (h, c, 0, 0))],
        out_shape=[jax.ShapeDtypeStruct((S, D), bf16),
                   jax.ShapeDtypeStruct((HA, nc, A_HEAD, A_HEAD), f32)],
        scratch_shapes=[pltpu.VMEM((HP, A_HEAD, A_HEAD), f32)],
        compiler_params=_cp("parallel", "arbitrary"))(ymain, ymain, ymain, ymain, lb, an)


def hgrn_bwd(name, ymain, lb, an, saved, dmixed, D):
    S = ymain.shape[0]
    HA, nc = D // A_HEAD, S // A_CHUNK
    HP = _hgrn_heads_per_step(HA)
    W = HP * A_HEAD
    rev = lambda c: nc - 1 - c

    def body(q, fp, v, gt, lb_ref, an_ref, sv_ref, do_ref, dq, df, dv, dg, dlb, dan, dst):
        @pl.when(pl.program_id(1) == 0)
        def _():
            dst[...] = jnp.zeros_like(dst)
            dlb[...] = jnp.zeros_like(dlb)
            dan[...] = jnp.zeros_like(dan)

        for j in range(HP):
            cols = slice(j * A_HEAD, (j + 1) * A_HEAD)
            _, vjp = jax.vjp(_hgrn_chunk, sv_ref[j], _head(q, j), _head(fp, j), _head(v, j), _head(gt, j),
                             _head(lb_ref, j), _head(an_ref, j))
            g = vjp((dst[j], _head(do_ref, j).astype(f32)))
            dst[j] = g[0]
            for r, x in zip((dq, df, dv, dg), g[1:5]):
                r[:, cols] = x.astype(r.dtype)
            dlb[:, cols] += g[5]
            dan[:, cols] += g[6]

    blk = pl.BlockSpec((A_CHUNK, W), lambda h, c: (rev(c), h))
    vec = pl.BlockSpec((1, W), lambda h, c: (0, h))
    return pl.pallas_call(
        body, name=name, grid=(HA // HP, nc),
        in_specs=_hgrn_in_specs(HA, HP, rev) + [
            pl.BlockSpec((HP, None, A_HEAD, A_HEAD), lambda h, c: (h, rev(c), 0, 0)), blk],
        out_specs=[blk] * 4 + [vec] * 2,
        out_shape=[jax.ShapeDtypeStruct((S, D), bf16)] * 4 + [jax.ShapeDtypeStruct((1, D), f32)] * 2,
        scratch_shapes=[pltpu.VMEM((HP, A_HEAD, A_HEAD), f32)],
        compiler_params=_cp("parallel", "arbitrary"))(ymain, ymain, ymain, ymain, lb, an, saved, dmixed)


def _ssd_chunk(hp, xs, bm, cm, z, dtr, dtb, alog, dsk, bn, g, R):
    L, GW = xs.shape
    HB = dtr.shape[1]
    R8 = max(R, 8)
    dt = jax.nn.softplus(dtr + dtb)
    a = -jnp.exp(alog)
    row = lax.broadcasted_iota(jnp.int32, (L, L), 0)
    col = lax.broadcasted_iota(jnp.int32, (L, L), 1)
    causal = col <= row
    cs = jnp.dot(causal.astype(f32), dt * a, precision=HI, preferred_element_type=f32)
    eh = lax.broadcasted_iota(jnp.int32, (HB, GW), 0)
    ec = lax.broadcasted_iota(jnp.int32, (HB, GW), 1)
    spread = (eh == g * R + ec // B_HEAD).astype(f32)
    sh = lax.broadcasted_iota(jnp.int32, (R8, HB), 1)
    sr = lax.broadcasted_iota(jnp.int32, (R8, HB), 0)
    pick_t = jnp.logical_and(sh == g * R + sr, sr < R).astype(f32)
    dtf = jnp.dot(dt, spread, precision=HI, preferred_element_type=f32)
    csf = jnp.dot(cs, spread, precision=HI, preferred_element_type=f32)
    dsf = jnp.dot(jnp.broadcast_to(dsk, (8, HB)), spread, precision=HI, preferred_element_type=f32)[0:1, :]
    cs_col = lax.dot_general(cs, pick_t, NT, precision=HI, preferred_element_type=f32)
    cs_row = lax.dot_general(pick_t, cs, NT, precision=HI, preferred_element_type=f32)
    xdt = xs * dtf
    cb = lax.dot_general(cm.astype(bf16), bm.astype(bf16), NT, preferred_element_type=f32)
    lane_head = lax.broadcasted_iota(jnp.int32, (1, GW), 1) // B_HEAD
    y = jnp.zeros((L, GW), f32)
    for r in range(R):
        seg = cs_col[:, r:r + 1] - cs_row[r:r + 1, :]
        dec = jnp.where(causal, jnp.exp(jnp.where(causal, seg, 0.0)), 0.0)
        xm = jnp.where(lane_head == r, xdt, 0.0)
        y = y + jnp.dot((cb * dec).astype(bf16), xm.astype(bf16), preferred_element_type=f32)
    csl = csf[L - 1:L, :]
    dte = jnp.exp(csl - csf)
    states = lax.dot_general(bm.astype(bf16), (xdt * dte).astype(bf16), TN, preferred_element_type=f32)
    y_off = jnp.dot(cm.astype(bf16), hp.astype(bf16), preferred_element_type=f32) * jnp.exp(csf)
    hn = hp * jnp.exp(csl) + states
    gated = (y + y_off + dsf * xs) * _silu(z)
    out = gated * lax.rsqrt(jnp.mean(gated * gated, axis=-1, keepdims=True) + RMS_EPS) * bn
    return hn, out


def _ssd_in_specs(D, HB, cidx):
    L, GW, N = B_CHUNK, D // B_GROUPS, B_STATE
    zoff, boff = 4 * D // GW, D // N
    return [
        pl.BlockSpec((L, GW), lambda c, g: (cidx(c), g)),
        pl.BlockSpec((L, N), lambda c, g: (cidx(c), boff + g)),
        pl.BlockSpec((L, N), lambda c, g: (cidx(c), boff + B_GROUPS + g)),
        pl.BlockSpec((L, GW), lambda c, g: (cidx(c), zoff + g)),
        pl.BlockSpec((L, HB), lambda c, g: (cidx(c), 0)),
        pl.BlockSpec((1, HB), lambda c, g: (0, 0)),
        pl.BlockSpec((1, HB), lambda c, g: (0, 0)),
        pl.BlockSpec((1, HB), lambda c, g: (0, 0)),
        pl.BlockSpec((1, GW), lambda c, g: (0, g)),
    ]


def ssd_fwd(name, xact, ymain, dtr, dtb, alog, dsk, bn, D):
    S, HB = dtr.shape
    nc, GW, R = S // B_CHUNK, D // B_GROUPS, HB // B_GROUPS

    def body(xs, bm, cm, z, dt_ref, dtb_ref, al_ref, ds_ref, bn_ref, o_ref, sv_ref, hs):
        g = pl.program_id(1)

        @pl.when(pl.program_id(0) == 0)
        def _():
            hs[g] = jnp.zeros((B_STATE, GW), f32)

        hp = hs[g]
        sv_ref[...] = hp
        hn, out = _ssd_chunk(hp, xs[...], bm[...], cm[...], z[...], dt_ref[...], dtb_ref[...], al_ref[...],
                             ds_ref[...], bn_ref[...], g, R)
        hs[g] = hn
        o_ref[...] = out.astype(o_ref.dtype)

    return pl.pallas_call(
        body, name=name, grid=(nc, B_GROUPS),
        in_specs=_ssd_in_specs(D, HB, lambda c: c),
        out_specs=[pl.BlockSpec((B_CHUNK, GW), lambda c, g: (c, g)),
                   pl.BlockSpec((None, None, B_STATE, GW), lambda c, g: (c, g, 0, 0))],
        out_shape=[jax.ShapeDtypeStruct((S, D), bf16),
                   jax.ShapeDtypeStruct((nc, B_GROUPS, B_STATE, GW), f32)],
        scratch_shapes=[pltpu.VMEM((B_GROUPS, B_STATE, GW), f32)],
        compiler_params=_cp("arbitrary", "arbitrary"))(xact, xact, xact, ymain, dtr, dtb, alog, dsk, bn)


def ssd_bwd(name, xact, ymain, dtr, dtb, alog, dsk, bn, saved, dmixed, D):
    S, HB = dtr.shape
    nc, GW, R = S // B_CHUNK, D // B_GROUPS, HB // B_GROUPS
    rev = lambda c: nc - 1 - c
    ooff = D // GW

    def body(xs, bm, cm, z, dt_ref, dtb_ref, al_ref, ds_ref, bn_ref, sv_ref, do_ref,
             dxs, dbm, dcm, dz, ddt, ddtb, dal, dds, dbn, dhs):
        c, g = pl.program_id(0), pl.program_id(1)

        @pl.when(c == 0)
        def _():
            dhs[g] = jnp.zeros((B_STATE, GW), f32)
            dbn[g] = jnp.zeros((1, GW), f32)

        @pl.when(jnp.logical_and(c == 0, g == 0))
        def _():
            ddtb[...] = jnp.zeros_like(ddtb)
            dal[...] = jnp.zeros_like(dal)
            dds[...] = jnp.zeros_like(dds)

        @pl.when(g == 0)
        def _():
            ddt[...] = jnp.zeros_like(ddt)

        fn = functools.partial(_ssd_chunk, g=g, R=R)
        _, vjp = jax.vjp(fn, sv_ref[...], xs[...], bm[...], cm[...], z[...], dt_ref[...], dtb_ref[...],
                         al_ref[...], ds_ref[...], bn_ref[...])
        gr = vjp((dhs[g], do_ref[...].astype(f32)))
        dhs[g] = gr[0]
        dxs[...] = gr[1]
        dbm[...] = gr[2]
        dcm[...] = gr[3]
        dz[...] = gr[4].astype(dz.dtype)
        ddt[...] += gr[5]
        ddtb[...] += gr[6]
        dal[...] += gr[7]
        dds[...] += gr[8]
        dbn[g] += gr[9]

    hb_vec = pl.BlockSpec((1, HB), lambda c, g: (0, 0))
    return pl.pallas_call(
        body, name=name, grid=(nc, B_GROUPS),
        in_specs=_ssd_in_specs(D, HB, rev) + [
            pl.BlockSpec((None, None, B_STATE, GW), lambda c, g: (rev(c), g, 0, 0)),
            pl.BlockSpec((B_CHUNK, GW), lambda c, g: (rev(c), ooff + g))],
        out_specs=[pl.BlockSpec((B_CHUNK, GW), lambda c, g: (rev(c), g)),
                   pl.BlockSpec((B_CHUNK, B_STATE), lambda c, g: (rev(c), g)),
                   pl.BlockSpec((B_CHUNK, B_STATE), lambda c, g: (rev(c), g)),
                   pl.BlockSpec((B_CHUNK, GW), lambda c, g: (rev(c), g)),
                   pl.BlockSpec((B_CHUNK, HB), lambda c, g: (rev(c), 0)),
                   hb_vec, hb_vec, hb_vec,
                   pl.BlockSpec((B_GROUPS, 1, GW), lambda c, g: (0, 0, 0))],
        out_shape=[jax.ShapeDtypeStruct((S, D), f32),
                   jax.ShapeDtypeStruct((S, B_GROUPS * B_STATE), f32),
                   jax.ShapeDtypeStruct((S, B_GROUPS * B_STATE), f32),
                   jax.ShapeDtypeStruct((S, D), bf16),
                   jax.ShapeDtypeStruct((S, HB), f32),
                   jax.ShapeDtypeStruct((1, HB), f32), jax.ShapeDtypeStruct((1, HB), f32),
                   jax.ShapeDtypeStruct((1, HB), f32),
                   jax.ShapeDtypeStruct((B_GROUPS, 1, GW), f32)],
        scratch_shapes=[pltpu.VMEM((B_GROUPS, B_STATE, GW), f32)],
        compiler_params=_cp("arbitrary", "arbitrary"))(
            xact, xact, xact, ymain, dtr, dtb, alog, dsk, bn, saved, dmixed)


def _conv_taps(xp, w_ref, b_ref, r0, K):
    acc = jnp.broadcast_to(b_ref[...], (CONV_ROWS, b_ref.shape[1]))
    for k in range(K):
        acc = acc + w_ref[k:k + 1, :] * xp[r0 + CONV_PAD - (K - 1) + k:r0 + CONV_PAD - (K - 1) + k + CONV_ROWS, :]
    return acc


def conv_fwd(name, x, xoff, w, b, act, out_dtype):
    S = x.shape[0]
    K, CW = w.shape
    tc = CONV_CH

    def body(x_ref, w_ref, b_ref, o_ref, xp):
        xp[0:CONV_PAD, :] = jnp.zeros((CONV_PAD, tc), f32)
        xp[CONV_PAD:CONV_PAD + S, :] = x_ref[...].astype(f32)
        for r0 in range(0, S, CONV_ROWS):
            acc = _conv_taps(xp, w_ref, b_ref, r0, K)
            if act:
                acc = _silu(acc)
            o_ref[r0:r0 + CONV_ROWS, :] = acc.astype(o_ref.dtype)

    return pl.pallas_call(
        body, name=name, grid=(CW // tc,),
        in_specs=[pl.BlockSpec((S, tc), lambda j: (0, j + xoff)),
                  pl.BlockSpec((K, tc), lambda j: (0, j)),
                  pl.BlockSpec((1, tc), lambda j: (0, j))],
        out_specs=pl.BlockSpec((S, tc), lambda j: (0, j)),
        out_shape=jax.ShapeDtypeStruct((S, CW), out_dtype),
        scratch_shapes=[pltpu.VMEM((S + CONV_PAD, tc), f32)],
        compiler_params=_cp("parallel"))(x, w, b)


def conv_bwd(name, x, xoff, w, b, dout, act, dx_dtype):
    S = x.shape[0]
    K, CW = w.shape
    tc = CONV_CH

    def body(x_ref, w_ref, b_ref, d_ref, dx_ref, dw_ref, db_ref, xp, dp):
        xp[0:CONV_PAD, :] = jnp.zeros((CONV_PAD, tc), f32)
        xp[CONV_PAD:CONV_PAD + S, :] = x_ref[...].astype(f32)
        dp[S:S + CONV_PAD, :] = jnp.zeros((CONV_PAD, tc), f32)
        db = jnp.zeros((1, tc), f32)
        for r0 in range(0, S, CONV_ROWS):
            d = d_ref[r0:r0 + CONV_ROWS, :].astype(f32)
            if act:
                pre = _conv_taps(xp, w_ref, b_ref, r0, K)
                s = _sigmoid(pre)
                d = d * (s + pre * s * (1.0 - s))
            dp[r0:r0 + CONV_ROWS, :] = d
            db = db + jnp.sum(d, axis=0, keepdims=True)
        db_ref[...] = db
        for r0 in range(0, S, CONV_ROWS):
            acc = jnp.zeros((CONV_ROWS, tc), f32)
            for k in range(K):
                acc = acc + w_ref[k:k + 1, :] * dp[r0 + (K - 1 - k):r0 + (K - 1 - k) + CONV_ROWS, :]
            dx_ref[r0:r0 + CONV_ROWS, :] = acc.astype(dx_ref.dtype)
        for k in range(K):
            acc = jnp.zeros((1, tc), f32)
            for r0 in range(0, S, CONV_ROWS):
                lo = r0 + CONV_PAD - (K - 1) + k
                acc = acc + jnp.sum(dp[r0:r0 + CONV_ROWS, :] * xp[lo:lo + CONV_ROWS, :], axis=0, keepdims=True)
            dw_ref[k:k + 1, :] = acc

    return pl.pallas_call(
        body, name=name, grid=(CW // tc,),
        in_specs=[pl.BlockSpec((S, tc), lambda j: (0, j + xoff)),
                  pl.BlockSpec((K, tc), lambda j: (0, j)),
                  pl.BlockSpec((1, tc), lambda j: (0, j)),
                  pl.BlockSpec((S, tc), lambda j: (0, j))],
        out_specs=[pl.BlockSpec((S, tc), lambda j: (0, j)),
                   pl.BlockSpec((K, tc), lambda j: (0, j)),
                   pl.BlockSpec((1, tc), lambda j: (0, j))],
        out_shape=[jax.ShapeDtypeStruct((S, CW), dx_dtype),
                   jax.ShapeDtypeStruct((K, CW), f32),
                   jax.ShapeDtypeStruct((1, CW), f32)],
        scratch_shapes=[pltpu.VMEM((S + CONV_PAD, tc), f32), pltpu.VMEM((S + CONV_PAD, tc), f32)],
        compiler_params=_cp("parallel"))(x, w, b, dout)


def loss_head(name, y, target, tm):
    S, D = y.shape

    def body(y_ref, t_ref, dy_ref, l_ref):
        @pl.when(pl.program_id(0) == 0)
        def _():
            l_ref[...] = jnp.zeros_like(l_ref)

        err = y_ref[...] - t_ref[...]
        dy_ref[...] = err * (1.0 / D)
        l_ref[...] += jnp.sum(err * err) * (0.5 / D)

    dy, l = pl.pallas_call(
        body, name=name, grid=(S // tm,),
        in_specs=[pl.BlockSpec((tm, D), lambda i: (i, 0))] * 2,
        out_specs=[pl.BlockSpec((tm, D), lambda i: (i, 0)), pl.BlockSpec((8, LANES), lambda i: (0, 0))],
        out_shape=[jax.ShapeDtypeStruct((S, D), f32), jax.ShapeDtypeStruct((8, LANES), f32)],
        compiler_params=_cp("arbitrary"))(y, target)
    return dy, l[0, 0]


def _flat2d_tiles(rows, cols, itemsize, target_bytes):
    tc = _tile(cols, 1024) if cols % LANES == 0 else cols
    cap = max(8, target_bytes // (tc * itemsize))
    tr = _tile(rows, cap, 16) if rows % 16 == 0 else rows
    return tr, tc


def adamw(name, w, g, m, v):
    shape = w.shape
    cols = shape[-1]
    rows = math.prod(shape[:-1])
    tr, tc = _flat2d_tiles(rows, cols, 4, 1 << 20)
    c1 = 1.0 - ADAM_B1 ** ADAM_STEP
    c2 = 1.0 - ADAM_B2 ** ADAM_STEP

    def body(w_ref, g_ref, m_ref, v_ref, d_ref, nm_ref, nv_ref, g_out_ref):
        gg = g_ref[...]
        nm = ADAM_B1 * m_ref[...] + (1.0 - ADAM_B1) * gg
        nv = ADAM_B2 * v_ref[...] + (1.0 - ADAM_B2) * (gg * gg)
        d_ref[...] = -ADAM_LR * ((nm / c1) / (jnp.sqrt(nv / c2) + ADAM_EPS) + ADAM_WD * w_ref[...])
        nm_ref[...] = nm
        nv_ref[...] = nv
        g_out_ref[...] = gg

    spec = pl.BlockSpec((tr, tc), lambda i, j: (i, j))
    outs = pl.pallas_call(
        body, name=name, grid=(rows // tr, cols // tc), in_specs=[spec] * 4, out_specs=[spec] * 4,
        out_shape=[jax.ShapeDtypeStruct((rows, cols), f32)] * 4,
        compiler_params=_cp("parallel", "parallel"))(*[a.reshape(rows, cols) for a in (w, g, m, v)])
    return [o.reshape(shape) for o in outs]


def _core_index():
    return lax.axis_index("c").astype(jnp.int32).reshape(1)


def _half_rows_tile(Rh, C):
    return _tile(Rh, max(16, (2 << 20) // (C * 2)), 16)


def rs_add(name, G, buf):
    P, _, Rh, C = G.shape
    tr = _half_rows_tile(Rh, C)

    def body(c_ref, g_ref, b_ref, o_ref):
        o_ref[...] = (g_ref[...].astype(f32) + b_ref[...].astype(f32)).astype(o_ref.dtype)

    return pl.pallas_call(
        body, name=name,
        grid_spec=pltpu.PrefetchScalarGridSpec(
            num_scalar_prefetch=1, grid=(P, Rh // tr),
            in_specs=[pl.BlockSpec((None, None, tr, C), lambda p, i, c: (p, c[0], i, 0)),
                      pl.BlockSpec((None, tr, C), lambda p, i, c: (p, i, 0))],
            out_specs=pl.BlockSpec((None, tr, C), lambda p, i, c: (p, i, 0))),
        out_shape=jax.ShapeDtypeStruct((P, Rh, C), bf16),
        compiler_params=_cp("parallel", "parallel"))(_core_index(), G, buf)


def _chip_indices():
    x, y, c = lax.axis_index("x"), lax.axis_index("y"), lax.axis_index("c")
    ids = [2 * x + y] + [2 * _flip(x, fx) + _flip(y, fy) for fx, fy in _CHIP_FLIPS] + [c]
    return [i.astype(jnp.int32).reshape(1) for i in ids]


def rs_sum4(name, pair, buf, final, layer):
    P, Rh, C = buf.shape
    tr = _half_rows_tile(Rh, C)

    def body(i0, i1, i2, i3, ic, b0, b1, b2, b3, f_ref, o_ref):
        o_ref[...] = ((b0[...].astype(f32) + b1[...].astype(f32)) + b2[...].astype(f32)) + b3[...].astype(f32)

    blk = (None, tr, C)
    return pl.pallas_call(
        body, name=name,
        grid_spec=pltpu.PrefetchScalarGridSpec(
            num_scalar_prefetch=5, grid=(Rh // tr,),
            in_specs=[pl.BlockSpec(blk, lambda i, *ids, k=k: (ids[k][0], i, 0)) for k in range(P)] + [_ANY],
            out_specs=pl.BlockSpec((None, None, tr, C), lambda i, *ids: (layer, ids[4][0], i, 0))),
        out_shape=jax.ShapeDtypeStruct(final.shape, final.dtype),
        input_output_aliases={9: 0},
        compiler_params=_cp("parallel"))(*_chip_indices(), pair, buf, buf, buf, final)


def place_own(name, w, layer):
    _, R, C = w.shape
    tr = _tile(R, max(16, (2 << 20) // (C * 2)), 16)

    def body(q, w_ref, own_ref, land_ref):
        wb = w_ref[...].astype(bf16)
        own_ref[...] = wb
        land_ref[...] = wb

    return pl.pallas_call(
        body, name=name,
        grid_spec=pltpu.PrefetchScalarGridSpec(
            num_scalar_prefetch=1, grid=(R // tr,),
            in_specs=[pl.BlockSpec((None, tr, C), lambda i, q: (layer, i, 0))],
            out_specs=[pl.BlockSpec((tr, C), lambda i, q: (i, 0)),
                       pl.BlockSpec((None, tr, C), lambda i, q: (q[0], i, 0))]),
        out_shape=[jax.ShapeDtypeStruct((R, C), bf16), jax.ShapeDtypeStruct((N_CHIPS, R, C), bf16)],
        compiler_params=_cp("parallel"))(_chip_indices()[0], w)


_ANY = pl.BlockSpec(memory_space=pl.ANY)
_CHIP_FLIPS = ((1, 0), (0, 1), (1, 1))


def _place():
    return lax.axis_index("x"), lax.axis_index("y"), lax.axis_index("c")


def _flip(v, f):
    return 1 - v if f else v


def _remote(src, dst, ssem, rsem, dev):
    return pltpu.make_async_remote_copy(src_ref=src, dst_ref=dst, send_sem=ssem, recv_sem=rsem,
                                        device_id=dev, device_id_type=MESH)


_HBM = pl.BlockSpec(memory_space=pltpu.HBM)
_SEM = pl.BlockSpec(memory_space=pltpu.SEMAPHORE)
_DATAFLOW = pltpu.SideEffectType.DATAFLOW_SIDE_EFFECTING
_TOKEN = jax.ShapeDtypeStruct((8, LANES), f32)


def _in_hbm(a):
    return pltpu.with_memory_space_constraint(a, pltpu.HBM)


def _uninit(tag, shape, dtype):
    def body(o_ref):
        pass

    return pl.pallas_call(body, name="uninit_" + tag, out_specs=_ANY,
                          out_shape=jax.ShapeDtypeStruct(shape, dtype))()


def _hbm_like(a):
    return pltpu.HBM(a.shape, a.dtype)


def gather_ici_start(name, groups, dep=None):
    sizes = [len(g) for g in groups]
    owns = [o for g in groups for o, _ in g]
    lands = [l for g in groups for _, l in g]
    n, ng = len(owns), len(groups)
    deps = [] if dep is None else [dep]

    def body(*refs):
        own, land = refs[:n], refs[n:2 * n]
        sems = refs[2 * n + len(deps):2 * n + len(deps) + 2 * ng]
        token = refs[-1]
        x, y, c = _place()
        q = 2 * x + y
        t = 0
        for gi, size in enumerate(sizes):
            for j in range(size):
                for k, (fx, fy) in enumerate(_CHIP_FLIPS):
                    _remote(own[t].at[c], land[t].at[q, c], sems[2 * gi].at[3 * j + k], sems[2 * gi + 1].at[3 * j + k],
                            (_flip(x, fx), _flip(y, fy), c)).start()
                t += 1
        token[...] = jnp.zeros_like(token)

    sem_shapes = [pltpu.SemaphoreType.DMA((3 * size,)) for size in sizes for _ in range(2)]
    res = pl.pallas_call(
        body, name=name,
        in_specs=[_HBM] * (2 * n) + [_ANY] * len(deps),
        out_specs=[_SEM] * (2 * ng) + [_HBM] * (2 * n) + [pl.BlockSpec(memory_space=pltpu.VMEM)],
        out_shape=sem_shapes + [_hbm_like(a) for a in owns + lands] + [_TOKEN],
        input_output_aliases={i: 2 * ng + i for i in range(2 * n)},
        compiler_params=pltpu.CompilerParams(has_side_effects=_DATAFLOW),
    )(*[_in_hbm(a) for a in owns + lands], *deps)
    own_thru, land_thru = res[2 * ng:2 * ng + n], res[2 * ng + n:2 * ng + 2 * n]
    handles, t = [], 0
    for gi, size in enumerate(sizes):
        handles.append((res[2 * gi], res[2 * gi + 1], list(own_thru[t:t + size]), list(land_thru[t:t + size])))
        t += size
    return handles, res[-1]


def gather_ici_wait(name, handle, after):
    send, recv, owns, lands = handle
    n = len(owns)

    def body(*refs):
        own, land = refs[:n], refs[n:2 * n]
        send_ref, recv_ref = refs[2 * n], refs[2 * n + 1]
        x, y, c = _place()
        for j in range(n):
            for k, (fx, fy) in enumerate(_CHIP_FLIPS):
                px, py = _flip(x, fx), _flip(y, fy)
                cp = _remote(own[j].at[c], land[j].at[2 * px + py, c], send_ref.at[3 * j + k], recv_ref.at[3 * j + k],
                             (px, py, c))
                cp.wait_send()
                cp.wait_recv()

    res = pl.pallas_call(
        body, name=name,
        in_specs=[_HBM] * (2 * n) + [_SEM, _SEM, _ANY],
        out_specs=[_HBM] * (2 * n),
        out_shape=[_hbm_like(a) for a in owns + lands],
        input_output_aliases={i: i for i in range(2 * n)},
        compiler_params=pltpu.CompilerParams(has_side_effects=_DATAFLOW),
    )(*owns, *lands, send, recv, after)
    return list(res[n:])


def _split_start(name, body, arrays, n_sems, dep=None):
    n = len(arrays)
    deps = [] if dep is None else [dep]

    def kernel_body(*refs):
        m = n + len(deps)
        body(refs[:n], refs[m], refs[m + 1])
        refs[-1][...] = jnp.zeros_like(refs[-1])

    res = pl.pallas_call(
        kernel_body, name=name,
        in_specs=[_HBM] * n + [_ANY] * len(deps),
        out_specs=[_SEM, _SEM] + [_HBM] * n + [pl.BlockSpec(memory_space=pltpu.VMEM)],
        out_shape=[pltpu.SemaphoreType.DMA((n_sems,)), pltpu.SemaphoreType.DMA((n_sems,))]
        + [_hbm_like(a) for a in arrays] + [_TOKEN],
        input_output_aliases={i: 2 + i for i in range(n)},
        compiler_params=pltpu.CompilerParams(has_side_effects=_DATAFLOW),
    )(*[_in_hbm(a) for a in arrays], *deps)
    return res[0], res[1], list(res[2:2 + n]), res[-1]


def _split_wait(name, body, handle, after):
    send, recv, arrays, _ = handle
    n = len(arrays)

    def kernel_body(*refs):
        body(refs[:n], refs[n], refs[n + 1])

    res = pl.pallas_call(
        kernel_body, name=name,
        in_specs=[_HBM] * n + [_SEM, _SEM, _ANY],
        out_specs=[_HBM] * n,
        out_shape=[_hbm_like(a) for a in arrays],
        input_output_aliases={i: i for i in range(n)},
        compiler_params=pltpu.CompilerParams(has_side_effects=_DATAFLOW),
    )(*arrays, send, recv, after)
    return list(res)


def _forward_copies(land, send, recv):
    x, y, c = _place()
    for t in range(len(land)):
        for k, (fx, fy) in enumerate(_CHIP_FLIPS):
            slab = land[t].at[2 * _flip(x, fx) + _flip(y, fy), c]
            yield _remote(slab, slab, send.at[3 * t + k], recv.at[3 * t + k], (x, y, 1 - c))


def forward_start(name, lands):
    def body(land, send, recv):
        for cp in _forward_copies(land, send, recv):
            cp.start()

    return _split_start(name, body, lands, 3 * len(lands))


def forward_wait(name, handle, after):
    def body(land, send, recv):
        for cp in _forward_copies(land, send, recv):
            cp.wait_send()
            cp.wait_recv()

    return _split_wait(name, body, handle, after)


def sibling_start(name, Gs, dep=None):
    T = len(Gs)
    bufs = [_uninit(f"{name}_{t}", (G.shape[0],) + G.shape[2:], G.dtype) for t, G in enumerate(Gs)]

    def body(refs, send, recv):
        x, y, c = _place()
        for t in range(T):
            _remote(refs[t].at[:, 1 - c], refs[T + t], send.at[t], recv.at[t], (x, y, 1 - c)).start()

    return _split_start(name, body, list(Gs) + bufs, T, dep)


def sibling_wait(name, handle, after):
    T = len(handle[2]) // 2

    def body(refs, send, recv):
        x, y, c = _place()
        for t in range(T):
            cp = _remote(refs[t].at[:, 1 - c], refs[T + t], send.at[t], recv.at[t], (x, y, 1 - c))
            cp.wait_send()
            cp.wait_recv()

    res = _split_wait(name, body, handle, after)
    return res[:T], res[T:]


def reduce_ici_start(name, Ss):
    T = len(Ss)
    lands = [_uninit(f"{name}_{t}", S.shape, S.dtype) for t, S in enumerate(Ss)]

    def body(*refs):
        s, land = refs[:T], refs[T:2 * T]
        send, recv = refs[2 * T], refs[2 * T + 1]
        token = refs[-1]
        x, y, c = _place()
        q = 2 * x + y
        for t in range(T):
            for k, (fx, fy) in enumerate(_CHIP_FLIPS):
                px, py = _flip(x, fx), _flip(y, fy)
                _remote(s[t].at[2 * px + py], land[t].at[q], send.at[3 * t + k], recv.at[3 * t + k],
                        (px, py, c)).start()
        token[...] = jnp.zeros_like(token)

    res = pl.pallas_call(
        body, name=name,
        in_specs=[_HBM] * (2 * T),
        out_specs=[_SEM, _SEM] + [_HBM] * (2 * T) + [pl.BlockSpec(memory_space=pltpu.VMEM)],
        out_shape=[pltpu.SemaphoreType.DMA((3 * T,)), pltpu.SemaphoreType.DMA((3 * T,))]
        + [_hbm_like(a) for a in Ss + lands] + [_TOKEN],
        input_output_aliases={i: 2 + i for i in range(2 * T)},
        compiler_params=pltpu.CompilerParams(has_side_effects=_DATAFLOW),
    )(*[_in_hbm(a) for a in Ss + lands])
    return res[0], res[1], list(res[2:2 + T]), list(res[2 + T:2 + 2 * T]), res[-1]


def reduce_ici_wait(name, handle, after):
    send, recv, Ss, lands, _ = handle
    T = len(Ss)

    def body(*refs):
        s, land = refs[:T], refs[T:2 * T]
        send_ref, recv_ref = refs[2 * T], refs[2 * T + 1]
        x, y, c = _place()
        for t in range(T):
            for k, (fx, fy) in enumerate(_CHIP_FLIPS):
                px, py = _flip(x, fx), _flip(y, fy)
                cp = _remote(s[t].at[2 * px + py], land[t].at[2 * px + py], send_ref.at[3 * t + k], recv_ref.at[3 * t + k],
                             (px, py, c))
                cp.wait_send()
                cp.wait_recv()

    res = pl.pallas_call(
        body, name=name,
        in_specs=[_HBM] * (2 * T) + [_SEM, _SEM, _ANY],
        out_specs=[_HBM] * (2 * T),
        out_shape=[_hbm_like(a) for a in Ss + lands],
        input_output_aliases={i: i for i in range(2 * T)},
        compiler_params=pltpu.CompilerParams(has_side_effects=_DATAFLOW),
    )(*Ss, *lands, send, recv, after)
    return list(res[:T]), list(res[T:])


def rs_share(name, tots, layers):
    T = len(tots)

    def body(*refs):
        o = refs[T:2 * T]
        send, recv = refs[2 * T:]
        x, y, c = _place()
        cps = []
        for t in range(T):
            mine = o[t].at[layers[t], c]
            cp = _remote(mine, mine, send.at[t], recv.at[t], (x, y, 1 - c))
            cp.start()
            cps.append(cp)
        for t in range(T):
            other = o[t].at[layers[t], 1 - c]
            _remote(other, other, send.at[t], recv.at[t], (x, y, 1 - c)).wait_recv()
        for cp in cps:
            cp.wait_send()

    return pl.pallas_call(
        body, name=name, in_specs=[_ANY] * T, out_specs=[_ANY] * T,
        out_shape=[jax.ShapeDtypeStruct(s.shape, s.dtype) for s in tots],
        input_output_aliases={t: t for t in range(T)},
        scratch_shapes=[pltpu.SemaphoreType.DMA((T,)), pltpu.SemaphoreType.DMA((T,))],
        )(*tots)


def all_reduce_small(name, vec):
    rows = vec.shape[0]
    flips = [(fx, fy, fc) for fx in (0, 1) for fy in (0, 1) for fc in (0, 1)][1:]

    def body(v_ref, o_ref, buf, send, recv):
        x, y, c = _place()
        me = 4 * x + 2 * y + c
        buf[me] = v_ref[...]
        cps = []
        for k, (fx, fy, fc) in enumerate(flips):
            cp = _remote(buf.at[me], buf.at[me], send.at[k], recv.at[k],
                         (_flip(x, fx), _flip(y, fy), _flip(c, fc)))
            cp.start()
            cps.append(cp)
        for k, (fx, fy, fc) in enumerate(flips):
            slab = buf.at[4 * _flip(x, fx) + 2 * _flip(y, fy) + _flip(c, fc)]
            _remote(slab, slab, send.at[k], recv.at[k], (x, y, c)).wait_recv()
        for cp in cps:
            cp.wait_send()
        acc = buf[0]
        for d in range(1, N_DEV):
            acc = acc + buf[d]
        o_ref[...] = acc

    return pl.pallas_call(
        body, name=name,
        in_specs=[pl.BlockSpec(memory_space=pltpu.VMEM)], out_specs=pl.BlockSpec(memory_space=pltpu.VMEM),
        out_shape=jax.ShapeDtypeStruct((rows, LANES), f32),
        scratch_shapes=[pltpu.VMEM((N_DEV, rows, LANES), f32),
                        pltpu.SemaphoreType.DMA((N_DEV - 1,)), pltpu.SemaphoreType.DMA((N_DEV - 1,))],
        compiler_params=pltpu.CompilerParams(vmem_limit_bytes=V7X_VMEM_LIMIT))(vec)


def _all_done(arrays):
    return jnp.stack([a[(0,) * a.ndim].astype(f32) for a in arrays]).sum(keepdims=True)


def _pack(arrays):
    flat = jnp.concatenate([a.reshape(-1) for a in arrays])
    n = flat.shape[0]
    rows = -(-n // (8 * LANES)) * 8
    return jnp.pad(flat, (0, rows * LANES - n)).reshape(rows, LANES)


def _unpack(vec, shapes):
    flat = vec.reshape(-1)
    out, pos = [], 0
    for s in shapes:
        n = math.prod(s)
        out.append(flat[pos:pos + n].reshape(s))
        pos += n
    return out


def _f_first(x, g):
    return x, _rms(x, g)


def _f_mid(h, m, gp, gn):
    h1 = h + _rms(m, gp)
    return h1, _rms(h1, gn)


def _f_mid_bias(h, m, b, gp, gn):
    h1 = h + _rms(m + b, gp)
    return h1, _rms(h1, gn)


def _f_last(h, m, gp):
    return (h + _rms(m, gp),)


def _f_swiglu(gate, up):
    return (_silu(gate) * up,)


def _swiglu_tile(gate, up):
    return _silu(gate.astype(f32)) * up.astype(f32)


def _swiglu_bwd_tile(d_act, gate, up):
    _, vjp = jax.vjp(_f_swiglu, gate.astype(f32), up.astype(f32))
    return vjp((d_act,))


def _f_glu(a, g, ba, bg):
    return ((a + ba) * _sigmoid(g + bg),)


def _f_ln_silu(x, g, b):
    mu = jnp.mean(x, axis=-1, keepdims=True)
    xc = x - mu
    y = xc * lax.rsqrt(jnp.mean(xc * xc, axis=-1, keepdims=True) + LN_EPS) * g + b
    return (_silu(y),)


def _f_lower_bounds(logits):
    n = logits.shape[0]
    e = jnp.exp(logits - jnp.max(logits, axis=0, keepdims=True))
    p = e / jnp.sum(e, axis=0, keepdims=True)
    layer = lax.broadcasted_iota(jnp.int32, logits.shape, 0)
    out = -jnp.broadcast_to(p[0:1, :], logits.shape)
    for j in range(n):
        out = out + jnp.where(layer >= j, p[j:j + 1, :], 0.0)
    return (out,)


WEIGHT_NAMES = ['mix_pre_g', 'mix_post_g', 'ffn_pre_g', 'ffn_post_g', 'hgrn_lb_logits', 'even_w_in',
                'hgrn_norm_g', 'ssd_conv_w', 'ssd_conv_b', 'ssd_dt_bias', 'ssd_a_log', 'ssd_d', 'ssd_norm_g',
                'even_w_out', 'conf_w1', 'conf_b1', 'conf_dw_w', 'conf_dw_b', 'conf_ln_g', 'conf_ln_b',
                'conf_w2', 'conf_b2', 'ffn_w_gate', 'ffn_w_up', 'ffn_w_down']
BIG = ['even_w_in', 'even_w_out', 'conf_w1', 'conf_w2', 'ffn_w_gate', 'ffn_w_up', 'ffn_w_down']
SMALL_SHARDED = {'ssd_conv_w': 2, 'conf_b1': 1, 'conf_dw_w': 2, 'conf_dw_b': 1, 'conf_ln_g': 1,
                 'conf_ln_b': 1, 'conf_b2': 1}


def _train_step(x, target, w, m, v):
    S, D = x.shape[1], x.shape[2]
    x2, t2 = x[0], target[0]
    NL = w['mix_pre_g'].shape[0]
    HB = w['ssd_dt_bias'].shape[1]
    GN = B_GROUPS * B_STATE
    xw, yw, cw = _place()
    chip = 2 * xw + yw
    tm = _tile(S, 128, 8)
    row1 = lambda a, i: a[i:i + 1]

    sharded = list(SMALL_SHARDED)
    placed = []
    for n in sharded:
        ax, a = SMALL_SHARDED[n], w[n]
        full = jnp.zeros(a.shape[:ax] + (a.shape[ax] * N_CHIPS,) + a.shape[ax + 1:], f32)
        start = [0] * a.ndim
        start[ax] = chip * a.shape[ax]
        placed.append(lax.dynamic_update_slice(full, jnp.where(cw == 0, a, 0.0), start))
    whole = dict(zip(sharded, _unpack(all_reduce_small("gather_small", _pack(placed)), [p.shape for p in placed])))
    small = {n: whole.get(n, w[n]) for n in WEIGHT_NAMES if n not in BIG}

    def mixer_keys(layer):
        names = ('even_w_in', 'even_w_out') if layer % 2 == 0 else ('conf_w1', 'conf_w2')
        return [(n, layer // 2) for n in names]

    def ffn_keys(layer):
        return [(n, layer) for n in ('ffn_w_gate', 'ffn_w_up', 'ffn_w_down')]

    groups = [keys(layer) for layer in range(NL) for keys in (mixer_keys, ffn_keys)]
    halves = lambda a: a.reshape(a.shape[:-2] + (2, a.shape[-2] // 2, a.shape[-1]))
    own, land = {}, {}

    def start_groups(name, some, dep=None):
        for g in some:
            for n, l in g:
                own[n, l], land[n, l] = place_own("place_own", w[n], l)
        return gather_ici_start(name, [[(halves(own[k]), halves(land[k])) for k in g] for g in some], dep)

    first, first_started = start_groups("gather_start_first", groups[:1])
    rest, rest_started = start_groups("gather_start_rest", groups[1:], first_started)
    handles = first + rest
    W, handed = {}, {}

    def fetch_begin(gi, after):
        arrived = gather_ici_wait(f"gather_wait_{gi}", handles[gi], after)
        handed[gi] = forward_start(f"forward_start_{gi}", arrived)
        return handed[gi][-1]

    def fetch_end(gi, after):
        for k, a in zip(groups[gi], forward_wait(f"forward_wait_{gi}", handed[gi], after)):
            W[k] = a.reshape((N_CHIPS, 1) + own[k].shape)

    WM = 6 * D + 2 * GN
    w_main, w_dt = {}, {}

    n_even = small['hgrn_lb_logits'].shape[0]
    (lbs,) = _stage_fwd("lower_bounds", _f_lower_bounds, [small['hgrn_lb_logits']], [], [(D, f32)], tm=n_even)
    saved = []
    h = x2
    (u,) = _stage_fwd("pre_norm", lambda a, g: (_rms(a, g),), [h],
                      [row1(small['mix_pre_g'], 0) + rest_started[0, 0]], [(D, bf16)], tm=tm)
    fetch_begin(0, u)
    for layer in range(NL):
        li = layer // 2
        r = {'h': h, 'u': u}
        fetch_end(2 * layer, u)
        if layer % 2 == 0:
            wm, wd = join_in_proj("in_proj_join", W['even_w_in', li][:, 0], WM)
            w_main[li], w_dt[li] = wm[None, None], wd[None, None]
            r['ymain'] = mm_nn_col("in_proj", u, w_main[li], 0)
            r['dtr'] = mm_nn_col("in_proj_dt", u, w_dt[li], 0)
            r['xact'] = conv_fwd("ssd_conv", r['ymain'], 5 * D // CONV_CH, small['ssd_conv_w'][li],
                                 row1(small['ssd_conv_b'], li), True, f32)
            o_a, r['hg_st'] = hgrn_fwd("hgrn", r['ymain'], row1(lbs, li), row1(small['hgrn_norm_g'], li), D)
            begun = fetch_begin(2 * layer + 1, o_a)[0, 0]
            o_b, r['ssd_st'] = ssd_fwd("ssd", r['xact'], r['ymain'], r['dtr'], row1(small['ssd_dt_bias'], li),
                                       row1(small['ssd_a_log'], li), row1(small['ssd_d'], li),
                                       row1(small['ssd_norm_g'], li) + begun, D)
            r['mixed'] = jnp.concatenate([o_a, o_b], axis=1)
            r['m'] = mm_nn_row("out_proj", r['mixed'], W['even_w_out', li], 0)
            mid_fn, mid_par = _f_mid, []
        else:
            r['c1'] = mm_nn_col("conf_in", u, W['conf_w1', li], 0)
            b1 = row1(small['conf_b1'], li)
            tn = _tile(D, 512)
            (r['glu'],) = _stage_fwd("conf_glu", _f_glu, [r['c1'], (r['c1'], D // tn)], [b1, (b1, D // tn)],
                                     [(D, f32)], tm=tm, tn=tn)
            r['cc'] = conv_fwd("conf_conv", r['glu'], 0, small['conf_dw_w'][li], row1(small['conf_dw_b'], li),
                               False, f32)
            begun = fetch_begin(2 * layer + 1, r['cc'])[0, 0]
            (r['c2'],) = _stage_fwd("conf_ln", _f_ln_silu, [r['cc']],
                                    [row1(small['conf_ln_g'], li) + begun, row1(small['conf_ln_b'], li)],
                                    [(D, bf16)], tm=tm)
            r['m'] = mm_nn_row("conf_out", r['c2'], W['conf_w2', li], 0)
            mid_fn, mid_par = _f_mid_bias, [row1(small['conf_b2'], li)]
        r['mid_fn'] = mid_fn
        r['mid_par'] = mid_par + [row1(small['mix_post_g'], layer), row1(small['ffn_pre_g'], layer)]
        r['h1'], r['u2'] = _stage_fwd("mid_norm", mid_fn, [h, r['m']], r['mid_par'], [(D, f32), (D, bf16)], tm=tm)
        fetch_end(2 * layer + 1, r['u2'])
        r['gate'] = mm_nn_col("ffn_gate", r['u2'], W['ffn_w_gate', layer], 0, bf16)
        begun = fetch_begin(2 * layer + 2, r['gate']) if layer + 1 < NL else None
        r['up'] = mm_nn_col("ffn_up", r['u2'], W['ffn_w_up', layer], 0, bf16, dep=begun)
        r['dn'] = mm_nn_row("ffn_down", [r['gate'], r['up']], W['ffn_w_down', layer], 0, a_fn=_swiglu_tile)
        if layer + 1 < NL:
            r['end_fn'] = _f_mid
            r['end_par'] = [row1(small['ffn_post_g'], layer), row1(small['mix_pre_g'], layer + 1)]
            h, u = _stage_fwd("end_norm", _f_mid, [r['h1'], r['dn']], r['end_par'], [(D, f32), (D, bf16)], tm=tm)
        else:
            r['end_fn'] = _f_last
            r['end_par'] = [row1(small['ffn_post_g'], layer)]
            (h,) = _stage_fwd("last_norm", _f_last, [r['h1'], r['dn']], r['end_par'], [(D, f32)], tm=tm)
        saved.append(r)

    dy, loss_local = loss_head("loss_head", h, t2, tm)
    loss = lax.psum(loss_local, ("x", "y", "c"))

    gs_rows = {n: [None] * small[n].shape[0] for n in small}
    gs = {}

    def put(n, i, val):
        gs_rows[n][i] = val.reshape(small[n].shape[1:])

    final = {n: _uninit(n, (w[n].shape[0], 2, w[n].shape[1] // 2, w[n].shape[2]), f32) for n in BIG}
    to_sibling, to_chips = [], []

    def reduce_begin(gi, parts, dep=None):
        handle = sibling_start(f"sibling_start_{gi}", [halves(p) for p in parts], dep)
        to_sibling.append((gi, handle))
        return handle[-1][0, 0]

    def reduce_middle(after):
        gi, handle = to_sibling.pop(0)
        parts, from_sib = sibling_wait(f"sibling_wait_{gi}", handle, after)
        sums = [rs_add("reduce_add", a, b) for a, b in zip(parts, from_sib)]
        handle = reduce_ici_start(f"reduce_start_{gi}", sums)
        to_chips.append((gi, handle))
        return handle[-1][0, 0]

    def reduce_finish(after):
        gi, handle = to_chips.pop(0)
        keys = groups[gi]
        sums, lands = reduce_ici_wait(f"reduce_wait_{gi}", handle, after)
        for (n, l), s_, b_ in zip(keys, sums, lands):
            final[n] = rs_sum4("reduce_sum", s_, b_, final[n], l)
        names = [n for n, _ in keys]
        shared = rs_share("reduce_share", [final[n] for n in names], [l for _, l in keys])
        final.update(zip(names, shared))

    def reduce_step(gi, parts, newest, dep=None):
        zero = reduce_begin(gi, parts, dep)
        if to_chips:
            reduce_finish(newest)
        if len(to_sibling) > 1:
            zero = zero + reduce_middle(newest)
        return zero

    grads, first_layer = {}, {}

    def small_gradients(dh, du_parts, started):
        (first_layer['grad_x'],), pg = _stage_bwd("pre_norm_bwd", _f_first, [x2],
                                                   [row1(small['mix_pre_g'], 0) + started],
                                                   [[dh], du_parts], [f32], tm=tm)
        put('mix_pre_g', 0, pg[0])
        dlbs = jnp.concatenate([saved[2 * i]['dlb'] for i in range(n_even)], axis=0)
        (dlogits,), _ = _stage_bwd("lower_bounds_bwd", _f_lower_bounds, [small['hgrn_lb_logits']], [],
                                   [[dlbs]], [f32], tm=n_even)
        names_s = [n for n in WEIGHT_NAMES if n not in BIG]
        for n in names_s:
            gs[n] = dlogits if n == 'hgrn_lb_logits' else jnp.stack(gs_rows[n])
        total = all_reduce_small("reduce_small", _pack([gs[n] for n in names_s]))
        for n, a in zip(names_s, _unpack(total, [gs[n].shape for n in names_s])):
            if n in SMALL_SHARDED:
                ax = SMALL_SHARDED[n]
                size = w[n].shape[ax]
                start = [0] * a.ndim
                start[ax] = chip * size
                a = lax.dynamic_slice(a, start, a.shape[:ax] + (size,) + a.shape[ax + 1:])
            grads[n] = a
        return total

    dh = dy
    du_parts = None
    started = None
    for layer in reversed(range(NL)):
        li = layer // 2
        r = saved[layer]
        cts = [[dh]] if du_parts is None else [[dh], du_parts]
        par = r['end_par'] if started is None else [r['end_par'][0] + started] + r['end_par'][1:]
        (dh1, d_dn), pg = _stage_bwd("end_norm_bwd", r['end_fn'], [r['h1'], r['dn']], par, cts,
                                     [f32, bf16], tm=tm)
        put('ffn_post_g', layer, pg[0])
        if du_parts is not None:
            put('mix_pre_g', layer + 1, pg[1])
        d_gate, d_up = mm_nt_row("ffn_down_dx", d_dn, W['ffn_w_down', layer], 0, bf16,
                                 tail=(_swiglu_bwd_tile, [r['gate'], r['up']]), n_out=2)
        g_down = mm_tn_row("ffn_down_dw", [r['gate'], r['up']], d_dn, N_CHIPS, a_fn=_swiglu_tile)
        du_a = mm_nt_col("ffn_gate_dx", d_gate, W['ffn_w_gate', layer], 0)
        du_b = mm_nt_col("ffn_up_dx", d_up, W['ffn_w_up', layer], 0)
        g_gate = mm_tn_col("ffn_gate_dw", r['u2'], d_gate, N_CHIPS)
        g_up = mm_tn_col("ffn_up_dw", r['u2'], d_up, N_CHIPS)
        started = reduce_step(2 * layer + 1, [g_gate, g_up, g_down], g_up)
        par = r['mid_par'][:-1] + [r['mid_par'][-1] + started]
        (dh, dm), pg = _stage_bwd("mid_norm_bwd", r['mid_fn'], [r['h'], r['m']], par,
                                  [[dh1], [du_a, du_b]], [f32, bf16], tm=tm)
        put('mix_post_g', layer, pg[-2])
        put('ffn_pre_g', layer, pg[-1])
        if layer % 2 == 0:
            d_mixed = mm_nt_row("out_proj_dx", dm, W['even_w_out', li], 0)
            g_out = mm_tn_row("out_proj_dw", r['mixed'], dm, N_CHIPS)
            dxs, dbm, dcm, dz, ddt, ddtb, dal, dds, dbn = ssd_bwd(
                "ssd_bwd", r['xact'], r['ymain'], r['dtr'], row1(small['ssd_dt_bias'], li),
                row1(small['ssd_a_log'], li), row1(small['ssd_d'], li), row1(small['ssd_norm_g'], li),
                r['ssd_st'], d_mixed, D)
            put('ssd_dt_bias', li, ddtb)
            put('ssd_a_log', li, dal)
            put('ssd_d', li, dds)
            put('ssd_norm_g', li, dbn)
            d_xact = jnp.concatenate([dxs, dbm, dcm], axis=1)
            d_xbc, dcw, dcb = conv_bwd("ssd_conv_bwd", r['ymain'], 5 * D // CONV_CH, small['ssd_conv_w'][li],
                                       row1(small['ssd_conv_b'], li), d_xact, True, bf16)
            put('ssd_conv_w', li, dcw)
            put('ssd_conv_b', li, dcb)
            dq, df, dv, dg, dlb, dan = hgrn_bwd("hgrn_bwd", r['ymain'], row1(lbs, li), row1(small['hgrn_norm_g'], li),
                                               r['hg_st'], d_mixed, D)
            put('hgrn_norm_g', li, dan)
            r['dlb'] = dlb
            d_main = jnp.concatenate([dq, df, dv, dg, dz, d_xbc], axis=1)
            du_parts = [mm_nt_col("in_proj_dx", d_main, w_main[li], 0),
                        mm_nt_col("in_proj_dt_dx", ddt, w_dt[li], 0)]
            g_in = split_in_proj("in_proj_split", mm_tn_col("in_proj_dw", r['u'], d_main, 1)[0],
                                 mm_tn_col("in_proj_dt_dw", r['u'], ddt, 1)[0], N_CHIPS)
            mixer_parts, newest = [g_in, g_out], g_in
        else:
            put('conf_b2', li, pg[0])
            d_c2 = mm_nt_row("conf_out_dx", dm, W['conf_w2', li], 0)
            g_w2 = mm_tn_row("conf_out_dw", r['c2'], dm, N_CHIPS)
            (d_cc,), pl_ = _stage_bwd("conf_ln_bwd", _f_ln_silu, [r['cc']],
                                      [row1(small['conf_ln_g'], li), row1(small['conf_ln_b'], li)],
                                      [[d_c2]], [f32], tm=tm)
            put('conf_ln_g', li, pl_[0])
            put('conf_ln_b', li, pl_[1])
            d_glu, ddw, ddb = conv_bwd("conf_conv_bwd", r['glu'], 0, small['conf_dw_w'][li],
                                       row1(small['conf_dw_b'], li), d_cc, False, f32)
            put('conf_dw_w', li, ddw)
            put('conf_dw_b', li, ddb)
            b1 = row1(small['conf_b1'], li)
            tn = _tile(D, 512)
            (da, dg_), pb = _stage_bwd("conf_glu_bwd", _f_glu, [r['c1'], (r['c1'], D // tn)], [b1, (b1, D // tn)],
                                       [[d_glu]], [bf16, bf16], tm=tm, tn=tn)
            put('conf_b1', li, jnp.concatenate([pb[0], pb[1]], axis=1))
            d_c1 = jnp.concatenate([da, dg_], axis=1)
            du_parts = [mm_nt_col("conf_in_dx", d_c1, W['conf_w1', li], 0)]
            g_w1 = mm_tn_col("conf_in_dw", r['u'], d_c1, N_CHIPS)
            mixer_parts, newest = [g_w1, g_w2], g_w1
        small_done = small_gradients(dh, du_parts, started) if layer == 0 else None
        started = reduce_step(2 * layer, mixer_parts, newest, small_done)
    started = started + reduce_middle(mixer_parts[0])

    delta, new_m, new_v = {}, {}, {}

    def update(names):
        for n in names:
            g = final[n].reshape(w[n].shape) if n in BIG else grads[n]
            delta[n], new_m[n], new_v[n], grads[n] = adamw("adamw", w[n], g, m[n], v[n])
        return _all_done([delta[n] for n in names])

    second_last = [n for n, _ in groups[to_chips[0][0]]]
    last = [n for n, _ in groups[to_chips[1][0]]]
    grads[WEIGHT_NAMES[0]] = grads[WEIGHT_NAMES[0]] + started
    grad_x2 = first_layer['grad_x']
    done = update([n for n in WEIGHT_NAMES if n not in last + second_last])
    reduce_finish(done)
    done = update(second_last)
    reduce_finish(done)
    update(last)
    return (loss, grad_x2[None], *[grads[n] for n in WEIGHT_NAMES], *[delta[n] for n in WEIGHT_NAMES],
            *[new_m[n] for n in WEIGHT_NAMES], *[new_v[n] for n in WEIGHT_NAMES])


def kernel(x, mix_pre_g, mix_post_g, ffn_pre_g, ffn_post_g, hgrn_lb_logits, even_w_in, hgrn_norm_g, ssd_conv_w, ssd_conv_b, ssd_dt_bias, ssd_a_log, ssd_d, ssd_norm_g, even_w_out, conf_w1, conf_b1, conf_dw_w, conf_dw_b, conf_ln_g, conf_ln_b, conf_w2, conf_b2, ffn_w_gate, ffn_w_up, ffn_w_down, loss_target, m_mix_pre_g, m_mix_post_g, m_ffn_pre_g, m_ffn_post_g, m_hgrn_lb_logits, m_even_w_in, m_hgrn_norm_g, m_ssd_conv_w, m_ssd_conv_b, m_ssd_dt_bias, m_ssd_a_log, m_ssd_d, m_ssd_norm_g, m_even_w_out, m_conf_w1, m_conf_b1, m_conf_dw_w, m_conf_dw_b, m_conf_ln_g, m_conf_ln_b, m_conf_w2, m_conf_b2, m_ffn_w_gate, m_ffn_w_up, m_ffn_w_down, v_mix_pre_g, v_mix_post_g, v_ffn_pre_g, v_ffn_post_g, v_hgrn_lb_logits, v_even_w_in, v_hgrn_norm_g, v_ssd_conv_w, v_ssd_conv_b, v_ssd_dt_bias, v_ssd_a_log, v_ssd_d, v_ssd_norm_g, v_even_w_out, v_conf_w1, v_conf_b1, v_conf_dw_w, v_conf_dw_b, v_conf_ln_g, v_conf_ln_b, v_conf_w2, v_conf_b2, v_ffn_w_gate, v_ffn_w_up, v_ffn_w_down):
    args = locals()
    w = {n: args[n] for n in WEIGHT_NAMES}
    m = {n: args["m_" + n] for n in WEIGHT_NAMES}
    v = {n: args["v_" + n] for n in WEIGHT_NAMES}
    return _train_step(x, loss_target, w, m, v)
```

```python
import functools
import math

import jax
import jax.numpy as jnp
from jax import lax
from jax.experimental import pallas as pl
from jax.experimental.pallas import tpu as pltpu

f32 = jnp.float32
bf16 = jnp.bfloat16
MESH = pl.DeviceIdType.MESH
HI = lax.Precision.HIGHEST

A_HEAD = 128
A_CHUNK = 64
A_SUB = 8
A_REF = 4
A_EXP_CAP = 60.0
A_HEADS_PER_STEP = 16
A_F_MIN = 1e-6
B_HEAD = 64
B_GROUPS = 4
B_STATE = 128
B_CONV = 4
B_CHUNK = 128
C_KERNEL = 31
RMS_EPS = 1e-6
LN_EPS = 1e-5
ADAM_LR = 0.001
ADAM_B1 = 0.9
ADAM_B2 = 0.999
ADAM_EPS = 1e-08
ADAM_WD = 0.01
ADAM_STEP = 10

N_CHIPS = 4
N_DEV = 8
V7X_VMEM_LIMIT = 56 * 1024 * 1024
LANES = 128
CONV_PAD = 32
CONV_ROWS = 128
CONV_CH = 256

NN = (((1,), (0,)), ((), ()))
NT = (((1,), (1,)), ((), ()))
TN = (((0,), (0,)), ((), ()))


def _tile(n, cap, unit=LANES):
    best = None
    for t in range(unit, min(n, cap) + 1, unit):
        if n % t == 0:
            best = t
    return n if best is None else best


def _cp(*sem):
    return pltpu.CompilerParams(dimension_semantics=sem, vmem_limit_bytes=V7X_VMEM_LIMIT)


def _sigmoid(x):
    return jax.nn.sigmoid(x)


def _silu(x):
    return x * jax.nn.sigmoid(x)


def _rms(x, g):
    return x * lax.rsqrt(jnp.mean(x * x, axis=-1, keepdims=True) + RMS_EPS) * g


def _pair(a):
    return a if isinstance(a, tuple) else (a, 0)


def _stage(name, fn, rows, params, outs, par_outs=(), *, tm, tn=None):
    rows = [_pair(r) for r in rows]
    params = [_pair(p) for p in params]
    S = rows[0][0].shape[0]
    n_in, n_o = len(rows) + len(params), len(outs)
    if tn is None:
        grid = (S // tm,)
        in_specs = [pl.BlockSpec((tm, a.shape[1]), lambda i: (i, 0)) for a, _ in rows]
        in_specs += [pl.BlockSpec(a.shape, lambda i: (0, 0)) for a, _ in params]
        out_specs = [pl.BlockSpec((tm, w), lambda i: (i, 0)) for w, _ in outs]
        out_specs += [pl.BlockSpec((k, w), lambda i: (0, 0)) for k, w in par_outs]
        row_axis = 0
        sem = ("arbitrary",) if par_outs else ("parallel",)
    else:
        grid = (outs[0][0] // tn, S // tm)
        in_specs = [pl.BlockSpec((tm, tn), lambda j, i, o=o: (i, j + o)) for _, o in rows]
        in_specs += [pl.BlockSpec((a.shape[0], tn), lambda j, i, o=o: (0, j + o)) for a, o in params]
        out_specs = [pl.BlockSpec((tm, tn), lambda j, i: (i, j)) for _ in outs]
        out_specs += [pl.BlockSpec((k, tn), lambda j, i: (0, j)) for k, _ in par_outs]
        row_axis = 1
        sem = ("parallel", "arbitrary") if par_outs else ("parallel", "parallel")
    out_shape = [jax.ShapeDtypeStruct((S, w), d) for w, d in outs]
    out_shape += [jax.ShapeDtypeStruct((k, w), f32) for k, w in par_outs]

    def body(*refs):
        res = fn(*[r[...] for r in refs[:n_in]])
        for r, v in zip(refs[n_in:n_in + n_o], res[:n_o]):
            r[...] = v.astype(r.dtype)
        if par_outs:
            acc_refs = refs[n_in + n_o:]

            @pl.when(pl.program_id(row_axis) == 0)
            def _():
                for r in acc_refs:
                    r[...] = jnp.zeros_like(r)

            for r, v in zip(acc_refs, res[n_o:]):
                r[...] += v

    return pl.pallas_call(
        body, name=name, grid=grid, in_specs=in_specs, out_specs=out_specs, out_shape=out_shape,
        compiler_params=_cp(*sem))(*[a for a, _ in rows], *[a for a, _ in params])


def _stage_fwd(name, fn, rows, params, outs, *, tm, tn=None):
    n_r = len(rows)

    def ffn(*t):
        return fn(*[v.astype(f32) for v in t[:n_r]], *t[n_r:])

    return _stage(name, ffn, rows, params, outs, tm=tm, tn=tn)


def _stage_bwd(name, fn, rows, params, cts, drow, *, tm, tn=None):
    rows = [_pair(r) for r in rows]
    params = [_pair(p) for p in params]
    n_r, n_p = len(rows), len(params)
    flat_ct = [_pair(c) for group in cts for c in group]
    counts = [len(group) for group in cts]
    need = [i for i, d in enumerate(drow) if d is not None]

    def bfn(*t):
        r = [v.astype(f32) for v in t[:n_r]]
        c = t[n_r:n_r + len(flat_ct)]
        p = list(t[n_r + len(flat_ct):])
        res, vjp = jax.vjp(fn, *r, *p)
        ct, pos = [], 0
        for o, k in zip(res, counts):
            s = c[pos].astype(f32)
            for e in range(1, k):
                s = s + c[pos + e].astype(f32)
            pos += k
            ct.append(s.astype(o.dtype))
        g = vjp(tuple(ct))
        return tuple(g[i] for i in need) + tuple(g[n_r:])

    if tn is None:
        outs = [(rows[i][0].shape[1], drow[i]) for i in need]
        par_outs = [p.shape for p, _ in params]
    else:
        w_all = flat_ct[0][0].shape[1]
        outs = [(w_all, drow[i]) for i in need]
        par_outs = [(p.shape[0], w_all) for p, _ in params]
    res = _stage(name, bfn, rows + flat_ct, params, outs, par_outs, tm=tm, tn=tn)
    return res[:len(need)], res[len(need):]


def _mm(name, dims, a, b, grid, a_spec, b_spec, o_spec, out_shape, acc_shape, a_fn=None, tail=None, dep=None):
    nk = grid[2]
    a_list = list(a) if isinstance(a, (list, tuple)) else [a]
    na = len(a_list)
    t_fn, t_arrays = tail if tail is not None else (None, [])
    ne = len(t_arrays)
    deps = [] if dep is None else [dep]
    multi = isinstance(out_shape, (list, tuple))

    n_out = len(out_shape) if multi else 1

    def body(*refs):
        a_refs, b_ref, t_refs = refs[:na], refs[na], refs[na + 1:na + 1 + ne]
        first_out = na + 1 + ne + len(deps)
        o_refs = refs[first_out:first_out + n_out]
        k = pl.program_id(2)

        def prod():
            lhs = a_refs[0][...] if a_fn is None else a_fn(*[r[...] for r in a_refs])
            return lax.dot_general(lhs.astype(bf16), b_ref[...].astype(bf16), dims, preferred_element_type=f32)

        def finish(total):
            res = (total,) if t_fn is None else t_fn(total, *[r[...] for r in t_refs])
            for r, val in zip(o_refs, res):
                r[...] = val.astype(r.dtype)

        if nk == 1:
            finish(prod())
        else:
            acc = refs[-1]

            @pl.when(k == 0)
            def _():
                acc[...] = prod()

            if nk > 2:
                @pl.when(jnp.logical_and(k > 0, k < nk - 1))
                def _():
                    acc[...] += prod()

            @pl.when(k == nk - 1)
            def _():
                finish(acc[...] + prod())

    return pl.pallas_call(
        body, name=name, grid=grid,
        in_specs=[a_spec] * na + [b_spec] + [o_spec] * ne + [pl.BlockSpec(memory_space=pl.ANY)] * len(deps),
        out_specs=[o_spec] * len(out_shape) if multi else o_spec, out_shape=out_shape,
        scratch_shapes=[pltpu.VMEM(acc_shape, f32)] if nk > 1 else [],
        compiler_params=_cp("parallel", "parallel", "arbitrary"))(*a_list, b, *t_arrays, *deps)


def _mm_tiles(S, roomy=True):
    return _tile(S, 1024 if roomy else 512)


def join_in_proj(name, parts, width_main):
    P, D, CI = parts.shape
    rest = P * CI - width_main
    tr = _tile(D, 128, 16)

    def body(p_ref, m_ref, d_ref):
        for j in range(P - 1):
            m_ref[:, j * CI:(j + 1) * CI] = p_ref[j]
        m_ref[:, (P - 1) * CI:width_main] = p_ref[P - 1][:, :CI - rest]
        d_ref[...] = p_ref[P - 1][:, CI - rest:]

    return pl.pallas_call(
        body, name=name, grid=(D // tr,),
        in_specs=[pl.BlockSpec((P, tr, CI), lambda i: (0, i, 0))],
        out_specs=[pl.BlockSpec((tr, width_main), lambda i: (i, 0)), pl.BlockSpec((tr, rest), lambda i: (i, 0))],
        out_shape=[jax.ShapeDtypeStruct((D, width_main), parts.dtype), jax.ShapeDtypeStruct((D, rest), parts.dtype)],
        compiler_params=_cp("parallel"))(parts)


def split_in_proj(name, main, rest, P):
    D, WM = main.shape
    nr = rest.shape[1]
    CI = (WM + nr) // P
    tr = _tile(D, 128, 16)

    def body(m_ref, d_ref, p_ref):
        for j in range(P - 1):
            p_ref[j] = m_ref[:, j * CI:(j + 1) * CI]
        p_ref[P - 1, :, :CI - nr] = m_ref[:, (P - 1) * CI:WM]
        p_ref[P - 1, :, CI - nr:] = d_ref[...]

    return pl.pallas_call(
        body, name=name, grid=(D // tr,),
        in_specs=[pl.BlockSpec((tr, WM), lambda i: (i, 0)), pl.BlockSpec((tr, nr), lambda i: (i, 0))],
        out_specs=pl.BlockSpec((P, tr, CI), lambda i: (0, i, 0)),
        out_shape=jax.ShapeDtypeStruct((P, D, CI), main.dtype),
        compiler_params=_cp("parallel"))(main, rest)


def mm_nn_col(name, a, W, li, out_dtype=f32, dep=None):
    P, _, K, C = W.shape
    S = a.shape[0]
    tm, tn, tk = _mm_tiles(S), _tile(C, 1536), _tile(K, 2048)
    nc = C // tn
    return _mm(name, NN, a, W, (S // tm, P * nc, K // tk),
               pl.BlockSpec((tm, tk), lambda i, j, k: (i, k)),
               pl.BlockSpec((None, None, tk, tn), lambda i, j, k: (j // nc, li, k, j % nc)),
               pl.BlockSpec((tm, tn), lambda i, j, k: (i, j)),
               jax.ShapeDtypeStruct((S, P * C), out_dtype), (tm, tn), dep=dep)


def mm_nn_row(name, a, W, li, out_dtype=f32, a_fn=None):
    P, _, R, N = W.shape
    S = (a[0] if a_fn is not None else a).shape[0]
    tm, tn, tk = _mm_tiles(S, a_fn is None), _tile(N, 1024), _tile(R, 2048)
    nr = R // tk
    return _mm(name, NN, a, W, (S // tm, N // tn, P * nr),
               pl.BlockSpec((tm, tk), lambda i, j, k: (i, k)),
               pl.BlockSpec((None, None, tk, tn), lambda i, j, k: (k // nr, li, k % nr, j)),
               pl.BlockSpec((tm, tn), lambda i, j, k: (i, j)),
               jax.ShapeDtypeStruct((S, N), out_dtype), (tm, tn), a_fn=a_fn)


def mm_nt_col(name, dy, W, li, out_dtype=f32):
    P, _, K, C = W.shape
    S = dy.shape[0]
    tm, tn, tk = _mm_tiles(S), _tile(K, 1024), _tile(C, 2048)
    nc = C // tk
    return _mm(name, NT, dy, W, (S // tm, K // tn, P * nc),
               pl.BlockSpec((tm, tk), lambda i, j, k: (i, k)),
               pl.BlockSpec((None, None, tn, tk), lambda i, j, k: (k // nc, li, j, k % nc)),
               pl.BlockSpec((tm, tn), lambda i, j, k: (i, j)),
               jax.ShapeDtypeStruct((S, K), out_dtype), (tm, tn))


def mm_nt_row(name, dy, W, li, out_dtype=f32, tail=None, n_out=None):
    P, _, R, N = W.shape
    S = dy.shape[0]
    tm, tn, tk = _mm_tiles(S, tail is None), _tile(R, 1536), _tile(N, 2048)
    nr = R // tn
    out = jax.ShapeDtypeStruct((S, P * R), out_dtype)
    return _mm(name, NT, dy, W, (S // tm, P * nr, N // tk),
               pl.BlockSpec((tm, tk), lambda i, j, k: (i, k)),
               pl.BlockSpec((None, None, tn, tk), lambda i, j, k: (j // nr, li, j % nr, k)),
               pl.BlockSpec((tm, tn), lambda i, j, k: (i, j)),
               out if n_out is None else [out] * n_out, (tm, tn), tail=tail)


def mm_tn_col(name, a, dy, P):
    S, K = a.shape
    C = dy.shape[1] // P
    tm, tn, tk = _tile(K, 1024), _tile(C, 1536), _tile(S, 2048)
    nc = C // tn
    return _mm(name, TN, a, dy, (K // tm, P * nc, S // tk),
               pl.BlockSpec((tk, tm), lambda i, j, k: (k, i)),
               pl.BlockSpec((tk, tn), lambda i, j, k: (k, j)),
               pl.BlockSpec((None, tm, tn), lambda i, j, k: (j // nc, i, j % nc)),
               jax.ShapeDtypeStruct((P, K, C), bf16), (tm, tn))


def mm_tn_row(name, a, dy, P, a_fn=None):
    S, N = dy.shape
    R = (a[0] if a_fn is not None else a).shape[1] // P
    tm, tn, tk = _tile(R, 1536), _tile(N, 1024), _tile(S, 2048 if a_fn is None else 1024)
    nr = R // tm
    return _mm(name, TN, a, dy, (P * nr, N // tn, S // tk),
               pl.BlockSpec((tk, tm), lambda i, j, k: (k, i)),
               pl.BlockSpec((tk, tn), lambda i, j, k: (k, j)),
               pl.BlockSpec((None, tm, tn), lambda i, j, k: (i // nr, i % nr, j)),
               jax.ShapeDtypeStruct((P, R, N), bf16), (tm, tn), a_fn=a_fn)


def _hgrn_chunk(st, q, fp, v, gt, lb, an):
    C = q.shape[0]
    sig = _sigmoid(fp)
    f = lb + (1.0 - lb) * sig
    kk = (1.0 - lb) * (1.0 - sig)
    g = jnp.log(jnp.maximum(f, A_F_MIN))
    qs = _silu(q)
    row = lax.broadcasted_iota(jnp.int32, (C, C), 0)
    col = lax.broadcasted_iota(jnp.int32, (C, C), 1)
    tri = (col <= row).astype(f32)
    b = jnp.dot(tri, g, precision=HI, preferred_element_type=f32)
    o_inter = lax.dot_general((qs * jnp.exp(b)).astype(bf16), st.astype(bf16), NT,
                              preferred_element_type=f32)
    T, NB = A_SUB, C // A_SUB
    refs = [b[i * T + A_REF:i * T + A_REF + 1, :] for i in range(NB)]
    ref_q = jnp.concatenate([jnp.broadcast_to(r, (T, A_HEAD)) for r in refs], axis=0)
    ref_k = jnp.concatenate([jnp.broadcast_to(r, (C, A_HEAD)) for r in refs], axis=0)
    q_t = qs * jnp.exp(b - ref_q)
    k_t = jnp.concatenate([kk] * NB, axis=0) * jnp.exp(
        jnp.minimum(ref_k - jnp.concatenate([b] * NB, axis=0), A_EXP_CAP))
    s = lax.dot_general(q_t.astype(bf16), k_t.astype(bf16), NT, preferred_element_type=f32)
    trow = lax.broadcasted_iota(jnp.int32, (C, NB * C), 0)
    scol = lax.broadcasted_iota(jnp.int32, (C, NB * C), 1)
    keep = jnp.logical_and(scol // C == trow // T, scol % C <= trow)
    s = jnp.where(keep, s, 0.0)
    o_intra = jnp.dot(s.astype(bf16), jnp.concatenate([v.astype(bf16)] * NB, axis=0),
                      preferred_element_type=f32)
    bl = b[C - 1:C, :]
    kd = kk * jnp.exp(bl - b)
    st_new = st * jnp.exp(bl) + lax.dot_general(v.astype(bf16), kd.astype(bf16), TN,
                                                preferred_element_type=f32)
    o = o_inter + o_intra
    y = o * lax.rsqrt(jnp.mean(o * o, axis=-1, keepdims=True) + RMS_EPS) * an * _silu(gt)
    return st_new, y


def _hgrn_heads_per_step(HA):
    return A_HEADS_PER_STEP if HA % A_HEADS_PER_STEP == 0 else 1


def _hgrn_in_specs(HA, HP, cidx):
    W = HP * A_HEAD
    specs = [pl.BlockSpec((A_CHUNK, W), lambda h, c, s=s: (cidx(c), s * (HA // HP) + h)) for s in range(4)]
    specs += [pl.BlockSpec((1, W), lambda h, c: (0, h))] * 2
    return specs


def _head(ref, j):
    return ref[:, j * A_HEAD:(j + 1) * A_HEAD]


def hgrn_fwd(name, ymain, lb, an, D):
    S = ymain.shape[0]
    HA, nc = D // A_HEAD, S // A_CHUNK
    HP = _hgrn_heads_per_step(HA)
    W = HP * A_HEAD

    def body(q, fp, v, gt, lb_ref, an_ref, o_ref, sv_ref, st):
        @pl.when(pl.program_id(1) == 0)
        def _():
            st[...] = jnp.zeros_like(st)

        sv_ref[...] = st[...]
        for j in range(HP):
            st_new, y = _hgrn_chunk(st[j], _head(q, j), _head(fp, j), _head(v, j), _head(gt, j),
                                    _head(lb_ref, j), _head(an_ref, j))
            st[j] = st_new
            o_ref[:, j * A_HEAD:(j + 1) * A_HEAD] = y.astype(o_ref.dtype)

    return pl.pallas_call(
        body, name=name, grid=(HA // HP, nc),
        in_specs=_hgrn_in_specs(HA, HP, lambda c: c),
        out_specs=[pl.BlockSpec((A_CHUNK, W), lambda h, c: (c, h)),
                   pl.BlockSpec((HP, None, A_HEAD, A_HEAD), lambda h, c: (h, c, 0, 0))],
        out_shape=[jax.ShapeDtypeStruct((S, D), bf16),
                   jax.ShapeDtypeStruct((HA, nc, A_HEAD, A_HEAD), f32)],
        scratch_shapes=[pltpu.VMEM((HP, A_HEAD, A_HEAD), f32)],
        compiler_params=_cp("parallel", "arbitrary"))(ymain, ymain, ymain, ymain, lb, an)


def hgrn_bwd(name, ymain, lb, an, saved, dmixed, D):
    S = ymain.shape[0]
    HA, nc = D // A_HEAD, S // A_CHUNK
    HP = _hgrn_heads_per_step(HA)
    W = HP * A_HEAD
    rev = lambda c: nc - 1 - c

    def body(q, fp, v, gt, lb_ref, an_ref, sv_ref, do_ref, dq, df, dv, dg, dlb, dan, dst):
        @pl.when(pl.program_id(1) == 0)
        def _():
            dst[...] = jnp.zeros_like(dst)
            dlb[...] = jnp.zeros_like(dlb)
            dan[...] = jnp.zeros_like(dan)

        for j in range(HP):
            cols = slice(j * A_HEAD, (j + 1) * A_HEAD)
            _, vjp = jax.vjp(_hgrn_chunk, sv_ref[j], _head(q, j), _head(fp, j), _head(v, j), _head(gt, j),
                             _head(lb_ref, j), _head(an_ref, j))
            g = vjp((dst[j], _head(do_ref, j).astype(f32)))
            dst[j] = g[0]
            for r, x in zip((dq, df, dv, dg), g[1:5]):
                r[:, cols] = x.astype(r.dtype)
            dlb[:, cols] += g[5]
            dan[:, cols] += g[6]

    blk = pl.BlockSpec((A_CHUNK, W), lambda h, c: (rev(c), h))
    vec = pl.BlockSpec((1, W), lambda h, c: (0, h))
    return pl.pallas_call(
        body, name=name, grid=(HA // HP, nc),
        in_specs=_hgrn_in_specs(HA, HP, rev) + [
            pl.BlockSpec((HP, None, A_HEAD, A_HEAD), lambda h, c: (h, rev(c), 0, 0)), blk],
        out_specs=[blk] * 4 + [vec] * 2,
        out_shape=[jax.ShapeDtypeStruct((S, D), bf16)] * 4 + [jax.ShapeDtypeStruct((1, D), f32)] * 2,
        scratch_shapes=[pltpu.VMEM((HP, A_HEAD, A_HEAD), f32)],
        compiler_params=_cp("parallel", "arbitrary"))(ymain, ymain, ymain, ymain, lb, an, saved, dmixed)


def _ssd_chunk(hp, xs, bm, cm, z, dtr, dtb, alog, dsk, bn, g, R):
    L, GW = xs.shape
    HB = dtr.shape[1]
    R8 = max(R, 8)
    dt = jax.nn.softplus(dtr + dtb)
    a = -jnp.exp(alog)
    row = lax.broadcasted_iota(jnp.int32, (L, L), 0)
    col = lax.broadcasted_iota(jnp.int32, (L, L), 1)
    causal = col <= row
    cs = jnp.dot(causal.astype(f32), dt * a, precision=HI, preferred_element_type=f32)
    eh = lax.broadcasted_iota(jnp.int32, (HB, GW), 0)
    ec = lax.broadcasted_iota(jnp.int32, (HB, GW), 1)
    spread = (eh == g * R + ec // B_HEAD).astype(f32)
    sh = lax.broadcasted_iota(jnp.int32, (R8, HB), 1)
    sr = lax.broadcasted_iota(jnp.int32, (R8, HB), 0)
    pick_t = jnp.logical_and(sh == g * R + sr, sr < R).astype(f32)
    dtf = jnp.dot(dt, spread, precision=HI, preferred_element_type=f32)
    csf = jnp.dot(cs, spread, precision=HI, preferred_element_type=f32)
    dsf = jnp.dot(jnp.broadcast_to(dsk, (8, HB)), spread, precision=HI, preferred_element_type=f32)[0:1, :]
    cs_col = lax.dot_general(cs, pick_t, NT, precision=HI, preferred_element_type=f32)
    cs_row = lax.dot_general(pick_t, cs, NT, precision=HI, preferred_element_type=f32)
    xdt = xs * dtf
    cb = lax.dot_general(cm.astype(bf16), bm.astype(bf16), NT, preferred_element_type=f32)
    lane_head = lax.broadcasted_iota(jnp.int32, (1, GW), 1) // B_HEAD
    y = jnp.zeros((L, GW), f32)
    for r in range(R):
        seg = cs_col[:, r:r + 1] - cs_row[r:r + 1, :]
        dec = jnp.where(causal, jnp.exp(jnp.where(causal, seg, 0.0)), 0.0)
        xm = jnp.where(lane_head == r, xdt, 0.0)
        y = y + jnp.dot((cb * dec).astype(bf16), xm.astype(bf16), preferred_element_type=f32)
    csl = csf[L - 1:L, :]
    dte = jnp.exp(csl - csf)
    states = lax.dot_general(bm.astype(bf16), (xdt * dte).astype(bf16), TN, preferred_element_type=f32)
    y_off = jnp.dot(cm.astype(bf16), hp.astype(bf16), preferred_element_type=f32) * jnp.exp(csf)
    hn = hp * jnp.exp(csl) + states
    gated = (y + y_off + dsf * xs) * _silu(z)
    out = gated * lax.rsqrt(jnp.mean(gated * gated, axis=-1, keepdims=True) + RMS_EPS) * bn
    return hn, out


def _ssd_in_specs(D, HB, cidx):
    L, GW, N = B_CHUNK, D // B_GROUPS, B_STATE
    zoff, boff = 4 * D // GW, D // N
    return [
        pl.BlockSpec((L, GW), lambda c, g: (cidx(c), g)),
        pl.BlockSpec((L, N), lambda c, g: (cidx(c), boff + g)),
        pl.BlockSpec((L, N), lambda c, g: (cidx(c), boff + B_GROUPS + g)),
        pl.BlockSpec((L, GW), lambda c, g: (cidx(c), zoff + g)),
        pl.BlockSpec((L, HB), lambda c, g: (cidx(c), 0)),
        pl.BlockSpec((1, HB), lambda c, g: (0, 0)),
        pl.BlockSpec((1, HB), lambda c, g: (0, 0)),
        pl.BlockSpec((1, HB), lambda c, g: (0, 0)),
        pl.BlockSpec((1, GW), lambda c, g: (0, g)),
    ]


def ssd_fwd(name, xact, ymain, dtr, dtb, alog, dsk, bn, D):
    S, HB = dtr.shape
    nc, GW, R = S // B_CHUNK, D // B_GROUPS, HB // B_GROUPS

    def body(xs, bm, cm, z, dt_ref, dtb_ref, al_ref, ds_ref, bn_ref, o_ref, sv_ref, hs):
        g = pl.program_id(1)

        @pl.when(pl.program_id(0) == 0)
        def _():
            hs[g] = jnp.zeros((B_STATE, GW), f32)

        hp = hs[g]
        sv_ref[...] = hp
        hn, out = _ssd_chunk(hp, xs[...], bm[...], cm[...], z[...], dt_ref[...], dtb_ref[...], al_ref[...],
                             ds_ref[...], bn_ref[...], g, R)
        hs[g] = hn
        o_ref[...] = out.astype(o_ref.dtype)

    return pl.pallas_call(
        body, name=name, grid=(nc, B_GROUPS),
        in_specs=_ssd_in_specs(D, HB, lambda c: c),
        out_specs=[pl.BlockSpec((B_CHUNK, GW), lambda c, g: (c, g)),
                   pl.BlockSpec((None, None, B_STATE, GW), lambda c, g: (c, g, 0, 0))],
        out_shape=[jax.ShapeDtypeStruct((S, D), bf16),
                   jax.ShapeDtypeStruct((nc, B_GROUPS, B_STATE, GW), f32)],
        scratch_shapes=[pltpu.VMEM((B_GROUPS, B_STATE, GW), f32)],
        compiler_params=_cp("arbitrary", "arbitrary"))(xact, xact, xact, ymain, dtr, dtb, alog, dsk, bn)


def ssd_bwd(name, xact, ymain, dtr, dtb, alog, dsk, bn, saved, dmixed, D):
    S, HB = dtr.shape
    nc, GW, R = S // B_CHUNK, D // B_GROUPS, HB // B_GROUPS
    rev = lambda c: nc - 1 - c
    ooff = D // GW

    def body(xs, bm, cm, z, dt_ref, dtb_ref, al_ref, ds_ref, bn_ref, sv_ref, do_ref,
             dxs, dbm, dcm, dz, ddt, ddtb, dal, dds, dbn, dhs):
        c, g = pl.program_id(0), pl.program_id(1)

        @pl.when(c == 0)
        def _():
            dhs[g] = jnp.zeros((B_STATE, GW), f32)
            dbn[g] = jnp.zeros((1, GW), f32)

        @pl.when(jnp.logical_and(c == 0, g == 0))
        def _():
            ddtb[...] = jnp.zeros_like(ddtb)
            dal[...] = jnp.zeros_like(dal)
            dds[...] = jnp.zeros_like(dds)

        @pl.when(g == 0)
        def _():
            ddt[...] = jnp.zeros_like(ddt)

        fn = functools.partial(_ssd_chunk, g=g, R=R)
        _, vjp = jax.vjp(fn, sv_ref[...], xs[...], bm[...], cm[...], z[...], dt_ref[...], dtb_ref[...],
                         al_ref[...], ds_ref[...], bn_ref[...])
        gr = vjp((dhs[g], do_ref[...].astype(f32)))
        dhs[g] = gr[0]
        dxs[...] = gr[1]
        dbm[...] = gr[2]
        dcm[...] = gr[3]
        dz[...] = gr[4].astype(dz.dtype)
        ddt[...] += gr[5]
        ddtb[...] += gr[6]
        dal[...] += gr[7]
        dds[...] += gr[8]
        dbn[g] += gr[9]

    hb_vec = pl.BlockSpec((1, HB), lambda c, g: (0, 0))
    return pl.pallas_call(
        body, name=name, grid=(nc, B_GROUPS),
        in_specs=_ssd_in_specs(D, HB, rev) + [
            pl.BlockSpec((None, None, B_STATE, GW), lambda c, g: (rev(c), g, 0, 0)),
            pl.BlockSpec((B_CHUNK, GW), lambda c, g: (rev(c), ooff + g))],
        out_specs=[pl.BlockSpec((B_CHUNK, GW), lambda c, g: (rev(c), g)),
                   pl.BlockSpec((B_CHUNK, B_STATE), lambda c, g: (rev(c), g)),
                   pl.BlockSpec((B_CHUNK, B_STATE), lambda c, g: (rev(c), g)),
                   pl.BlockSpec((B_CHUNK, GW), lambda c, g: (rev(c), g)),
                   pl.BlockSpec((B_CHUNK, HB), lambda c, g: (rev(c), 0)),
                   hb_vec, hb_vec, hb_vec,
                   pl.BlockSpec((B_GROUPS, 1, GW), lambda c, g: (0, 0, 0))],
        out_shape=[jax.ShapeDtypeStruct((S, D), f32),
                   jax.ShapeDtypeStruct((S, B_GROUPS * B_STATE), f32),
                   jax.ShapeDtypeStruct((S, B_GROUPS * B_STATE), f32),
                   jax.ShapeDtypeStruct((S, D), bf16),
                   jax.ShapeDtypeStruct((S, HB), f32),
                   jax.ShapeDtypeStruct((1, HB), f32), jax.ShapeDtypeStruct((1, HB), f32),
                   jax.ShapeDtypeStruct((1, HB), f32),
                   jax.ShapeDtypeStruct((B_GROUPS, 1, GW), f32)],
        scratch_shapes=[pltpu.VMEM((B_GROUPS, B_STATE, GW), f32)],
        compiler_params=_cp("arbitrary", "arbitrary"))(
            xact, xact, xact, ymain, dtr, dtb, alog, dsk, bn, saved, dmixed)


def _conv_taps(xp, w_ref, b_ref, r0, K):
    acc = jnp.broadcast_to(b_ref[...], (CONV_ROWS, b_ref.shape[1]))
    for k in range(K):
        acc = acc + w_ref[k:k + 1, :] * xp[r0 + CONV_PAD - (K - 1) + k:r0 + CONV_PAD - (K - 1) + k + CONV_ROWS, :]
    return acc


def conv_fwd(name, x, xoff, w, b, act, out_dtype):
    S = x.shape[0]
    K, CW = w.shape
    tc = CONV_CH

    def body(x_ref, w_ref, b_ref, o_ref, xp):
        xp[0:CONV_PAD, :] = jnp.zeros((CONV_PAD, tc), f32)
        xp[CONV_PAD:CONV_PAD + S, :] = x_ref[...].astype(f32)
        for r0 in range(0, S, CONV_ROWS):
            acc = _conv_taps(xp, w_ref, b_ref, r0, K)
            if act:
                acc = _silu(acc)
            o_ref[r0:r0 + CONV_ROWS, :] = acc.astype(o_ref.dtype)

    return pl.pallas_call(
        body, name=name, grid=(CW // tc,),
        in_specs=[pl.BlockSpec((S, tc), lambda j: (0, j + xoff)),
                  pl.BlockSpec((K, tc), lambda j: (0, j)),
                  pl.BlockSpec((1, tc), lambda j: (0, j))],
        out_specs=pl.BlockSpec((S, tc), lambda j: (0, j)),
        out_shape=jax.ShapeDtypeStruct((S, CW), out_dtype),
        scratch_shapes=[pltpu.VMEM((S + CONV_PAD, tc), f32)],
        compiler_params=_cp("parallel"))(x, w, b)


def conv_bwd(name, x, xoff, w, b, dout, act, dx_dtype):
    S = x.shape[0]
    K, CW = w.shape
    tc = CONV_CH

    def body(x_ref, w_ref, b_ref, d_ref, dx_ref, dw_ref, db_ref, xp, dp):
        xp[0:CONV_PAD, :] = jnp.zeros((CONV_PAD, tc), f32)
        xp[CONV_PAD:CONV_PAD + S, :] = x_ref[...].astype(f32)
        dp[S:S + CONV_PAD, :] = jnp.zeros((CONV_PAD, tc), f32)
        db = jnp.zeros((1, tc), f32)
        for r0 in range(0, S, CONV_ROWS):
            d = d_ref[r0:r0 + CONV_ROWS, :].astype(f32)
            if act:
                pre = _conv_taps(xp, w_ref, b_ref, r0, K)
                s = _sigmoid(pre)
                d = d * (s + pre * s * (1.0 - s))
            dp[r0:r0 + CONV_ROWS, :] = d
            db = db + jnp.sum(d, axis=0, keepdims=True)
        db_ref[...] = db
        for r0 in range(0, S, CONV_ROWS):
            acc = jnp.zeros((CONV_ROWS, tc), f32)
            for k in range(K):
                acc = acc + w_ref[k:k + 1, :] * dp[r0 + (K - 1 - k):r0 + (K - 1 - k) + CONV_ROWS, :]
            dx_ref[r0:r0 + CONV_ROWS, :] = acc.astype(dx_ref.dtype)
        for k in range(K):
            acc = jnp.zeros((1, tc), f32)
            for r0 in range(0, S, CONV_ROWS):
                lo = r0 + CONV_PAD - (K - 1) + k
                acc = acc + jnp.sum(dp[r0:r0 + CONV_ROWS, :] * xp[lo:lo + CONV_ROWS, :], axis=0, keepdims=True)
            dw_ref[k:k + 1, :] = acc

    return pl.pallas_call(
        body, name=name, grid=(CW // tc,),
        in_specs=[pl.BlockSpec((S, tc), lambda j: (0, j + xoff)),
                  pl.BlockSpec((K, tc), lambda j: (0, j)),
                  pl.BlockSpec((1, tc), lambda j: (0, j)),
                  pl.BlockSpec((S, tc), lambda j: (0, j))],
        out_specs=[pl.BlockSpec((S, tc), lambda j: (0, j)),
                   pl.BlockSpec((K, tc), lambda j: (0, j)),
                   pl.BlockSpec((1, tc), lambda j: (0, j))],
        out_shape=[jax.ShapeDtypeStruct((S, CW), dx_dtype),
                   jax.ShapeDtypeStruct((K, CW), f32),
                   jax.ShapeDtypeStruct((1, CW), f32)],
        scratch_shapes=[pltpu.VMEM((S + CONV_PAD, tc), f32), pltpu.VMEM((S + CONV_PAD, tc), f32)],
        compiler_params=_cp("parallel"))(x, w, b, dout)


def loss_head(name, y, target, tm):
    S, D = y.shape

    def body(y_ref, t_ref, dy_ref, l_ref):
        @pl.when(pl.program_id(0) == 0)
        def _():
            l_ref[...] = jnp.zeros_like(l_ref)

        err = y_ref[...] - t_ref[...]
        dy_ref[...] = err * (1.0 / D)
        l_ref[...] += jnp.sum(err * err) * (0.5 / D)

    dy, l = pl.pallas_call(
        body, name=name, grid=(S // tm,),
        in_specs=[pl.BlockSpec((tm, D), lambda i: (i, 0))] * 2,
        out_specs=[pl.BlockSpec((tm, D), lambda i: (i, 0)), pl.BlockSpec((8, LANES), lambda i: (0, 0))],
        out_shape=[jax.ShapeDtypeStruct((S, D), f32), jax.ShapeDtypeStruct((8, LANES), f32)],
        compiler_params=_cp("arbitrary"))(y, target)
    return dy, l[0, 0]


def _flat2d_tiles(rows, cols, itemsize, target_bytes):
    tc = _tile(cols, 1024) if cols % LANES == 0 else cols
    if tc < 512:
        tc = cols
    cap = max(8, target_bytes // (tc * itemsize))
    tr = _tile(rows, cap, 16) if rows % 16 == 0 else rows
    return tr, tc


def adamw(name, w, g, m, v):
    shape = w.shape
    cols = shape[-1]
    rows = math.prod(shape[:-1])
    tr, tc = _flat2d_tiles(rows, cols, 4, 1 << 20)
    c1 = 1.0 - ADAM_B1 ** ADAM_STEP
    c2 = 1.0 - ADAM_B2 ** ADAM_STEP

    def body(w_ref, g_ref, m_ref, v_ref, d_ref, nm_ref, nv_ref, g_out_ref):
        gg = g_ref[...]
        nm = ADAM_B1 * m_ref[...] + (1.0 - ADAM_B1) * gg
        nv = ADAM_B2 * v_ref[...] + (1.0 - ADAM_B2) * (gg * gg)
        d_ref[...] = -ADAM_LR * ((nm / c1) / (jnp.sqrt(nv / c2) + ADAM_EPS) + ADAM_WD * w_ref[...])
        nm_ref[...] = nm
        nv_ref[...] = nv
        g_out_ref[...] = gg

    spec = pl.BlockSpec((tr, tc), lambda i, j: (i, j))
    outs = pl.pallas_call(
        body, name=name, grid=(rows // tr, cols // tc), in_specs=[spec] * 4, out_specs=[spec] * 4,
        out_shape=[jax.ShapeDtypeStruct((rows, cols), f32)] * 4,
        compiler_params=_cp("parallel", "parallel"))(*[a.reshape(rows, cols) for a in (w, g, m, v)])
    return [o.reshape(shape) for o in outs]


def _core_index():
    return lax.axis_index("c").astype(jnp.int32).reshape(1)


def _half_rows_tile(Rh, C):
    return _tile(Rh, max(16, (2 << 20) // (C * 2)), 16)


def rs_add(name, G, buf):
    P, _, Rh, C = G.shape
    tr = _half_rows_tile(Rh, C)

    def body(c_ref, g_ref, b_ref, o_ref):
        o_ref[...] = (g_ref[...].astype(f32) + b_ref[...].astype(f32)).astype(o_ref.dtype)

    return pl.pallas_call(
        body, name=name,
        grid_spec=pltpu.PrefetchScalarGridSpec(
            num_scalar_prefetch=1, grid=(P, Rh // tr),
            in_specs=[pl.BlockSpec((None, None, tr, C), lambda p, i, c: (p, c[0], i, 0)),
                      pl.BlockSpec((None, tr, C), lambda p, i, c: (p, i, 0))],
            out_specs=pl.BlockSpec((None, tr, C), lambda p, i, c: (p, i, 0))),
        out_shape=jax.ShapeDtypeStruct((P, Rh, C), bf16),
        compiler_params=_cp("parallel", "parallel"))(_core_index(), G, buf)


def _chip_indices():
    x, y, c = lax.axis_index("x"), lax.axis_index("y"), lax.axis_index("c")
    ids = [2 * x + y] + [2 * _flip(x, fx) + _flip(y, fy) for fx, fy in _CHIP_FLIPS] + [c]
    return [i.astype(jnp.int32).reshape(1) for i in ids]


def rs_sum4(name, pair, buf, final, layer):
    P, Rh, C = buf.shape
    tr = _half_rows_tile(Rh, C)

    def body(i0, i1, i2, i3, ic, b0, b1, b2, b3, f_ref, o_ref):
        o_ref[...] = ((b0[...].astype(f32) + b1[...].astype(f32)) + b2[...].astype(f32)) + b3[...].astype(f32)

    blk = (None, tr, C)
    return pl.pallas_call(
        body, name=name,
        grid_spec=pltpu.PrefetchScalarGridSpec(
            num_scalar_prefetch=5, grid=(Rh // tr,),
            in_specs=[pl.BlockSpec(blk, lambda i, *ids, k=k: (ids[k][0], i, 0)) for k in range(P)] + [_ANY],
            out_specs=pl.BlockSpec((None, None, tr, C), lambda i, *ids: (layer, ids[4][0], i, 0))),
        out_shape=jax.ShapeDtypeStruct(final.shape, final.dtype),
        input_output_aliases={9: 0},
        compiler_params=_cp("parallel"))(*_chip_indices(), pair, buf, buf, buf, final)


def place_own(name, w, layer):
    _, R, C = w.shape
    tr = _tile(R, max(16, (2 << 20) // (C * 2)), 16)

    def body(q, w_ref, own_ref, land_ref):
        wb = w_ref[...].astype(bf16)
        own_ref[...] = wb
        land_ref[...] = wb

    return pl.pallas_call(
        body, name=name,
        grid_spec=pltpu.PrefetchScalarGridSpec(
            num_scalar_prefetch=1, grid=(R // tr,),
            in_specs=[pl.BlockSpec((None, tr, C), lambda i, q: (layer, i, 0))],
            out_specs=[pl.BlockSpec((tr, C), lambda i, q: (i, 0)),
                       pl.BlockSpec((None, tr, C), lambda i, q: (q[0], i, 0))]),
        out_shape=[jax.ShapeDtypeStruct((R, C), bf16), jax.ShapeDtypeStruct((N_CHIPS, R, C), bf16)],
        compiler_params=_cp("parallel"))(_chip_indices()[0], w)


_ANY = pl.BlockSpec(memory_space=pl.ANY)
_CHIP_FLIPS = ((1, 0), (0, 1), (1, 1))


def _place():
    return lax.axis_index("x"), lax.axis_index("y"), lax.axis_index("c")


def _flip(v, f):
    return 1 - v if f else v


def _remote(src, dst, ssem, rsem, dev):
    return pltpu.make_async_remote_copy(src_ref=src, dst_ref=dst, send_sem=ssem, recv_sem=rsem,
                                        device_id=dev, device_id_type=MESH)


_HBM = pl.BlockSpec(memory_space=pltpu.HBM)
_SEM = pl.BlockSpec(memory_space=pltpu.SEMAPHORE)
_DATAFLOW = pltpu.SideEffectType.DATAFLOW_SIDE_EFFECTING
_TOKEN = jax.ShapeDtypeStruct((8, LANES), f32)


def _in_hbm(a):
    return pltpu.with_memory_space_constraint(a, pltpu.HBM)


def _uninit(tag, shape, dtype):
    def body(o_ref):
        pass

    return pl.pallas_call(body, name="uninit_" + tag, out_specs=_ANY,
                          out_shape=jax.ShapeDtypeStruct(shape, dtype))()


def _hbm_like(a):
    return pltpu.HBM(a.shape, a.dtype)


def gather_ici_start(name, groups, dep=None):
    sizes = [len(g) for g in groups]
    owns = [o for g in groups for o, _ in g]
    lands = [l for g in groups for _, l in g]
    n, ng = len(owns), len(groups)
    deps = [] if dep is None else [dep]

    def body(*refs):
        own, land = refs[:n], refs[n:2 * n]
        sems = refs[2 * n + len(deps):2 * n + len(deps) + 2 * ng]
        token = refs[-1]
        x, y, c = _place()
        q = 2 * x + y
        t = 0
        for gi, size in enumerate(sizes):
            for j in range(size):
                for k, (fx, fy) in enumerate(_CHIP_FLIPS):
                    _remote(own[t].at[c], land[t].at[q, c], sems[2 * gi].at[3 * j + k], sems[2 * gi + 1].at[3 * j + k],
                            (_flip(x, fx), _flip(y, fy), c)).start()
                t += 1
        token[...] = jnp.zeros_like(token)

    sem_shapes = [pltpu.SemaphoreType.DMA((3 * size,)) for size in sizes for _ in range(2)]
    res = pl.pallas_call(
        body, name=name,
        in_specs=[_HBM] * (2 * n) + [_ANY] * len(deps),
        out_specs=[_SEM] * (2 * ng) + [_HBM] * (2 * n) + [pl.BlockSpec(memory_space=pltpu.VMEM)],
        out_shape=sem_shapes + [_hbm_like(a) for a in owns + lands] + [_TOKEN],
        input_output_aliases={i: 2 * ng + i for i in range(2 * n)},
        compiler_params=pltpu.CompilerParams(has_side_effects=_DATAFLOW),
    )(*[_in_hbm(a) for a in owns + lands], *deps)
    own_thru, land_thru = res[2 * ng:2 * ng + n], res[2 * ng + n:2 * ng + 2 * n]
    handles, t = [], 0
    for gi, size in enumerate(sizes):
        handles.append((res[2 * gi], res[2 * gi + 1], list(own_thru[t:t + size]), list(land_thru[t:t + size])))
        t += size
    return handles, res[-1]


def gather_ici_wait(name, handle, after):
    send, recv, owns, lands = handle
    n = len(owns)

    def body(*refs):
        own, land = refs[:n], refs[n:2 * n]
        send_ref, recv_ref = refs[2 * n], refs[2 * n + 1]
        x, y, c = _place()
        for j in range(n):
            for k, (fx, fy) in enumerate(_CHIP_FLIPS):
                px, py = _flip(x, fx), _flip(y, fy)
                cp = _remote(own[j].at[c], land[j].at[2 * px + py, c], send_ref.at[3 * j + k], recv_ref.at[3 * j + k],
                             (px, py, c))
                cp.wait_send()
                cp.wait_recv()

    res = pl.pallas_call(
        body, name=name,
        in_specs=[_HBM] * (2 * n) + [_SEM, _SEM, _ANY],
        out_specs=[_HBM] * (2 * n),
        out_shape=[_hbm_like(a) for a in owns + lands],
        input_output_aliases={i: i for i in range(2 * n)},
        compiler_params=pltpu.CompilerParams(has_side_effects=_DATAFLOW),
    )(*owns, *lands, send, recv, after)
    return list(res[n:])


def _split_start(name, body, arrays, n_sems, dep=None):
    n = len(arrays)
    deps = [] if dep is None else [dep]

    def kernel_body(*refs):
        m = n + len(deps)
        body(refs[:n], refs[m], refs[m + 1])
        refs[-1][...] = jnp.zeros_like(refs[-1])

    res = pl.pallas_call(
        kernel_body, name=name,
        in_specs=[_HBM] * n + [_ANY] * len(deps),
        out_specs=[_SEM, _SEM] + [_HBM] * n + [pl.BlockSpec(memory_space=pltpu.VMEM)],
        out_shape=[pltpu.SemaphoreType.DMA((n_sems,)), pltpu.SemaphoreType.DMA((n_sems,))]
        + [_hbm_like(a) for a in arrays] + [_TOKEN],
        input_output_aliases={i: 2 + i for i in range(n)},
        compiler_params=pltpu.CompilerParams(has_side_effects=_DATAFLOW),
    )(*[_in_hbm(a) for a in arrays], *deps)
    return res[0], res[1], list(res[2:2 + n]), res[-1]


def _split_wait(name, body, handle, after):
    send, recv, arrays, _ = handle
    n = len(arrays)

    def kernel_body(*refs):
        body(refs[:n], refs[n], refs[n + 1])

    res = pl.pallas_call(
        kernel_body, name=name,
        in_specs=[_HBM] * n + [_SEM, _SEM, _ANY],
        out_specs=[_HBM] * n,
        out_shape=[_hbm_like(a) for a in arrays],
        input_output_aliases={i: i for i in range(n)},
        compiler_params=pltpu.CompilerParams(has_side_effects=_DATAFLOW),
    )(*arrays, send, recv, after)
    return list(res)


def _forward_copies(land, send, recv):
    x, y, c = _place()
    for t in range(len(land)):
        for k, (fx, fy) in enumerate(_CHIP_FLIPS):
            slab = land[t].at[2 * _flip(x, fx) + _flip(y, fy), c]
            yield _remote(slab, slab, send.at[3 * t + k], recv.at[3 * t + k], (x, y, 1 - c))


def forward_start(name, lands):
    def body(land, send, recv):
        for cp in _forward_copies(land, send, recv):
            cp.start()

    return _split_start(name, body, lands, 3 * len(lands))


def forward_wait(name, handle, after):
    def body(land, send, recv):
        for cp in _forward_copies(land, send, recv):
            cp.wait_send()
            cp.wait_recv()

    return _split_wait(name, body, handle, after)


def sibling_start(name, Gs, dep=None):
    T = len(Gs)
    bufs = [_uninit(f"{name}_{t}", (G.shape[0],) + G.shape[2:], G.dtype) for t, G in enumerate(Gs)]

    def body(refs, send, recv):
        x, y, c = _place()
        for t in range(T):
            _remote(refs[t].at[:, 1 - c], refs[T + t], send.at[t], recv.at[t], (x, y, 1 - c)).start()

    return _split_start(name, body, list(Gs) + bufs, T, dep)


def sibling_wait(name, handle, after):
    T = len(handle[2]) // 2

    def body(refs, send, recv):
        x, y, c = _place()
        for t in range(T):
            cp = _remote(refs[t].at[:, 1 - c], refs[T + t], send.at[t], recv.at[t], (x, y, 1 - c))
            cp.wait_send()
            cp.wait_recv()

    res = _split_wait(name, body, handle, after)
    return res[:T], res[T:]


def reduce_ici_start(name, Ss):
    T = len(Ss)
    lands = [_uninit(f"{name}_{t}", S.shape, S.dtype) for t, S in enumerate(Ss)]

    def body(*refs):
        s, land = refs[:T], refs[T:2 * T]
        send, recv = refs[2 * T], refs[2 * T + 1]
        token = refs[-1]
        x, y, c = _place()
        q = 2 * x + y
        for t in range(T):
            for k, (fx, fy) in enumerate(_CHIP_FLIPS):
                px, py = _flip(x, fx), _flip(y, fy)
                _remote(s[t].at[2 * px + py], land[t].at[q], send.at[3 * t + k], recv.at[3 * t + k],
                        (px, py, c)).start()
        token[...] = jnp.zeros_like(token)

    res = pl.pallas_call(
        body, name=name,
        in_specs=[_HBM] * (2 * T),
        out_specs=[_SEM, _SEM] + [_HBM] * (2 * T) + [pl.BlockSpec(memory_space=pltpu.VMEM)],
        out_shape=[pltpu.SemaphoreType.DMA((3 * T,)), pltpu.SemaphoreType.DMA((3 * T,))]
        + [_hbm_like(a) for a in Ss + lands] + [_TOKEN],
        input_output_aliases={i: 2 + i for i in range(2 * T)},
        compiler_params=pltpu.CompilerParams(has_side_effects=_DATAFLOW),
    )(*[_in_hbm(a) for a in Ss + lands])
    return res[0], res[1], list(res[2:2 + T]), list(res[2 + T:2 + 2 * T]), res[-1]


def reduce_ici_wait(name, handle, after):
    send, recv, Ss, lands, _ = handle
    T = len(Ss)

    def body(*refs):
        s, land = refs[:T], refs[T:2 * T]
        send_ref, recv_ref = refs[2 * T], refs[2 * T + 1]
        x, y, c = _place()
        for t in range(T):
            for k, (fx, fy) in enumerate(_CHIP_FLIPS):
                px, py = _flip(x, fx), _flip(y, fy)
                cp = _remote(s[t].at[2 * px + py], land[t].at[2 * px + py], send_ref.at[3 * t + k], recv_ref.at[3 * t + k],
                             (px, py, c))
                cp.wait_send()
                cp.wait_recv()

    res = pl.pallas_call(
        body, name=name,
        in_specs=[_HBM] * (2 * T) + [_SEM, _SEM, _ANY],
        out_specs=[_HBM] * (2 * T),
        out_shape=[_hbm_like(a) for a in Ss + lands],
        input_output_aliases={i: i for i in range(2 * T)},
        compiler_params=pltpu.CompilerParams(has_side_effects=_DATAFLOW),
    )(*Ss, *lands, send, recv, after)
    return list(res[:T]), list(res[T:])


def rs_share(name, tots, layers):
    T = len(tots)

    def body(*refs):
        o = refs[T:2 * T]
        send, recv = refs[2 * T:]
        x, y, c = _place()
        cps = []
        for t in range(T):
            mine = o[t].at[layers[t], c]
            cp = _remote(mine, mine, send.at[t], recv.at[t], (x, y, 1 - c))
            cp.start()
            cps.append(cp)
        for t in range(T):
            other = o[t].at[layers[t], 1 - c]
            _remote(other, other, send.at[t], recv.at[t], (x, y, 1 - c)).wait_recv()
        for cp in cps:
            cp.wait_send()

    return pl.pallas_call(
        body, name=name, in_specs=[_ANY] * T, out_specs=[_ANY] * T,
        out_shape=[jax.ShapeDtypeStruct(s.shape, s.dtype) for s in tots],
        input_output_aliases={t: t for t in range(T)},
        scratch_shapes=[pltpu.SemaphoreType.DMA((T,)), pltpu.SemaphoreType.DMA((T,))],
        )(*tots)


def all_reduce_small(name, vec):
    rows = vec.shape[0]
    flips = [(fx, fy, fc) for fx in (0, 1) for fy in (0, 1) for fc in (0, 1)][1:]

    def body(v_ref, o_ref, buf, send, recv):
        x, y, c = _place()
        me = 4 * x + 2 * y + c
        buf[me] = v_ref[...]
        cps = []
        for k, (fx, fy, fc) in enumerate(flips):
            cp = _remote(buf.at[me], buf.at[me], send.at[k], recv.at[k],
                         (_flip(x, fx), _flip(y, fy), _flip(c, fc)))
            cp.start()
            cps.append(cp)
        for k, (fx, fy, fc) in enumerate(flips):
            slab = buf.at[4 * _flip(x, fx) + 2 * _flip(y, fy) + _flip(c, fc)]
            _remote(slab, slab, send.at[k], recv.at[k], (x, y, c)).wait_recv()
        for cp in cps:
            cp.wait_send()
        acc = buf[0]
        for d in range(1, N_DEV):
            acc = acc + buf[d]
        o_ref[...] = acc

    return pl.pallas_call(
        body, name=name,
        in_specs=[pl.BlockSpec(memory_space=pltpu.VMEM)], out_specs=pl.BlockSpec(memory_space=pltpu.VMEM),
        out_shape=jax.ShapeDtypeStruct((rows, LANES), f32),
        scratch_shapes=[pltpu.VMEM((N_DEV, rows, LANES), f32),
                        pltpu.SemaphoreType.DMA((N_DEV - 1,)), pltpu.SemaphoreType.DMA((N_DEV - 1,))],
        compiler_params=pltpu.CompilerParams(vmem_limit_bytes=V7X_VMEM_LIMIT))(vec)


def _all_done(arrays):
    return jnp.stack([a[(0,) * a.ndim].astype(f32) for a in arrays]).sum(keepdims=True)


def _pack(arrays):
    flat = jnp.concatenate([a.reshape(-1) for a in arrays])
    n = flat.shape[0]
    rows = -(-n // (8 * LANES)) * 8
    return jnp.pad(flat, (0, rows * LANES - n)).reshape(rows, LANES)


def _unpack(vec, shapes):
    flat = vec.reshape(-1)
    out, pos = [], 0
    for s in shapes:
        n = math.prod(s)
        out.append(flat[pos:pos + n].reshape(s))
        pos += n
    return out


def _f_first(x, g):
    return x, _rms(x, g)


def _f_mid(h, m, gp, gn):
    h1 = h + _rms(m, gp)
    return h1, _rms(h1, gn)


def _f_mid_bias(h, m, b, gp, gn):
    h1 = h + _rms(m + b, gp)
    return h1, _rms(h1, gn)


def _f_last(h, m, gp):
    return (h + _rms(m, gp),)


def _f_swiglu(gate, up):
    return (_silu(gate) * up,)


def _swiglu_tile(gate, up):
    return _silu(gate.astype(f32)) * up.astype(f32)


def _swiglu_bwd_tile(d_act, gate, up):
    _, vjp = jax.vjp(_f_swiglu, gate.astype(f32), up.astype(f32))
    return vjp((d_act,))


def _f_glu(a, g, ba, bg):
    return ((a + ba) * _sigmoid(g + bg),)


def _f_ln_silu(x, g, b):
    mu = jnp.mean(x, axis=-1, keepdims=True)
    xc = x - mu
    y = xc * lax.rsqrt(jnp.mean(xc * xc, axis=-1, keepdims=True) + LN_EPS) * g + b
    return (_silu(y),)


def _f_lower_bounds(logits):
    n = logits.shape[0]
    e = jnp.exp(logits - jnp.max(logits, axis=0, keepdims=True))
    p = e / jnp.sum(e, axis=0, keepdims=True)
    layer = lax.broadcasted_iota(jnp.int32, logits.shape, 0)
    out = -jnp.broadcast_to(p[0:1, :], logits.shape)
    for j in range(n):
        out = out + jnp.where(layer >= j, p[j:j + 1, :], 0.0)
    return (out,)


WEIGHT_NAMES = ['mix_pre_g', 'mix_post_g', 'ffn_pre_g', 'ffn_post_g', 'hgrn_lb_logits', 'even_w_in',
                'hgrn_norm_g', 'ssd_conv_w', 'ssd_conv_b', 'ssd_dt_bias', 'ssd_a_log', 'ssd_d', 'ssd_norm_g',
                'even_w_out', 'conf_w1', 'conf_b1', 'conf_dw_w', 'conf_dw_b', 'conf_ln_g', 'conf_ln_b',
                'conf_w2', 'conf_b2', 'ffn_w_gate', 'ffn_w_up', 'ffn_w_down']
BIG = ['even_w_in', 'even_w_out', 'conf_w1', 'conf_w2', 'ffn_w_gate', 'ffn_w_up', 'ffn_w_down']
SMALL_SHARDED = {'ssd_conv_w': 2, 'conf_b1': 1, 'conf_dw_w': 2, 'conf_dw_b': 1, 'conf_ln_g': 1,
                 'conf_ln_b': 1, 'conf_b2': 1}


def _train_step(x, target, w, m, v):
    S, D = x.shape[1], x.shape[2]
    x2, t2 = x[0], target[0]
    NL = w['mix_pre_g'].shape[0]
    HB = w['ssd_dt_bias'].shape[1]
    GN = B_GROUPS * B_STATE
    xw, yw, cw = _place()
    chip = 2 * xw + yw
    tm = _tile(S, 128, 8)
    row1 = lambda a, i: a[i:i + 1]

    sharded = list(SMALL_SHARDED)
    placed = []
    for n in sharded:
        ax, a = SMALL_SHARDED[n], w[n]
        full = jnp.zeros(a.shape[:ax] + (a.shape[ax] * N_CHIPS,) + a.shape[ax + 1:], f32)
        start = [0] * a.ndim
        start[ax] = chip * a.shape[ax]
        placed.append(lax.dynamic_update_slice(full, jnp.where(cw == 0, a, 0.0), start))
    whole = dict(zip(sharded, _unpack(all_reduce_small("gather_small", _pack(placed)), [p.shape for p in placed])))
    small = {n: whole.get(n, w[n]) for n in WEIGHT_NAMES if n not in BIG}

    def mixer_keys(layer):
        names = ('even_w_in', 'even_w_out') if layer % 2 == 0 else ('conf_w1', 'conf_w2')
        return [(n, layer // 2) for n in names]

    def ffn_keys(layer):
        return [(n, layer) for n in ('ffn_w_gate', 'ffn_w_up', 'ffn_w_down')]

    groups = [keys(layer) for layer in range(NL) for keys in (mixer_keys, ffn_keys)]
    halves = lambda a: a.reshape(a.shape[:-2] + (2, a.shape[-2] // 2, a.shape[-1]))
    own, land = {}, {}

    def start_groups(name, some, dep=None):
        for g in some:
            for n, l in g:
                own[n, l], land[n, l] = place_own("place_own", w[n], l)
        return gather_ici_start(name, [[(halves(own[k]), halves(land[k])) for k in g] for g in some], dep)

    first, first_started = start_groups("gather_start_first", groups[:1])
    rest, rest_started = start_groups("gather_start_rest", groups[1:], first_started)
    handles = first + rest
    W, handed = {}, {}

    def fetch_begin(gi, after):
        arrived = gather_ici_wait(f"gather_wait_{gi}", handles[gi], after)
        handed[gi] = forward_start(f"forward_start_{gi}", arrived)
        return handed[gi][-1]

    def fetch_end(gi, after):
        for k, a in zip(groups[gi], forward_wait(f"forward_wait_{gi}", handed[gi], after)):
            W[k] = a.reshape((N_CHIPS, 1) + own[k].shape)

    WM = 6 * D + 2 * GN
    w_main, w_dt = {}, {}

    n_even = small['hgrn_lb_logits'].shape[0]
    (lbs,) = _stage_fwd("lower_bounds", _f_lower_bounds, [small['hgrn_lb_logits']], [], [(D, f32)], tm=n_even)
    saved = []
    h = x2
    (u,) = _stage_fwd("pre_norm", lambda a, g: (_rms(a, g),), [h],
                      [row1(small['mix_pre_g'], 0) + rest_started[0, 0]], [(D, bf16)], tm=tm)
    fetch_begin(0, u)
    for layer in range(NL):
        li = layer // 2
        r = {'h': h, 'u': u}
        fetch_end(2 * layer, u)
        if layer % 2 == 0:
            wm, wd = join_in_proj("in_proj_join", W['even_w_in', li][:, 0], WM)
            w_main[li], w_dt[li] = wm[None, None], wd[None, None]
            r['ymain'] = mm_nn_col("in_proj", u, w_main[li], 0)
            r['dtr'] = mm_nn_col("in_proj_dt", u, w_dt[li], 0)
            r['xact'] = conv_fwd("ssd_conv", r['ymain'], 5 * D // CONV_CH, small['ssd_conv_w'][li],
                                 row1(small['ssd_conv_b'], li), True, f32)
            o_a, r['hg_st'] = hgrn_fwd("hgrn", r['ymain'], row1(lbs, li), row1(small['hgrn_norm_g'], li), D)
            begun = fetch_begin(2 * layer + 1, o_a)[0, 0]
            o_b, r['ssd_st'] = ssd_fwd("ssd", r['xact'], r['ymain'], r['dtr'], row1(small['ssd_dt_bias'], li),
                                       row1(small['ssd_a_log'], li), row1(small['ssd_d'], li),
                                       row1(small['ssd_norm_g'], li) + begun, D)
            r['mixed'] = jnp.concatenate([o_a, o_b], axis=1)
            r['m'] = mm_nn_row("out_proj", r['mixed'], W['even_w_out', li], 0)
            mid_fn, mid_par = _f_mid, []
        else:
            r['c1'] = mm_nn_col("conf_in", u, W['conf_w1', li], 0)
            b1 = row1(small['conf_b1'], li)
            tn = _tile(D, 512)
            (r['glu'],) = _stage_fwd("conf_glu", _f_glu, [r['c1'], (r['c1'], D // tn)], [b1, (b1, D // tn)],
                                     [(D, f32)], tm=tm, tn=tn)
            r['cc'] = conv_fwd("conf_conv", r['glu'], 0, small['conf_dw_w'][li], row1(small['conf_dw_b'], li),
                               False, f32)
            begun = fetch_begin(2 * layer + 1, r['cc'])[0, 0]
            (r['c2'],) = _stage_fwd("conf_ln", _f_ln_silu, [r['cc']],
                                    [row1(small['conf_ln_g'], li) + begun, row1(small['conf_ln_b'], li)],
                                    [(D, bf16)], tm=tm)
            r['m'] = mm_nn_row("conf_out", r['c2'], W['conf_w2', li], 0)
            mid_fn, mid_par = _f_mid_bias, [row1(small['conf_b2'], li)]
        r['mid_fn'] = mid_fn
        r['mid_par'] = mid_par + [row1(small['mix_post_g'], layer), row1(small['ffn_pre_g'], layer)]
        r['h1'], r['u2'] = _stage_fwd("mid_norm", mid_fn, [h, r['m']], r['mid_par'], [(D, f32), (D, bf16)], tm=tm)
        fetch_end(2 * layer + 1, r['u2'])
        r['gate'] = mm_nn_col("ffn_gate", r['u2'], W['ffn_w_gate', layer], 0, bf16)
        begun = fetch_begin(2 * layer + 2, r['gate']) if layer + 1 < NL else None
        r['up'] = mm_nn_col("ffn_up", r['u2'], W['ffn_w_up', layer], 0, bf16, dep=begun)
        r['dn'] = mm_nn_row("ffn_down", [r['gate'], r['up']], W['ffn_w_down', layer], 0, a_fn=_swiglu_tile)
        if layer + 1 < NL:
            r['end_fn'] = _f_mid
            r['end_par'] = [row1(small['ffn_post_g'], layer), row1(small['mix_pre_g'], layer + 1)]
            h, u = _stage_fwd("end_norm", _f_mid, [r['h1'], r['dn']], r['end_par'], [(D, f32), (D, bf16)], tm=tm)
        else:
            r['end_fn'] = _f_last
            r['end_par'] = [row1(small['ffn_post_g'], layer)]
            (h,) = _stage_fwd("last_norm", _f_last, [r['h1'], r['dn']], r['end_par'], [(D, f32)], tm=tm)
        saved.append(r)

    dy, loss_local = loss_head("loss_head", h, t2, tm)
    loss = lax.psum(loss_local, ("x", "y", "c"))

    gs_rows = {n: [None] * small[n].shape[0] for n in small}
    gs = {}

    def put(n, i, val):
        gs_rows[n][i] = val.reshape(small[n].shape[1:])

    final = {n: _uninit(n, (w[n].shape[0], 2, w[n].shape[1] // 2, w[n].shape[2]), f32) for n in BIG}
    to_sibling, to_chips = [], []

    def reduce_begin(gi, parts, dep=None):
        handle = sibling_start(f"sibling_start_{gi}", [halves(p) for p in parts], dep)
        to_sibling.append((gi, handle))
        return handle[-1][0, 0]

    def reduce_middle(after):
        gi, handle = to_sibling.pop(0)
        parts, from_sib = sibling_wait(f"sibling_wait_{gi}", handle, after)
        sums = [rs_add("reduce_add", a, b) for a, b in zip(parts, from_sib)]
        handle = reduce_ici_start(f"reduce_start_{gi}", sums)
        to_chips.append((gi, handle))
        return handle[-1][0, 0]

    def reduce_finish(after):
        gi, handle = to_chips.pop(0)
        keys = groups[gi]
        sums, lands = reduce_ici_wait(f"reduce_wait_{gi}", handle, after)
        for (n, l), s_, b_ in zip(keys, sums, lands):
            final[n] = rs_sum4("reduce_sum", s_, b_, final[n], l)
        names = [n for n, _ in keys]
        shared = rs_share("reduce_share", [final[n] for n in names], [l for _, l in keys])
        final.update(zip(names, shared))

    def reduce_step(gi, parts, newest, dep=None):
        zero = reduce_begin(gi, parts, dep)
        if to_chips:
            reduce_finish(newest)
        if len(to_sibling) > 1:
            zero = zero + reduce_middle(newest)
        return zero

    grads, first_layer = {}, {}

    def small_gradients(dh, du_parts, started):
        (first_layer['grad_x'],), pg = _stage_bwd("pre_norm_bwd", _f_first, [x2],
                                                   [row1(small['mix_pre_g'], 0) + started],
                                                   [[dh], du_parts], [f32], tm=tm)
        put('mix_pre_g', 0, pg[0])
        dlbs = jnp.concatenate([saved[2 * i]['dlb'] for i in range(n_even)], axis=0)
        (dlogits,), _ = _stage_bwd("lower_bounds_bwd", _f_lower_bounds, [small['hgrn_lb_logits']], [],
                                   [[dlbs]], [f32], tm=n_even)
        names_s = [n for n in WEIGHT_NAMES if n not in BIG]
        for n in names_s:
            gs[n] = dlogits if n == 'hgrn_lb_logits' else jnp.stack(gs_rows[n])
        total = all_reduce_small("reduce_small", _pack([gs[n] for n in names_s]))
        for n, a in zip(names_s, _unpack(total, [gs[n].shape for n in names_s])):
            if n in SMALL_SHARDED:
                ax = SMALL_SHARDED[n]
                size = w[n].shape[ax]
                start = [0] * a.ndim
                start[ax] = chip * size
                a = lax.dynamic_slice(a, start, a.shape[:ax] + (size,) + a.shape[ax + 1:])
            grads[n] = a
        return total

    dh = dy
    du_parts = None
    started = None
    for layer in reversed(range(NL)):
        li = layer // 2
        r = saved[layer]
        cts = [[dh]] if du_parts is None else [[dh], du_parts]
        par = r['end_par'] if started is None else [r['end_par'][0] + started] + r['end_par'][1:]
        (dh1, d_dn), pg = _stage_bwd("end_norm_bwd", r['end_fn'], [r['h1'], r['dn']], par, cts,
                                     [f32, bf16], tm=tm)
        put('ffn_post_g', layer, pg[0])
        if du_parts is not None:
            put('mix_pre_g', layer + 1, pg[1])
        d_gate, d_up = mm_nt_row("ffn_down_dx", d_dn, W['ffn_w_down', layer], 0, bf16,
                                 tail=(_swiglu_bwd_tile, [r['gate'], r['up']]), n_out=2)
        g_down = mm_tn_row("ffn_down_dw", [r['gate'], r['up']], d_dn, N_CHIPS, a_fn=_swiglu_tile)
        du_a = mm_nt_col("ffn_gate_dx", d_gate, W['ffn_w_gate', layer], 0)
        du_b = mm_nt_col("ffn_up_dx", d_up, W['ffn_w_up', layer], 0)
        g_gate = mm_tn_col("ffn_gate_dw", r['u2'], d_gate, N_CHIPS)
        g_up = mm_tn_col("ffn_up_dw", r['u2'], d_up, N_CHIPS)
        started = reduce_step(2 * layer + 1, [g_gate, g_up, g_down], g_up)
        par = r['mid_par'][:-1] + [r['mid_par'][-1] + started]
        (dh, dm), pg = _stage_bwd("mid_norm_bwd", r['mid_fn'], [r['h'], r['m']], par,
                                  [[dh1], [du_a, du_b]], [f32, bf16], tm=tm)
        put('mix_post_g', layer, pg[-2])
        put('ffn_pre_g', layer, pg[-1])
        if layer % 2 == 0:
            d_mixed = mm_nt_row("out_proj_dx", dm, W['even_w_out', li], 0)
            g_out = mm_tn_row("out_proj_dw", r['mixed'], dm, N_CHIPS)
            dxs, dbm, dcm, dz, ddt, ddtb, dal, dds, dbn = ssd_bwd(
                "ssd_bwd", r['xact'], r['ymain'], r['dtr'], row1(small['ssd_dt_bias'], li),
                row1(small['ssd_a_log'], li), row1(small['ssd_d'], li), row1(small['ssd_norm_g'], li),
                r['ssd_st'], d_mixed, D)
            put('ssd_dt_bias', li, ddtb)
            put('ssd_a_log', li, dal)
            put('ssd_d', li, dds)
            put('ssd_norm_g', li, dbn)
            d_xact = jnp.concatenate([dxs, dbm, dcm], axis=1)
            d_xbc, dcw, dcb = conv_bwd("ssd_conv_bwd", r['ymain'], 5 * D // CONV_CH, small['ssd_conv_w'][li],
                                       row1(small['ssd_conv_b'], li), d_xact, True, bf16)
            put('ssd_conv_w', li, dcw)
            put('ssd_conv_b', li, dcb)
            dq, df, dv, dg, dlb, dan = hgrn_bwd("hgrn_bwd", r['ymain'], row1(lbs, li), row1(small['hgrn_norm_g'], li),
                                               r['hg_st'], d_mixed, D)
            put('hgrn_norm_g', li, dan)
            r['dlb'] = dlb
            d_main = jnp.concatenate([dq, df, dv, dg, dz, d_xbc], axis=1)
            du_parts = [mm_nt_col("in_proj_dx", d_main, w_main[li], 0),
                        mm_nt_col("in_proj_dt_dx", ddt, w_dt[li], 0)]
            g_in = split_in_proj("in_proj_split", mm_tn_col("in_proj_dw", r['u'], d_main, 1)[0],
                                 mm_tn_col("in_proj_dt_dw", r['u'], ddt, 1)[0], N_CHIPS)
            mixer_parts, newest = [g_in, g_out], g_in
        else:
            put('conf_b2', li, pg[0])
            d_c2 = mm_nt_row("conf_out_dx", dm, W['conf_w2', li], 0)
            g_w2 = mm_tn_row("conf_out_dw", r['c2'], dm, N_CHIPS)
            (d_cc,), pl_ = _stage_bwd("conf_ln_bwd", _f_ln_silu, [r['cc']],
                                      [row1(small['conf_ln_g'], li), row1(small['conf_ln_b'], li)],
                                      [[d_c2]], [f32], tm=tm)
            put('conf_ln_g', li, pl_[0])
            put('conf_ln_b', li, pl_[1])
            d_glu, ddw, ddb = conv_bwd("conf_conv_bwd", r['glu'], 0, small['conf_dw_w'][li],
                                       row1(small['conf_dw_b'], li), d_cc, False, f32)
            put('conf_dw_w', li, ddw)
            put('conf_dw_b', li, ddb)
            b1 = row1(small['conf_b1'], li)
            tn = _tile(D, 512)
            (da, dg_), pb = _stage_bwd("conf_glu_bwd", _f_glu, [r['c1'], (r['c1'], D // tn)], [b1, (b1, D // tn)],
                                       [[d_glu]], [bf16, bf16], tm=tm, tn=tn)
            put('conf_b1', li, jnp.concatenate([pb[0], pb[1]], axis=1))
            d_c1 = jnp.concatenate([da, dg_], axis=1)
            du_parts = [mm_nt_col("conf_in_dx", d_c1, W['conf_w1', li], 0)]
            g_w1 = mm_tn_col("conf_in_dw", r['u'], d_c1, N_CHIPS)
            mixer_parts, newest = [g_w1, g_w2], g_w1
        small_done = small_gradients(dh, du_parts, started) if layer == 0 else None
        started = reduce_step(2 * layer, mixer_parts, newest, small_done)
    started = started + reduce_middle(mixer_parts[0])

    delta, new_m, new_v = {}, {}, {}

    def update(names):
        for n in names:
            g = final[n].reshape(w[n].shape) if n in BIG else grads[n]
            delta[n], new_m[n], new_v[n], grads[n] = adamw("adamw", w[n], g, m[n], v[n])
        return _all_done([delta[n] for n in names])

    second_last = [n for n, _ in groups[to_chips[0][0]]]
    last = [n for n, _ in groups[to_chips[1][0]]]
    grads[WEIGHT_NAMES[0]] = grads[WEIGHT_NAMES[0]] + started
    grad_x2 = first_layer['grad_x']
    done = update([n for n in WEIGHT_NAMES if n not in last + second_last])
    reduce_finish(done)
    done = update(second_last)
    reduce_finish(done)
    update(last)
    return (loss, grad_x2[None], *[grads[n] for n in WEIGHT_NAMES], *[delta[n] for n in WEIGHT_NAMES],
            *[new_m[n] for n in WEIGHT_NAMES], *[new_v[n] for n in WEIGHT_NAMES])


def kernel(x, mix_pre_g, mix_post_g, ffn_pre_g, ffn_post_g, hgrn_lb_logits, even_w_in, hgrn_norm_g, ssd_conv_w, ssd_conv_b, ssd_dt_bias, ssd_a_log, ssd_d, ssd_norm_g, even_w_out, conf_w1, conf_b1, conf_dw_w, conf_dw_b, conf_ln_g, conf_ln_b, conf_w2, conf_b2, ffn_w_gate, ffn_w_up, ffn_w_down, loss_target, m_mix_pre_g, m_mix_post_g, m_ffn_pre_g, m_ffn_post_g, m_hgrn_lb_logits, m_even_w_in, m_hgrn_norm_g, m_ssd_conv_w, m_ssd_conv_b, m_ssd_dt_bias, m_ssd_a_log, m_ssd_d, m_ssd_norm_g, m_even_w_out, m_conf_w1, m_conf_b1, m_conf_dw_w, m_conf_dw_b, m_conf_ln_g, m_conf_ln_b, m_conf_w2, m_conf_b2, m_ffn_w_gate, m_ffn_w_up, m_ffn_w_down, v_mix_pre_g, v_mix_post_g, v_ffn_pre_g, v_ffn_post_g, v_hgrn_lb_logits, v_even_w_in, v_hgrn_norm_g, v_ssd_conv_w, v_ssd_conv_b, v_ssd_dt_bias, v_ssd_a_log, v_ssd_d, v_ssd_norm_g, v_even_w_out, v_conf_w1, v_conf_b1, v_conf_dw_w, v_conf_dw_b, v_conf_ln_g, v_conf_ln_b, v_conf_w2, v_conf_b2, v_ffn_w_gate, v_ffn_w_up, v_ffn_w_down):
    args = locals()
    w = {n: args[n] for n in WEIGHT_NAMES}
    m = {n: args["m_" + n] for n in WEIGHT_NAMES}
    v = {n: args["v_" + n] for n in WEIGHT_NAMES}
    return _train_step(x, loss_target, w, m, v)
```
